```python
import functools
import jax, jax.numpy as jnp
from jax import lax
import numpy as np

D_MODEL = 1024
BATCH = 2
SEQ = 8192
DEPTH = 1
DEC_BATCH = 32
DEC_SEQ = 64
PAST_LEN = 2048

CHUNK = 64
LEFT_CHUNKS = 8
WINDOW_ROWS = LEFT_CHUNKS * CHUNK
BAND = (LEFT_CHUNKS + 1) * CHUNK
D_MIX = D_MODEL
W_A = D_MIX // 2
H_A = 8
DH_A = W_A // H_A
W_B = D_MIX - W_A
H_B = 4
DK_B = W_B // H_B
DV_B = W_B // H_B
REL_CLIP = 128
GLA_BLOCK = 16
EPS = 1e-6
ATTN_SCALE = DH_A ** -0.5
NEG_INF = -1e30
N_IN = 4 * W_A + 4 * W_B
SPLITS = [W_A, 2 * W_A, 3 * W_A, 4 * W_A, 4 * W_A + W_B, 4 * W_A + 2 * W_B, 4 * W_A + 3 * W_B]

kernel_name = 'hymba_chunkattn_hgrn2_stream_step'


def rms_norm(x, g):
    xf = x.astype(jnp.float32)
    y = xf * lax.rsqrt(jnp.mean(xf * xf, axis=-1, keepdims=True) + EPS)
    return (y * g.astype(jnp.float32)).astype(x.dtype)


def band_attention(q, k, v, rel, valid, rel_bias):
    idx = jnp.clip(rel, -REL_CLIP, REL_CLIP) + REL_CLIP
    bias = rel_bias[:, idx].astype(jnp.float32)
    s = jnp.einsum('bnqhd,bnkhd->bnhqk', q, k).astype(jnp.float32) * ATTN_SCALE + bias
    s = jnp.where(valid[None, :, None], s, NEG_INF)
    p = jax.nn.softmax(s, axis=-1).astype(v.dtype)
    return jnp.einsum('bnhqk,bnkhd->bnqhd', p, v)


def prompt_attention(q, k, v, rel_bias):
    B, T, H, Dh = q.shape
    nc = T // CHUNK
    pad = ((0, 0), (WINDOW_ROWS, 0), (0, 0), (0, 0))
    kc = jnp.pad(k, pad).reshape(B, nc + LEFT_CHUNKS, CHUNK, H, Dh)
    vc = jnp.pad(v, pad).reshape(B, nc + LEFT_CHUNKS, CHUNK, H, Dh)
    kb = jnp.concatenate([kc[:, j:j + nc] for j in range(LEFT_CHUNKS + 1)], axis=2)
    vb = jnp.concatenate([vc[:, j:j + nc] for j in range(LEFT_CHUNKS + 1)], axis=2)
    qc = q.reshape(B, nc, CHUNK, H, Dh)
    band = jnp.arange(BAND)
    rel = jnp.arange(CHUNK)[:, None] + WINDOW_ROWS - band[None, :]
    key_chunk = jnp.arange(nc)[:, None] - LEFT_CHUNKS + (band // CHUNK)[None, :]
    valid = (key_chunk >= 0)[:, None, :]
    return band_attention(qc, kb, vb, rel, valid, rel_bias).reshape(B, T, H, Dh)


def sample_attention(q, k, v, ck, cv, rel_bias):
    B, T, H, Dh = q.shape
    R = ck.shape[1]
    kk = jnp.concatenate([ck, k], axis=1)[:, None]
    vv = jnp.concatenate([cv, v], axis=1)[:, None]
    qpos = PAST_LEN + jnp.arange(T)
    kpos = jnp.concatenate([PAST_LEN - R + jnp.arange(R), qpos])
    rel = qpos[:, None] - kpos[None, :]
    qch = qpos // CHUNK
    kch = kpos // CHUNK
    valid = ((kch[None, :] >= qch[:, None] - LEFT_CHUNKS) & (kch[None, :] <= qch[:, None])
             & (kpos[None, :] >= 0))[None]
    return band_attention(q[:, None], kk, vv, rel, valid, rel_bias)[:, 0]


def gla_recurrence(q, k, v, logf, s0):
    B, T, H, DK = q.shape
    DV = v.shape[-1]
    f32 = jnp.float32
    q, k, v, logf = (a.astype(f32) for a in (q, k, v, logf))
    pad = (-T) % GLA_BLOCK
    if pad:
        pw = ((0, 0), (0, pad), (0, 0), (0, 0))
        q, k, v, logf = (jnp.pad(a, pw) for a in (q, k, v, logf))
    nb = (T + pad) // GLA_BLOCK

    def blocks(a):
        return a.reshape(B, nb, GLA_BLOCK, H, a.shape[-1]).swapaxes(0, 1)

    mask = jnp.tril(jnp.ones((GLA_BLOCK, GLA_BLOCK), dtype=bool))

    def step(S, inp):
        qb, kb, vb, gb = inp
        b = jnp.cumsum(gb, axis=1)
        qt = qb * jnp.exp(b)
        kt = kb * jnp.exp(-b)
        A = jnp.where(mask, jnp.einsum('bthk,bshk->bhts', qt, kt), 0.0)
        o = jnp.einsum('bthk,bhkv->bthv', qt, S) + jnp.einsum('bhts,bshv->bthv', A, vb)
        bl = b[:, -1]
        S = jnp.exp(bl)[..., None] * S + jnp.einsum('bshk,bshv->bhkv', kb * jnp.exp(bl[:, None] - b), vb)
        return S, o

    S, o = lax.scan(step, s0.astype(f32), (blocks(q), blocks(k), blocks(v), blocks(logf)))
    o = o.swapaxes(0, 1).reshape(B, nb * GLA_BLOCK, H, DV)[:, :T]
    return o, S.astype(s0.dtype)


def hgrn2_branch(hq, hf, hi, hg, lb, norm_g, s0):
    B, T, _ = hq.shape
    f = lb + (1.0 - lb) * jax.nn.sigmoid(hf.astype(jnp.float32))
    logf = jnp.log(f)
    kk = 1.0 - f
    q = jax.nn.silu(hq)
    heads = lambda a: a.reshape(B, T, H_B, a.shape[-1] // H_B)
    o, S = gla_recurrence(heads(q), heads(kk), heads(hi), heads(logf), s0)
    o = rms_norm(o.astype(hq.dtype), norm_g).reshape(B, T, W_B)
    return o * jax.nn.silu(hg), S


def mixer_layer(x, ln_g, w_in, lb, norm_g, w_out, attn_fn, s0):
    B, T, _ = x.shape
    aq, ak, av, ag, hq, hf, hi, hg = jnp.split(rms_norm(x, ln_g) @ w_in, SPLITS, axis=-1)
    heads = lambda a: a.reshape(B, T, H_A, DH_A)
    k = heads(ak)
    v = heads(av)
    o_a = attn_fn(heads(aq), k, v).reshape(B, T, W_A) * jax.nn.silu(ag)
    o_b, S = hgrn2_branch(hq, hf, hi, hg, lb, norm_g, s0)
    y = x + jnp.concatenate([o_a, o_b], axis=-1) @ w_out
    return y, k, v, S


def setup_inputs(seed: int = 0) -> dict:
    key = jax.random.key(seed)
    ks = jax.random.split(key, 12)
    f32 = jnp.float32
    R = min(WINDOW_ROWS, PAST_LEN)
    return {
        'x_prompt': jax.random.normal(ks[0], (BATCH, SEQ, D_MODEL), f32),
        'x_sample': jax.random.normal(ks[1], (DEC_BATCH, DEC_SEQ, D_MODEL), f32),
        'cache_attn_k': jax.random.normal(ks[2], (DEPTH, DEC_BATCH, R, H_A, DH_A), f32),
        'cache_attn_v': jax.random.normal(ks[3], (DEPTH, DEC_BATCH, R, H_A, DH_A), f32),
        'state_hgrn': 0.5 * jax.random.normal(ks[4], (DEPTH, DEC_BATCH, H_B, DK_B, DV_B), f32),
        'ln_in_g': 1.0 + 0.02 * jax.random.normal(ks[5], (DEPTH, D_MODEL), f32),
        'w_in': jax.random.normal(ks[6], (DEPTH, D_MODEL, N_IN), f32) * D_MODEL ** -0.5,
        'rel_bias': 0.1 * jax.random.normal(ks[7], (DEPTH, H_A, 2 * REL_CLIP + 1), f32),
        'lb_gamma': 0.1 * jax.random.normal(ks[8], (DEPTH + 1, W_B), f32),
        'hg_norm_g': 1.0 + 0.02 * jax.random.normal(ks[9], (DEPTH, H_B, DV_B), f32),
        'w_out': jax.random.normal(ks[10], (DEPTH, D_MIX, D_MODEL), f32) * D_MIX ** -0.5,
        'ln_f_g': 1.0 + 0.02 * jax.random.normal(ks[11], (D_MODEL,), f32),
    }


def reference(x_prompt, x_sample, cache_attn_k, cache_attn_v, state_hgrn, ln_in_g, w_in,
              rel_bias, lb_gamma, hg_norm_g, w_out, ln_f_g):
    lb_all = jnp.cumsum(jax.nn.softmax(lb_gamma.astype(jnp.float32), axis=0), axis=0)
    hp, hs = x_prompt, x_sample
    B, T, _ = x_prompt.shape
    prompt_rows = min(WINDOW_ROWS, T)
    s0_prompt = jnp.zeros((B, H_B, DK_B, DV_B), x_prompt.dtype)
    kp_l, vp_l, sp_l, ks_l, vs_l, ss_l = [], [], [], [], [], []
    for l in range(DEPTH):
        attn_p = functools.partial(prompt_attention, rel_bias=rel_bias[l])
        attn_s = functools.partial(sample_attention, ck=cache_attn_k[l], cv=cache_attn_v[l],
                                   rel_bias=rel_bias[l])
        hp, kp, vp, sp = mixer_layer(hp, ln_in_g[l], w_in[l], lb_all[l], hg_norm_g[l], w_out[l],
                                     attn_p, s0_prompt)
        hs, kd, vd, sd = mixer_layer(hs, ln_in_g[l], w_in[l], lb_all[l], hg_norm_g[l], w_out[l],
                                     attn_s, state_hgrn[l])
        kp_l.append(kp[:, T - prompt_rows:])
        vp_l.append(vp[:, T - prompt_rows:])
        sp_l.append(sp)
        ks_l.append(kd)
        vs_l.append(vd)
        ss_l.append(sd)
    y_prompt = rms_norm(hp, ln_f_g)
    y_sample = rms_norm(hs, ln_f_g)
    new_k_prompt = jnp.stack(kp_l)
    new_v_prompt = jnp.stack(vp_l)
    new_hgrn_prompt = jnp.stack(sp_l)
    new_k_sample = jnp.stack(ks_l)
    new_v_sample = jnp.stack(vs_l)
    new_hgrn_sample = jnp.stack(ss_l)
    return (y_prompt, y_sample, new_k_prompt, new_v_prompt, new_hgrn_prompt,
            new_k_sample, new_v_sample, new_hgrn_sample)
```

```python
import functools

import jax
import jax.numpy as jnp
from jax import lax
from jax.experimental import pallas as pl
from jax.experimental.pallas import tpu as pltpu

F32 = jnp.float32
BF16 = jnp.bfloat16

D_MODEL = 1024
CHUNK = 64
LEFT_CHUNKS = 8
WINDOW_ROWS = LEFT_CHUNKS * CHUNK
BAND = WINDOW_ROWS + CHUNK
W_A = 512
H_A = 8
DH_A = 64
W_B = 512
H_B = 4
DK_B = 128
DV_B = 128
REL_CLIP = 128
GLA_BLOCK = 16
EPS = 1e-6
ATTN_SCALE = DH_A ** -0.5
NEG_INF = -1e30
N_IN = 4 * W_A + 4 * W_B
PAST_LEN = 2048

LANES = 128
HEAD_PAIRS = H_A // 2
REL_TABLE = 2 * REL_CLIP + 1
REL_TABLE_PAD = 384
TOEPLITZ_LEN = 640
VMEM_LIMIT = 56 * 1024 * 1024


def _params(semantics):
    return pltpu.CompilerParams(dimension_semantics=semantics,
                                vmem_limit_bytes=VMEM_LIMIT)


def _silu(x):
    return x * (1.0 / (1.0 + jnp.exp(-x)))


def _sigmoid(x):
    return 1.0 / (1.0 + jnp.exp(-x))


INPROJ_ROWS = 512
INPROJ_COLS = 512


def _inproj_kernel(x_ref, g_ref, w_ref, o_ref):
    x = x_ref[...]
    ms = jnp.mean(x * x, axis=-1, keepdims=True)
    xn = (x * lax.rsqrt(ms + EPS) * g_ref[...]).astype(BF16)
    for n0 in range(0, N_IN, INPROJ_COLS):
        o_ref[:, n0:n0 + INPROJ_COLS] = jnp.dot(
            xn, w_ref[:, n0:n0 + INPROJ_COLS], preferred_element_type=F32)


def _inproj(x2d, g, w_bf16, name):
    rows = x2d.shape[0]
    tm = min(INPROJ_ROWS, rows)
    return pl.pallas_call(
        _inproj_kernel,
        out_shape=jax.ShapeDtypeStruct((rows, N_IN), F32),
        grid=(rows // tm,),
        in_specs=[
            pl.BlockSpec((tm, D_MODEL), lambda i: (i, 0)),
            pl.BlockSpec((1, D_MODEL), lambda i: (0, 0)),
            pl.BlockSpec((D_MODEL, N_IN), lambda i: (0, 0)),
        ],
        out_specs=pl.BlockSpec((tm, N_IN), lambda i: (i, 0)),
        compiler_params=_params(("arbitrary",)),
        name=name,
    )(x2d, g, w_bf16)


def _bias_kernel(rb_ref, o_ref):
    m = lax.broadcasted_iota(jnp.int32, (REL_TABLE_PAD, TOEPLITZ_LEN), 1)
    t = lax.broadcasted_iota(jnp.int32, (REL_TABLE_PAD, TOEPLITZ_LEN), 0)
    idx = jnp.clip(BAND - 1 - m, -REL_CLIP, REL_CLIP) + REL_CLIP
    onehot = jnp.where(idx == t, 1.0, 0.0).astype(BF16)
    rb = rb_ref[...]
    hi = rb.astype(BF16)
    r1 = rb - hi.astype(F32)
    mid = r1.astype(BF16)
    lo = (r1 - mid.astype(F32)).astype(BF16)
    toep = (jnp.dot(hi, onehot, preferred_element_type=F32)
            + jnp.dot(mid, onehot, preferred_element_type=F32)
            + jnp.dot(lo, onehot, preferred_element_type=F32))
    for h in range(H_A):
        rows = jnp.broadcast_to(toep[h:h + 1, :], (CHUNK, TOEPLITZ_LEN))
        rolled = pltpu.roll(rows, TOEPLITZ_LEN - (CHUNK - 1), 1, stride=1, stride_axis=0)
        o_ref[h] = rolled[:, :BAND]


def _bias_table(rel_bias_l):
    rb = jnp.pad(rel_bias_l, ((0, 0), (0, REL_TABLE_PAD - REL_TABLE)))
    return pl.pallas_call(
        _bias_kernel,
        out_shape=jax.ShapeDtypeStruct((H_A, CHUNK, BAND), F32),
        name="rel_bias_table",
    )(rb)


def _attn_chunk(q, ag, kband, vband, bias2, valid_from):
    lane = lax.broadcasted_iota(jnp.int32, (CHUNK, LANES), 1)
    first = lane < DH_A
    qs = q * ATTN_SCALE
    qbd = jnp.concatenate([jnp.where(first, qs, 0.0), jnp.where(first, 0.0, qs)],
                          axis=0).astype(BF16)
    s = lax.dot_general(qbd, kband, (((1,), (1,)), ((), ())),
                        preferred_element_type=F32) + bias2
    if valid_from is not None:
        col = lax.broadcasted_iota(jnp.int32, (2 * CHUNK, BAND), 1)
        s = jnp.where(col >= valid_from, s, NEG_INF)
    mx = jnp.max(s, axis=-1, keepdims=True)
    e = jnp.exp(s - mx)
    denom = jnp.sum(e, axis=-1, keepdims=True)
    pv = jnp.dot(e.astype(BF16), vband, preferred_element_type=F32)
    o2 = pv * (1.0 / denom)
    o = jnp.where(first, o2[:CHUNK], o2[CHUNK:])
    return o * _silu(ag)


ATTN_ROWS = 256


def _attn_prompt_kernel(q_ref, k_ref, v_ref, ag_ref, bias_ref, o_ref, kh_ref, vh_ref):
    i = pl.program_id(1)

    @pl.when(i == 0)
    def _():
        kh_ref[0:WINDOW_ROWS, :] = jnp.zeros((WINDOW_ROWS, W_A), BF16)
        vh_ref[0:WINDOW_ROWS, :] = jnp.zeros((WINDOW_ROWS, W_A), BF16)

    kh_ref[WINDOW_ROWS:, :] = k_ref[...].astype(BF16)
    vh_ref[WINDOW_ROWS:, :] = v_ref[...].astype(BF16)
    chunks = ATTN_ROWS // CHUNK
    for cc in range(chunks):
        r0 = cc * CHUNK
        valid_from = (LEFT_CHUNKS - (i * chunks + cc)) * CHUNK
        for p in range(HEAD_PAIRS):
            c0 = p * LANES
            o = _attn_chunk(q_ref[r0:r0 + CHUNK, c0:c0 + LANES],
                            ag_ref[r0:r0 + CHUNK, c0:c0 + LANES],
                            kh_ref[r0:r0 + BAND, c0:c0 + LANES],
                            vh_ref[r0:r0 + BAND, c0:c0 + LANES],
                            bias_ref[p], valid_from)
            o_ref[r0:r0 + CHUNK, c0:c0 + LANES] = o.astype(o_ref.dtype)
    kh_ref[0:WINDOW_ROWS, :] = kh_ref[ATTN_ROWS:ATTN_ROWS + WINDOW_ROWS, :]
    vh_ref[0:WINDOW_ROWS, :] = vh_ref[ATTN_ROWS:ATTN_ROWS + WINDOW_ROWS, :]


def _attn_prompt(p2d, bias2, batch, seq):
    nt = seq // ATTN_ROWS

    def col(c):
        return pl.BlockSpec((ATTN_ROWS, W_A), lambda b, i: (b * nt + i, c))

    return pl.pallas_call(
        _attn_prompt_kernel,
        out_shape=jax.ShapeDtypeStruct((batch * seq, W_A), BF16),
        grid=(batch, nt),
        in_specs=[col(0), col(1), col(2), col(3),
                  pl.BlockSpec((HEAD_PAIRS, 2 * CHUNK, BAND), lambda b, i: (0, 0, 0))],
        out_specs=pl.BlockSpec((ATTN_ROWS, W_A), lambda b, i: (b * nt + i, 0)),
        scratch_shapes=[pltpu.VMEM((WINDOW_ROWS + ATTN_ROWS, W_A), BF16),
                        pltpu.VMEM((WINDOW_ROWS + ATTN_ROWS, W_A), BF16)],
        compiler_params=_params(("arbitrary", "arbitrary")),
        name="attn_prompt",
    )(p2d, p2d, p2d, p2d, bias2)


def _attn_sample_kernel(q_ref, k_ref, v_ref, ag_ref, ck_ref, cv_ref, bias_ref, o_ref,
                        kh_ref, vh_ref):
    kh_ref[0:WINDOW_ROWS, :] = ck_ref[0].astype(BF16)
    vh_ref[0:WINDOW_ROWS, :] = cv_ref[0].astype(BF16)
    kh_ref[WINDOW_ROWS:, :] = k_ref[...].astype(BF16)
    vh_ref[WINDOW_ROWS:, :] = v_ref[...].astype(BF16)
    for p in range(HEAD_PAIRS):
        c0 = p * LANES
        o = _attn_chunk(q_ref[:, c0:c0 + LANES], ag_ref[:, c0:c0 + LANES],
                        kh_ref[:, c0:c0 + LANES], vh_ref[:, c0:c0 + LANES],
                        bias_ref[p], None)
        o_ref[:, c0:c0 + LANES] = o.astype(o_ref.dtype)


def _attn_sample(p2d, cache_k, cache_v, bias2, batch):
    def col(c):
        return pl.BlockSpec((CHUNK, W_A), lambda b: (b, c))

    cache = pl.BlockSpec((1, WINDOW_ROWS, W_A), lambda b: (b, 0, 0))
    return pl.pallas_call(
        _attn_sample_kernel,
        out_shape=jax.ShapeDtypeStruct((batch * CHUNK, W_A), BF16),
        grid=(batch,),
        in_specs=[col(0), col(1), col(2), col(3), cache, cache,
                  pl.BlockSpec((HEAD_PAIRS, 2 * CHUNK, BAND), lambda b: (0, 0, 0))],
        out_specs=pl.BlockSpec((CHUNK, W_A), lambda b: (b, 0)),
        scratch_shapes=[pltpu.VMEM((BAND, W_A), BF16), pltpu.VMEM((BAND, W_A), BF16)],
        compiler_params=_params(("arbitrary",)),
        name="attn_sample",
    )(p2d, p2d, p2d, p2d, cache_k, cache_v, bias2)


def _block_cumsum(x):
    row = lax.broadcasted_iota(jnp.int32, x.shape, 0) & (GLA_BLOCK - 1)
    s = 1
    while s < GLA_BLOCK:
        x = x + jnp.where(row >= s, pltpu.roll(x, s, 0), 0.0)
        s *= 2
    return x


def _hgrn_kernel(hq_ref, hf_ref, hi_ref, hg_ref, lb_ref, ng_ref, s0_ref,
                 o_ref, sout_ref, st_ref, qt_ref, kt_ref, kd_ref, v_ref, dec_ref):
    i = pl.program_id(1)
    rows = hq_ref.shape[0]
    nb = rows // GLA_BLOCK

    @pl.when(i == 0)
    def _():
        for h in range(H_B):
            st_ref[h] = s0_ref[0, h].T

    lb = lb_ref[...]
    f = lb + (1.0 - lb) * _sigmoid(hf_ref[...])
    b = _block_cumsum(jnp.log(f))
    b3 = b.reshape(nb, GLA_BLOCK, W_B)
    bl = jnp.broadcast_to(b3[:, GLA_BLOCK - 1:GLA_BLOCK, :],
                          (nb, GLA_BLOCK, W_B)).reshape(rows, W_B)
    kk = 1.0 - f
    qt_ref[...] = (_silu(hq_ref[...]) * jnp.exp(b)).astype(BF16)
    kt_ref[...] = (kk * jnp.exp(-b)).astype(BF16)
    kd_ref[...] = (kk * jnp.exp(bl - b)).astype(BF16)
    v_ref[...] = hi_ref[...].astype(BF16)
    dec_ref[...] = jnp.exp(bl)

    tri = (lax.broadcasted_iota(jnp.int32, (GLA_BLOCK, GLA_BLOCK), 0)
           >= lax.broadcasted_iota(jnp.int32, (GLA_BLOCK, GLA_BLOCK), 1))

    def block(j, carry):
        r0 = pl.multiple_of(j * GLA_BLOCK, GLA_BLOCK)
        rs = pl.ds(r0, GLA_BLOCK)
        for h in range(H_B):
            cs = slice(h * DK_B, (h + 1) * DK_B)
            qt = qt_ref[rs, cs]
            kt = kt_ref[rs, cs]
            kd = kd_ref[rs, cs]
            v = v_ref[rs, cs]
            st = st_ref[h]
            a = lax.dot_general(qt, kt, (((1,), (1,)), ((), ())),
                                preferred_element_type=F32)
            a = jnp.where(tri, a, 0.0).astype(BF16)
            o = (lax.dot_general(qt, st.astype(BF16), (((1,), (1,)), ((), ())),
                                 preferred_element_type=F32)
                 + jnp.dot(a, v, preferred_element_type=F32))
            ut = lax.dot_general(v, kd, (((0,), (0,)), ((), ())),
                                 preferred_element_type=F32)
            dec = dec_ref[pl.ds(r0, 8), cs]
            st3 = st.reshape(DV_B // 8, 8, DK_B) * dec[None] + ut.reshape(DV_B // 8, 8, DK_B)
            st_ref[h] = st3.reshape(DV_B, DK_B)
            y = o * lax.rsqrt(jnp.mean(o * o, axis=-1, keepdims=True) + EPS) * ng_ref[:, cs]
            o_ref[rs, cs] = (y * _silu(hg_ref[rs, cs])).astype(o_ref.dtype)
        return carry

    lax.fori_loop(0, nb, block, 0)

    @pl.when(i == pl.num_programs(1) - 1)
    def _():
        for h in range(H_B):
            sout_ref[0, h] = st_ref[h].T


def _hgrn(p2d, lb, ng, s0, batch, seq, rows, name):
    nt = seq // rows

    def col(c):
        return pl.BlockSpec((rows, W_B), lambda b, i: (b * nt + i, c))

    vec = pl.BlockSpec((1, W_B), lambda b, i: (0, 0))
    state = pl.BlockSpec((1, H_B, DK_B, DV_B), lambda b, i: (b, 0, 0, 0))
    return pl.pallas_call(
        _hgrn_kernel,
        out_shape=(jax.ShapeDtypeStruct((batch * seq, W_B), BF16),
                   jax.ShapeDtypeStruct((batch, H_B, DK_B, DV_B), F32)),
        grid=(batch, nt),
        in_specs=[col(4), col(5), col(6), col(7), vec, vec, state],
        out_specs=(pl.BlockSpec((rows, W_B), lambda b, i: (b * nt + i, 0)), state),
        scratch_shapes=[pltpu.VMEM((H_B, DV_B, DK_B), F32),
                        pltpu.VMEM((rows, W_B), BF16),
                        pltpu.VMEM((rows, W_B), BF16),
                        pltpu.VMEM((rows, W_B), BF16),
                        pltpu.VMEM((rows, W_B), BF16),
                        pltpu.VMEM((rows, W_B), F32)],
        compiler_params=_params(("arbitrary", "arbitrary")),
        name=name,
    )(p2d, p2d, p2d, p2d, lb, ng, s0)


OUTPROJ_ROWS = 512


def _outproj_kernel(oa_ref, ob_ref, x_ref, w_ref, g_ref, y_ref):
    h = (x_ref[...]
         + jnp.dot(oa_ref[...], w_ref[0:W_A, :], preferred_element_type=F32)
         + jnp.dot(ob_ref[...], w_ref[W_A:, :], preferred_element_type=F32))
    ms = jnp.mean(h * h, axis=-1, keepdims=True)
    y_ref[...] = h * lax.rsqrt(ms + EPS) * g_ref[...]


def _outproj(oa, ob, x2d, w_bf16, g, name):
    rows = x2d.shape[0]
    tm = min(OUTPROJ_ROWS, rows)
    return pl.pallas_call(
        _outproj_kernel,
        out_shape=jax.ShapeDtypeStruct((rows, D_MODEL), F32),
        grid=(rows // tm,),
        in_specs=[
            pl.BlockSpec((tm, W_A), lambda i: (i, 0)),
            pl.BlockSpec((tm, W_B), lambda i: (i, 0)),
            pl.BlockSpec((tm, D_MODEL), lambda i: (i, 0)),
            pl.BlockSpec((W_A + W_B, D_MODEL), lambda i: (0, 0)),
            pl.BlockSpec((1, D_MODEL), lambda i: (0, 0)),
        ],
        out_specs=pl.BlockSpec((tm, D_MODEL), lambda i: (i, 0)),
        compiler_params=_params(("arbitrary",)),
        name=name,
    )(oa, ob, x2d, w_bf16, g)


HGRN_PROMPT_ROWS = 256


def kernel(x_prompt, x_sample, cache_attn_k, cache_attn_v, state_hgrn, ln_in_g, w_in,
           rel_bias, lb_gamma, hg_norm_g, w_out, ln_f_g):
    batch, seq, _ = x_prompt.shape
    dec_batch, dec_seq, _ = x_sample.shape
    depth = w_in.shape[0]
    assert depth == 1 and dec_seq == CHUNK and PAST_LEN % CHUNK == 0
    assert cache_attn_k.shape[2] == WINDOW_ROWS

    lb_all = jnp.cumsum(jax.nn.softmax(lb_gamma.astype(F32), axis=0), axis=0)
    lb = lb_all[0].reshape(1, W_B)
    ng = hg_norm_g[0].reshape(1, W_B)
    g_in = ln_in_g[0].reshape(1, D_MODEL)
    g_f = ln_f_g.reshape(1, D_MODEL)
    w_in_b = w_in[0].astype(BF16)
    w_out_b = w_out[0].astype(BF16)

    xp = x_prompt.reshape(batch * seq, D_MODEL)
    xs = x_sample.reshape(dec_batch * dec_seq, D_MODEL)

    bias2 = _bias_table(rel_bias[0]).reshape(HEAD_PAIRS, 2 * CHUNK, BAND)

    pp = _inproj(xp, g_in, w_in_b, "inproj_prompt")
    ps = _inproj(xs, g_in, w_in_b, "inproj_sample")

    oa_p = _attn_prompt(pp, bias2, batch, seq)
    ck = cache_attn_k[0].reshape(dec_batch, WINDOW_ROWS, W_A)
    cv = cache_attn_v[0].reshape(dec_batch, WINDOW_ROWS, W_A)
    oa_s = _attn_sample(ps, ck, cv, bias2, dec_batch)

    s0_p = jnp.zeros((batch, H_B, DK_B, DV_B), F32)
    ob_p, st_p = _hgrn(pp, lb, ng, s0_p, batch, seq, HGRN_PROMPT_ROWS, "hgrn_prompt")
    ob_s, st_s = _hgrn(ps, lb, ng, state_hgrn[0], dec_batch, dec_seq, dec_seq, "hgrn_sample")

    y_p = _outproj(oa_p, ob_p, xp, w_out_b, g_f, "outproj_prompt").reshape(batch, seq, D_MODEL)
    y_s = _outproj(oa_s, ob_s, xs, w_out_b, g_f, "outproj_sample").reshape(
        dec_batch, dec_seq, D_MODEL)

    rows_p = min(WINDOW_ROWS, seq)
    pp3 = pp.reshape(batch, seq, N_IN)
    k_p = pp3[:, seq - rows_p:, W_A:2 * W_A].reshape(1, batch, rows_p, H_A, DH_A)
    v_p = pp3[:, seq - rows_p:, 2 * W_A:3 * W_A].reshape(1, batch, rows_p, H_A, DH_A)
    ps3 = ps.reshape(dec_batch, dec_seq, N_IN)
    k_s = ps3[:, :, W_A:2 * W_A].reshape(1, dec_batch, dec_seq, H_A, DH_A)
    v_s = ps3[:, :, 2 * W_A:3 * W_A].reshape(1, dec_batch, dec_seq, H_A, DH_A)

    return (y_p, y_s, k_p, v_p, st_p[None], k_s, v_s, st_s[None])
```

```python
import functools

import jax
import jax.numpy as jnp
from jax import lax
from jax.experimental import pallas as pl
from jax.experimental.pallas import tpu as pltpu

F32 = jnp.float32
BF16 = jnp.bfloat16

D_MODEL = 1024
CHUNK = 64
LEFT_CHUNKS = 8
WINDOW_ROWS = LEFT_CHUNKS * CHUNK
BAND = WINDOW_ROWS + CHUNK
W_A = 512
H_A = 8
DH_A = 64
W_B = 512
H_B = 4
DK_B = 128
DV_B = 128
REL_CLIP = 128
GLA_BLOCK = 16
EPS = 1e-6
ATTN_SCALE = DH_A ** -0.5
NEG_INF = -1e30
N_IN = 4 * W_A + 4 * W_B
PAST_LEN = 2048

LANES = 128
HEAD_PAIRS = H_A // 2
REL_TABLE = 2 * REL_CLIP + 1
REL_TABLE_PAD = 384
TOEPLITZ_LEN = 640
VMEM_LIMIT = 56 * 1024 * 1024


def _params(semantics):
    return pltpu.CompilerParams(dimension_semantics=semantics,
                                vmem_limit_bytes=VMEM_LIMIT)


def _silu(x):
    return x * (1.0 / (1.0 + jnp.exp(-x)))


def _sigmoid(x):
    return 1.0 / (1.0 + jnp.exp(-x))


INPROJ_ROWS = 512
INPROJ_COLS = 512


def _inproj_kernel(x_ref, g_ref, w_ref, o_ref):
    x = x_ref[...]
    ms = jnp.mean(x * x, axis=-1, keepdims=True)
    xn = (x * lax.rsqrt(ms + EPS) * g_ref[...]).astype(BF16)
    for n0 in range(0, N_IN, INPROJ_COLS):
        o_ref[:, n0:n0 + INPROJ_COLS] = jnp.dot(
            xn, w_ref[:, n0:n0 + INPROJ_COLS], preferred_element_type=F32)


def _inproj(x2d, g, w_bf16, name):
    rows = x2d.shape[0]
    tm = min(INPROJ_ROWS, rows)
    return pl.pallas_call(
        _inproj_kernel,
        out_shape=jax.ShapeDtypeStruct((rows, N_IN), F32),
        grid=(rows // tm,),
        in_specs=[
            pl.BlockSpec((tm, D_MODEL), lambda i: (i, 0)),
            pl.BlockSpec((1, D_MODEL), lambda i: (0, 0)),
            pl.BlockSpec((D_MODEL, N_IN), lambda i: (0, 0)),
        ],
        out_specs=pl.BlockSpec((tm, N_IN), lambda i: (i, 0)),
        compiler_params=_params(("arbitrary",)),
        name=name,
    )(x2d, g, w_bf16)


def _bias_kernel(rb_ref, o_ref):
    m = lax.broadcasted_iota(jnp.int32, (REL_TABLE_PAD, TOEPLITZ_LEN), 1)
    t = lax.broadcasted_iota(jnp.int32, (REL_TABLE_PAD, TOEPLITZ_LEN), 0)
    idx = jnp.clip(BAND - 1 - m, -REL_CLIP, REL_CLIP) + REL_CLIP
    onehot = jnp.where(idx == t, 1.0, 0.0).astype(BF16)
    rb = rb_ref[...]
    hi = rb.astype(BF16)
    r1 = rb - hi.astype(F32)
    mid = r1.astype(BF16)
    lo = (r1 - mid.astype(F32)).astype(BF16)
    toep = (jnp.dot(hi, onehot, preferred_element_type=F32)
            + jnp.dot(mid, onehot, preferred_element_type=F32)
            + jnp.dot(lo, onehot, preferred_element_type=F32))
    for h in range(H_A):
        rows = jnp.broadcast_to(toep[h:h + 1, :], (CHUNK, TOEPLITZ_LEN))
        rolled = pltpu.roll(rows, TOEPLITZ_LEN - (CHUNK - 1), 1, stride=1, stride_axis=0)
        o_ref[h] = rolled[:, :BAND]


def _bias_table(rel_bias_l):
    rb = jnp.pad(rel_bias_l, ((0, 0), (0, REL_TABLE_PAD - REL_TABLE)))
    return pl.pallas_call(
        _bias_kernel,
        out_shape=jax.ShapeDtypeStruct((H_A, CHUNK, BAND), F32),
        name="rel_bias_table",
    )(rb)


def _attn_chunk(q, ag, kband, vband, bias2, valid_from):
    lane = lax.broadcasted_iota(jnp.int32, (CHUNK, LANES), 1)
    first = lane < DH_A
    qs = q * ATTN_SCALE
    qbd = jnp.concatenate([jnp.where(first, qs, 0.0), jnp.where(first, 0.0, qs)],
                          axis=0).astype(BF16)
    s = lax.dot_general(qbd, kband, (((1,), (1,)), ((), ())),
                        preferred_element_type=F32) + bias2
    if valid_from is not None:
        col = lax.broadcasted_iota(jnp.int32, (2 * CHUNK, BAND), 1)
        s = jnp.where(col >= valid_from, s, NEG_INF)
    mx = jnp.max(s, axis=-1, keepdims=True)
    e = jnp.exp(s - mx)
    denom = jnp.sum(e, axis=-1, keepdims=True)
    pv = jnp.dot(e.astype(BF16), vband, preferred_element_type=F32)
    o2 = pv * (1.0 / denom)
    o = jnp.where(first, o2[:CHUNK], o2[CHUNK:])
    return o * _silu(ag)


ATTN_ROWS = 256


def _attn_prompt_kernel(q_ref, k_ref, v_ref, ag_ref, bias_ref, o_ref, kh_ref, vh_ref):
    i = pl.program_id(1)

    @pl.when(i == 0)
    def _():
        kh_ref[0:WINDOW_ROWS, :] = jnp.zeros((WINDOW_ROWS, W_A), BF16)
        vh_ref[0:WINDOW_ROWS, :] = jnp.zeros((WINDOW_ROWS, W_A), BF16)

    kh_ref[WINDOW_ROWS:, :] = k_ref[...].astype(BF16)
    vh_ref[WINDOW_ROWS:, :] = v_ref[...].astype(BF16)
    chunks = ATTN_ROWS // CHUNK
    for cc in range(chunks):
        r0 = cc * CHUNK
        valid_from = (LEFT_CHUNKS - (i * chunks + cc)) * CHUNK
        for p in range(HEAD_PAIRS):
            c0 = p * LANES
            o = _attn_chunk(q_ref[r0:r0 + CHUNK, c0:c0 + LANES],
                            ag_ref[r0:r0 + CHUNK, c0:c0 + LANES],
                            kh_ref[r0:r0 + BAND, c0:c0 + LANES],
                            vh_ref[r0:r0 + BAND, c0:c0 + LANES],
                            bias_ref[p], valid_from)
            o_ref[r0:r0 + CHUNK, c0:c0 + LANES] = o.astype(o_ref.dtype)
    kh_ref[0:WINDOW_ROWS, :] = kh_ref[ATTN_ROWS:ATTN_ROWS + WINDOW_ROWS, :]
    vh_ref[0:WINDOW_ROWS, :] = vh_ref[ATTN_ROWS:ATTN_ROWS + WINDOW_ROWS, :]


def _attn_prompt(p2d, bias2, batch, seq):
    nt = seq // ATTN_ROWS

    def col(c):
        return pl.BlockSpec((ATTN_ROWS, W_A), lambda b, i: (b * nt + i, c))

    return pl.pallas_call(
        _attn_prompt_kernel,
        out_shape=jax.ShapeDtypeStruct((batch * seq, W_A), BF16),
        grid=(batch, nt),
        in_specs=[col(0), col(1), col(2), col(3),
                  pl.BlockSpec((HEAD_PAIRS, 2 * CHUNK, BAND), lambda b, i: (0, 0, 0))],
        out_specs=pl.BlockSpec((ATTN_ROWS, W_A), lambda b, i: (b * nt + i, 0)),
        scratch_shapes=[pltpu.VMEM((WINDOW_ROWS + ATTN_ROWS, W_A), BF16),
                        pltpu.VMEM((WINDOW_ROWS + ATTN_ROWS, W_A), BF16)],
        compiler_params=_params(("arbitrary", "arbitrary")),
        name="attn_prompt",
    )(p2d, p2d, p2d, p2d, bias2)


def _attn_sample_kernel(q_ref, k_ref, v_ref, ag_ref, ck_ref, cv_ref, bias_ref, o_ref,
                        kh_ref, vh_ref):
    kh_ref[0:WINDOW_ROWS, :] = ck_ref[0].astype(BF16)
    vh_ref[0:WINDOW_ROWS, :] = cv_ref[0].astype(BF16)
    kh_ref[WINDOW_ROWS:, :] = k_ref[...].astype(BF16)
    vh_ref[WINDOW_ROWS:, :] = v_ref[...].astype(BF16)
    for p in range(HEAD_PAIRS):
        c0 = p * LANES
        o = _attn_chunk(q_ref[:, c0:c0 + LANES], ag_ref[:, c0:c0 + LANES],
                        kh_ref[:, c0:c0 + LANES], vh_ref[:, c0:c0 + LANES],
                        bias_ref[p], None)
        o_ref[:, c0:c0 + LANES] = o.astype(o_ref.dtype)


def _attn_sample(p2d, cache_k, cache_v, bias2, batch):
    def col(c):
        return pl.BlockSpec((CHUNK, W_A), lambda b: (b, c))

    cache = pl.BlockSpec((1, WINDOW_ROWS, W_A), lambda b: (b, 0, 0))
    return pl.pallas_call(
        _attn_sample_kernel,
        out_shape=jax.ShapeDtypeStruct((batch * CHUNK, W_A), BF16),
        grid=(batch,),
        in_specs=[col(0), col(1), col(2), col(3), cache, cache,
                  pl.BlockSpec((HEAD_PAIRS, 2 * CHUNK, BAND), lambda b: (0, 0, 0))],
        out_specs=pl.BlockSpec((CHUNK, W_A), lambda b: (b, 0)),
        scratch_shapes=[pltpu.VMEM((BAND, W_A), BF16), pltpu.VMEM((BAND, W_A), BF16)],
        compiler_params=_params(("arbitrary",)),
        name="attn_sample",
    )(p2d, p2d, p2d, p2d, cache_k, cache_v, bias2)


HGRN_GROUP_BLOCKS = 4
HGRN_GROUP = HGRN_GROUP_BLOCKS * GLA_BLOCK


def _block_cumsum(x):
    row = lax.broadcasted_iota(jnp.int32, x.shape, 0) & (GLA_BLOCK - 1)
    s = 1
    while s < GLA_BLOCK:
        x = x + jnp.where(row >= s, pltpu.roll(x, s, 0), 0.0)
        s *= 2
    return x


def _nt(a, b):
    return lax.dot_general(a, b, (((1,), (1,)), ((), ())), preferred_element_type=F32)


def _hgrn_kernel(hq_ref, hf_ref, hi_ref, hg_ref, lb_ref, ng_ref, s0_ref,
                 o_ref, sout_ref, st_ref, qt_ref, kt_ref, kd_ref, qs_ref, q1_ref,
                 k1_ref, kdp_ref, v_ref, dec_ref):
    i = pl.program_id(1)
    nseq, seq_rows = hq_ref.shape[0], hq_ref.shape[1]
    rows = nseq * seq_rows
    nb = rows // GLA_BLOCK

    @pl.when(i == 0)
    def _():
        for s in range(nseq):
            for h in range(H_B):
                st_ref[s, h] = s0_ref[s, h].T

    def flat(ref):
        return ref[...].reshape(rows, W_B)

    lb = lb_ref[...]
    f = lb + (1.0 - lb) * _sigmoid(flat(hf_ref))
    b = _block_cumsum(jnp.log(f))
    b3 = b.reshape(nb, GLA_BLOCK, W_B)
    bl = jnp.broadcast_to(b3[:, GLA_BLOCK - 1:GLA_BLOCK, :],
                          (nb, GLA_BLOCK, W_B)).reshape(rows, W_B)
    blk = (lax.broadcasted_iota(jnp.int32, (rows, W_B), 0) // GLA_BLOCK) & (HGRN_GROUP_BLOCKS - 1)
    prev = [pltpu.roll(bl, d * GLA_BLOCK, 0) for d in range(1, HGRN_GROUP_BLOCKS)]
    nxt = [pltpu.roll(bl, rows - d * GLA_BLOCK, 0) for d in range(1, HGRN_GROUP_BLOCKS)]
    c_in = sum(jnp.where(blk >= d, prev[d - 1], 0.0) for d in range(1, HGRN_GROUP_BLOCKS))
    c_out = sum(jnp.where(blk <= HGRN_GROUP_BLOCKS - 1 - d, nxt[d - 1], 0.0)
                for d in range(1, HGRN_GROUP_BLOCKS))
    kk = 1.0 - f
    qt = _silu(flat(hq_ref)) * jnp.exp(b)
    kd = kk * jnp.exp(bl - b)
    qt_ref[...] = qt.astype(BF16)
    kt_ref[...] = (kk * jnp.exp(-b)).astype(BF16)
    kd_ref[...] = kd.astype(BF16)
    qs_ref[...] = (qt * jnp.exp(c_in)).astype(BF16)
    kdp_ref[...] = (kd * jnp.exp(c_out)).astype(BF16)
    q1_ref[...] = (qt * jnp.exp(jnp.where(blk == 3, prev[0], 0.0))).astype(BF16)
    k1_ref[...] = (kd * jnp.exp(jnp.where(blk == 0, nxt[0], 0.0))).astype(BF16)
    v_ref[...] = flat(hi_ref).astype(BF16)
    dec_ref[...] = jnp.exp(c_in + bl + c_out)

    r_i = lax.broadcasted_iota(jnp.int32, (HGRN_GROUP, HGRN_GROUP), 0)
    c_i = lax.broadcasted_iota(jnp.int32, (HGRN_GROUP, HGRN_GROUP), 1)
    rb, cb = r_i // GLA_BLOCK, c_i // GLA_BLOCK
    m_diag = (rb == cb) & (r_i >= c_i)
    m_adj = rb == cb + 1
    m_far = rb >= cb + 2

    for s in range(nseq):
        st = [st_ref[s, h] for h in range(H_B)]
        for g in range(seq_rows // HGRN_GROUP):
            r0 = s * seq_rows + g * HGRN_GROUP
            rs = slice(r0, r0 + HGRN_GROUP)
            for h in range(H_B):
                cs = slice(h * DK_B, (h + 1) * DK_B)
                qt_g = qt_ref[rs, cs]
                kd_g = kd_ref[rs, cs]
                v_g = v_ref[rs, cs]
                a = jnp.where(m_diag, _nt(qt_g, kt_ref[rs, cs]),
                              jnp.where(m_adj, _nt(qt_g, kd_g),
                                        jnp.where(m_far, _nt(q1_ref[rs, cs], k1_ref[rs, cs]), 0.0)))
                o = (_nt(qs_ref[rs, cs], st[h].astype(BF16))
                     + jnp.dot(a.astype(BF16), v_g, preferred_element_type=F32))
                ut = lax.dot_general(v_g, kdp_ref[rs, cs], (((0,), (0,)), ((), ())),
                                     preferred_element_type=F32)
                dec = dec_ref[r0:r0 + 8, cs]
                st[h] = (st[h].reshape(DV_B // 8, 8, DK_B) * dec[None]
                         + ut.reshape(DV_B // 8, 8, DK_B)).reshape(DV_B, DK_B)
                y = o * lax.rsqrt(jnp.mean(o * o, axis=-1, keepdims=True) + EPS) * ng_ref[:, cs]
                gs = slice(g * HGRN_GROUP, (g + 1) * HGRN_GROUP)
                o_ref[s, gs, cs] = (y * _silu(hg_ref[s, gs, cs])).astype(o_ref.dtype)
        for h in range(H_B):
            st_ref[s, h] = st[h]

    @pl.when(i == pl.num_programs(1) - 1)
    def _():
        for s in range(nseq):
            for h in range(H_B):
                sout_ref[s, h] = st_ref[s, h].T


def _hgrn(p3d, lb, ng, s0, nseq, rows, name):
    batch, seq, _ = p3d.shape
    assert batch % nseq == 0 and seq % rows == 0 and rows % HGRN_GROUP == 0

    def col(c):
        return pl.BlockSpec((nseq, rows, W_B), lambda b, i: (b, i, c))

    vec = pl.BlockSpec((1, W_B), lambda b, i: (0, 0))
    state = pl.BlockSpec((nseq, H_B, DK_B, DV_B), lambda b, i: (b, 0, 0, 0))
    n = nseq * rows
    return pl.pallas_call(
        _hgrn_kernel,
        out_shape=(jax.ShapeDtypeStruct((batch, seq, W_B), BF16),
                   jax.ShapeDtypeStruct((batch, H_B, DK_B, DV_B), F32)),
        grid=(batch // nseq, seq // rows),
        in_specs=[col(4), col(5), col(6), col(7), vec, vec, state],
        out_specs=(pl.BlockSpec((nseq, rows, W_B), lambda b, i: (b, i, 0)), state),
        scratch_shapes=([pltpu.VMEM((nseq, H_B, DV_B, DK_B), F32)]
                        + [pltpu.VMEM((n, W_B), BF16)] * 8
                        + [pltpu.VMEM((n, W_B), F32)]),
        compiler_params=_params(("arbitrary", "arbitrary")),
        name=name,
    )(p3d, p3d, p3d, p3d, lb, ng, s0)


OUTPROJ_ROWS = 512


def _outproj_kernel(oa_ref, ob_ref, x_ref, w_ref, g_ref, y_ref):
    h = (x_ref[...]
         + jnp.dot(oa_ref[...], w_ref[0:W_A, :], preferred_element_type=F32)
         + jnp.dot(ob_ref[...], w_ref[W_A:, :], preferred_element_type=F32))
    ms = jnp.mean(h * h, axis=-1, keepdims=True)
    y_ref[...] = h * lax.rsqrt(ms + EPS) * g_ref[...]


def _outproj(oa, ob, x2d, w_bf16, g, name):
    rows = x2d.shape[0]
    tm = min(OUTPROJ_ROWS, rows)
    return pl.pallas_call(
        _outproj_kernel,
        out_shape=jax.ShapeDtypeStruct((rows, D_MODEL), F32),
        grid=(rows // tm,),
        in_specs=[
            pl.BlockSpec((tm, W_A), lambda i: (i, 0)),
            pl.BlockSpec((tm, W_B), lambda i: (i, 0)),
            pl.BlockSpec((tm, D_MODEL), lambda i: (i, 0)),
            pl.BlockSpec((W_A + W_B, D_MODEL), lambda i: (0, 0)),
            pl.BlockSpec((1, D_MODEL), lambda i: (0, 0)),
        ],
        out_specs=pl.BlockSpec((tm, D_MODEL), lambda i: (i, 0)),
        compiler_params=_params(("arbitrary",)),
        name=name,
    )(oa, ob, x2d, w_bf16, g)


HGRN_PROMPT_ROWS = 256
HGRN_PROMPT_SEQS = 2
HGRN_SAMPLE_SEQS = 8


def kernel(x_prompt, x_sample, cache_attn_k, cache_attn_v, state_hgrn, ln_in_g, w_in,
           rel_bias, lb_gamma, hg_norm_g, w_out, ln_f_g):
    batch, seq, _ = x_prompt.shape
    dec_batch, dec_seq, _ = x_sample.shape
    depth = w_in.shape[0]
    assert depth == 1 and dec_seq == CHUNK and PAST_LEN % CHUNK == 0
    assert cache_attn_k.shape[2] == WINDOW_ROWS

    lb_all = jnp.cumsum(jax.nn.softmax(lb_gamma.astype(F32), axis=0), axis=0)
    lb = lb_all[0].reshape(1, W_B)
    ng = hg_norm_g[0].reshape(1, W_B)
    g_in = ln_in_g[0].reshape(1, D_MODEL)
    g_f = ln_f_g.reshape(1, D_MODEL)
    w_in_b = w_in[0].astype(BF16)
    w_out_b = w_out[0].astype(BF16)

    xp = x_prompt.reshape(batch * seq, D_MODEL)
    xs = x_sample.reshape(dec_batch * dec_seq, D_MODEL)

    bias2 = _bias_table(rel_bias[0]).reshape(HEAD_PAIRS, 2 * CHUNK, BAND)

    pp = _inproj(xp, g_in, w_in_b, "inproj_prompt")
    ps = _inproj(xs, g_in, w_in_b, "inproj_sample")

    oa_p = _attn_prompt(pp, bias2, batch, seq)
    ck = cache_attn_k[0].reshape(dec_batch, WINDOW_ROWS, W_A)
    cv = cache_attn_v[0].reshape(dec_batch, WINDOW_ROWS, W_A)
    oa_s = _attn_sample(ps, ck, cv, bias2, dec_batch)

    s0_p = jnp.zeros((batch, H_B, DK_B, DV_B), F32)
    ob_p, st_p = _hgrn(pp.reshape(batch, seq, N_IN), lb, ng, s0_p,
                       HGRN_PROMPT_SEQS, HGRN_PROMPT_ROWS, "hgrn_prompt")
    ob_s, st_s = _hgrn(ps.reshape(dec_batch, dec_seq, N_IN), lb, ng, state_hgrn[0],
                       HGRN_SAMPLE_SEQS, dec_seq, "hgrn_sample")
    ob_p = ob_p.reshape(batch * seq, W_B)
    ob_s = ob_s.reshape(dec_batch * dec_seq, W_B)

    y_p = _outproj(oa_p, ob_p, xp, w_out_b, g_f, "outproj_prompt").reshape(batch, seq, D_MODEL)
    y_s = _outproj(oa_s, ob_s, xs, w_out_b, g_f, "outproj_sample").reshape(
        dec_batch, dec_seq, D_MODEL)

    rows_p = min(WINDOW_ROWS, seq)
    pp3 = pp.reshape(batch, seq, N_IN)
    k_p = pp3[:, seq - rows_p:, W_A:2 * W_A].reshape(1, batch, rows_p, H_A, DH_A)
    v_p = pp3[:, seq - rows_p:, 2 * W_A:3 * W_A].reshape(1, batch, rows_p, H_A, DH_A)
    ps3 = ps.reshape(dec_batch, dec_seq, N_IN)
    k_s = ps3[:, :, W_A:2 * W_A].reshape(1, dec_batch, dec_seq, H_A, DH_A)
    v_s = ps3[:, :, 2 * W_A:3 * W_A].reshape(1, dec_batch, dec_seq, H_A, DH_A)

    return (y_p, y_s, k_p, v_p, st_p[None], k_s, v_s, st_s[None])
```

```python
import jax
import jax.numpy as jnp
from jax import lax
from jax.experimental import pallas as pl
from jax.experimental.pallas import tpu as pltpu

F32 = jnp.float32
BF16 = jnp.bfloat16

D_MODEL = 1024
CHUNK = 64
LEFT_CHUNKS = 8
WINDOW_ROWS = LEFT_CHUNKS * CHUNK
BAND = WINDOW_ROWS + CHUNK
W_A = 512
H_A = 8
DH_A = 64
W_B = 512
H_B = 4
DK_B = 128
DV_B = 128
REL_CLIP = 128
GLA_BLOCK = 16
EPS = 1e-6
ATTN_SCALE = DH_A ** -0.5
NEG_INF = -1e30
N_IN = 4 * W_A + 4 * W_B
PAST_LEN = 2048

LANES = 128
HEAD_PAIRS = H_A // 2
REL_TABLE = 2 * REL_CLIP + 1
REL_TABLE_PAD = 384
TOEPLITZ_LEN = 640
VMEM_LIMIT = 56 * 1024 * 1024


def _params(semantics):
    return pltpu.CompilerParams(dimension_semantics=semantics,
                                vmem_limit_bytes=VMEM_LIMIT)


def _silu(x):
    return x * (1.0 / (1.0 + jnp.exp(-x)))


def _sigmoid(x):
    return 1.0 / (1.0 + jnp.exp(-x))


def _nt(a, b):
    return lax.dot_general(a, b, (((1,), (1,)), ((), ())), preferred_element_type=F32)


INPROJ_ROWS = 512
INPROJ_COLS = 512


def _inproj_kernel(x_ref, g_ref, w_ref, o_ref):
    x = x_ref[...]
    ms = jnp.mean(x * x, axis=-1, keepdims=True)
    xn = (x * lax.rsqrt(ms + EPS) * g_ref[...]).astype(BF16)
    for n0 in range(0, N_IN, INPROJ_COLS):
        o_ref[:, n0:n0 + INPROJ_COLS] = jnp.dot(
            xn, w_ref[:, n0:n0 + INPROJ_COLS], preferred_element_type=F32)


def _inproj(x2d, g, w_bf16, name):
    rows = x2d.shape[0]
    tm = min(INPROJ_ROWS, rows)
    return pl.pallas_call(
        _inproj_kernel,
        out_shape=jax.ShapeDtypeStruct((rows, N_IN), F32),
        grid=(rows // tm,),
        in_specs=[
            pl.BlockSpec((tm, D_MODEL), lambda i: (i, 0)),
            pl.BlockSpec((1, D_MODEL), lambda i: (0, 0)),
            pl.BlockSpec((D_MODEL, N_IN), lambda i: (0, 0)),
        ],
        out_specs=pl.BlockSpec((tm, N_IN), lambda i: (i, 0)),
        compiler_params=_params(("arbitrary",)),
        name=name,
    )(x2d, g, w_bf16)


PAIR_BAND = BAND + CHUNK


def _bias_kernel(rb_ref, pair_ref, single_ref):
    m = lax.broadcasted_iota(jnp.int32, (REL_TABLE_PAD, TOEPLITZ_LEN), 1)
    t = lax.broadcasted_iota(jnp.int32, (REL_TABLE_PAD, TOEPLITZ_LEN), 0)
    idx = jnp.clip(BAND - 1 - m, -REL_CLIP, REL_CLIP) + REL_CLIP
    onehot = jnp.where(idx == t, 1.0, 0.0).astype(BF16)
    rb = rb_ref[...]
    hi = rb.astype(BF16)
    r1 = rb - hi.astype(F32)
    mid = r1.astype(BF16)
    lo = (r1 - mid.astype(F32)).astype(BF16)
    toep = (jnp.dot(hi, onehot, preferred_element_type=F32)
            + jnp.dot(mid, onehot, preferred_element_type=F32)
            + jnp.dot(lo, onehot, preferred_element_type=F32))
    key = lax.broadcasted_iota(jnp.int32, (CHUNK, TOEPLITZ_LEN), 1)
    for p in range(HEAD_PAIRS):
        for u in range(2):
            halves = []
            for h in (2 * p, 2 * p + 1):
                rows = jnp.broadcast_to(toep[h:h + 1, :], (CHUNK, TOEPLITZ_LEN))
                shift = (TOEPLITZ_LEN - (CHUNK - 1) + CHUNK * u) % TOEPLITZ_LEN
                rolled = pltpu.roll(rows, shift, 1, stride=1, stride_axis=0)
                seen = (key >= CHUNK * u) & (key < CHUNK * u + BAND)
                halves.append(jnp.where(seen, rolled, NEG_INF))
            tile_t = jnp.concatenate(halves, axis=0).T
            pair_ref[p, :, u * LANES:(u + 1) * LANES] = tile_t
            if u == 0:
                single_ref[p] = tile_t[:BAND, :]


def _bias_tables(rel_bias_l):
    rb = jnp.pad(rel_bias_l, ((0, 0), (0, REL_TABLE_PAD - REL_TABLE)))
    return pl.pallas_call(
        _bias_kernel,
        out_shape=(jax.ShapeDtypeStruct((HEAD_PAIRS, PAIR_BAND, 2 * LANES), F32),
                   jax.ShapeDtypeStruct((HEAD_PAIRS, BAND, LANES), F32)),
        name="rel_bias_table",
    )(rb)


def _attn_unit(qs, ags, k2, vt, bias_t, valid_from):
    first = lax.broadcasted_iota(jnp.int32, (CHUNK, LANES), 1) < DH_A
    parts = []
    for q in qs:
        q = q * ATTN_SCALE
        parts += [jnp.where(first, q, 0.0), jnp.where(first, 0.0, q)]
    qbd = jnp.concatenate(parts, axis=0).astype(BF16)
    s = _nt(k2, qbd) + bias_t
    if valid_from is not None:
        key = lax.broadcasted_iota(jnp.int32, s.shape, 0)
        s = jnp.where(key >= valid_from, s, NEG_INF)
    mx = jnp.max(s, axis=0, keepdims=True)
    e = jnp.exp(s - mx)
    denom = jnp.sum(e, axis=0, keepdims=True)
    ot = jnp.dot(vt, e.astype(BF16), preferred_element_type=F32)
    o2 = (ot * (1.0 / denom)).T
    outs = []
    for u, ag in enumerate(ags):
        r = 2 * CHUNK * u
        o = jnp.where(first, o2[r:r + CHUNK], o2[r + CHUNK:r + 2 * CHUNK])
        outs.append(o * _silu(ag))
    return outs


ATTN_ROWS = 512


def _attn_prompt_kernel(q_ref, k_ref, v_ref, ag_ref, bias_ref, o_ref, kh_ref, vth_ref):
    i = pl.program_id(1)

    @pl.when(i == 0)
    def _():
        kh_ref[0:WINDOW_ROWS, :] = jnp.zeros((WINDOW_ROWS, W_A), BF16)
        vth_ref[:, :, 0:WINDOW_ROWS] = jnp.zeros((HEAD_PAIRS, LANES, WINDOW_ROWS), BF16)

    kh_ref[WINDOW_ROWS:, :] = k_ref[...].astype(BF16)
    for p in range(HEAD_PAIRS):
        vth_ref[p, :, WINDOW_ROWS:] = v_ref[:, p * LANES:(p + 1) * LANES].T.astype(BF16)

    def body(first_step):
        for cp in range(ATTN_ROWS // (2 * CHUNK)):
            r0 = cp * 2 * CHUNK
            valid_from = (LEFT_CHUNKS - 2 * cp) * CHUNK if first_step else None
            for p in range(HEAD_PAIRS):
                cs = slice(p * LANES, (p + 1) * LANES)
                rs = [slice(r0 + u * CHUNK, r0 + (u + 1) * CHUNK) for u in range(2)]
                outs = _attn_unit([q_ref[r, cs] for r in rs], [ag_ref[r, cs] for r in rs],
                                  kh_ref[r0:r0 + PAIR_BAND, cs],
                                  vth_ref[p, :, r0:r0 + PAIR_BAND],
                                  bias_ref[p], valid_from)
                for r, o in zip(rs, outs):
                    o_ref[r, cs] = o.astype(o_ref.dtype)

    @pl.when(i == 0)
    def _():
        body(True)

    @pl.when(i != 0)
    def _():
        body(False)

    kh_ref[0:WINDOW_ROWS, :] = kh_ref[ATTN_ROWS:ATTN_ROWS + WINDOW_ROWS, :]
    vth_ref[:, :, 0:WINDOW_ROWS] = vth_ref[:, :, ATTN_ROWS:ATTN_ROWS + WINDOW_ROWS]


def _attn_prompt(p2d, bias_pair, batch, seq):
    nt = seq // ATTN_ROWS

    def col(c):
        return pl.BlockSpec((ATTN_ROWS, W_A), lambda b, i: (b * nt + i, c))

    return pl.pallas_call(
        _attn_prompt_kernel,
        out_shape=jax.ShapeDtypeStruct((batch * seq, W_A), BF16),
        grid=(batch, nt),
        in_specs=[col(0), col(1), col(2), col(3),
                  pl.BlockSpec((HEAD_PAIRS, PAIR_BAND, 2 * LANES), lambda b, i: (0, 0, 0))],
        out_specs=pl.BlockSpec((ATTN_ROWS, W_A), lambda b, i: (b * nt + i, 0)),
        scratch_shapes=[pltpu.VMEM((WINDOW_ROWS + ATTN_ROWS, W_A), BF16),
                        pltpu.VMEM((HEAD_PAIRS, LANES, WINDOW_ROWS + ATTN_ROWS), BF16)],
        compiler_params=_params(("arbitrary", "arbitrary")),
        name="attn_prompt",
    )(p2d, p2d, p2d, p2d, bias_pair)


ATTN_SAMPLE_SEQS = 4


def _attn_sample_kernel(q_ref, k_ref, v_ref, ag_ref, ck_ref, cv_ref, bias_ref, o_ref,
                        kh_ref, vth_ref):
    for s in range(ATTN_SAMPLE_SEQS):
        rs = slice(s * CHUNK, (s + 1) * CHUNK)
        kh_ref[s, 0:WINDOW_ROWS, :] = ck_ref[s].astype(BF16)
        kh_ref[s, WINDOW_ROWS:, :] = k_ref[rs, :].astype(BF16)
        for p in range(HEAD_PAIRS):
            cs = slice(p * LANES, (p + 1) * LANES)
            vth_ref[s, p, :, 0:WINDOW_ROWS] = cv_ref[s, :, cs].T.astype(BF16)
            vth_ref[s, p, :, WINDOW_ROWS:] = v_ref[rs, cs].T.astype(BF16)
        for p in range(HEAD_PAIRS):
            cs = slice(p * LANES, (p + 1) * LANES)
            (o,) = _attn_unit([q_ref[rs, cs]], [ag_ref[rs, cs]], kh_ref[s, :, cs],
                              vth_ref[s, p], bias_ref[p], None)
            o_ref[rs, cs] = o.astype(o_ref.dtype)


def _attn_sample(p2d, cache_k, cache_v, bias_single, batch):
    ns = ATTN_SAMPLE_SEQS

    def col(c):
        return pl.BlockSpec((ns * CHUNK, W_A), lambda b: (b, c))

    cache = pl.BlockSpec((ns, WINDOW_ROWS, W_A), lambda b: (b, 0, 0))
    return pl.pallas_call(
        _attn_sample_kernel,
        out_shape=jax.ShapeDtypeStruct((batch * CHUNK, W_A), BF16),
        grid=(batch // ns,),
        in_specs=[col(0), col(1), col(2), col(3), cache, cache,
                  pl.BlockSpec((HEAD_PAIRS, BAND, LANES), lambda b: (0, 0, 0))],
        out_specs=pl.BlockSpec((ns * CHUNK, W_A), lambda b: (b, 0)),
        scratch_shapes=[pltpu.VMEM((ns, BAND, W_A), BF16),
                        pltpu.VMEM((ns, HEAD_PAIRS, LANES, BAND), BF16)],
        compiler_params=_params(("arbitrary",)),
        name="attn_sample",
    )(p2d, p2d, p2d, p2d, cache_k, cache_v, bias_single)


HGRN_GROUP_BLOCKS = 4
HGRN_GROUP = HGRN_GROUP_BLOCKS * GLA_BLOCK


def _block_cumsum(x):
    row = lax.broadcasted_iota(jnp.int32, x.shape, 0) & (GLA_BLOCK - 1)
    s = 1
    while s < GLA_BLOCK:
        x = x + jnp.where(row >= s, pltpu.roll(x, s, 0), 0.0)
        s *= 2
    return x


def _hgrn_kernel(hq_ref, hf_ref, hi_ref, hg_ref, lb_ref, ng_ref, s0_ref,
                 o_ref, sout_ref, st_ref, qt_ref, kt_ref, kd_ref, qs_ref, q1_ref,
                 k1_ref, kdp_ref, v_ref, dec_ref):
    i = pl.program_id(1)
    nseq, seq_rows = hq_ref.shape[0], hq_ref.shape[1]
    rows = nseq * seq_rows
    nb = rows // GLA_BLOCK

    @pl.when(i == 0)
    def _():
        for s in range(nseq):
            for h in range(H_B):
                st_ref[s, h] = s0_ref[s, h].T

    def flat(ref):
        return ref[...].reshape(rows, W_B)

    lb = lb_ref[...]
    f = lb + (1.0 - lb) * _sigmoid(flat(hf_ref))
    b = _block_cumsum(jnp.log(f))
    b3 = b.reshape(nb, GLA_BLOCK, W_B)
    bl = jnp.broadcast_to(b3[:, GLA_BLOCK - 1:GLA_BLOCK, :],
                          (nb, GLA_BLOCK, W_B)).reshape(rows, W_B)
    blk = (lax.broadcasted_iota(jnp.int32, (rows, W_B), 0) // GLA_BLOCK) & (HGRN_GROUP_BLOCKS - 1)
    prev = [pltpu.roll(bl, d * GLA_BLOCK, 0) for d in range(1, HGRN_GROUP_BLOCKS)]
    nxt = [pltpu.roll(bl, rows - d * GLA_BLOCK, 0) for d in range(1, HGRN_GROUP_BLOCKS)]
    c_in = sum(jnp.where(blk >= d, prev[d - 1], 0.0) for d in range(1, HGRN_GROUP_BLOCKS))
    c_out = sum(jnp.where(blk <= HGRN_GROUP_BLOCKS - 1 - d, nxt[d - 1], 0.0)
                for d in range(1, HGRN_GROUP_BLOCKS))
    kk = 1.0 - f
    qt = _silu(flat(hq_ref)) * jnp.exp(b)
    kd = kk * jnp.exp(bl - b)
    qt_ref[...] = qt.astype(BF16)
    kt_ref[...] = (kk * jnp.exp(-b)).astype(BF16)
    kd_ref[...] = kd.astype(BF16)
    qs_ref[...] = (qt * jnp.exp(c_in)).astype(BF16)
    kdp_ref[...] = (kd * jnp.exp(c_out)).astype(BF16)
    q1_ref[...] = (qt * jnp.exp(jnp.where(blk == 3, prev[0], 0.0))).astype(BF16)
    k1_ref[...] = (kd * jnp.exp(jnp.where(blk == 0, nxt[0], 0.0))).astype(BF16)
    v_ref[...] = flat(hi_ref).astype(BF16)
    dec_ref[...] = jnp.exp(c_in + bl + c_out)

    r_i = lax.broadcasted_iota(jnp.int32, (HGRN_GROUP, HGRN_GROUP), 0)
    c_i = lax.broadcasted_iota(jnp.int32, (HGRN_GROUP, HGRN_GROUP), 1)
    rb, cb = r_i // GLA_BLOCK, c_i // GLA_BLOCK
    m_diag = (rb == cb) & (r_i >= c_i)
    m_adj = rb == cb + 1
    m_far = rb >= cb + 2

    for s in range(nseq):
        st = [st_ref[s, h] for h in range(H_B)]
        for g in range(seq_rows // HGRN_GROUP):
            r0 = s * seq_rows + g * HGRN_GROUP
            rs = slice(r0, r0 + HGRN_GROUP)
            for h in range(H_B):
                cs = slice(h * DK_B, (h + 1) * DK_B)
                qt_g = qt_ref[rs, cs]
                kd_g = kd_ref[rs, cs]
                v_g = v_ref[rs, cs]
                a = jnp.where(m_diag, _nt(qt_g, kt_ref[rs, cs]),
                              jnp.where(m_adj, _nt(qt_g, kd_g),
                                        jnp.where(m_far, _nt(q1_ref[rs, cs], k1_ref[rs, cs]), 0.0)))
                o = (_nt(qs_ref[rs, cs], st[h].astype(BF16))
                     + jnp.dot(a.astype(BF16), v_g, preferred_element_type=F32))
                ut = lax.dot_general(v_g, kdp_ref[rs, cs], (((0,), (0,)), ((), ())),
                                     preferred_element_type=F32)
                dec = dec_ref[r0:r0 + 8, cs]
                st[h] = (st[h].reshape(DV_B // 8, 8, DK_B) * dec[None]
                         + ut.reshape(DV_B // 8, 8, DK_B)).reshape(DV_B, DK_B)
                y = o * lax.rsqrt(jnp.mean(o * o, axis=-1, keepdims=True) + EPS) * ng_ref[:, cs]
                gs = slice(g * HGRN_GROUP, (g + 1) * HGRN_GROUP)
                o_ref[s, gs, cs] = (y * _silu(hg_ref[s, gs, cs])).astype(o_ref.dtype)
        for h in range(H_B):
            st_ref[s, h] = st[h]

    @pl.when(i == pl.num_programs(1) - 1)
    def _():
        for s in range(nseq):
            for h in range(H_B):
                sout_ref[s, h] = st_ref[s, h].T


def _hgrn(p3d, lb, ng, s0, nseq, rows, name):
    batch, seq, _ = p3d.shape
    assert batch % nseq == 0 and seq % rows == 0 and rows % HGRN_GROUP == 0

    def col(c):
        return pl.BlockSpec((nseq, rows, W_B), lambda b, i: (b, i, c))

    vec = pl.BlockSpec((1, W_B), lambda b, i: (0, 0))
    state = pl.BlockSpec((nseq, H_B, DK_B, DV_B), lambda b, i: (b, 0, 0, 0))
    n = nseq * rows
    return pl.pallas_call(
        _hgrn_kernel,
        out_shape=(jax.ShapeDtypeStruct((batch, seq, W_B), BF16),
                   jax.ShapeDtypeStruct((batch, H_B, DK_B, DV_B), F32)),
        grid=(batch // nseq, seq // rows),
        in_specs=[col(4), col(5), col(6), col(7), vec, vec, state],
        out_specs=(pl.BlockSpec((nseq, rows, W_B), lambda b, i: (b, i, 0)), state),
        scratch_shapes=([pltpu.VMEM((nseq, H_B, DV_B, DK_B), F32)]
                        + [pltpu.VMEM((n, W_B), BF16)] * 8
                        + [pltpu.VMEM((n, W_B), F32)]),
        compiler_params=_params(("arbitrary", "arbitrary")),
        name=name,
    )(p3d, p3d, p3d, p3d, lb, ng, s0)


OUTPROJ_ROWS = 512


def _outproj_kernel(oa_ref, ob_ref, x_ref, w_ref, g_ref, y_ref):
    h = (x_ref[...]
         + jnp.dot(oa_ref[...], w_ref[0:W_A, :], preferred_element_type=F32)
         + jnp.dot(ob_ref[...], w_ref[W_A:, :], preferred_element_type=F32))
    ms = jnp.mean(h * h, axis=-1, keepdims=True)
    y_ref[...] = h * lax.rsqrt(ms + EPS) * g_ref[...]


def _outproj(oa, ob, x2d, w_bf16, g, name):
    rows = x2d.shape[0]
    tm = min(OUTPROJ_ROWS, rows)
    return pl.pallas_call(
        _outproj_kernel,
        out_shape=jax.ShapeDtypeStruct((rows, D_MODEL), F32),
        grid=(rows // tm,),
        in_specs=[
            pl.BlockSpec((tm, W_A), lambda i: (i, 0)),
            pl.BlockSpec((tm, W_B), lambda i: (i, 0)),
            pl.BlockSpec((tm, D_MODEL), lambda i: (i, 0)),
            pl.BlockSpec((W_A + W_B, D_MODEL), lambda i: (0, 0)),
            pl.BlockSpec((1, D_MODEL), lambda i: (0, 0)),
        ],
        out_specs=pl.BlockSpec((tm, D_MODEL), lambda i: (i, 0)),
        compiler_params=_params(("arbitrary",)),
        name=name,
    )(oa, ob, x2d, w_bf16, g)


HGRN_PROMPT_ROWS = 256
HGRN_PROMPT_SEQS = 2
HGRN_SAMPLE_SEQS = 8


def kernel(x_prompt, x_sample, cache_attn_k, cache_attn_v, state_hgrn, ln_in_g, w_in,
           rel_bias, lb_gamma, hg_norm_g, w_out, ln_f_g):
    batch, seq, _ = x_prompt.shape
    dec_batch, dec_seq, _ = x_sample.shape
    depth = w_in.shape[0]
    assert depth == 1 and dec_seq == CHUNK and PAST_LEN % CHUNK == 0
    assert cache_attn_k.shape[2] == WINDOW_ROWS

    lb_all = jnp.cumsum(jax.nn.softmax(lb_gamma.astype(F32), axis=0), axis=0)
    lb = lb_all[0].reshape(1, W_B)
    ng = hg_norm_g[0].reshape(1, W_B)
    g_in = ln_in_g[0].reshape(1, D_MODEL)
    g_f = ln_f_g.reshape(1, D_MODEL)
    w_in_b = w_in[0].astype(BF16)
    w_out_b = w_out[0].astype(BF16)

    xp = x_prompt.reshape(batch * seq, D_MODEL)
    xs = x_sample.reshape(dec_batch * dec_seq, D_MODEL)

    bias_pair, bias_single = _bias_tables(rel_bias[0])

    pp = _inproj(xp, g_in, w_in_b, "inproj_prompt")
    ps = _inproj(xs, g_in, w_in_b, "inproj_sample")

    oa_p = _attn_prompt(pp, bias_pair, batch, seq)
    ck = cache_attn_k[0].reshape(dec_batch, WINDOW_ROWS, W_A)
    cv = cache_attn_v[0].reshape(dec_batch, WINDOW_ROWS, W_A)
    oa_s = _attn_sample(ps, ck, cv, bias_single, dec_batch)

    s0_p = jnp.zeros((batch, H_B, DK_B, DV_B), F32)
    ob_p, st_p = _hgrn(pp.reshape(batch, seq, N_IN), lb, ng, s0_p,
                       HGRN_PROMPT_SEQS, HGRN_PROMPT_ROWS, "hgrn_prompt")
    ob_s, st_s = _hgrn(ps.reshape(dec_batch, dec_seq, N_IN), lb, ng, state_hgrn[0],
                       HGRN_SAMPLE_SEQS, dec_seq, "hgrn_sample")
    ob_p = ob_p.reshape(batch * seq, W_B)
    ob_s = ob_s.reshape(dec_batch * dec_seq, W_B)

    y_p = _outproj(oa_p, ob_p, xp, w_out_b, g_f, "outproj_prompt").reshape(batch, seq, D_MODEL)
    y_s = _outproj(oa_s, ob_s, xs, w_out_b, g_f, "outproj_sample").reshape(
        dec_batch, dec_seq, D_MODEL)

    rows_p = min(WINDOW_ROWS, seq)
    pp3 = pp.reshape(batch, seq, N_IN)
    k_p = pp3[:, seq - rows_p:, W_A:2 * W_A].reshape(1, batch, rows_p, H_A, DH_A)
    v_p = pp3[:, seq - rows_p:, 2 * W_A:3 * W_A].reshape(1, batch, rows_p, H_A, DH_A)
    ps3 = ps.reshape(dec_batch, dec_seq, N_IN)
    k_s = ps3[:, :, W_A:2 * W_A].reshape(1, dec_batch, dec_seq, H_A, DH_A)
    v_s = ps3[:, :, 2 * W_A:3 * W_A].reshape(1, dec_batch, dec_seq, H_A, DH_A)

    return (y_p, y_s, k_p, v_p, st_p[None], k_s, v_s, st_s[None])
```

```python
import jax
import jax.numpy as jnp
from jax import lax
from jax.experimental import pallas as pl
from jax.experimental.pallas import tpu as pltpu

F32 = jnp.float32
BF16 = jnp.bfloat16

D_MODEL = 1024
CHUNK = 64
LEFT_CHUNKS = 8
WINDOW_ROWS = LEFT_CHUNKS * CHUNK
BAND = WINDOW_ROWS + CHUNK
W_A = 512
H_A = 8
DH_A = 64
W_B = 512
H_B = 4
DK_B = 128
DV_B = 128
REL_CLIP = 128
GLA_BLOCK = 16
EPS = 1e-6
ATTN_SCALE = DH_A ** -0.5
NEG_INF = -1e30
N_IN = 4 * W_A + 4 * W_B
PAST_LEN = 2048

LANES = 128
HEAD_PAIRS = H_A // 2
REL_TABLE = 2 * REL_CLIP + 1
REL_TABLE_PAD = 384
TOEPLITZ_LEN = 640
VMEM_LIMIT = 56 * 1024 * 1024


def _params(semantics):
    return pltpu.CompilerParams(dimension_semantics=semantics,
                                vmem_limit_bytes=VMEM_LIMIT)


def _silu(x):
    return x * (1.0 / (1.0 + jnp.exp(-x)))


def _sigmoid(x):
    return 1.0 / (1.0 + jnp.exp(-x))


def _nt(a, b):
    return lax.dot_general(a, b, (((1,), (1,)), ((), ())), preferred_element_type=F32)


INPROJ_ROWS = 512
INPROJ_COLS = 512


def _inproj_kernel(x_ref, g_ref, w_ref, o_ref):
    x = x_ref[...]
    ms = jnp.mean(x * x, axis=-1, keepdims=True)
    xn = (x * lax.rsqrt(ms + EPS) * g_ref[...]).astype(BF16)
    for n0 in range(0, N_IN, INPROJ_COLS):
        o_ref[:, n0:n0 + INPROJ_COLS] = jnp.dot(
            xn, w_ref[:, n0:n0 + INPROJ_COLS], preferred_element_type=F32)


def _inproj(x2d, g, w_bf16, name):
    rows = x2d.shape[0]
    tm = min(INPROJ_ROWS, rows)
    return pl.pallas_call(
        _inproj_kernel,
        out_shape=jax.ShapeDtypeStruct((rows, N_IN), F32),
        grid=(rows // tm,),
        in_specs=[
            pl.BlockSpec((tm, D_MODEL), lambda i: (i, 0)),
            pl.BlockSpec((1, D_MODEL), lambda i: (0, 0)),
            pl.BlockSpec((D_MODEL, N_IN), lambda i: (0, 0)),
        ],
        out_specs=pl.BlockSpec((tm, N_IN), lambda i: (i, 0)),
        compiler_params=_params(("arbitrary",)),
        name=name,
    )(x2d, g, w_bf16)


PAIR_BAND = BAND + CHUNK


def _bias_kernel(rb_ref, pair_ref, single_ref):
    m = lax.broadcasted_iota(jnp.int32, (REL_TABLE_PAD, TOEPLITZ_LEN), 1)
    t = lax.broadcasted_iota(jnp.int32, (REL_TABLE_PAD, TOEPLITZ_LEN), 0)
    idx = jnp.clip(BAND - 1 - m, -REL_CLIP, REL_CLIP) + REL_CLIP
    onehot = jnp.where(idx == t, 1.0, 0.0).astype(BF16)
    rb = rb_ref[...]
    hi = rb.astype(BF16)
    r1 = rb - hi.astype(F32)
    mid = r1.astype(BF16)
    lo = (r1 - mid.astype(F32)).astype(BF16)
    toep = (jnp.dot(hi, onehot, preferred_element_type=F32)
            + jnp.dot(mid, onehot, preferred_element_type=F32)
            + jnp.dot(lo, onehot, preferred_element_type=F32))
    key = lax.broadcasted_iota(jnp.int32, (CHUNK, TOEPLITZ_LEN), 1)
    for p in range(HEAD_PAIRS):
        for u in range(2):
            halves = []
            for h in (2 * p, 2 * p + 1):
                rows = jnp.broadcast_to(toep[h:h + 1, :], (CHUNK, TOEPLITZ_LEN))
                shift = (TOEPLITZ_LEN - (CHUNK - 1) + CHUNK * u) % TOEPLITZ_LEN
                rolled = pltpu.roll(rows, shift, 1, stride=1, stride_axis=0)
                seen = (key >= CHUNK * u) & (key < CHUNK * u + BAND)
                halves.append(jnp.where(seen, rolled, NEG_INF))
            tile_t = jnp.concatenate(halves, axis=0).T
            pair_ref[p, :, u * LANES:(u + 1) * LANES] = tile_t
            if u == 0:
                single_ref[p] = tile_t[:BAND, :]


def _bias_tables(rel_bias_l):
    rb = jnp.pad(rel_bias_l, ((0, 0), (0, REL_TABLE_PAD - REL_TABLE)))
    return pl.pallas_call(
        _bias_kernel,
        out_shape=(jax.ShapeDtypeStruct((HEAD_PAIRS, PAIR_BAND, 2 * LANES), F32),
                   jax.ShapeDtypeStruct((HEAD_PAIRS, BAND, LANES), F32)),
        name="rel_bias_table",
    )(rb)


def _attn_probs(qs, k2, bias_t, valid_from):
    first = lax.broadcasted_iota(jnp.int32, (CHUNK, LANES), 1) < DH_A
    parts = []
    for q in qs:
        q = q * ATTN_SCALE
        parts += [jnp.where(first, q, 0.0), jnp.where(first, 0.0, q)]
    qbd = jnp.concatenate(parts, axis=0).astype(BF16)
    s = _nt(k2, qbd) + bias_t
    if valid_from is not None:
        key = lax.broadcasted_iota(jnp.int32, s.shape, 0)
        s = jnp.where(key >= valid_from, s, NEG_INF)
    mx = jnp.max(s, axis=0, keepdims=True)
    e = jnp.exp(s - mx)
    return e.astype(BF16), jnp.sum(e, axis=0, keepdims=True)


def _attn_apply(probs, ags, vt):
    e, denom = probs
    first = lax.broadcasted_iota(jnp.int32, (CHUNK, LANES), 1) < DH_A
    ot = jnp.dot(vt, e, preferred_element_type=F32)
    o2 = (ot * (1.0 / denom)).T
    outs = []
    for u, ag in enumerate(ags):
        r = 2 * CHUNK * u
        o = jnp.where(first, o2[r:r + CHUNK], o2[r + CHUNK:r + 2 * CHUNK])
        outs.append(o * _silu(ag))
    return outs


ATTN_PIPE_DEPTH = 3


def _attn_pipeline(n_units, probs_fn, apply_fn):
    pending = {}
    for t in range(n_units + ATTN_PIPE_DEPTH):
        if t < n_units:
            pending[t] = probs_fn(t)
        if t >= ATTN_PIPE_DEPTH:
            apply_fn(t - ATTN_PIPE_DEPTH, pending.pop(t - ATTN_PIPE_DEPTH))


ATTN_ROWS = 512


def _attn_prompt_kernel(q_ref, k_ref, v_ref, ag_ref, bias_ref, o_ref, kh_ref, vth_ref):
    i = pl.program_id(1)

    @pl.when(i == 0)
    def _():
        kh_ref[0:WINDOW_ROWS, :] = jnp.zeros((WINDOW_ROWS, W_A), BF16)
        vth_ref[:, :, 0:WINDOW_ROWS] = jnp.zeros((HEAD_PAIRS, LANES, WINDOW_ROWS), BF16)

    kh_ref[WINDOW_ROWS:, :] = k_ref[...].astype(BF16)
    for p in range(HEAD_PAIRS):
        vth_ref[p, :, WINDOW_ROWS:] = v_ref[:, p * LANES:(p + 1) * LANES].T.astype(BF16)

    def body(first_step):
        def unit(t):
            cp, p = divmod(t, HEAD_PAIRS)
            r0 = cp * 2 * CHUNK
            rs = [slice(r0 + u * CHUNK, r0 + (u + 1) * CHUNK) for u in range(2)]
            return cp, p, r0, rs, slice(p * LANES, (p + 1) * LANES)

        def probs(t):
            cp, p, r0, rs, cs = unit(t)
            valid_from = (LEFT_CHUNKS - 2 * cp) * CHUNK if first_step else None
            return _attn_probs([q_ref[r, cs] for r in rs], kh_ref[r0:r0 + PAIR_BAND, cs],
                               bias_ref[p], valid_from)

        def apply(t, pr):
            cp, p, r0, rs, cs = unit(t)
            outs = _attn_apply(pr, [ag_ref[r, cs] for r in rs],
                               vth_ref[p, :, r0:r0 + PAIR_BAND])
            for r, o in zip(rs, outs):
                o_ref[r, cs] = o.astype(o_ref.dtype)

        _attn_pipeline((ATTN_ROWS // (2 * CHUNK)) * HEAD_PAIRS, probs, apply)

    @pl.when(i == 0)
    def _():
        body(True)

    @pl.when(i != 0)
    def _():
        body(False)

    kh_ref[0:WINDOW_ROWS, :] = kh_ref[ATTN_ROWS:ATTN_ROWS + WINDOW_ROWS, :]
    vth_ref[:, :, 0:WINDOW_ROWS] = vth_ref[:, :, ATTN_ROWS:ATTN_ROWS + WINDOW_ROWS]


def _attn_prompt(p2d, bias_pair, batch, seq):
    nt = seq // ATTN_ROWS

    def col(c):
        return pl.BlockSpec((ATTN_ROWS, W_A), lambda b, i: (b * nt + i, c))

    return pl.pallas_call(
        _attn_prompt_kernel,
        out_shape=jax.ShapeDtypeStruct((batch * seq, W_A), BF16),
        grid=(batch, nt),
        in_specs=[col(0), col(1), col(2), col(3),
                  pl.BlockSpec((HEAD_PAIRS, PAIR_BAND, 2 * LANES), lambda b, i: (0, 0, 0))],
        out_specs=pl.BlockSpec((ATTN_ROWS, W_A), lambda b, i: (b * nt + i, 0)),
        scratch_shapes=[pltpu.VMEM((WINDOW_ROWS + ATTN_ROWS, W_A), BF16),
                        pltpu.VMEM((HEAD_PAIRS, LANES, WINDOW_ROWS + ATTN_ROWS), BF16)],
        compiler_params=_params(("arbitrary", "arbitrary")),
        name="attn_prompt",
    )(p2d, p2d, p2d, p2d, bias_pair)


ATTN_SAMPLE_SEQS = 4


def _attn_sample_kernel(q_ref, k_ref, v_ref, ag_ref, ck_ref, cv_ref, bias_ref, o_ref,
                        kh_ref, vth_ref):
    for s in range(ATTN_SAMPLE_SEQS):
        rs = slice(s * CHUNK, (s + 1) * CHUNK)
        kh_ref[s, 0:WINDOW_ROWS, :] = ck_ref[s].astype(BF16)
        kh_ref[s, WINDOW_ROWS:, :] = k_ref[rs, :].astype(BF16)
        for p in range(HEAD_PAIRS):
            cs = slice(p * LANES, (p + 1) * LANES)
            vth_ref[s, p, :, 0:WINDOW_ROWS] = cv_ref[s, :, cs].T.astype(BF16)
            vth_ref[s, p, :, WINDOW_ROWS:] = v_ref[rs, cs].T.astype(BF16)

    def unit(t):
        s, p = divmod(t, HEAD_PAIRS)
        return s, p, slice(s * CHUNK, (s + 1) * CHUNK), slice(p * LANES, (p + 1) * LANES)

    def probs(t):
        s, p, rs, cs = unit(t)
        return _attn_probs([q_ref[rs, cs]], kh_ref[s, :, cs], bias_ref[p], None)

    def apply(t, pr):
        s, p, rs, cs = unit(t)
        (o,) = _attn_apply(pr, [ag_ref[rs, cs]], vth_ref[s, p])
        o_ref[rs, cs] = o.astype(o_ref.dtype)

    _attn_pipeline(ATTN_SAMPLE_SEQS * HEAD_PAIRS, probs, apply)


def _attn_sample(p2d, cache_k, cache_v, bias_single, batch):
    ns = ATTN_SAMPLE_SEQS

    def col(c):
        return pl.BlockSpec((ns * CHUNK, W_A), lambda b: (b, c))

    cache = pl.BlockSpec((ns, WINDOW_ROWS, W_A), lambda b: (b, 0, 0))
    return pl.pallas_call(
        _attn_sample_kernel,
        out_shape=jax.ShapeDtypeStruct((batch * CHUNK, W_A), BF16),
        grid=(batch // ns,),
        in_specs=[col(0), col(1), col(2), col(3), cache, cache,
                  pl.BlockSpec((HEAD_PAIRS, BAND, LANES), lambda b: (0, 0, 0))],
        out_specs=pl.BlockSpec((ns * CHUNK, W_A), lambda b: (b, 0)),
        scratch_shapes=[pltpu.VMEM((ns, BAND, W_A), BF16),
                        pltpu.VMEM((ns, HEAD_PAIRS, LANES, BAND), BF16)],
        compiler_params=_params(("arbitrary",)),
        name="attn_sample",
    )(p2d, p2d, p2d, p2d, cache_k, cache_v, bias_single)


HGRN_GROUP_BLOCKS = 4
HGRN_GROUP = HGRN_GROUP_BLOCKS * GLA_BLOCK


def _block_cumsum(x):
    row = lax.broadcasted_iota(jnp.int32, x.shape, 0) & (GLA_BLOCK - 1)
    s = 1
    while s < GLA_BLOCK:
        x = x + jnp.where(row >= s, pltpu.roll(x, s, 0), 0.0)
        s *= 2
    return x


def _hgrn_kernel(hq_ref, hf_ref, hi_ref, hg_ref, lb_ref, ng_ref, s0_ref,
                 o_ref, sout_ref, st_ref, qt_ref, kt_ref, kd_ref, qs_ref, q1_ref,
                 k1_ref, kdp_ref, v_ref, dec_ref):
    i = pl.program_id(1)
    nseq, seq_rows = hq_ref.shape[0], hq_ref.shape[1]
    rows = nseq * seq_rows
    nb = rows // GLA_BLOCK

    @pl.when(i == 0)
    def _():
        for s in range(nseq):
            for h in range(H_B):
                st_ref[s, h] = s0_ref[s, h].T

    def flat(ref):
        return ref[...].reshape(rows, W_B)

    lb = lb_ref[...]
    f = lb + (1.0 - lb) * _sigmoid(flat(hf_ref))
    b = _block_cumsum(jnp.log(f))
    b3 = b.reshape(nb, GLA_BLOCK, W_B)
    bl = jnp.broadcast_to(b3[:, GLA_BLOCK - 1:GLA_BLOCK, :],
                          (nb, GLA_BLOCK, W_B)).reshape(rows, W_B)
    blk = (lax.broadcasted_iota(jnp.int32, (rows, W_B), 0) // GLA_BLOCK) & (HGRN_GROUP_BLOCKS - 1)
    prev = [pltpu.roll(bl, d * GLA_BLOCK, 0) for d in range(1, HGRN_GROUP_BLOCKS)]
    nxt = [pltpu.roll(bl, rows - d * GLA_BLOCK, 0) for d in range(1, HGRN_GROUP_BLOCKS)]
    c_in = sum(jnp.where(blk >= d, prev[d - 1], 0.0) for d in range(1, HGRN_GROUP_BLOCKS))
    c_out = sum(jnp.where(blk <= HGRN_GROUP_BLOCKS - 1 - d, nxt[d - 1], 0.0)
                for d in range(1, HGRN_GROUP_BLOCKS))
    kk = 1.0 - f
    qt = _silu(flat(hq_ref)) * jnp.exp(b)
    kd = kk * jnp.exp(bl - b)
    qt_ref[...] = qt.astype(BF16)
    kt_ref[...] = (kk * jnp.exp(-b)).astype(BF16)
    kd_ref[...] = kd.astype(BF16)
    qs_ref[...] = (qt * jnp.exp(c_in)).astype(BF16)
    kdp_ref[...] = (kd * jnp.exp(c_out)).astype(BF16)
    q1_ref[...] = (qt * jnp.exp(jnp.where(blk == 3, prev[0], 0.0))).astype(BF16)
    k1_ref[...] = (kd * jnp.exp(jnp.where(blk == 0, nxt[0], 0.0))).astype(BF16)
    v_ref[...] = flat(hi_ref).astype(BF16)
    dec_ref[...] = jnp.exp(c_in + bl + c_out)

    r_i = lax.broadcasted_iota(jnp.int32, (HGRN_GROUP, HGRN_GROUP), 0)
    c_i = lax.broadcasted_iota(jnp.int32, (HGRN_GROUP, HGRN_GROUP), 1)
    rb, cb = r_i // GLA_BLOCK, c_i // GLA_BLOCK
    m_diag = (rb == cb) & (r_i >= c_i)
    m_adj = rb == cb + 1
    m_far = rb >= cb + 2

    for s in range(nseq):
        st = [st_ref[s, h] for h in range(H_B)]
        for g in range(seq_rows // HGRN_GROUP):
            r0 = s * seq_rows + g * HGRN_GROUP
            rs = slice(r0, r0 + HGRN_GROUP)
            for h in range(H_B):
                cs = slice(h * DK_B, (h + 1) * DK_B)
                qt_g = qt_ref[rs, cs]
                kd_g = kd_ref[rs, cs]
                v_g = v_ref[rs, cs]
                a = jnp.where(m_diag, _nt(qt_g, kt_ref[rs, cs]),
                              jnp.where(m_adj, _nt(qt_g, kd_g),
                                        jnp.where(m_far, _nt(q1_ref[rs, cs], k1_ref[rs, cs]), 0.0)))
                o = (_nt(qs_ref[rs, cs], st[h].astype(BF16))
                     + jnp.dot(a.astype(BF16), v_g, preferred_element_type=F32))
                ut = lax.dot_general(v_g, kdp_ref[rs, cs], (((0,), (0,)), ((), ())),
                                     preferred_element_type=F32)
                dec = dec_ref[r0:r0 + 8, cs]
                st[h] = (st[h].reshape(DV_B // 8, 8, DK_B) * dec[None]
                         + ut.reshape(DV_B // 8, 8, DK_B)).reshape(DV_B, DK_B)
                y = o * lax.rsqrt(jnp.mean(o * o, axis=-1, keepdims=True) + EPS) * ng_ref[:, cs]
                gs = slice(g * HGRN_GROUP, (g + 1) * HGRN_GROUP)
                o_ref[s, gs, cs] = (y * _silu(hg_ref[s, gs, cs])).astype(o_ref.dtype)
        for h in range(H_B):
            st_ref[s, h] = st[h]

    @pl.when(i == pl.num_programs(1) - 1)
    def _():
        for s in range(nseq):
            for h in range(H_B):
                sout_ref[s, h] = st_ref[s, h].T


def _hgrn(p3d, lb, ng, s0, nseq, rows, name):
    batch, seq, _ = p3d.shape
    assert batch % nseq == 0 and seq % rows == 0 and rows % HGRN_GROUP == 0

    def col(c):
        return pl.BlockSpec((nseq, rows, W_B), lambda b, i: (b, i, c))

    vec = pl.BlockSpec((1, W_B), lambda b, i: (0, 0))
    state = pl.BlockSpec((nseq, H_B, DK_B, DV_B), lambda b, i: (b, 0, 0, 0))
    n = nseq * rows
    return pl.pallas_call(
        _hgrn_kernel,
        out_shape=(jax.ShapeDtypeStruct((batch, seq, W_B), BF16),
                   jax.ShapeDtypeStruct((batch, H_B, DK_B, DV_B), F32)),
        grid=(batch // nseq, seq // rows),
        in_specs=[col(4), col(5), col(6), col(7), vec, vec, state],
        out_specs=(pl.BlockSpec((nseq, rows, W_B), lambda b, i: (b, i, 0)), state),
        scratch_shapes=([pltpu.VMEM((nseq, H_B, DV_B, DK_B), F32)]
                        + [pltpu.VMEM((n, W_B), BF16)] * 8
                        + [pltpu.VMEM((n, W_B), F32)]),
        compiler_params=_params(("arbitrary", "arbitrary")),
        name=name,
    )(p3d, p3d, p3d, p3d, lb, ng, s0)


OUTPROJ_ROWS = 512


def _outproj_kernel(oa_ref, ob_ref, x_ref, w_ref, g_ref, y_ref):
    h = (x_ref[...]
         + jnp.dot(oa_ref[...], w_ref[0:W_A, :], preferred_element_type=F32)
         + jnp.dot(ob_ref[...], w_ref[W_A:, :], preferred_element_type=F32))
    ms = jnp.mean(h * h, axis=-1, keepdims=True)
    y_ref[...] = h * lax.rsqrt(ms + EPS) * g_ref[...]


def _outproj(oa, ob, x2d, w_bf16, g, name):
    rows = x2d.shape[0]
    tm = min(OUTPROJ_ROWS, rows)
    return pl.pallas_call(
        _outproj_kernel,
        out_shape=jax.ShapeDtypeStruct((rows, D_MODEL), F32),
        grid=(rows // tm,),
        in_specs=[
            pl.BlockSpec((tm, W_A), lambda i: (i, 0)),
            pl.BlockSpec((tm, W_B), lambda i: (i, 0)),
            pl.BlockSpec((tm, D_MODEL), lambda i: (i, 0)),
            pl.BlockSpec((W_A + W_B, D_MODEL), lambda i: (0, 0)),
            pl.BlockSpec((1, D_MODEL), lambda i: (0, 0)),
        ],
        out_specs=pl.BlockSpec((tm, D_MODEL), lambda i: (i, 0)),
        compiler_params=_params(("arbitrary",)),
        name=name,
    )(oa, ob, x2d, w_bf16, g)


HGRN_PROMPT_ROWS = 256
HGRN_PROMPT_SEQS = 2
HGRN_SAMPLE_SEQS = 8


def kernel(x_prompt, x_sample, cache_attn_k, cache_attn_v, state_hgrn, ln_in_g, w_in,
           rel_bias, lb_gamma, hg_norm_g, w_out, ln_f_g):
    batch, seq, _ = x_prompt.shape
    dec_batch, dec_seq, _ = x_sample.shape
    depth = w_in.shape[0]
    assert depth == 1 and dec_seq == CHUNK and PAST_LEN % CHUNK == 0
    assert cache_attn_k.shape[2] == WINDOW_ROWS

    lb_all = jnp.cumsum(jax.nn.softmax(lb_gamma.astype(F32), axis=0), axis=0)
    lb = lb_all[0].reshape(1, W_B)
    ng = hg_norm_g[0].reshape(1, W_B)
    g_in = ln_in_g[0].reshape(1, D_MODEL)
    g_f = ln_f_g.reshape(1, D_MODEL)
    w_in_b = w_in[0].astype(BF16)
    w_out_b = w_out[0].astype(BF16)

    xp = x_prompt.reshape(batch * seq, D_MODEL)
    xs = x_sample.reshape(dec_batch * dec_seq, D_MODEL)

    bias_pair, bias_single = _bias_tables(rel_bias[0])

    pp = _inproj(xp, g_in, w_in_b, "inproj_prompt")
    ps = _inproj(xs, g_in, w_in_b, "inproj_sample")

    oa_p = _attn_prompt(pp, bias_pair, batch, seq)
    ck = cache_attn_k[0].reshape(dec_batch, WINDOW_ROWS, W_A)
    cv = cache_attn_v[0].reshape(dec_batch, WINDOW_ROWS, W_A)
    oa_s = _attn_sample(ps, ck, cv, bias_single, dec_batch)

    s0_p = jnp.zeros((batch, H_B, DK_B, DV_B), F32)
    ob_p, st_p = _hgrn(pp.reshape(batch, seq, N_IN), lb, ng, s0_p,
                       HGRN_PROMPT_SEQS, HGRN_PROMPT_ROWS, "hgrn_prompt")
    ob_s, st_s = _hgrn(ps.reshape(dec_batch, dec_seq, N_IN), lb, ng, state_hgrn[0],
                       HGRN_SAMPLE_SEQS, dec_seq, "hgrn_sample")
    ob_p = ob_p.reshape(batch * seq, W_B)
    ob_s = ob_s.reshape(dec_batch * dec_seq, W_B)

    y_p = _outproj(oa_p, ob_p, xp, w_out_b, g_f, "outproj_prompt").reshape(batch, seq, D_MODEL)
    y_s = _outproj(oa_s, ob_s, xs, w_out_b, g_f, "outproj_sample").reshape(
        dec_batch, dec_seq, D_MODEL)

    rows_p = min(WINDOW_ROWS, seq)
    pp3 = pp.reshape(batch, seq, N_IN)
    k_p = pp3[:, seq - rows_p:, W_A:2 * W_A].reshape(1, batch, rows_p, H_A, DH_A)
    v_p = pp3[:, seq - rows_p:, 2 * W_A:3 * W_A].reshape(1, batch, rows_p, H_A, DH_A)
    ps3 = ps.reshape(dec_batch, dec_seq, N_IN)
    k_s = ps3[:, :, W_A:2 * W_A].reshape(1, dec_batch, dec_seq, H_A, DH_A)
    v_s = ps3[:, :, 2 * W_A:3 * W_A].reshape(1, dec_batch, dec_seq, H_A, DH_A)

    return (y_p, y_s, k_p, v_p, st_p[None], k_s, v_s, st_s[None])
```

```python
import jax
import jax.numpy as jnp
from jax import lax
from jax.experimental import pallas as pl
from jax.experimental.pallas import tpu as pltpu

F32 = jnp.float32
BF16 = jnp.bfloat16

D_MODEL = 1024
CHUNK = 64
LEFT_CHUNKS = 8
WINDOW_ROWS = LEFT_CHUNKS * CHUNK
BAND = WINDOW_ROWS + CHUNK
W_A = 512
H_A = 8
DH_A = 64
W_B = 512
H_B = 4
DK_B = 128
DV_B = 128
REL_CLIP = 128
GLA_BLOCK = 16
EPS = 1e-6
ATTN_SCALE = DH_A ** -0.5
NEG_INF = -1e30
N_IN = 4 * W_A + 4 * W_B
PAST_LEN = 2048

LANES = 128
HEAD_PAIRS = H_A // 2
REL_TABLE = 2 * REL_CLIP + 1
REL_TABLE_PAD = 384
TOEPLITZ_LEN = 640
VMEM_LIMIT = 56 * 1024 * 1024


def _params(semantics):
    return pltpu.CompilerParams(dimension_semantics=semantics,
                                vmem_limit_bytes=VMEM_LIMIT)


def _silu(x):
    return x * (1.0 / (1.0 + jnp.exp(-x)))


def _sigmoid(x):
    return 1.0 / (1.0 + jnp.exp(-x))


def _nt(a, b):
    return lax.dot_general(a, b, (((1,), (1,)), ((), ())), preferred_element_type=F32)


def _emit_ahead(n_units, depth, first_fn, second_fn):
    pending = {}
    for t in range(n_units + depth):
        if t < n_units:
            pending[t] = first_fn(t)
        if t >= depth:
            second_fn(t - depth, pending.pop(t - depth))


INPROJ_ROWS = 512
INPROJ_COLS = 512


def _inproj_kernel(x_ref, g_ref, w_ref, o_ref):
    x = x_ref[...]
    ms = jnp.mean(x * x, axis=-1, keepdims=True)
    xn = (x * lax.rsqrt(ms + EPS) * g_ref[...]).astype(BF16)
    for n0 in range(0, N_IN, INPROJ_COLS):
        o_ref[:, n0:n0 + INPROJ_COLS] = jnp.dot(
            xn, w_ref[:, n0:n0 + INPROJ_COLS], preferred_element_type=F32)


def _inproj(x2d, g, w_bf16, name):
    rows = x2d.shape[0]
    tm = min(INPROJ_ROWS, rows)
    return pl.pallas_call(
        _inproj_kernel,
        out_shape=jax.ShapeDtypeStruct((rows, N_IN), F32),
        grid=(rows // tm,),
        in_specs=[
            pl.BlockSpec((tm, D_MODEL), lambda i: (i, 0)),
            pl.BlockSpec((1, D_MODEL), lambda i: (0, 0)),
            pl.BlockSpec((D_MODEL, N_IN), lambda i: (0, 0)),
        ],
        out_specs=pl.BlockSpec((tm, N_IN), lambda i: (i, 0)),
        compiler_params=_params(("arbitrary",)),
        name=name,
    )(x2d, g, w_bf16)


PAIR_BAND = BAND + CHUNK


def _bias_kernel(rb_ref, pair_ref, single_ref):
    m = lax.broadcasted_iota(jnp.int32, (REL_TABLE_PAD, TOEPLITZ_LEN), 1)
    t = lax.broadcasted_iota(jnp.int32, (REL_TABLE_PAD, TOEPLITZ_LEN), 0)
    idx = jnp.clip(BAND - 1 - m, -REL_CLIP, REL_CLIP) + REL_CLIP
    onehot = jnp.where(idx == t, 1.0, 0.0).astype(BF16)
    rb = rb_ref[...]
    hi = rb.astype(BF16)
    r1 = rb - hi.astype(F32)
    mid = r1.astype(BF16)
    lo = (r1 - mid.astype(F32)).astype(BF16)
    toep = (jnp.dot(hi, onehot, preferred_element_type=F32)
            + jnp.dot(mid, onehot, preferred_element_type=F32)
            + jnp.dot(lo, onehot, preferred_element_type=F32))
    key = lax.broadcasted_iota(jnp.int32, (CHUNK, TOEPLITZ_LEN), 1)
    for p in range(HEAD_PAIRS):
        for u in range(2):
            halves = []
            for h in (2 * p, 2 * p + 1):
                rows = jnp.broadcast_to(toep[h:h + 1, :], (CHUNK, TOEPLITZ_LEN))
                shift = (TOEPLITZ_LEN - (CHUNK - 1) + CHUNK * u) % TOEPLITZ_LEN
                rolled = pltpu.roll(rows, shift, 1, stride=1, stride_axis=0)
                seen = (key >= CHUNK * u) & (key < CHUNK * u + BAND)
                halves.append(jnp.where(seen, rolled, NEG_INF))
            tile_t = jnp.concatenate(halves, axis=0).T
            pair_ref[p, :, u * LANES:(u + 1) * LANES] = tile_t
            if u == 0:
                single_ref[p] = tile_t[:BAND, :]


def _bias_tables(rel_bias_l):
    rb = jnp.pad(rel_bias_l, ((0, 0), (0, REL_TABLE_PAD - REL_TABLE)))
    return pl.pallas_call(
        _bias_kernel,
        out_shape=(jax.ShapeDtypeStruct((HEAD_PAIRS, PAIR_BAND, 2 * LANES), F32),
                   jax.ShapeDtypeStruct((HEAD_PAIRS, BAND, LANES), F32)),
        name="rel_bias_table",
    )(rb)


def _attn_probs(qs, k2, bias_t, valid_from):
    first = lax.broadcasted_iota(jnp.int32, (CHUNK, LANES), 1) < DH_A
    parts = []
    for q in qs:
        q = q * ATTN_SCALE
        parts += [jnp.where(first, q, 0.0), jnp.where(first, 0.0, q)]
    qbd = jnp.concatenate(parts, axis=0).astype(BF16)
    s = _nt(k2, qbd) + bias_t
    if valid_from is not None:
        key = lax.broadcasted_iota(jnp.int32, s.shape, 0)
        s = jnp.where(key >= valid_from, s, NEG_INF)
    mx = jnp.max(s, axis=0, keepdims=True)
    e = jnp.exp(s - mx)
    return e.astype(BF16), jnp.sum(e, axis=0, keepdims=True)


def _attn_apply(probs, ags, vt):
    e, denom = probs
    first = lax.broadcasted_iota(jnp.int32, (CHUNK, LANES), 1) < DH_A
    ot = jnp.dot(vt, e, preferred_element_type=F32)
    o2 = (ot * (1.0 / denom)).T
    outs = []
    for u, ag in enumerate(ags):
        r = 2 * CHUNK * u
        o = jnp.where(first, o2[r:r + CHUNK], o2[r + CHUNK:r + 2 * CHUNK])
        outs.append(o * _silu(ag))
    return outs


ATTN_PIPE_DEPTH = 3


def _attn_pipeline(n_units, probs_fn, apply_fn):
    _emit_ahead(n_units, ATTN_PIPE_DEPTH, probs_fn, apply_fn)


ATTN_ROWS = 512


def _attn_prompt_kernel(q_ref, k_ref, v_ref, ag_ref, bias_ref, o_ref, kh_ref, vth_ref):
    i = pl.program_id(1)

    @pl.when(i == 0)
    def _():
        kh_ref[0:WINDOW_ROWS, :] = jnp.zeros((WINDOW_ROWS, W_A), BF16)
        vth_ref[:, :, 0:WINDOW_ROWS] = jnp.zeros((HEAD_PAIRS, LANES, WINDOW_ROWS), BF16)

    kh_ref[WINDOW_ROWS:, :] = k_ref[...].astype(BF16)
    for p in range(HEAD_PAIRS):
        vth_ref[p, :, WINDOW_ROWS:] = v_ref[:, p * LANES:(p + 1) * LANES].T.astype(BF16)

    def body(first_step):
        def unit(t):
            cp, p = divmod(t, HEAD_PAIRS)
            r0 = cp * 2 * CHUNK
            rs = [slice(r0 + u * CHUNK, r0 + (u + 1) * CHUNK) for u in range(2)]
            return cp, p, r0, rs, slice(p * LANES, (p + 1) * LANES)

        def probs(t):
            cp, p, r0, rs, cs = unit(t)
            valid_from = (LEFT_CHUNKS - 2 * cp) * CHUNK if first_step else None
            return _attn_probs([q_ref[r, cs] for r in rs], kh_ref[r0:r0 + PAIR_BAND, cs],
                               bias_ref[p], valid_from)

        def apply(t, pr):
            cp, p, r0, rs, cs = unit(t)
            outs = _attn_apply(pr, [ag_ref[r, cs] for r in rs],
                               vth_ref[p, :, r0:r0 + PAIR_BAND])
            for r, o in zip(rs, outs):
                o_ref[r, cs] = o.astype(o_ref.dtype)

        _attn_pipeline((ATTN_ROWS // (2 * CHUNK)) * HEAD_PAIRS, probs, apply)

    @pl.when(i == 0)
    def _():
        body(True)

    @pl.when(i != 0)
    def _():
        body(False)

    kh_ref[0:WINDOW_ROWS, :] = kh_ref[ATTN_ROWS:ATTN_ROWS + WINDOW_ROWS, :]
    vth_ref[:, :, 0:WINDOW_ROWS] = vth_ref[:, :, ATTN_ROWS:ATTN_ROWS + WINDOW_ROWS]


def _attn_prompt(p2d, bias_pair, batch, seq):
    nt = seq // ATTN_ROWS

    def col(c):
        return pl.BlockSpec((ATTN_ROWS, W_A), lambda b, i: (b * nt + i, c))

    return pl.pallas_call(
        _attn_prompt_kernel,
        out_shape=jax.ShapeDtypeStruct((batch * seq, W_A), BF16),
        grid=(batch, nt),
        in_specs=[col(0), col(1), col(2), col(3),
                  pl.BlockSpec((HEAD_PAIRS, PAIR_BAND, 2 * LANES), lambda b, i: (0, 0, 0))],
        out_specs=pl.BlockSpec((ATTN_ROWS, W_A), lambda b, i: (b * nt + i, 0)),
        scratch_shapes=[pltpu.VMEM((WINDOW_ROWS + ATTN_ROWS, W_A), BF16),
                        pltpu.VMEM((HEAD_PAIRS, LANES, WINDOW_ROWS + ATTN_ROWS), BF16)],
        compiler_params=_params(("arbitrary", "arbitrary")),
        name="attn_prompt",
    )(p2d, p2d, p2d, p2d, bias_pair)


ATTN_SAMPLE_SEQS = 4


def _attn_sample_kernel(q_ref, k_ref, v_ref, ag_ref, ck_ref, cv_ref, bias_ref, o_ref,
                        kh_ref, vth_ref):
    for s in range(ATTN_SAMPLE_SEQS):
        rs = slice(s * CHUNK, (s + 1) * CHUNK)
        kh_ref[s, 0:WINDOW_ROWS, :] = ck_ref[s].astype(BF16)
        kh_ref[s, WINDOW_ROWS:, :] = k_ref[rs, :].astype(BF16)
        for p in range(HEAD_PAIRS):
            cs = slice(p * LANES, (p + 1) * LANES)
            vth_ref[s, p, :, 0:WINDOW_ROWS] = cv_ref[s, :, cs].T.astype(BF16)
            vth_ref[s, p, :, WINDOW_ROWS:] = v_ref[rs, cs].T.astype(BF16)

    def unit(t):
        s, p = divmod(t, HEAD_PAIRS)
        return s, p, slice(s * CHUNK, (s + 1) * CHUNK), slice(p * LANES, (p + 1) * LANES)

    def probs(t):
        s, p, rs, cs = unit(t)
        return _attn_probs([q_ref[rs, cs]], kh_ref[s, :, cs], bias_ref[p], None)

    def apply(t, pr):
        s, p, rs, cs = unit(t)
        (o,) = _attn_apply(pr, [ag_ref[rs, cs]], vth_ref[s, p])
        o_ref[rs, cs] = o.astype(o_ref.dtype)

    _attn_pipeline(ATTN_SAMPLE_SEQS * HEAD_PAIRS, probs, apply)


def _attn_sample(p2d, cache_k, cache_v, bias_single, batch):
    ns = ATTN_SAMPLE_SEQS

    def col(c):
        return pl.BlockSpec((ns * CHUNK, W_A), lambda b: (b, c))

    cache = pl.BlockSpec((ns, WINDOW_ROWS, W_A), lambda b: (b, 0, 0))
    return pl.pallas_call(
        _attn_sample_kernel,
        out_shape=jax.ShapeDtypeStruct((batch * CHUNK, W_A), BF16),
        grid=(batch // ns,),
        in_specs=[col(0), col(1), col(2), col(3), cache, cache,
                  pl.BlockSpec((HEAD_PAIRS, BAND, LANES), lambda b: (0, 0, 0))],
        out_specs=pl.BlockSpec((ns * CHUNK, W_A), lambda b: (b, 0)),
        scratch_shapes=[pltpu.VMEM((ns, BAND, W_A), BF16),
                        pltpu.VMEM((ns, HEAD_PAIRS, LANES, BAND), BF16)],
        compiler_params=_params(("arbitrary",)),
        name="attn_sample",
    )(p2d, p2d, p2d, p2d, cache_k, cache_v, bias_single)


HGRN_GROUP_BLOCKS = 4
HGRN_GROUP = HGRN_GROUP_BLOCKS * GLA_BLOCK
HGRN_PIPE_DEPTH = 4


def _block_cumsum(x):
    row = lax.broadcasted_iota(jnp.int32, x.shape, 0) & (GLA_BLOCK - 1)
    s = 1
    while s < GLA_BLOCK:
        x = x + jnp.where(row >= s, pltpu.roll(x, s, 0), 0.0)
        s *= 2
    return x


def _hgrn_kernel(hq_ref, hf_ref, hi_ref, hg_ref, lb_ref, ng_ref, s0_ref,
                 o_ref, sout_ref, st_ref, qt_ref, kt_ref, kd_ref, qs_ref, q1_ref,
                 k1_ref, kdp_ref, v_ref, dec_ref):
    i = pl.program_id(1)
    nseq, seq_rows = hq_ref.shape[0], hq_ref.shape[1]
    rows = nseq * seq_rows
    nb = rows // GLA_BLOCK

    @pl.when(i == 0)
    def _():
        for s in range(nseq):
            for h in range(H_B):
                st_ref[s, h] = s0_ref[s, h].T

    def flat(ref):
        return ref[...].reshape(rows, W_B)

    lb = lb_ref[...]
    f = lb + (1.0 - lb) * _sigmoid(flat(hf_ref))
    b = _block_cumsum(jnp.log(f))
    b3 = b.reshape(nb, GLA_BLOCK, W_B)
    bl = jnp.broadcast_to(b3[:, GLA_BLOCK - 1:GLA_BLOCK, :],
                          (nb, GLA_BLOCK, W_B)).reshape(rows, W_B)
    blk = (lax.broadcasted_iota(jnp.int32, (rows, W_B), 0) // GLA_BLOCK) & (HGRN_GROUP_BLOCKS - 1)
    prev = [pltpu.roll(bl, d * GLA_BLOCK, 0) for d in range(1, HGRN_GROUP_BLOCKS)]
    nxt = [pltpu.roll(bl, rows - d * GLA_BLOCK, 0) for d in range(1, HGRN_GROUP_BLOCKS)]
    c_in = sum(jnp.where(blk >= d, prev[d - 1], 0.0) for d in range(1, HGRN_GROUP_BLOCKS))
    c_out = sum(jnp.where(blk <= HGRN_GROUP_BLOCKS - 1 - d, nxt[d - 1], 0.0)
                for d in range(1, HGRN_GROUP_BLOCKS))
    kk = 1.0 - f
    qt = _silu(flat(hq_ref)) * jnp.exp(b)
    kd = kk * jnp.exp(bl - b)
    qt_ref[...] = qt.astype(BF16)
    kt_ref[...] = (kk * jnp.exp(-b)).astype(BF16)
    kd_ref[...] = kd.astype(BF16)
    qs_ref[...] = (qt * jnp.exp(c_in)).astype(BF16)
    kdp_ref[...] = (kd * jnp.exp(c_out)).astype(BF16)
    q1_ref[...] = (qt * jnp.exp(jnp.where(blk == 3, prev[0], 0.0))).astype(BF16)
    k1_ref[...] = (kd * jnp.exp(jnp.where(blk == 0, nxt[0], 0.0))).astype(BF16)
    v_ref[...] = flat(hi_ref).astype(BF16)
    dec_ref[...] = jnp.exp(c_in + bl + c_out)

    r_i = lax.broadcasted_iota(jnp.int32, (HGRN_GROUP, HGRN_GROUP), 0)
    c_i = lax.broadcasted_iota(jnp.int32, (HGRN_GROUP, HGRN_GROUP), 1)
    rb, cb = r_i // GLA_BLOCK, c_i // GLA_BLOCK
    m_diag = (rb == cb) & (r_i >= c_i)
    m_adj = rb == cb + 1
    m_far = rb >= cb + 2

    groups = seq_rows // HGRN_GROUP
    st = {(s, h): st_ref[s, h] for s in range(nseq) for h in range(H_B)}

    def unit(t):
        sg, h = divmod(t, H_B)
        s, g = divmod(sg, groups)
        r0 = s * seq_rows + g * HGRN_GROUP
        return s, g, h, r0, slice(r0, r0 + HGRN_GROUP), slice(h * DK_B, (h + 1) * DK_B)

    def local(t):
        s, g, h, r0, rs, cs = unit(t)
        qt_g = qt_ref[rs, cs]
        a = jnp.where(m_diag, _nt(qt_g, kt_ref[rs, cs]),
                      jnp.where(m_adj, _nt(qt_g, kd_ref[rs, cs]),
                                jnp.where(m_far, _nt(q1_ref[rs, cs], k1_ref[rs, cs]), 0.0)))
        ut = lax.dot_general(v_ref[rs, cs], kdp_ref[rs, cs], (((0,), (0,)), ((), ())),
                             preferred_element_type=F32)
        return a.astype(BF16), ut

    def carry(t, loc):
        s, g, h, r0, rs, cs = unit(t)
        a, ut = loc
        o = (_nt(qs_ref[rs, cs], st[s, h].astype(BF16))
             + jnp.dot(a, v_ref[rs, cs], preferred_element_type=F32))
        dec = dec_ref[r0:r0 + 8, cs]
        st[s, h] = (st[s, h].reshape(DV_B // 8, 8, DK_B) * dec[None]
                    + ut.reshape(DV_B // 8, 8, DK_B)).reshape(DV_B, DK_B)
        y = o * lax.rsqrt(jnp.mean(o * o, axis=-1, keepdims=True) + EPS) * ng_ref[:, cs]
        gs = slice(g * HGRN_GROUP, (g + 1) * HGRN_GROUP)
        o_ref[s, gs, cs] = (y * _silu(hg_ref[s, gs, cs])).astype(o_ref.dtype)

    _emit_ahead(nseq * groups * H_B, HGRN_PIPE_DEPTH, local, carry)
    for (s, h), val in st.items():
        st_ref[s, h] = val

    @pl.when(i == pl.num_programs(1) - 1)
    def _():
        for s in range(nseq):
            for h in range(H_B):
                sout_ref[s, h] = st_ref[s, h].T


def _hgrn(p3d, lb, ng, s0, nseq, rows, name):
    batch, seq, _ = p3d.shape
    assert batch % nseq == 0 and seq % rows == 0 and rows % HGRN_GROUP == 0

    def col(c):
        return pl.BlockSpec((nseq, rows, W_B), lambda b, i: (b, i, c))

    vec = pl.BlockSpec((1, W_B), lambda b, i: (0, 0))
    state = pl.BlockSpec((nseq, H_B, DK_B, DV_B), lambda b, i: (b, 0, 0, 0))
    n = nseq * rows
    return pl.pallas_call(
        _hgrn_kernel,
        out_shape=(jax.ShapeDtypeStruct((batch, seq, W_B), BF16),
                   jax.ShapeDtypeStruct((batch, H_B, DK_B, DV_B), F32)),
        grid=(batch // nseq, seq // rows),
        in_specs=[col(4), col(5), col(6), col(7), vec, vec, state],
        out_specs=(pl.BlockSpec((nseq, rows, W_B), lambda b, i: (b, i, 0)), state),
        scratch_shapes=([pltpu.VMEM((nseq, H_B, DV_B, DK_B), F32)]
                        + [pltpu.VMEM((n, W_B), BF16)] * 8
                        + [pltpu.VMEM((n, W_B), F32)]),
        compiler_params=_params(("arbitrary", "arbitrary")),
        name=name,
    )(p3d, p3d, p3d, p3d, lb, ng, s0)


OUTPROJ_ROWS = 512


def _outproj_kernel(oa_ref, ob_ref, x_ref, w_ref, g_ref, y_ref):
    h = (x_ref[...]
         + jnp.dot(oa_ref[...], w_ref[0:W_A, :], preferred_element_type=F32)
         + jnp.dot(ob_ref[...], w_ref[W_A:, :], preferred_element_type=F32))
    ms = jnp.mean(h * h, axis=-1, keepdims=True)
    y_ref[...] = h * lax.rsqrt(ms + EPS) * g_ref[...]


def _outproj(oa, ob, x2d, w_bf16, g, name):
    rows = x2d.shape[0]
    tm = min(OUTPROJ_ROWS, rows)
    return pl.pallas_call(
        _outproj_kernel,
        out_shape=jax.ShapeDtypeStruct((rows, D_MODEL), F32),
        grid=(rows // tm,),
        in_specs=[
            pl.BlockSpec((tm, W_A), lambda i: (i, 0)),
            pl.BlockSpec((tm, W_B), lambda i: (i, 0)),
            pl.BlockSpec((tm, D_MODEL), lambda i: (i, 0)),
            pl.BlockSpec((W_A + W_B, D_MODEL), lambda i: (0, 0)),
            pl.BlockSpec((1, D_MODEL), lambda i: (0, 0)),
        ],
        out_specs=pl.BlockSpec((tm, D_MODEL), lambda i: (i, 0)),
        compiler_params=_params(("arbitrary",)),
        name=name,
    )(oa, ob, x2d, w_bf16, g)


HGRN_PROMPT_ROWS = 256
HGRN_PROMPT_SEQS = 2
HGRN_SAMPLE_SEQS = 8


def kernel(x_prompt, x_sample, cache_attn_k, cache_attn_v, state_hgrn, ln_in_g, w_in,
           rel_bias, lb_gamma, hg_norm_g, w_out, ln_f_g):
    batch, seq, _ = x_prompt.shape
    dec_batch, dec_seq, _ = x_sample.shape
    depth = w_in.shape[0]
    assert depth == 1 and dec_seq == CHUNK and PAST_LEN % CHUNK == 0
    assert cache_attn_k.shape[2] == WINDOW_ROWS

    lb_all = jnp.cumsum(jax.nn.softmax(lb_gamma.astype(F32), axis=0), axis=0)
    lb = lb_all[0].reshape(1, W_B)
    ng = hg_norm_g[0].reshape(1, W_B)
    g_in = ln_in_g[0].reshape(1, D_MODEL)
    g_f = ln_f_g.reshape(1, D_MODEL)
    w_in_b = w_in[0].astype(BF16)
    w_out_b = w_out[0].astype(BF16)

    xp = x_prompt.reshape(batch * seq, D_MODEL)
    xs = x_sample.reshape(dec_batch * dec_seq, D_MODEL)

    bias_pair, bias_single = _bias_tables(rel_bias[0])

    pp = _inproj(xp, g_in, w_in_b, "inproj_prompt")
    ps = _inproj(xs, g_in, w_in_b, "inproj_sample")

    oa_p = _attn_prompt(pp, bias_pair, batch, seq)
    ck = cache_attn_k[0].reshape(dec_batch, WINDOW_ROWS, W_A)
    cv = cache_attn_v[0].reshape(dec_batch, WINDOW_ROWS, W_A)
    oa_s = _attn_sample(ps, ck, cv, bias_single, dec_batch)

    s0_p = jnp.zeros((batch, H_B, DK_B, DV_B), F32)
    ob_p, st_p = _hgrn(pp.reshape(batch, seq, N_IN), lb, ng, s0_p,
                       HGRN_PROMPT_SEQS, HGRN_PROMPT_ROWS, "hgrn_prompt")
    ob_s, st_s = _hgrn(ps.reshape(dec_batch, dec_seq, N_IN), lb, ng, state_hgrn[0],
                       HGRN_SAMPLE_SEQS, dec_seq, "hgrn_sample")
    ob_p = ob_p.reshape(batch * seq, W_B)
    ob_s = ob_s.reshape(dec_batch * dec_seq, W_B)

    y_p = _outproj(oa_p, ob_p, xp, w_out_b, g_f, "outproj_prompt").reshape(batch, seq, D_MODEL)
    y_s = _outproj(oa_s, ob_s, xs, w_out_b, g_f, "outproj_sample").reshape(
        dec_batch, dec_seq, D_MODEL)

    rows_p = min(WINDOW_ROWS, seq)
    pp3 = pp.reshape(batch, seq, N_IN)
    k_p = pp3[:, seq - rows_p:, W_A:2 * W_A].reshape(1, batch, rows_p, H_A, DH_A)
    v_p = pp3[:, seq - rows_p:, 2 * W_A:3 * W_A].reshape(1, batch, rows_p, H_A, DH_A)
    ps3 = ps.reshape(dec_batch, dec_seq, N_IN)
    k_s = ps3[:, :, W_A:2 * W_A].reshape(1, dec_batch, dec_seq, H_A, DH_A)
    v_s = ps3[:, :, 2 * W_A:3 * W_A].reshape(1, dec_batch, dec_seq, H_A, DH_A)

    return (y_p, y_s, k_p, v_p, st_p[None], k_s, v_s, st_s[None])
```

```python
import jax
import jax.numpy as jnp
from jax import lax
from jax.experimental import pallas as pl
from jax.experimental.pallas import tpu as pltpu

F32 = jnp.float32
BF16 = jnp.bfloat16

D_MODEL = 1024
CHUNK = 64
LEFT_CHUNKS = 8
WINDOW_ROWS = LEFT_CHUNKS * CHUNK
BAND = WINDOW_ROWS + CHUNK
W_A = 512
H_A = 8
DH_A = 64
W_B = 512
H_B = 4
DK_B = 128
DV_B = 128
REL_CLIP = 128
GLA_BLOCK = 16
EPS = 1e-6
ATTN_SCALE = DH_A ** -0.5
NEG_INF = -1e30
N_IN = 4 * W_A + 4 * W_B
PAST_LEN = 2048

LANES = 128
HEAD_PAIRS = H_A // 2
REL_TABLE = 2 * REL_CLIP + 1
REL_TABLE_PAD = 384
TOEPLITZ_LEN = 640
VMEM_LIMIT = 56 * 1024 * 1024


def _params(semantics):
    return pltpu.CompilerParams(dimension_semantics=semantics,
                                vmem_limit_bytes=VMEM_LIMIT)


def _silu(x):
    return x * (1.0 / (1.0 + jnp.exp(-x)))


def _sigmoid(x):
    return 1.0 / (1.0 + jnp.exp(-x))


def _nt(a, b):
    return lax.dot_general(a, b, (((1,), (1,)), ((), ())), preferred_element_type=F32)


def _emit_ahead(n_units, depth, first_fn, second_fn):
    pending = {}
    for t in range(n_units + depth):
        if t < n_units:
            pending[t] = first_fn(t)
        if t >= depth:
            second_fn(t - depth, pending.pop(t - depth))


INPROJ_ROWS = 512
INPROJ_COLS = 512


def _inproj_kernel(x_ref, g_ref, w_ref, o_ref):
    x = x_ref[...]
    ms = jnp.mean(x * x, axis=-1, keepdims=True)
    xn = (x * lax.rsqrt(ms + EPS) * g_ref[...]).astype(BF16)
    for n0 in range(0, N_IN, INPROJ_COLS):
        o_ref[:, n0:n0 + INPROJ_COLS] = jnp.dot(
            xn, w_ref[:, n0:n0 + INPROJ_COLS], preferred_element_type=F32)


def _inproj(x2d, g, w_bf16, name):
    rows = x2d.shape[0]
    tm = min(INPROJ_ROWS, rows)
    return pl.pallas_call(
        _inproj_kernel,
        out_shape=jax.ShapeDtypeStruct((rows, N_IN), F32),
        grid=(rows // tm,),
        in_specs=[
            pl.BlockSpec((tm, D_MODEL), lambda i: (i, 0)),
            pl.BlockSpec((1, D_MODEL), lambda i: (0, 0)),
            pl.BlockSpec((D_MODEL, N_IN), lambda i: (0, 0)),
        ],
        out_specs=pl.BlockSpec((tm, N_IN), lambda i: (i, 0)),
        compiler_params=_params(("arbitrary",)),
        name=name,
    )(x2d, g, w_bf16)


PAIR_BAND = BAND + CHUNK


def _bias_kernel(rb_ref, pair_ref, single_ref):
    m = lax.broadcasted_iota(jnp.int32, (REL_TABLE_PAD, TOEPLITZ_LEN), 1)
    t = lax.broadcasted_iota(jnp.int32, (REL_TABLE_PAD, TOEPLITZ_LEN), 0)
    idx = jnp.clip(BAND - 1 - m, -REL_CLIP, REL_CLIP) + REL_CLIP
    onehot = jnp.where(idx == t, 1.0, 0.0).astype(BF16)
    rb = rb_ref[...]
    hi = rb.astype(BF16)
    r1 = rb - hi.astype(F32)
    mid = r1.astype(BF16)
    lo = (r1 - mid.astype(F32)).astype(BF16)
    toep = (jnp.dot(hi, onehot, preferred_element_type=F32)
            + jnp.dot(mid, onehot, preferred_element_type=F32)
            + jnp.dot(lo, onehot, preferred_element_type=F32))
    key = lax.broadcasted_iota(jnp.int32, (CHUNK, TOEPLITZ_LEN), 1)
    for p in range(HEAD_PAIRS):
        for u in range(2):
            halves = []
            for h in (2 * p, 2 * p + 1):
                rows = jnp.broadcast_to(toep[h:h + 1, :], (CHUNK, TOEPLITZ_LEN))
                shift = (TOEPLITZ_LEN - (CHUNK - 1) + CHUNK * u) % TOEPLITZ_LEN
                rolled = pltpu.roll(rows, shift, 1, stride=1, stride_axis=0)
                seen = (key >= CHUNK * u) & (key < CHUNK * u + BAND)
                halves.append(jnp.where(seen, rolled, NEG_INF))
            tile_t = jnp.concatenate(halves, axis=0).T
            pair_ref[p, :, u * LANES:(u + 1) * LANES] = tile_t
            if u == 0:
                single_ref[p] = tile_t[:BAND, :]


def _bias_tables(rel_bias_l):
    rb = jnp.pad(rel_bias_l, ((0, 0), (0, REL_TABLE_PAD - REL_TABLE)))
    return pl.pallas_call(
        _bias_kernel,
        out_shape=(jax.ShapeDtypeStruct((HEAD_PAIRS, PAIR_BAND, 2 * LANES), F32),
                   jax.ShapeDtypeStruct((HEAD_PAIRS, BAND, LANES), F32)),
        name="rel_bias_table",
    )(rb)


def _attn_probs(qs, k2, bias_t, valid_from):
    first = lax.broadcasted_iota(jnp.int32, (CHUNK, LANES), 1) < DH_A
    parts = []
    for q in qs:
        q = q * ATTN_SCALE
        parts += [jnp.where(first, q, 0.0), jnp.where(first, 0.0, q)]
    qbd = jnp.concatenate(parts, axis=0).astype(BF16)
    s = _nt(k2, qbd) + bias_t
    if valid_from is not None:
        key = lax.broadcasted_iota(jnp.int32, s.shape, 0)
        s = jnp.where(key >= valid_from, s, NEG_INF)
    mx = jnp.max(s, axis=0, keepdims=True)
    e = jnp.exp(s - mx)
    return e.astype(BF16), jnp.sum(e, axis=0, keepdims=True)


def _attn_apply(probs, ags, vt):
    e, denom = probs
    first = lax.broadcasted_iota(jnp.int32, (CHUNK, LANES), 1) < DH_A
    ot = jnp.dot(vt, e, preferred_element_type=F32)
    o2 = (ot * (1.0 / denom)).T
    outs = []
    for u, ag in enumerate(ags):
        r = 2 * CHUNK * u
        o = jnp.where(first, o2[r:r + CHUNK], o2[r + CHUNK:r + 2 * CHUNK])
        outs.append(o * _silu(ag))
    return outs


ATTN_PIPE_DEPTH = 3


def _attn_pipeline(n_units, probs_fn, apply_fn):
    _emit_ahead(n_units, ATTN_PIPE_DEPTH, probs_fn, apply_fn)


ATTN_ROWS = 512


def _attn_prompt_kernel(q_ref, k_ref, v_ref, ag_ref, bias_ref, o_ref, kh_ref, vth_ref):
    i = pl.program_id(1)

    @pl.when(i == 0)
    def _():
        kh_ref[0:WINDOW_ROWS, :] = jnp.zeros((WINDOW_ROWS, W_A), BF16)
        vth_ref[:, :, 0:WINDOW_ROWS] = jnp.zeros((HEAD_PAIRS, LANES, WINDOW_ROWS), BF16)

    kh_ref[WINDOW_ROWS:, :] = k_ref[...].astype(BF16)
    for p in range(HEAD_PAIRS):
        vth_ref[p, :, WINDOW_ROWS:] = v_ref[:, p * LANES:(p + 1) * LANES].T.astype(BF16)

    def body(first_step):
        def unit(t):
            cp, p = divmod(t, HEAD_PAIRS)
            r0 = cp * 2 * CHUNK
            rs = [slice(r0 + u * CHUNK, r0 + (u + 1) * CHUNK) for u in range(2)]
            return cp, p, r0, rs, slice(p * LANES, (p + 1) * LANES)

        def probs(t):
            cp, p, r0, rs, cs = unit(t)
            valid_from = (LEFT_CHUNKS - 2 * cp) * CHUNK if first_step else None
            return _attn_probs([q_ref[r, cs] for r in rs], kh_ref[r0:r0 + PAIR_BAND, cs],
                               bias_ref[p], valid_from)

        def apply(t, pr):
            cp, p, r0, rs, cs = unit(t)
            outs = _attn_apply(pr, [ag_ref[r, cs] for r in rs],
                               vth_ref[p, :, r0:r0 + PAIR_BAND])
            for r, o in zip(rs, outs):
                o_ref[r, cs] = o.astype(o_ref.dtype)

        _attn_pipeline((ATTN_ROWS // (2 * CHUNK)) * HEAD_PAIRS, probs, apply)

    @pl.when(i == 0)
    def _():
        body(True)

    @pl.when(i != 0)
    def _():
        body(False)

    kh_ref[0:WINDOW_ROWS, :] = kh_ref[ATTN_ROWS:ATTN_ROWS + WINDOW_ROWS, :]
    vth_ref[:, :, 0:WINDOW_ROWS] = vth_ref[:, :, ATTN_ROWS:ATTN_ROWS + WINDOW_ROWS]


def _attn_prompt(p2d, bias_pair, batch, seq):
    nt = seq // ATTN_ROWS

    def col(c):
        return pl.BlockSpec((ATTN_ROWS, W_A), lambda b, i: (b * nt + i, c))

    return pl.pallas_call(
        _attn_prompt_kernel,
        out_shape=jax.ShapeDtypeStruct((batch * seq, W_A), BF16),
        grid=(batch, nt),
        in_specs=[col(0), col(1), col(2), col(3),
                  pl.BlockSpec((HEAD_PAIRS, PAIR_BAND, 2 * LANES), lambda b, i: (0, 0, 0))],
        out_specs=pl.BlockSpec((ATTN_ROWS, W_A), lambda b, i: (b * nt + i, 0)),
        scratch_shapes=[pltpu.VMEM((WINDOW_ROWS + ATTN_ROWS, W_A), BF16),
                        pltpu.VMEM((HEAD_PAIRS, LANES, WINDOW_ROWS + ATTN_ROWS), BF16)],
        compiler_params=_params(("arbitrary", "arbitrary")),
        name="attn_prompt",
    )(p2d, p2d, p2d, p2d, bias_pair)


ATTN_SAMPLE_SEQS = 4


def _attn_sample_kernel(q_ref, k_ref, v_ref, ag_ref, ck_ref, cv_ref, bias_ref, o_ref,
                        kh_ref, vth_ref):
    def head_pair(heads, p):
        return jnp.concatenate([heads[2 * p], heads[2 * p + 1]], axis=1)

    for s in range(ATTN_SAMPLE_SEQS):
        rs = slice(s * CHUNK, (s + 1) * CHUNK)
        kh_ref[s, WINDOW_ROWS:, :] = k_ref[rs, :].astype(BF16)
        for f0 in range(0, WINDOW_ROWS, LANES):
            fs = slice(f0, f0 + LANES)
            ck = jnp.swapaxes(ck_ref[s, fs], 0, 1)
            cv = jnp.swapaxes(cv_ref[s, fs], 0, 1)
            for p in range(HEAD_PAIRS):
                cs = slice(p * LANES, (p + 1) * LANES)
                kh_ref[s, fs, cs] = head_pair(ck, p).astype(BF16)
                vth_ref[s, p, :, fs] = head_pair(cv, p).T.astype(BF16)
        for p in range(HEAD_PAIRS):
            cs = slice(p * LANES, (p + 1) * LANES)
            vth_ref[s, p, :, WINDOW_ROWS:] = v_ref[rs, cs].T.astype(BF16)

    def unit(t):
        s, p = divmod(t, HEAD_PAIRS)
        return s, p, slice(s * CHUNK, (s + 1) * CHUNK), slice(p * LANES, (p + 1) * LANES)

    def probs(t):
        s, p, rs, cs = unit(t)
        return _attn_probs([q_ref[rs, cs]], kh_ref[s, :, cs], bias_ref[p], None)

    def apply(t, pr):
        s, p, rs, cs = unit(t)
        (o,) = _attn_apply(pr, [ag_ref[rs, cs]], vth_ref[s, p])
        o_ref[rs, cs] = o.astype(o_ref.dtype)

    _attn_pipeline(ATTN_SAMPLE_SEQS * HEAD_PAIRS, probs, apply)


def _attn_sample(p2d, cache_k, cache_v, bias_single, batch):
    ns = ATTN_SAMPLE_SEQS

    def col(c):
        return pl.BlockSpec((ns * CHUNK, W_A), lambda b: (b, c))

    cache = pl.BlockSpec((ns, WINDOW_ROWS, H_A, DH_A), lambda b: (b, 0, 0, 0))
    return pl.pallas_call(
        _attn_sample_kernel,
        out_shape=jax.ShapeDtypeStruct((batch * CHUNK, W_A), BF16),
        grid=(batch // ns,),
        in_specs=[col(0), col(1), col(2), col(3), cache, cache,
                  pl.BlockSpec((HEAD_PAIRS, BAND, LANES), lambda b: (0, 0, 0))],
        out_specs=pl.BlockSpec((ns * CHUNK, W_A), lambda b: (b, 0)),
        scratch_shapes=[pltpu.VMEM((ns, BAND, W_A), BF16),
                        pltpu.VMEM((ns, HEAD_PAIRS, LANES, BAND), BF16)],
        compiler_params=_params(("arbitrary",)),
        name="attn_sample",
    )(p2d, p2d, p2d, p2d, cache_k, cache_v, bias_single)


HGRN_GROUP_BLOCKS = 4
HGRN_GROUP = HGRN_GROUP_BLOCKS * GLA_BLOCK
HGRN_PIPE_DEPTH = 4


def _block_cumsum(x):
    row = lax.broadcasted_iota(jnp.int32, x.shape, 0) & (GLA_BLOCK - 1)
    s = 1
    while s < GLA_BLOCK:
        x = x + jnp.where(row >= s, pltpu.roll(x, s, 0), 0.0)
        s *= 2
    return x


def _hgrn_kernel(hq_ref, hf_ref, hi_ref, hg_ref, lb_ref, ng_ref, s0_ref,
                 o_ref, sout_ref, st_ref, qt_ref, kt_ref, kd_ref, qs_ref, q1_ref,
                 k1_ref, kdp_ref, v_ref, dec_ref):
    i = pl.program_id(1)
    nseq, seq_rows = hq_ref.shape[0], hq_ref.shape[1]
    rows = nseq * seq_rows
    nb = rows // GLA_BLOCK

    @pl.when(i == 0)
    def _():
        for s in range(nseq):
            for h in range(H_B):
                st_ref[s, h] = s0_ref[s, h].T

    def flat(ref):
        return ref[...].reshape(rows, W_B)

    lb = lb_ref[...]
    f = lb + (1.0 - lb) * _sigmoid(flat(hf_ref))
    b = _block_cumsum(jnp.log(f))
    b3 = b.reshape(nb, GLA_BLOCK, W_B)
    bl = jnp.broadcast_to(b3[:, GLA_BLOCK - 1:GLA_BLOCK, :],
                          (nb, GLA_BLOCK, W_B)).reshape(rows, W_B)
    blk = (lax.broadcasted_iota(jnp.int32, (rows, W_B), 0) // GLA_BLOCK) & (HGRN_GROUP_BLOCKS - 1)
    prev = [pltpu.roll(bl, d * GLA_BLOCK, 0) for d in range(1, HGRN_GROUP_BLOCKS)]
    nxt = [pltpu.roll(bl, rows - d * GLA_BLOCK, 0) for d in range(1, HGRN_GROUP_BLOCKS)]
    c_in = sum(jnp.where(blk >= d, prev[d - 1], 0.0) for d in range(1, HGRN_GROUP_BLOCKS))
    c_out = sum(jnp.where(blk <= HGRN_GROUP_BLOCKS - 1 - d, nxt[d - 1], 0.0)
                for d in range(1, HGRN_GROUP_BLOCKS))
    kk = 1.0 - f
    qt = _silu(flat(hq_ref)) * jnp.exp(b)
    kd = kk * jnp.exp(bl - b)
    qt_ref[...] = qt.astype(BF16)
    kt_ref[...] = (kk * jnp.exp(-b)).astype(BF16)
    kd_ref[...] = kd.astype(BF16)
    qs_ref[...] = (qt * jnp.exp(c_in)).astype(BF16)
    kdp_ref[...] = (kd * jnp.exp(c_out)).astype(BF16)
    q1_ref[...] = (qt * jnp.exp(jnp.where(blk == 3, prev[0], 0.0))).astype(BF16)
    k1_ref[...] = (kd * jnp.exp(jnp.where(blk == 0, nxt[0], 0.0))).astype(BF16)
    v_ref[...] = flat(hi_ref).astype(BF16)
    dec_ref[...] = jnp.exp(c_in + bl + c_out)

    r_i = lax.broadcasted_iota(jnp.int32, (HGRN_GROUP, HGRN_GROUP), 0)
    c_i = lax.broadcasted_iota(jnp.int32, (HGRN_GROUP, HGRN_GROUP), 1)
    rb, cb = r_i // GLA_BLOCK, c_i // GLA_BLOCK
    m_diag = (rb == cb) & (r_i >= c_i)
    m_adj = rb == cb + 1
    m_far = rb >= cb + 2

    groups = seq_rows // HGRN_GROUP
    st = {(s, h): st_ref[s, h] for s in range(nseq) for h in range(H_B)}

    def unit(t):
        sg, h = divmod(t, H_B)
        s, g = divmod(sg, groups)
        r0 = s * seq_rows + g * HGRN_GROUP
        return s, g, h, r0, slice(r0, r0 + HGRN_GROUP), slice(h * DK_B, (h + 1) * DK_B)

    def local(t):
        s, g, h, r0, rs, cs = unit(t)
        qt_g = qt_ref[rs, cs]
        a = jnp.where(m_diag, _nt(qt_g, kt_ref[rs, cs]),
                      jnp.where(m_adj, _nt(qt_g, kd_ref[rs, cs]),
                                jnp.where(m_far, _nt(q1_ref[rs, cs], k1_ref[rs, cs]), 0.0)))
        ut = lax.dot_general(v_ref[rs, cs], kdp_ref[rs, cs], (((0,), (0,)), ((), ())),
                             preferred_element_type=F32)
        return a.astype(BF16), ut

    def carry(t, loc):
        s, g, h, r0, rs, cs = unit(t)
        a, ut = loc
        o = (_nt(qs_ref[rs, cs], st[s, h].astype(BF16))
             + jnp.dot(a, v_ref[rs, cs], preferred_element_type=F32))
        dec = dec_ref[r0:r0 + 8, cs]
        st[s, h] = (st[s, h].reshape(DV_B // 8, 8, DK_B) * dec[None]
                    + ut.reshape(DV_B // 8, 8, DK_B)).reshape(DV_B, DK_B)
        y = o * lax.rsqrt(jnp.mean(o * o, axis=-1, keepdims=True) + EPS) * ng_ref[:, cs]
        gs = slice(g * HGRN_GROUP, (g + 1) * HGRN_GROUP)
        o_ref[s, gs, cs] = (y * _silu(hg_ref[s, gs, cs])).astype(o_ref.dtype)

    _emit_ahead(nseq * groups * H_B, HGRN_PIPE_DEPTH, local, carry)
    for (s, h), val in st.items():
        st_ref[s, h] = val

    @pl.when(i == pl.num_programs(1) - 1)
    def _():
        for s in range(nseq):
            for h in range(H_B):
                sout_ref[s, h] = st_ref[s, h].T


def _hgrn(p3d, lb, ng, s0, nseq, rows, name):
    batch, seq, _ = p3d.shape
    assert batch % nseq == 0 and seq % rows == 0 and rows % HGRN_GROUP == 0

    def col(c):
        return pl.BlockSpec((nseq, rows, W_B), lambda b, i: (b, i, c))

    vec = pl.BlockSpec((1, W_B), lambda b, i: (0, 0))
    state = pl.BlockSpec((nseq, H_B, DK_B, DV_B), lambda b, i: (b, 0, 0, 0))
    n = nseq * rows
    return pl.pallas_call(
        _hgrn_kernel,
        out_shape=(jax.ShapeDtypeStruct((batch, seq, W_B), BF16),
                   jax.ShapeDtypeStruct((batch, H_B, DK_B, DV_B), F32)),
        grid=(batch // nseq, seq // rows),
        in_specs=[col(4), col(5), col(6), col(7), vec, vec, state],
        out_specs=(pl.BlockSpec((nseq, rows, W_B), lambda b, i: (b, i, 0)), state),
        scratch_shapes=([pltpu.VMEM((nseq, H_B, DV_B, DK_B), F32)]
                        + [pltpu.VMEM((n, W_B), BF16)] * 8
                        + [pltpu.VMEM((n, W_B), F32)]),
        compiler_params=_params(("arbitrary", "arbitrary")),
        name=name,
    )(p3d, p3d, p3d, p3d, lb, ng, s0)


OUTPROJ_ROWS = 512


def _outproj_kernel(oa_ref, ob_ref, x_ref, w_ref, g_ref, y_ref):
    h = (x_ref[...]
         + jnp.dot(oa_ref[...], w_ref[0:W_A, :], preferred_element_type=F32)
         + jnp.dot(ob_ref[...], w_ref[W_A:, :], preferred_element_type=F32))
    ms = jnp.mean(h * h, axis=-1, keepdims=True)
    y_ref[...] = h * lax.rsqrt(ms + EPS) * g_ref[...]


def _outproj(oa, ob, x2d, w_bf16, g, name):
    rows = x2d.shape[0]
    tm = min(OUTPROJ_ROWS, rows)
    return pl.pallas_call(
        _outproj_kernel,
        out_shape=jax.ShapeDtypeStruct((rows, D_MODEL), F32),
        grid=(rows // tm,),
        in_specs=[
            pl.BlockSpec((tm, W_A), lambda i: (i, 0)),
            pl.BlockSpec((tm, W_B), lambda i: (i, 0)),
            pl.BlockSpec((tm, D_MODEL), lambda i: (i, 0)),
            pl.BlockSpec((W_A + W_B, D_MODEL), lambda i: (0, 0)),
            pl.BlockSpec((1, D_MODEL), lambda i: (0, 0)),
        ],
        out_specs=pl.BlockSpec((tm, D_MODEL), lambda i: (i, 0)),
        compiler_params=_params(("arbitrary",)),
        name=name,
    )(oa, ob, x2d, w_bf16, g)


HGRN_PROMPT_ROWS = 256
HGRN_PROMPT_SEQS = 2
HGRN_SAMPLE_SEQS = 8


def kernel(x_prompt, x_sample, cache_attn_k, cache_attn_v, state_hgrn, ln_in_g, w_in,
           rel_bias, lb_gamma, hg_norm_g, w_out, ln_f_g):
    batch, seq, _ = x_prompt.shape
    dec_batch, dec_seq, _ = x_sample.shape
    depth = w_in.shape[0]
    assert depth == 1 and dec_seq == CHUNK and PAST_LEN % CHUNK == 0
    assert cache_attn_k.shape[2] == WINDOW_ROWS

    lb_all = jnp.cumsum(jax.nn.softmax(lb_gamma.astype(F32), axis=0), axis=0)
    lb = lb_all[0].reshape(1, W_B)
    ng = hg_norm_g[0].reshape(1, W_B)
    g_in = ln_in_g[0].reshape(1, D_MODEL)
    g_f = ln_f_g.reshape(1, D_MODEL)
    w_in_b = w_in[0].astype(BF16)
    w_out_b = w_out[0].astype(BF16)

    xp = x_prompt.reshape(batch * seq, D_MODEL)
    xs = x_sample.reshape(dec_batch * dec_seq, D_MODEL)

    bias_pair, bias_single = _bias_tables(rel_bias[0])

    pp = _inproj(xp, g_in, w_in_b, "inproj_prompt")
    ps = _inproj(xs, g_in, w_in_b, "inproj_sample")

    oa_p = _attn_prompt(pp, bias_pair, batch, seq)
    oa_s = _attn_sample(ps, cache_attn_k[0], cache_attn_v[0], bias_single, dec_batch)

    s0_p = jnp.zeros((batch, H_B, DK_B, DV_B), F32)
    ob_p, st_p = _hgrn(pp.reshape(batch, seq, N_IN), lb, ng, s0_p,
                       HGRN_PROMPT_SEQS, HGRN_PROMPT_ROWS, "hgrn_prompt")
    ob_s, st_s = _hgrn(ps.reshape(dec_batch, dec_seq, N_IN), lb, ng, state_hgrn[0],
                       HGRN_SAMPLE_SEQS, dec_seq, "hgrn_sample")
    ob_p = ob_p.reshape(batch * seq, W_B)
    ob_s = ob_s.reshape(dec_batch * dec_seq, W_B)

    y_p = _outproj(oa_p, ob_p, xp, w_out_b, g_f, "outproj_prompt").reshape(batch, seq, D_MODEL)
    y_s = _outproj(oa_s, ob_s, xs, w_out_b, g_f, "outproj_sample").reshape(
        dec_batch, dec_seq, D_MODEL)

    rows_p = min(WINDOW_ROWS, seq)
    pp3 = pp.reshape(batch, seq, N_IN)
    k_p = pp3[:, seq - rows_p:, W_A:2 * W_A].reshape(1, batch, rows_p, H_A, DH_A)
    v_p = pp3[:, seq - rows_p:, 2 * W_A:3 * W_A].reshape(1, batch, rows_p, H_A, DH_A)
    ps3 = ps.reshape(dec_batch, dec_seq, N_IN)
    k_s = ps3[:, :, W_A:2 * W_A].reshape(1, dec_batch, dec_seq, H_A, DH_A)
    v_s = ps3[:, :, 2 * W_A:3 * W_A].reshape(1, dec_batch, dec_seq, H_A, DH_A)

    return (y_p, y_s, k_p, v_p, st_p[None], k_s, v_s, st_s[None])
```

```python
import jax
import jax.numpy as jnp
from jax import lax
from jax.experimental import pallas as pl
from jax.experimental.pallas import tpu as pltpu

F32 = jnp.float32
BF16 = jnp.bfloat16

D_MODEL = 1024
CHUNK = 64
LEFT_CHUNKS = 8
WINDOW_ROWS = LEFT_CHUNKS * CHUNK
BAND = WINDOW_ROWS + CHUNK
W_A = 512
H_A = 8
DH_A = 64
W_B = 512
H_B = 4
DK_B = 128
DV_B = 128
REL_CLIP = 128
GLA_BLOCK = 16
EPS = 1e-6
ATTN_SCALE = DH_A ** -0.5
NEG_INF = -1e30
N_IN = 4 * W_A + 4 * W_B
PAST_LEN = 2048

LANES = 128
HEAD_PAIRS = H_A // 2
REL_TABLE = 2 * REL_CLIP + 1
REL_TABLE_PAD = 384
TOEPLITZ_LEN = 640
VMEM_LIMIT = 56 * 1024 * 1024


def _params(semantics):
    return pltpu.CompilerParams(dimension_semantics=semantics,
                                vmem_limit_bytes=VMEM_LIMIT)


LOG2E = 1.4426950408889634


def _sigmoid(x):
    return 1.0 / (1.0 + jnp.exp2(x * -LOG2E))


def _silu(x):
    return x * _sigmoid(x)


def _nt(a, b):
    return lax.dot_general(a, b, (((1,), (1,)), ((), ())), preferred_element_type=F32)


def _emit_ahead(n_units, depth, first_fn, second_fn):
    pending = {}
    for t in range(n_units + depth):
        if t < n_units:
            pending[t] = first_fn(t)
        if t >= depth:
            second_fn(t - depth, pending.pop(t - depth))


INPROJ_ROWS = 512
INPROJ_COLS = 512


def _inproj_kernel(x_ref, g_ref, w_ref, o_ref):
    x = x_ref[...]
    ms = jnp.mean(x * x, axis=-1, keepdims=True)
    xn = (x * lax.rsqrt(ms + EPS) * g_ref[...]).astype(BF16)
    for n0 in range(0, N_IN, INPROJ_COLS):
        o_ref[:, n0:n0 + INPROJ_COLS] = jnp.dot(
            xn, w_ref[:, n0:n0 + INPROJ_COLS], preferred_element_type=F32)


def _inproj(x2d, g, w_bf16, name):
    rows = x2d.shape[0]
    tm = min(INPROJ_ROWS, rows)
    return pl.pallas_call(
        _inproj_kernel,
        out_shape=jax.ShapeDtypeStruct((rows, N_IN), F32),
        grid=(rows // tm,),
        in_specs=[
            pl.BlockSpec((tm, D_MODEL), lambda i: (i, 0)),
            pl.BlockSpec((1, D_MODEL), lambda i: (0, 0)),
            pl.BlockSpec((D_MODEL, N_IN), lambda i: (0, 0)),
        ],
        out_specs=pl.BlockSpec((tm, N_IN), lambda i: (i, 0)),
        compiler_params=_params(("arbitrary",)),
        name=name,
    )(x2d, g, w_bf16)


PAIR_BAND = BAND + CHUNK


def _bias_kernel(rb_ref, pair_ref, single_ref):
    m = lax.broadcasted_iota(jnp.int32, (REL_TABLE_PAD, TOEPLITZ_LEN), 1)
    t = lax.broadcasted_iota(jnp.int32, (REL_TABLE_PAD, TOEPLITZ_LEN), 0)
    idx = jnp.clip(BAND - 1 - m, -REL_CLIP, REL_CLIP) + REL_CLIP
    onehot = jnp.where(idx == t, 1.0, 0.0).astype(BF16)
    rb = rb_ref[...]
    hi = rb.astype(BF16)
    r1 = rb - hi.astype(F32)
    mid = r1.astype(BF16)
    lo = (r1 - mid.astype(F32)).astype(BF16)
    toep = (jnp.dot(hi, onehot, preferred_element_type=F32)
            + jnp.dot(mid, onehot, preferred_element_type=F32)
            + jnp.dot(lo, onehot, preferred_element_type=F32))
    key = lax.broadcasted_iota(jnp.int32, (CHUNK, TOEPLITZ_LEN), 1)
    for p in range(HEAD_PAIRS):
        for u in range(2):
            halves = []
            for h in (2 * p, 2 * p + 1):
                rows = jnp.broadcast_to(toep[h:h + 1, :], (CHUNK, TOEPLITZ_LEN))
                shift = (TOEPLITZ_LEN - (CHUNK - 1) + CHUNK * u) % TOEPLITZ_LEN
                rolled = pltpu.roll(rows, shift, 1, stride=1, stride_axis=0)
                seen = (key >= CHUNK * u) & (key < CHUNK * u + BAND)
                halves.append(jnp.where(seen, rolled * LOG2E, NEG_INF))
            tile_t = jnp.concatenate(halves, axis=0).T
            pair_ref[p, :, u * LANES:(u + 1) * LANES] = tile_t
            if u == 0:
                single_ref[p] = tile_t[:BAND, :]


def _bias_tables(rel_bias_l):
    rb = jnp.pad(rel_bias_l, ((0, 0), (0, REL_TABLE_PAD - REL_TABLE)))
    return pl.pallas_call(
        _bias_kernel,
        out_shape=(jax.ShapeDtypeStruct((HEAD_PAIRS, PAIR_BAND, 2 * LANES), F32),
                   jax.ShapeDtypeStruct((HEAD_PAIRS, BAND, LANES), F32)),
        name="rel_bias_table",
    )(rb)


def _attn_probs(qs, k2, bias_t, valid_from):
    first = lax.broadcasted_iota(jnp.int32, (CHUNK, LANES), 1) < DH_A
    parts = []
    for q in qs:
        q = q * (ATTN_SCALE * LOG2E)
        parts += [jnp.where(first, q, 0.0), jnp.where(first, 0.0, q)]
    qbd = jnp.concatenate(parts, axis=0).astype(BF16)
    s = _nt(k2, qbd) + bias_t
    if valid_from is not None:
        key = lax.broadcasted_iota(jnp.int32, s.shape, 0)
        s = jnp.where(key >= valid_from, s, NEG_INF)
    mx = jnp.max(s, axis=0, keepdims=True)
    e = jnp.exp2(s - mx)
    return e.astype(BF16), jnp.sum(e, axis=0, keepdims=True)


def _attn_apply(probs, ags, vt):
    e, denom = probs
    first = lax.broadcasted_iota(jnp.int32, (CHUNK, LANES), 1) < DH_A
    ot = jnp.dot(vt, e, preferred_element_type=F32)
    o2 = (ot * (1.0 / denom)).T
    outs = []
    for u, ag in enumerate(ags):
        r = 2 * CHUNK * u
        o = jnp.where(first, o2[r:r + CHUNK], o2[r + CHUNK:r + 2 * CHUNK])
        outs.append(o * _silu(ag))
    return outs


ATTN_PIPE_DEPTH = 3


def _attn_pipeline(n_units, probs_fn, apply_fn):
    _emit_ahead(n_units, ATTN_PIPE_DEPTH, probs_fn, apply_fn)


ATTN_ROWS = 512


def _attn_prompt_kernel(q_ref, k_ref, v_ref, ag_ref, bias_ref, o_ref, kh_ref, vth_ref):
    i = pl.program_id(1)

    @pl.when(i == 0)
    def _():
        kh_ref[0:WINDOW_ROWS, :] = jnp.zeros((WINDOW_ROWS, W_A), BF16)
        vth_ref[:, :, 0:WINDOW_ROWS] = jnp.zeros((HEAD_PAIRS, LANES, WINDOW_ROWS), BF16)

    kh_ref[WINDOW_ROWS:, :] = k_ref[...].astype(BF16)
    for p in range(HEAD_PAIRS):
        vth_ref[p, :, WINDOW_ROWS:] = v_ref[:, p * LANES:(p + 1) * LANES].T.astype(BF16)

    def body(first_step):
        def unit(t):
            cp, p = divmod(t, HEAD_PAIRS)
            r0 = cp * 2 * CHUNK
            rs = [slice(r0 + u * CHUNK, r0 + (u + 1) * CHUNK) for u in range(2)]
            return cp, p, r0, rs, slice(p * LANES, (p + 1) * LANES)

        def probs(t):
            cp, p, r0, rs, cs = unit(t)
            valid_from = (LEFT_CHUNKS - 2 * cp) * CHUNK if first_step else None
            return _attn_probs([q_ref[r, cs] for r in rs], kh_ref[r0:r0 + PAIR_BAND, cs],
                               bias_ref[p], valid_from)

        def apply(t, pr):
            cp, p, r0, rs, cs = unit(t)
            outs = _attn_apply(pr, [ag_ref[r, cs] for r in rs],
                               vth_ref[p, :, r0:r0 + PAIR_BAND])
            for r, o in zip(rs, outs):
                o_ref[r, cs] = o.astype(o_ref.dtype)

        _attn_pipeline((ATTN_ROWS // (2 * CHUNK)) * HEAD_PAIRS, probs, apply)

    @pl.when(i == 0)
    def _():
        body(True)

    @pl.when(i != 0)
    def _():
        body(False)

    kh_ref[0:WINDOW_ROWS, :] = kh_ref[ATTN_ROWS:ATTN_ROWS + WINDOW_ROWS, :]
    vth_ref[:, :, 0:WINDOW_ROWS] = vth_ref[:, :, ATTN_ROWS:ATTN_ROWS + WINDOW_ROWS]


def _attn_prompt(p2d, bias_pair, batch, seq):
    nt = seq // ATTN_ROWS

    def col(c):
        return pl.BlockSpec((ATTN_ROWS, W_A), lambda b, i: (b * nt + i, c))

    return pl.pallas_call(
        _attn_prompt_kernel,
        out_shape=jax.ShapeDtypeStruct((batch * seq, W_A), BF16),
        grid=(batch, nt),
        in_specs=[col(0), col(1), col(2), col(3),
                  pl.BlockSpec((HEAD_PAIRS, PAIR_BAND, 2 * LANES), lambda b, i: (0, 0, 0))],
        out_specs=pl.BlockSpec((ATTN_ROWS, W_A), lambda b, i: (b * nt + i, 0)),
        scratch_shapes=[pltpu.VMEM((WINDOW_ROWS + ATTN_ROWS, W_A), BF16),
                        pltpu.VMEM((HEAD_PAIRS, LANES, WINDOW_ROWS + ATTN_ROWS), BF16)],
        compiler_params=_params(("arbitrary", "arbitrary")),
        name="attn_prompt",
    )(p2d, p2d, p2d, p2d, bias_pair)


ATTN_SAMPLE_SEQS = 4


def _attn_sample_kernel(q_ref, k_ref, v_ref, ag_ref, ck_ref, cv_ref, bias_ref, o_ref,
                        kh_ref, vth_ref):
    for s in range(ATTN_SAMPLE_SEQS):
        rs = slice(s * CHUNK, (s + 1) * CHUNK)
        kh_ref[s, 0:WINDOW_ROWS, :] = ck_ref[s]
        kh_ref[s, WINDOW_ROWS:, :] = k_ref[rs, :].astype(BF16)
        for p in range(HEAD_PAIRS):
            cs = slice(p * LANES, (p + 1) * LANES)
            vth_ref[s, p, :, 0:WINDOW_ROWS] = cv_ref[s, :, cs].astype(F32).T.astype(BF16)
            vth_ref[s, p, :, WINDOW_ROWS:] = v_ref[rs, cs].T.astype(BF16)

    def unit(t):
        s, p = divmod(t, HEAD_PAIRS)
        return s, p, slice(s * CHUNK, (s + 1) * CHUNK), slice(p * LANES, (p + 1) * LANES)

    def probs(t):
        s, p, rs, cs = unit(t)
        return _attn_probs([q_ref[rs, cs]], kh_ref[s, :, cs], bias_ref[p], None)

    def apply(t, pr):
        s, p, rs, cs = unit(t)
        (o,) = _attn_apply(pr, [ag_ref[rs, cs]], vth_ref[s, p])
        o_ref[rs, cs] = o.astype(o_ref.dtype)

    _attn_pipeline(ATTN_SAMPLE_SEQS * HEAD_PAIRS, probs, apply)


def _attn_sample(p2d, cache_k, cache_v, bias_single, batch):
    ns = ATTN_SAMPLE_SEQS

    def col(c):
        return pl.BlockSpec((ns * CHUNK, W_A), lambda b: (b, c))

    cache = pl.BlockSpec((ns, WINDOW_ROWS, W_A), lambda b: (b, 0, 0))
    return pl.pallas_call(
        _attn_sample_kernel,
        out_shape=jax.ShapeDtypeStruct((batch * CHUNK, W_A), BF16),
        grid=(batch // ns,),
        in_specs=[col(0), col(1), col(2), col(3), cache, cache,
                  pl.BlockSpec((HEAD_PAIRS, BAND, LANES), lambda b: (0, 0, 0))],
        out_specs=pl.BlockSpec((ns * CHUNK, W_A), lambda b: (b, 0)),
        scratch_shapes=[pltpu.VMEM((ns, BAND, W_A), BF16),
                        pltpu.VMEM((ns, HEAD_PAIRS, LANES, BAND), BF16)],
        compiler_params=_params(("arbitrary",)),
        name="attn_sample",
    )(p2d, p2d, p2d, p2d, cache_k, cache_v, bias_single)


HGRN_GROUP_BLOCKS = 4
HGRN_GROUP = HGRN_GROUP_BLOCKS * GLA_BLOCK
HGRN_PIPE_DEPTH = 4


def _block_cumsum(x):
    row = lax.broadcasted_iota(jnp.int32, x.shape, 0) & (GLA_BLOCK - 1)
    s = 1
    while s < GLA_BLOCK:
        x = x + jnp.where(row >= s, pltpu.roll(x, s, 0), 0.0)
        s *= 2
    return x


def _hgrn_kernel(hq_ref, hf_ref, hi_ref, hg_ref, lb_ref, ng_ref, s0_ref,
                 o_ref, sout_ref, st_ref, qt_ref, kt_ref, kd_ref, v_ref,
                 qtf_ref, kdf_ref, bl_ref):
    i = pl.program_id(1)
    nseq, seq_rows = hq_ref.shape[0], hq_ref.shape[1]
    rows = nseq * seq_rows
    nb = rows // GLA_BLOCK

    @pl.when(i == 0)
    def _():
        for s in range(nseq):
            for h in range(H_B):
                st_ref[s, h] = s0_ref[s, h].T

    def flat(ref):
        return ref[...].reshape(rows, W_B)

    lb = lb_ref[...]
    f = lb + (1.0 - lb) * _sigmoid(flat(hf_ref))
    b = _block_cumsum(jnp.log(f) * LOG2E)
    b3 = b.reshape(nb, GLA_BLOCK, W_B)
    bl = jnp.broadcast_to(b3[:, GLA_BLOCK - 1:GLA_BLOCK, :],
                          (nb, GLA_BLOCK, W_B)).reshape(rows, W_B)
    kk = 1.0 - f
    qt = _silu(flat(hq_ref)) * jnp.exp2(b)
    kd = kk * jnp.exp2(bl - b)
    qtf_ref[...] = qt
    kdf_ref[...] = kd
    bl_ref[...] = bl
    qt_ref[...] = qt.astype(BF16)
    kt_ref[...] = (kk * jnp.exp2(-b)).astype(BF16)
    kd_ref[...] = kd.astype(BF16)
    v_ref[...] = flat(hi_ref).astype(BF16)

    r_i = lax.broadcasted_iota(jnp.int32, (HGRN_GROUP, HGRN_GROUP), 0)
    c_i = lax.broadcasted_iota(jnp.int32, (HGRN_GROUP, HGRN_GROUP), 1)
    rb, cb = r_i // GLA_BLOCK, c_i // GLA_BLOCK
    m_diag = (rb == cb) & (r_i >= c_i)
    m_adj = rb == cb + 1
    m_far = rb >= cb + 2

    groups = seq_rows // HGRN_GROUP
    st = {(s, h): st_ref[s, h] for s in range(nseq) for h in range(H_B)}

    def unit(t):
        sg, h = divmod(t, H_B)
        s, g = divmod(sg, groups)
        r0 = s * seq_rows + g * HGRN_GROUP
        return s, g, h, r0, slice(r0, r0 + HGRN_GROUP), slice(h * DK_B, (h + 1) * DK_B)

    def block_decays(r0, cs):
        return [bl_ref[r0 + j * GLA_BLOCK:r0 + j * GLA_BLOCK + 1, cs]
                for j in range(HGRN_GROUP_BLOCKS)]

    def scaled(x, log2_scales):
        parts = []
        for j, sc in enumerate(log2_scales):
            xb = x[j * GLA_BLOCK:(j + 1) * GLA_BLOCK]
            parts.append(xb if sc is None else xb * jnp.exp2(sc))
        return jnp.concatenate(parts, axis=0).astype(BF16)

    def local(t):
        s, g, h, r0, rs, cs = unit(t)
        d0, d1, d2, d3 = block_decays(r0, cs)
        qt_g = qt_ref[rs, cs]
        kdf = kdf_ref[rs, cs]
        q1 = scaled(qtf_ref[rs, cs], [None, None, None, d2])
        k1 = scaled(kdf, [d1, None, None, None])
        a = jnp.where(m_diag, _nt(qt_g, kt_ref[rs, cs]),
                      jnp.where(m_adj, _nt(qt_g, kd_ref[rs, cs]),
                                jnp.where(m_far, _nt(q1, k1), 0.0)))
        kdp = scaled(kdf, [(d1 + d2) + d3, d2 + d3, d3, None])
        ut = lax.dot_general(v_ref[rs, cs], kdp, (((0,), (0,)), ((), ())),
                             preferred_element_type=F32)
        return a.astype(BF16), ut

    def carry(t, loc):
        s, g, h, r0, rs, cs = unit(t)
        a, ut = loc
        d0, d1, d2, d3 = block_decays(r0, cs)
        qs = scaled(qtf_ref[rs, cs], [None, d0, d0 + d1, (d0 + d1) + d2])
        o = (_nt(qs, st[s, h].astype(BF16))
             + jnp.dot(a, v_ref[rs, cs], preferred_element_type=F32))
        dec = jnp.exp2(((d0 + d1) + d2) + d3)
        st[s, h] = st[s, h] * dec + ut
        y = o * lax.rsqrt(jnp.mean(o * o, axis=-1, keepdims=True) + EPS) * ng_ref[:, cs]
        gs = slice(g * HGRN_GROUP, (g + 1) * HGRN_GROUP)
        o_ref[s, gs, cs] = (y * _silu(hg_ref[s, gs, cs])).astype(o_ref.dtype)

    _emit_ahead(nseq * groups * H_B, HGRN_PIPE_DEPTH, local, carry)
    for (s, h), val in st.items():
        st_ref[s, h] = val

    @pl.when(i == pl.num_programs(1) - 1)
    def _():
        for s in range(nseq):
            for h in range(H_B):
                sout_ref[s, h] = st_ref[s, h].T


def _hgrn(p3d, lb, ng, s0, nseq, rows, name):
    batch, seq, _ = p3d.shape
    assert batch % nseq == 0 and seq % rows == 0 and rows % HGRN_GROUP == 0

    def col(c):
        return pl.BlockSpec((nseq, rows, W_B), lambda b, i: (b, i, c))

    vec = pl.BlockSpec((1, W_B), lambda b, i: (0, 0))
    state = pl.BlockSpec((nseq, H_B, DK_B, DV_B), lambda b, i: (b, 0, 0, 0))
    n = nseq * rows
    return pl.pallas_call(
        _hgrn_kernel,
        out_shape=(jax.ShapeDtypeStruct((batch, seq, W_B), BF16),
                   jax.ShapeDtypeStruct((batch, H_B, DK_B, DV_B), F32)),
        grid=(batch // nseq, seq // rows),
        in_specs=[col(4), col(5), col(6), col(7), vec, vec, state],
        out_specs=(pl.BlockSpec((nseq, rows, W_B), lambda b, i: (b, i, 0)), state),
        scratch_shapes=([pltpu.VMEM((nseq, H_B, DV_B, DK_B), F32)]
                        + [pltpu.VMEM((n, W_B), BF16)] * 4
                        + [pltpu.VMEM((n, W_B), F32)] * 3),
        compiler_params=_params(("arbitrary", "arbitrary")),
        name=name,
    )(p3d, p3d, p3d, p3d, lb, ng, s0)


OUTPROJ_ROWS = 1024


def _outproj_kernel(oa_ref, ob_ref, x_ref, w_ref, g_ref, y_ref):
    h = (x_ref[...]
         + jnp.dot(oa_ref[...], w_ref[0:W_A, :], preferred_element_type=F32)
         + jnp.dot(ob_ref[...], w_ref[W_A:, :], preferred_element_type=F32))
    ms = jnp.mean(h * h, axis=-1, keepdims=True)
    y_ref[...] = h * lax.rsqrt(ms + EPS) * g_ref[...]


def _outproj(oa, ob, x2d, w_bf16, g, name):
    rows = x2d.shape[0]
    tm = min(OUTPROJ_ROWS, rows)
    return pl.pallas_call(
        _outproj_kernel,
        out_shape=jax.ShapeDtypeStruct((rows, D_MODEL), F32),
        grid=(rows // tm,),
        in_specs=[
            pl.BlockSpec((tm, W_A), lambda i: (i, 0)),
            pl.BlockSpec((tm, W_B), lambda i: (i, 0)),
            pl.BlockSpec((tm, D_MODEL), lambda i: (i, 0)),
            pl.BlockSpec((W_A + W_B, D_MODEL), lambda i: (0, 0)),
            pl.BlockSpec((1, D_MODEL), lambda i: (0, 0)),
        ],
        out_specs=pl.BlockSpec((tm, D_MODEL), lambda i: (i, 0)),
        compiler_params=_params(("arbitrary",)),
        name=name,
    )(oa, ob, x2d, w_bf16, g)


HGRN_PROMPT_ROWS = 256
HGRN_PROMPT_SEQS = 2
HGRN_SAMPLE_SEQS = 8


def kernel(x_prompt, x_sample, cache_attn_k, cache_attn_v, state_hgrn, ln_in_g, w_in,
           rel_bias, lb_gamma, hg_norm_g, w_out, ln_f_g):
    batch, seq, _ = x_prompt.shape
    dec_batch, dec_seq, _ = x_sample.shape
    depth = w_in.shape[0]
    assert depth == 1 and dec_seq == CHUNK and PAST_LEN % CHUNK == 0
    assert cache_attn_k.shape[2] == WINDOW_ROWS

    lb_all = jnp.cumsum(jax.nn.softmax(lb_gamma.astype(F32), axis=0), axis=0)
    lb = lb_all[0].reshape(1, W_B)
    ng = hg_norm_g[0].reshape(1, W_B)
    g_in = ln_in_g[0].reshape(1, D_MODEL)
    g_f = ln_f_g.reshape(1, D_MODEL)
    w_in_b = w_in[0].astype(BF16)
    w_out_b = w_out[0].astype(BF16)

    xp = x_prompt.reshape(batch * seq, D_MODEL)
    xs = x_sample.reshape(dec_batch * dec_seq, D_MODEL)

    bias_pair, bias_single = _bias_tables(rel_bias[0])

    pp = _inproj(xp, g_in, w_in_b, "inproj_prompt")
    ps = _inproj(xs, g_in, w_in_b, "inproj_sample")

    oa_p = _attn_prompt(pp, bias_pair, batch, seq)
    ck = cache_attn_k[0].astype(BF16).reshape(dec_batch, WINDOW_ROWS, W_A)
    cv = cache_attn_v[0].astype(BF16).reshape(dec_batch, WINDOW_ROWS, W_A)
    oa_s = _attn_sample(ps, ck, cv, bias_single, dec_batch)

    s0_p = jnp.zeros((batch, H_B, DK_B, DV_B), F32)
    ob_p, st_p = _hgrn(pp.reshape(batch, seq, N_IN), lb, ng, s0_p,
                       HGRN_PROMPT_SEQS, HGRN_PROMPT_ROWS, "hgrn_prompt")
    ob_s, st_s = _hgrn(ps.reshape(dec_batch, dec_seq, N_IN), lb, ng, state_hgrn[0],
                       HGRN_SAMPLE_SEQS, dec_seq, "hgrn_sample")
    ob_p = ob_p.reshape(batch * seq, W_B)
    ob_s = ob_s.reshape(dec_batch * dec_seq, W_B)

    y_p = _outproj(oa_p, ob_p, xp, w_out_b, g_f, "outproj_prompt").reshape(batch, seq, D_MODEL)
    y_s = _outproj(oa_s, ob_s, xs, w_out_b, g_f, "outproj_sample").reshape(
        dec_batch, dec_seq, D_MODEL)

    rows_p = min(WINDOW_ROWS, seq)
    pp3 = pp.reshape(batch, seq, N_IN)
    k_p = pp3[:, seq - rows_p:, W_A:2 * W_A].reshape(1, batch, rows_p, H_A, DH_A)
    v_p = pp3[:, seq - rows_p:, 2 * W_A:3 * W_A].reshape(1, batch, rows_p, H_A, DH_A)
    ps3 = ps.reshape(dec_batch, dec_seq, N_IN)
    k_s = ps3[:, :, W_A:2 * W_A].reshape(1, dec_batch, dec_seq, H_A, DH_A)
    v_s = ps3[:, :, 2 * W_A:3 * W_A].reshape(1, dec_batch, dec_seq, H_A, DH_A)

    return (y_p, y_s, k_p, v_p, st_p[None], k_s, v_s, st_s[None])
```

```python
import jax
import jax.numpy as jnp
from jax import lax
from jax.experimental import pallas as pl
from jax.experimental.pallas import tpu as pltpu

F32 = jnp.float32
BF16 = jnp.bfloat16

D_MODEL = 1024
CHUNK = 64
LEFT_CHUNKS = 8
WINDOW_ROWS = LEFT_CHUNKS * CHUNK
BAND = WINDOW_ROWS + CHUNK
W_A = 512
H_A = 8
DH_A = 64
W_B = 512
H_B = 4
DK_B = 128
DV_B = 128
REL_CLIP = 128
GLA_BLOCK = 16
EPS = 1e-6
ATTN_SCALE = DH_A ** -0.5
NEG_INF = -1e30
N_IN = 4 * W_A + 4 * W_B
PAST_LEN = 2048

LANES = 128
HEAD_PAIRS = H_A // 2
REL_TABLE = 2 * REL_CLIP + 1
REL_TABLE_PAD = 384
TOEPLITZ_LEN = 640
VMEM_LIMIT = 56 * 1024 * 1024


def _params(semantics):
    return pltpu.CompilerParams(dimension_semantics=semantics,
                                vmem_limit_bytes=VMEM_LIMIT)


LOG2E = 1.4426950408889634


def _sigmoid(x):
    return 1.0 / (1.0 + jnp.exp2(x * -LOG2E))


def _silu(x):
    return x * _sigmoid(x)


def _nt(a, b):
    return lax.dot_general(a, b, (((1,), (1,)), ((), ())), preferred_element_type=F32)


def _emit_ahead(n_units, depth, first_fn, second_fn):
    pending = {}
    for t in range(n_units + depth):
        if t < n_units:
            pending[t] = first_fn(t)
        if t >= depth:
            second_fn(t - depth, pending.pop(t - depth))


INPROJ_ROWS = 512
INPROJ_COLS = 512


def _inproj_kernel(x_ref, g_ref, w_ref, o_ref):
    x = x_ref[...]
    ms = jnp.mean(x * x, axis=-1, keepdims=True)
    xn = (x * lax.rsqrt(ms + EPS) * g_ref[...]).astype(BF16)
    for n0 in range(0, N_IN, INPROJ_COLS):
        o_ref[:, n0:n0 + INPROJ_COLS] = jnp.dot(
            xn, w_ref[:, n0:n0 + INPROJ_COLS], preferred_element_type=F32)


def _inproj(x2d, g, w_bf16, name):
    rows = x2d.shape[0]
    tm = min(INPROJ_ROWS, rows)
    return pl.pallas_call(
        _inproj_kernel,
        out_shape=jax.ShapeDtypeStruct((rows, N_IN), F32),
        grid=(rows // tm,),
        in_specs=[
            pl.BlockSpec((tm, D_MODEL), lambda i: (i, 0)),
            pl.BlockSpec((1, D_MODEL), lambda i: (0, 0)),
            pl.BlockSpec((D_MODEL, N_IN), lambda i: (0, 0)),
        ],
        out_specs=pl.BlockSpec((tm, N_IN), lambda i: (i, 0)),
        compiler_params=_params(("arbitrary",)),
        name=name,
    )(x2d, g, w_bf16)


PAIR_BAND = BAND + CHUNK


def _bias_kernel(rb_ref, pair_ref, single_ref):
    m = lax.broadcasted_iota(jnp.int32, (REL_TABLE_PAD, TOEPLITZ_LEN), 1)
    t = lax.broadcasted_iota(jnp.int32, (REL_TABLE_PAD, TOEPLITZ_LEN), 0)
    idx = jnp.clip(BAND - 1 - m, -REL_CLIP, REL_CLIP) + REL_CLIP
    onehot = jnp.where(idx == t, 1.0, 0.0).astype(BF16)
    rb = rb_ref[...]
    hi = rb.astype(BF16)
    r1 = rb - hi.astype(F32)
    mid = r1.astype(BF16)
    lo = (r1 - mid.astype(F32)).astype(BF16)
    toep = (jnp.dot(hi, onehot, preferred_element_type=F32)
            + jnp.dot(mid, onehot, preferred_element_type=F32)
            + jnp.dot(lo, onehot, preferred_element_type=F32))
    key = lax.broadcasted_iota(jnp.int32, (CHUNK, TOEPLITZ_LEN), 1)
    for p in range(HEAD_PAIRS):
        for u in range(2):
            halves = []
            for h in (2 * p, 2 * p + 1):
                rows = jnp.broadcast_to(toep[h:h + 1, :], (CHUNK, TOEPLITZ_LEN))
                shift = (TOEPLITZ_LEN - (CHUNK - 1) + CHUNK * u) % TOEPLITZ_LEN
                rolled = pltpu.roll(rows, shift, 1, stride=1, stride_axis=0)
                seen = (key >= CHUNK * u) & (key < CHUNK * u + BAND)
                halves.append(jnp.where(seen, rolled * LOG2E, NEG_INF))
            tile_t = jnp.concatenate(halves, axis=0).T
            pair_ref[p, :, u * LANES:(u + 1) * LANES] = tile_t
            if u == 0:
                single_ref[p] = tile_t[:BAND, :]


def _bias_tables(rel_bias_l):
    rb = jnp.pad(rel_bias_l, ((0, 0), (0, REL_TABLE_PAD - REL_TABLE)))
    return pl.pallas_call(
        _bias_kernel,
        out_shape=(jax.ShapeDtypeStruct((HEAD_PAIRS, PAIR_BAND, 2 * LANES), F32),
                   jax.ShapeDtypeStruct((HEAD_PAIRS, BAND, LANES), F32)),
        name="rel_bias_table",
    )(rb)


def _attn_probs(qs, k2, bias_t, valid_from):
    first = lax.broadcasted_iota(jnp.int32, (CHUNK, LANES), 1) < DH_A
    parts = []
    for q in qs:
        q = q * (ATTN_SCALE * LOG2E)
        parts += [jnp.where(first, q, 0.0), jnp.where(first, 0.0, q)]
    qbd = jnp.concatenate(parts, axis=0).astype(BF16)
    s = _nt(k2, qbd) + bias_t
    if valid_from is not None:
        key = lax.broadcasted_iota(jnp.int32, s.shape, 0)
        s = jnp.where(key >= valid_from, s, NEG_INF)
    mx = jnp.max(s, axis=0, keepdims=True)
    return jnp.exp2(s - mx).astype(BF16)


VT_ROWS = LANES + 16


def _ones_row_tile(keys):
    row = lax.broadcasted_iota(jnp.int32, (VT_ROWS - LANES, keys), 0)
    return jnp.where(row == 0, 1.0, 0.0).astype(BF16)


def _attn_apply(e, ags, vt):
    first = lax.broadcasted_iota(jnp.int32, (CHUNK, LANES), 1) < DH_A
    ot = jnp.dot(vt, e, preferred_element_type=F32)
    o2 = (ot[:LANES] * (1.0 / ot[LANES:LANES + 1])).T
    outs = []
    for u, ag in enumerate(ags):
        r = 2 * CHUNK * u
        o = jnp.where(first, o2[r:r + CHUNK], o2[r + CHUNK:r + 2 * CHUNK])
        outs.append(o * _silu(ag))
    return outs


ATTN_PIPE_DEPTH = 3


def _attn_pipeline(n_units, probs_fn, apply_fn):
    _emit_ahead(n_units, ATTN_PIPE_DEPTH, probs_fn, apply_fn)


ATTN_ROWS = 512


def _attn_prompt_kernel(q_ref, k_ref, v_ref, ag_ref, bias_ref, o_ref, kh_ref, vth_ref):
    i = pl.program_id(1)

    @pl.when(i == 0)
    def _():
        kh_ref[0:WINDOW_ROWS, :] = jnp.zeros((WINDOW_ROWS, W_A), BF16)
        vth_ref[:, 0:LANES, 0:WINDOW_ROWS] = jnp.zeros((HEAD_PAIRS, LANES, WINDOW_ROWS), BF16)
        for p in range(HEAD_PAIRS):
            vth_ref[p, LANES:, :] = _ones_row_tile(WINDOW_ROWS + ATTN_ROWS)

    kh_ref[WINDOW_ROWS:, :] = k_ref[...].astype(BF16)
    for p in range(HEAD_PAIRS):
        vth_ref[p, 0:LANES, WINDOW_ROWS:] = v_ref[:, p * LANES:(p + 1) * LANES].T.astype(BF16)

    def body(first_step):
        def unit(t):
            cp, p = divmod(t, HEAD_PAIRS)
            r0 = cp * 2 * CHUNK
            rs = [slice(r0 + u * CHUNK, r0 + (u + 1) * CHUNK) for u in range(2)]
            return cp, p, r0, rs, slice(p * LANES, (p + 1) * LANES)

        def probs(t):
            cp, p, r0, rs, cs = unit(t)
            valid_from = (LEFT_CHUNKS - 2 * cp) * CHUNK if first_step else None
            return _attn_probs([q_ref[r, cs] for r in rs], kh_ref[r0:r0 + PAIR_BAND, cs],
                               bias_ref[p], valid_from)

        def apply(t, pr):
            cp, p, r0, rs, cs = unit(t)
            outs = _attn_apply(pr, [ag_ref[r, cs] for r in rs],
                               vth_ref[p, :, r0:r0 + PAIR_BAND])
            for r, o in zip(rs, outs):
                o_ref[r, cs] = o.astype(o_ref.dtype)

        _attn_pipeline((ATTN_ROWS // (2 * CHUNK)) * HEAD_PAIRS, probs, apply)

    @pl.when(i == 0)
    def _():
        body(True)

    @pl.when(i != 0)
    def _():
        body(False)

    kh_ref[0:WINDOW_ROWS, :] = kh_ref[ATTN_ROWS:ATTN_ROWS + WINDOW_ROWS, :]
    vth_ref[:, 0:LANES, 0:WINDOW_ROWS] = vth_ref[:, 0:LANES, ATTN_ROWS:ATTN_ROWS + WINDOW_ROWS]


def _attn_prompt(p2d, bias_pair, batch, seq):
    nt = seq // ATTN_ROWS

    def col(c):
        return pl.BlockSpec((ATTN_ROWS, W_A), lambda b, i: (b * nt + i, c))

    return pl.pallas_call(
        _attn_prompt_kernel,
        out_shape=jax.ShapeDtypeStruct((batch * seq, W_A), BF16),
        grid=(batch, nt),
        in_specs=[col(0), col(1), col(2), col(3),
                  pl.BlockSpec((HEAD_PAIRS, PAIR_BAND, 2 * LANES), lambda b, i: (0, 0, 0))],
        out_specs=pl.BlockSpec((ATTN_ROWS, W_A), lambda b, i: (b * nt + i, 0)),
        scratch_shapes=[pltpu.VMEM((WINDOW_ROWS + ATTN_ROWS, W_A), BF16),
                        pltpu.VMEM((HEAD_PAIRS, VT_ROWS, WINDOW_ROWS + ATTN_ROWS), BF16)],
        compiler_params=_params(("arbitrary", "arbitrary")),
        name="attn_prompt",
    )(p2d, p2d, p2d, p2d, bias_pair)


ATTN_SAMPLE_SEQS = 4


def _attn_sample_kernel(q_ref, k_ref, v_ref, ag_ref, ck_ref, cv_ref, bias_ref, o_ref,
                        kh_ref, vth_ref):
    for s in range(ATTN_SAMPLE_SEQS):
        rs = slice(s * CHUNK, (s + 1) * CHUNK)
        kh_ref[s, 0:WINDOW_ROWS, :] = ck_ref[s].astype(BF16)
        kh_ref[s, WINDOW_ROWS:, :] = k_ref[rs, :].astype(BF16)
        for p in range(HEAD_PAIRS):
            cs = slice(p * LANES, (p + 1) * LANES)
            vth_ref[s, p, 0:LANES, 0:WINDOW_ROWS] = cv_ref[s, :, cs].T.astype(BF16)
            vth_ref[s, p, 0:LANES, WINDOW_ROWS:] = v_ref[rs, cs].T.astype(BF16)
            vth_ref[s, p, LANES:, :] = _ones_row_tile(BAND)

    def unit(t):
        s, p = divmod(t, HEAD_PAIRS)
        return s, p, slice(s * CHUNK, (s + 1) * CHUNK), slice(p * LANES, (p + 1) * LANES)

    def probs(t):
        s, p, rs, cs = unit(t)
        return _attn_probs([q_ref[rs, cs]], kh_ref[s, :, cs], bias_ref[p], None)

    def apply(t, pr):
        s, p, rs, cs = unit(t)
        (o,) = _attn_apply(pr, [ag_ref[rs, cs]], vth_ref[s, p])
        o_ref[rs, cs] = o.astype(o_ref.dtype)

    _attn_pipeline(ATTN_SAMPLE_SEQS * HEAD_PAIRS, probs, apply)


def _attn_sample(p2d, cache_k, cache_v, bias_single, batch):
    ns = ATTN_SAMPLE_SEQS

    def col(c):
        return pl.BlockSpec((ns * CHUNK, W_A), lambda b: (b, c))

    cache = pl.BlockSpec((ns, WINDOW_ROWS, W_A), lambda b: (b, 0, 0))
    return pl.pallas_call(
        _attn_sample_kernel,
        out_shape=jax.ShapeDtypeStruct((batch * CHUNK, W_A), BF16),
        grid=(batch // ns,),
        in_specs=[col(0), col(1), col(2), col(3), cache, cache,
                  pl.BlockSpec((HEAD_PAIRS, BAND, LANES), lambda b: (0, 0, 0))],
        out_specs=pl.BlockSpec((ns * CHUNK, W_A), lambda b: (b, 0)),
        scratch_shapes=[pltpu.VMEM((ns, BAND, W_A), BF16),
                        pltpu.VMEM((ns, HEAD_PAIRS, VT_ROWS, BAND), BF16)],
        compiler_params=_params(("arbitrary",)),
        name="attn_sample",
    )(p2d, p2d, p2d, p2d, cache_k, cache_v, bias_single)


HGRN_GROUP_BLOCKS = 4
HGRN_GROUP = HGRN_GROUP_BLOCKS * GLA_BLOCK
HGRN_PIPE_DEPTH = 4


def _block_cumsum(x):
    row = lax.broadcasted_iota(jnp.int32, x.shape, 0) & (GLA_BLOCK - 1)
    s = 1
    while s < GLA_BLOCK:
        x = x + jnp.where(row >= s, pltpu.roll(x, s, 0), 0.0)
        s *= 2
    return x


def _hgrn_kernel(hq_ref, hf_ref, hi_ref, hg_ref, lb_ref, ng_ref, s0_ref,
                 o_ref, sout_ref, st_ref, qt_ref, kt_ref, kd_ref, v_ref,
                 qtf_ref, kdf_ref, bl_ref):
    i = pl.program_id(1)
    nseq, seq_rows = hq_ref.shape[0], hq_ref.shape[1]
    rows = nseq * seq_rows
    nb = rows // GLA_BLOCK

    @pl.when(i == 0)
    def _():
        for s in range(nseq):
            for h in range(H_B):
                st_ref[s, h] = s0_ref[s, h].T

    def flat(ref):
        return ref[...].reshape(rows, W_B)

    lb = lb_ref[...]
    f = lb + (1.0 - lb) * _sigmoid(flat(hf_ref))
    b = _block_cumsum(jnp.log(f) * LOG2E)
    b3 = b.reshape(nb, GLA_BLOCK, W_B)
    bl = jnp.broadcast_to(b3[:, GLA_BLOCK - 1:GLA_BLOCK, :],
                          (nb, GLA_BLOCK, W_B)).reshape(rows, W_B)
    kk = 1.0 - f
    qt = _silu(flat(hq_ref)) * jnp.exp2(b)
    kd = kk * jnp.exp2(bl - b)
    qtf_ref[...] = qt
    kdf_ref[...] = kd
    bl_ref[...] = bl
    qt_ref[...] = qt.astype(BF16)
    kt_ref[...] = (kk * jnp.exp2(-b)).astype(BF16)
    kd_ref[...] = kd.astype(BF16)
    v_ref[...] = flat(hi_ref).astype(BF16)

    r_i = lax.broadcasted_iota(jnp.int32, (HGRN_GROUP, HGRN_GROUP), 0)
    c_i = lax.broadcasted_iota(jnp.int32, (HGRN_GROUP, HGRN_GROUP), 1)
    rb, cb = r_i // GLA_BLOCK, c_i // GLA_BLOCK
    m_diag = (rb == cb) & (r_i >= c_i)
    m_adj = rb == cb + 1
    m_far = rb >= cb + 2

    groups = seq_rows // HGRN_GROUP
    st = {(s, h): st_ref[s, h] for s in range(nseq) for h in range(H_B)}

    def unit(t):
        sg, h = divmod(t, H_B)
        s, g = divmod(sg, groups)
        r0 = s * seq_rows + g * HGRN_GROUP
        return s, g, h, r0, slice(r0, r0 + HGRN_GROUP), slice(h * DK_B, (h + 1) * DK_B)

    def block_decays(r0, cs):
        return [bl_ref[r0 + j * GLA_BLOCK:r0 + j * GLA_BLOCK + 1, cs]
                for j in range(HGRN_GROUP_BLOCKS)]

    def scaled(x, log2_scales):
        parts = []
        for j, sc in enumerate(log2_scales):
            xb = x[j * GLA_BLOCK:(j + 1) * GLA_BLOCK]
            parts.append(xb if sc is None else xb * jnp.exp2(sc))
        return jnp.concatenate(parts, axis=0).astype(BF16)

    def local(t):
        s, g, h, r0, rs, cs = unit(t)
        d0, d1, d2, d3 = block_decays(r0, cs)
        qt_g = qt_ref[rs, cs]
        kdf = kdf_ref[rs, cs]
        q1 = scaled(qtf_ref[rs, cs], [None, None, None, d2])
        k1 = scaled(kdf, [d1, None, None, None])
        a = jnp.where(m_diag, _nt(qt_g, kt_ref[rs, cs]),
                      jnp.where(m_adj, _nt(qt_g, kd_ref[rs, cs]),
                                jnp.where(m_far, _nt(q1, k1), 0.0)))
        kdp = scaled(kdf, [(d1 + d2) + d3, d2 + d3, d3, None])
        ut = lax.dot_general(v_ref[rs, cs], kdp, (((0,), (0,)), ((), ())),
                             preferred_element_type=F32)
        return a.astype(BF16), ut

    def carry(t, loc):
        s, g, h, r0, rs, cs = unit(t)
        a, ut = loc
        d0, d1, d2, d3 = block_decays(r0, cs)
        qs = scaled(qtf_ref[rs, cs], [None, d0, d0 + d1, (d0 + d1) + d2])
        o = (_nt(qs, st[s, h].astype(BF16))
             + jnp.dot(a, v_ref[rs, cs], preferred_element_type=F32))
        dec = jnp.exp2(((d0 + d1) + d2) + d3)
        st[s, h] = st[s, h] * dec + ut
        y = o * lax.rsqrt(jnp.mean(o * o, axis=-1, keepdims=True) + EPS) * ng_ref[:, cs]
        gs = slice(g * HGRN_GROUP, (g + 1) * HGRN_GROUP)
        o_ref[s, gs, cs] = (y * _silu(hg_ref[s, gs, cs])).astype(o_ref.dtype)

    _emit_ahead(nseq * groups * H_B, HGRN_PIPE_DEPTH, local, carry)
    for (s, h), val in st.items():
        st_ref[s, h] = val

    @pl.when(i == pl.num_programs(1) - 1)
    def _():
        for s in range(nseq):
            for h in range(H_B):
                sout_ref[s, h] = st_ref[s, h].T


def _hgrn(p3d, lb, ng, s0, nseq, rows, name):
    batch, seq, _ = p3d.shape
    assert batch % nseq == 0 and seq % rows == 0 and rows % HGRN_GROUP == 0

    def col(c):
        return pl.BlockSpec((nseq, rows, W_B), lambda b, i: (b, i, c))

    vec = pl.BlockSpec((1, W_B), lambda b, i: (0, 0))
    state = pl.BlockSpec((nseq, H_B, DK_B, DV_B), lambda b, i: (b, 0, 0, 0))
    n = nseq * rows
    return pl.pallas_call(
        _hgrn_kernel,
        out_shape=(jax.ShapeDtypeStruct((batch, seq, W_B), BF16),
                   jax.ShapeDtypeStruct((batch, H_B, DK_B, DV_B), F32)),
        grid=(batch // nseq, seq // rows),
        in_specs=[col(4), col(5), col(6), col(7), vec, vec, state],
        out_specs=(pl.BlockSpec((nseq, rows, W_B), lambda b, i: (b, i, 0)), state),
        scratch_shapes=([pltpu.VMEM((nseq, H_B, DV_B, DK_B), F32)]
                        + [pltpu.VMEM((n, W_B), BF16)] * 4
                        + [pltpu.VMEM((n, W_B), F32)] * 3),
        compiler_params=_params(("arbitrary", "arbitrary")),
        name=name,
    )(p3d, p3d, p3d, p3d, lb, ng, s0)


OUTPROJ_ROWS = 1024


def _outproj_kernel(oa_ref, ob_ref, x_ref, w_ref, g_ref, y_ref):
    h = (x_ref[...]
         + jnp.dot(oa_ref[...], w_ref[0:W_A, :], preferred_element_type=F32)
         + jnp.dot(ob_ref[...], w_ref[W_A:, :], preferred_element_type=F32))
    ms = jnp.mean(h * h, axis=-1, keepdims=True)
    y_ref[...] = h * lax.rsqrt(ms + EPS) * g_ref[...]


def _outproj(oa, ob, x2d, w_bf16, g, name):
    rows = x2d.shape[0]
    tm = min(OUTPROJ_ROWS, rows)
    return pl.pallas_call(
        _outproj_kernel,
        out_shape=jax.ShapeDtypeStruct((rows, D_MODEL), F32),
        grid=(rows // tm,),
        in_specs=[
            pl.BlockSpec((tm, W_A), lambda i: (i, 0)),
            pl.BlockSpec((tm, W_B), lambda i: (i, 0)),
            pl.BlockSpec((tm, D_MODEL), lambda i: (i, 0)),
            pl.BlockSpec((W_A + W_B, D_MODEL), lambda i: (0, 0)),
            pl.BlockSpec((1, D_MODEL), lambda i: (0, 0)),
        ],
        out_specs=pl.BlockSpec((tm, D_MODEL), lambda i: (i, 0)),
        compiler_params=_params(("arbitrary",)),
        name=name,
    )(oa, ob, x2d, w_bf16, g)


HGRN_PROMPT_ROWS = 256
HGRN_PROMPT_SEQS = 2
HGRN_SAMPLE_SEQS = 8


def kernel(x_prompt, x_sample, cache_attn_k, cache_attn_v, state_hgrn, ln_in_g, w_in,
           rel_bias, lb_gamma, hg_norm_g, w_out, ln_f_g):
    batch, seq, _ = x_prompt.shape
    dec_batch, dec_seq, _ = x_sample.shape
    depth = w_in.shape[0]
    assert depth == 1 and dec_seq == CHUNK and PAST_LEN % CHUNK == 0
    assert cache_attn_k.shape[2] == WINDOW_ROWS

    lb_all = jnp.cumsum(jax.nn.softmax(lb_gamma.astype(F32), axis=0), axis=0)
    lb = lb_all[0].reshape(1, W_B)
    ng = hg_norm_g[0].reshape(1, W_B)
    g_in = ln_in_g[0].reshape(1, D_MODEL)
    g_f = ln_f_g.reshape(1, D_MODEL)
    w_in_b = w_in[0].astype(BF16)
    w_out_b = w_out[0].astype(BF16)

    xp = x_prompt.reshape(batch * seq, D_MODEL)
    xs = x_sample.reshape(dec_batch * dec_seq, D_MODEL)

    bias_pair, bias_single = _bias_tables(rel_bias[0])

    pp = _inproj(xp, g_in, w_in_b, "inproj_prompt")
    ps = _inproj(xs, g_in, w_in_b, "inproj_sample")

    oa_p = _attn_prompt(pp, bias_pair, batch, seq)
    ck = cache_attn_k[0].reshape(dec_batch, WINDOW_ROWS, W_A)
    cv = cache_attn_v[0].reshape(dec_batch, WINDOW_ROWS, W_A)
    oa_s = _attn_sample(ps, ck, cv, bias_single, dec_batch)

    s0_p = jnp.zeros((batch, H_B, DK_B, DV_B), F32)
    ob_p, st_p = _hgrn(pp.reshape(batch, seq, N_IN), lb, ng, s0_p,
                       HGRN_PROMPT_SEQS, HGRN_PROMPT_ROWS, "hgrn_prompt")
    ob_s, st_s = _hgrn(ps.reshape(dec_batch, dec_seq, N_IN), lb, ng, state_hgrn[0],
                       HGRN_SAMPLE_SEQS, dec_seq, "hgrn_sample")
    ob_p = ob_p.reshape(batch * seq, W_B)
    ob_s = ob_s.reshape(dec_batch * dec_seq, W_B)

    y_p = _outproj(oa_p, ob_p, xp, w_out_b, g_f, "outproj_prompt").reshape(batch, seq, D_MODEL)
    y_s = _outproj(oa_s, ob_s, xs, w_out_b, g_f, "outproj_sample").reshape(
        dec_batch, dec_seq, D_MODEL)

    rows_p = min(WINDOW_ROWS, seq)
    pp3 = pp.reshape(batch, seq, N_IN)
    k_p = pp3[:, seq - rows_p:, W_A:2 * W_A].reshape(1, batch, rows_p, H_A, DH_A)
    v_p = pp3[:, seq - rows_p:, 2 * W_A:3 * W_A].reshape(1, batch, rows_p, H_A, DH_A)
    ps3 = ps.reshape(dec_batch, dec_seq, N_IN)
    k_s = ps3[:, :, W_A:2 * W_A].reshape(1, dec_batch, dec_seq, H_A, DH_A)
    v_s = ps3[:, :, 2 * W_A:3 * W_A].reshape(1, dec_batch, dec_seq, H_A, DH_A)

    return (y_p, y_s, k_p, v_p, st_p[None], k_s, v_s, st_s[None])
```

```python
import jax
import jax.numpy as jnp
from jax import lax
from jax.experimental import pallas as pl
from jax.experimental.pallas import tpu as pltpu

F32 = jnp.float32
BF16 = jnp.bfloat16

D_MODEL = 1024
CHUNK = 64
LEFT_CHUNKS = 8
WINDOW_ROWS = LEFT_CHUNKS * CHUNK
BAND = WINDOW_ROWS + CHUNK
W_A = 512
H_A = 8
DH_A = 64
W_B = 512
H_B = 4
DK_B = 128
DV_B = 128
REL_CLIP = 128
GLA_BLOCK = 16
EPS = 1e-6
ATTN_SCALE = DH_A ** -0.5
NEG_INF = -1e30
N_IN = 4 * W_A + 4 * W_B
PAST_LEN = 2048

LANES = 128
HEAD_PAIRS = H_A // 2
REL_TABLE = 2 * REL_CLIP + 1
REL_TABLE_PAD = 384
TOEPLITZ_LEN = 640
VMEM_LIMIT = 56 * 1024 * 1024


def _params(semantics):
    return pltpu.CompilerParams(dimension_semantics=semantics,
                                vmem_limit_bytes=VMEM_LIMIT)


LOG2E = 1.4426950408889634


def _sigmoid(x):
    return 1.0 / (1.0 + jnp.exp2(x * -LOG2E))


def _silu(x):
    return x * _sigmoid(x)


def _nt(a, b):
    return lax.dot_general(a, b, (((1,), (1,)), ((), ())), preferred_element_type=F32)


def _emit_ahead(n_units, depth, first_fn, second_fn):
    pending = {}
    for t in range(n_units + depth):
        if t < n_units:
            pending[t] = first_fn(t)
        if t >= depth:
            second_fn(t - depth, pending.pop(t - depth))


INPROJ_ROWS = 512
INPROJ_COLS = 512


def _inproj_kernel(x_ref, g_ref, w_ref, o_ref):
    x = x_ref[...]
    ms = jnp.mean(x * x, axis=-1, keepdims=True)
    xn = (x * lax.rsqrt(ms + EPS) * g_ref[...]).astype(BF16)
    for n0 in range(0, N_IN, INPROJ_COLS):
        o_ref[:, n0:n0 + INPROJ_COLS] = jnp.dot(
            xn, w_ref[:, n0:n0 + INPROJ_COLS], preferred_element_type=F32)


def _inproj(x2d, g, w_bf16, name):
    rows = x2d.shape[0]
    tm = min(INPROJ_ROWS, rows)
    return pl.pallas_call(
        _inproj_kernel,
        out_shape=jax.ShapeDtypeStruct((rows, N_IN), F32),
        grid=(rows // tm,),
        in_specs=[
            pl.BlockSpec((tm, D_MODEL), lambda i: (i, 0)),
            pl.BlockSpec((1, D_MODEL), lambda i: (0, 0)),
            pl.BlockSpec((D_MODEL, N_IN), lambda i: (0, 0)),
        ],
        out_specs=pl.BlockSpec((tm, N_IN), lambda i: (i, 0)),
        compiler_params=_params(("arbitrary",)),
        name=name,
    )(x2d, g, w_bf16)


PAIR_BAND = BAND + CHUNK


def _bias_kernel(rb_ref, pair_ref, single_ref):
    m = lax.broadcasted_iota(jnp.int32, (REL_TABLE_PAD, TOEPLITZ_LEN), 1)
    t = lax.broadcasted_iota(jnp.int32, (REL_TABLE_PAD, TOEPLITZ_LEN), 0)
    idx = jnp.clip(BAND - 1 - m, -REL_CLIP, REL_CLIP) + REL_CLIP
    onehot = jnp.where(idx == t, 1.0, 0.0).astype(BF16)
    rb = rb_ref[...]
    hi = rb.astype(BF16)
    r1 = rb - hi.astype(F32)
    mid = r1.astype(BF16)
    lo = (r1 - mid.astype(F32)).astype(BF16)
    toep = (jnp.dot(hi, onehot, preferred_element_type=F32)
            + jnp.dot(mid, onehot, preferred_element_type=F32)
            + jnp.dot(lo, onehot, preferred_element_type=F32))
    key = lax.broadcasted_iota(jnp.int32, (CHUNK, TOEPLITZ_LEN), 1)
    for p in range(HEAD_PAIRS):
        for u in range(2):
            halves = []
            for h in (2 * p, 2 * p + 1):
                rows = jnp.broadcast_to(toep[h:h + 1, :], (CHUNK, TOEPLITZ_LEN))
                shift = (TOEPLITZ_LEN - (CHUNK - 1) + CHUNK * u) % TOEPLITZ_LEN
                rolled = pltpu.roll(rows, shift, 1, stride=1, stride_axis=0)
                seen = (key >= CHUNK * u) & (key < CHUNK * u + BAND)
                halves.append(jnp.where(seen, rolled * LOG2E, NEG_INF))
            tile_t = jnp.concatenate(halves, axis=0).T
            pair_ref[p, :, u * LANES:(u + 1) * LANES] = tile_t
            if u == 0:
                single_ref[p] = tile_t[:BAND, :]


def _bias_tables(rel_bias_l):
    rb = jnp.pad(rel_bias_l, ((0, 0), (0, REL_TABLE_PAD - REL_TABLE)))
    return pl.pallas_call(
        _bias_kernel,
        out_shape=(jax.ShapeDtypeStruct((HEAD_PAIRS, PAIR_BAND, 2 * LANES), F32),
                   jax.ShapeDtypeStruct((HEAD_PAIRS, BAND, LANES), F32)),
        name="rel_bias_table",
    )(rb)


def _attn_probs(qs, k2, bias_t, valid_from):
    first = lax.broadcasted_iota(jnp.int32, (CHUNK, LANES), 1) < DH_A
    parts = []
    for q in qs:
        q = q * (ATTN_SCALE * LOG2E)
        parts += [jnp.where(first, q, 0.0), jnp.where(first, 0.0, q)]
    qbd = jnp.concatenate(parts, axis=0).astype(BF16)
    s = _nt(k2, qbd) + bias_t
    if valid_from is not None:
        key = lax.broadcasted_iota(jnp.int32, s.shape, 0)
        s = jnp.where(key >= valid_from, s, NEG_INF)
    mx = jnp.max(s, axis=0, keepdims=True)
    e = jnp.exp2(s - mx)
    return e.astype(BF16), jnp.sum(e, axis=0, keepdims=True)


def _attn_apply(probs, ags, vt):
    e, denom = probs
    first = lax.broadcasted_iota(jnp.int32, (CHUNK, LANES), 1) < DH_A
    ot = jnp.dot(vt, e, preferred_element_type=F32)
    o2 = (ot * (1.0 / denom)).T
    outs = []
    for u, ag in enumerate(ags):
        r = 2 * CHUNK * u
        o = jnp.where(first, o2[r:r + CHUNK], o2[r + CHUNK:r + 2 * CHUNK])
        outs.append(o * _silu(ag))
    return outs


ATTN_PIPE_DEPTH = 3


def _attn_pipeline(n_units, probs_fn, apply_fn):
    _emit_ahead(n_units, ATTN_PIPE_DEPTH, probs_fn, apply_fn)


ATTN_ROWS = 512


def _attn_prompt_kernel(q_ref, k_ref, v_ref, ag_ref, bias_ref, o_ref, kh_ref, vth_ref):
    i = pl.program_id(1)

    @pl.when(i == 0)
    def _():
        kh_ref[0:WINDOW_ROWS, :] = jnp.zeros((WINDOW_ROWS, W_A), BF16)
        vth_ref[:, :, 0:WINDOW_ROWS] = jnp.zeros((HEAD_PAIRS, LANES, WINDOW_ROWS), BF16)

    kh_ref[WINDOW_ROWS:, :] = k_ref[...].astype(BF16)
    for p in range(HEAD_PAIRS):
        vth_ref[p, :, WINDOW_ROWS:] = v_ref[:, p * LANES:(p + 1) * LANES].T.astype(BF16)

    def body(first_step):
        def unit(t):
            cp, p = divmod(t, HEAD_PAIRS)
            r0 = cp * 2 * CHUNK
            rs = [slice(r0 + u * CHUNK, r0 + (u + 1) * CHUNK) for u in range(2)]
            return cp, p, r0, rs, slice(p * LANES, (p + 1) * LANES)

        def probs(t):
            cp, p, r0, rs, cs = unit(t)
            valid_from = (LEFT_CHUNKS - 2 * cp) * CHUNK if first_step else None
            return _attn_probs([q_ref[r, cs] for r in rs], kh_ref[r0:r0 + PAIR_BAND, cs],
                               bias_ref[p], valid_from)

        def apply(t, pr):
            cp, p, r0, rs, cs = unit(t)
            outs = _attn_apply(pr, [ag_ref[r, cs] for r in rs],
                               vth_ref[p, :, r0:r0 + PAIR_BAND])
            for r, o in zip(rs, outs):
                o_ref[r, cs] = o.astype(o_ref.dtype)

        _attn_pipeline((ATTN_ROWS // (2 * CHUNK)) * HEAD_PAIRS, probs, apply)

    @pl.when(i == 0)
    def _():
        body(True)

    @pl.when(i != 0)
    def _():
        body(False)

    kh_ref[0:WINDOW_ROWS, :] = kh_ref[ATTN_ROWS:ATTN_ROWS + WINDOW_ROWS, :]
    vth_ref[:, :, 0:WINDOW_ROWS] = vth_ref[:, :, ATTN_ROWS:ATTN_ROWS + WINDOW_ROWS]


def _attn_prompt(p2d, bias_pair, batch, seq):
    nt = seq // ATTN_ROWS

    def col(c):
        return pl.BlockSpec((ATTN_ROWS, W_A), lambda b, i: (b * nt + i, c))

    return pl.pallas_call(
        _attn_prompt_kernel,
        out_shape=jax.ShapeDtypeStruct((batch * seq, W_A), BF16),
        grid=(batch, nt),
        in_specs=[col(0), col(1), col(2), col(3),
                  pl.BlockSpec((HEAD_PAIRS, PAIR_BAND, 2 * LANES), lambda b, i: (0, 0, 0))],
        out_specs=pl.BlockSpec((ATTN_ROWS, W_A), lambda b, i: (b * nt + i, 0)),
        scratch_shapes=[pltpu.VMEM((WINDOW_ROWS + ATTN_ROWS, W_A), BF16),
                        pltpu.VMEM((HEAD_PAIRS, LANES, WINDOW_ROWS + ATTN_ROWS), BF16)],
        compiler_params=_params(("arbitrary", "arbitrary")),
        name="attn_prompt",
    )(p2d, p2d, p2d, p2d, bias_pair)


ATTN_SAMPLE_SEQS = 4


def _attn_sample_kernel(q_ref, k_ref, v_ref, ag_ref, ckt_ref, cvt_ref, bias_ref, o_ref,
                        kh_ref, vth_ref):
    for s in range(ATTN_SAMPLE_SEQS):
        rs = slice(s * CHUNK, (s + 1) * CHUNK)
        kh_ref[s, WINDOW_ROWS:, :] = k_ref[rs, :].astype(BF16)
        for p in range(HEAD_PAIRS):
            cs = slice(p * LANES, (p + 1) * LANES)
            kh_ref[s, 0:WINDOW_ROWS, cs] = ckt_ref[s, cs, :].T.astype(BF16)
            vth_ref[s, p, :, 0:WINDOW_ROWS] = cvt_ref[s, cs, :].astype(BF16)
            vth_ref[s, p, :, WINDOW_ROWS:] = v_ref[rs, cs].T.astype(BF16)

    def unit(t):
        s, p = divmod(t, HEAD_PAIRS)
        return s, p, slice(s * CHUNK, (s + 1) * CHUNK), slice(p * LANES, (p + 1) * LANES)

    def probs(t):
        s, p, rs, cs = unit(t)
        return _attn_probs([q_ref[rs, cs]], kh_ref[s, :, cs], bias_ref[p], None)

    def apply(t, pr):
        s, p, rs, cs = unit(t)
        (o,) = _attn_apply(pr, [ag_ref[rs, cs]], vth_ref[s, p])
        o_ref[rs, cs] = o.astype(o_ref.dtype)

    _attn_pipeline(ATTN_SAMPLE_SEQS * HEAD_PAIRS, probs, apply)


def _attn_sample(p2d, cache_kt, cache_vt, bias_single, batch):
    ns = ATTN_SAMPLE_SEQS

    def col(c):
        return pl.BlockSpec((ns * CHUNK, W_A), lambda b: (b, c))

    cache = pl.BlockSpec((ns, W_A, WINDOW_ROWS), lambda b: (b, 0, 0))
    return pl.pallas_call(
        _attn_sample_kernel,
        out_shape=jax.ShapeDtypeStruct((batch * CHUNK, W_A), BF16),
        grid=(batch // ns,),
        in_specs=[col(0), col(1), col(2), col(3), cache, cache,
                  pl.BlockSpec((HEAD_PAIRS, BAND, LANES), lambda b: (0, 0, 0))],
        out_specs=pl.BlockSpec((ns * CHUNK, W_A), lambda b: (b, 0)),
        scratch_shapes=[pltpu.VMEM((ns, BAND, W_A), BF16),
                        pltpu.VMEM((ns, HEAD_PAIRS, LANES, BAND), BF16)],
        compiler_params=_params(("arbitrary",)),
        name="attn_sample",
    )(p2d, p2d, p2d, p2d, cache_kt, cache_vt, bias_single)


HGRN_GROUP_BLOCKS = 4
HGRN_GROUP = HGRN_GROUP_BLOCKS * GLA_BLOCK
HGRN_PIPE_DEPTH = 4


def _block_cumsum(x):
    row = lax.broadcasted_iota(jnp.int32, x.shape, 0) & (GLA_BLOCK - 1)
    s = 1
    while s < GLA_BLOCK:
        x = x + jnp.where(row >= s, pltpu.roll(x, s, 0), 0.0)
        s *= 2
    return x


def _hgrn_kernel(hq_ref, hf_ref, hi_ref, hg_ref, lb_ref, ng_ref, s0_ref,
                 o_ref, sout_ref, st_ref, qt_ref, kt_ref, kd_ref, v_ref,
                 qtf_ref, kdf_ref, bl_ref):
    i = pl.program_id(1)
    nseq, seq_rows = hq_ref.shape[0], hq_ref.shape[1]
    rows = nseq * seq_rows
    nb = rows // GLA_BLOCK

    @pl.when(i == 0)
    def _():
        for s in range(nseq):
            for h in range(H_B):
                st_ref[s, h] = s0_ref[s, h].T

    def flat(ref):
        return ref[...].reshape(rows, W_B)

    lb = lb_ref[...]
    f = lb + (1.0 - lb) * _sigmoid(flat(hf_ref))
    b = _block_cumsum(jnp.log(f) * LOG2E)
    b3 = b.reshape(nb, GLA_BLOCK, W_B)
    bl = jnp.broadcast_to(b3[:, GLA_BLOCK - 1:GLA_BLOCK, :],
                          (nb, GLA_BLOCK, W_B)).reshape(rows, W_B)
    kk = 1.0 - f
    qt = _silu(flat(hq_ref)) * jnp.exp2(b)
    kd = kk * jnp.exp2(bl - b)
    qtf_ref[...] = qt
    kdf_ref[...] = kd
    bl_ref[...] = bl
    qt_ref[...] = qt.astype(BF16)
    kt_ref[...] = (kk * jnp.exp2(-b)).astype(BF16)
    kd_ref[...] = kd.astype(BF16)
    v_ref[...] = flat(hi_ref).astype(BF16)

    r_i = lax.broadcasted_iota(jnp.int32, (HGRN_GROUP, HGRN_GROUP), 0)
    c_i = lax.broadcasted_iota(jnp.int32, (HGRN_GROUP, HGRN_GROUP), 1)
    rb, cb = r_i // GLA_BLOCK, c_i // GLA_BLOCK
    m_diag = (rb == cb) & (r_i >= c_i)
    m_adj = rb == cb + 1
    m_far = rb >= cb + 2

    groups = seq_rows // HGRN_GROUP
    st = {(s, h): st_ref[s, h] for s in range(nseq) for h in range(H_B)}

    def unit(t):
        sg, h = divmod(t, H_B)
        s, g = divmod(sg, groups)
        r0 = s * seq_rows + g * HGRN_GROUP
        return s, g, h, r0, slice(r0, r0 + HGRN_GROUP), slice(h * DK_B, (h + 1) * DK_B)

    def block_decays(r0, cs):
        return [bl_ref[r0 + j * GLA_BLOCK:r0 + j * GLA_BLOCK + 1, cs]
                for j in range(HGRN_GROUP_BLOCKS)]

    def scaled(x, log2_scales):
        parts = []
        for j, sc in enumerate(log2_scales):
            xb = x[j * GLA_BLOCK:(j + 1) * GLA_BLOCK]
            parts.append(xb if sc is None else xb * jnp.exp2(sc))
        return jnp.concatenate(parts, axis=0).astype(BF16)

    def local(t):
        s, g, h, r0, rs, cs = unit(t)
        d0, d1, d2, d3 = block_decays(r0, cs)
        qt_g = qt_ref[rs, cs]
        kdf = kdf_ref[rs, cs]
        q1 = scaled(qtf_ref[rs, cs], [None, None, None, d2])
        k1 = scaled(kdf, [d1, None, None, None])
        a = jnp.where(m_diag, _nt(qt_g, kt_ref[rs, cs]),
                      jnp.where(m_adj, _nt(qt_g, kd_ref[rs, cs]),
                                jnp.where(m_far, _nt(q1, k1), 0.0)))
        kdp = scaled(kdf, [(d1 + d2) + d3, d2 + d3, d3, None])
        ut = lax.dot_general(v_ref[rs, cs], kdp, (((0,), (0,)), ((), ())),
                             preferred_element_type=F32)
        return a.astype(BF16), ut

    def carry(t, loc):
        s, g, h, r0, rs, cs = unit(t)
        a, ut = loc
        d0, d1, d2, d3 = block_decays(r0, cs)
        qs = scaled(qtf_ref[rs, cs], [None, d0, d0 + d1, (d0 + d1) + d2])
        o = (_nt(qs, st[s, h].astype(BF16))
             + jnp.dot(a, v_ref[rs, cs], preferred_element_type=F32))
        dec = jnp.exp2(((d0 + d1) + d2) + d3)
        st[s, h] = st[s, h] * dec + ut
        y = o * lax.rsqrt(jnp.mean(o * o, axis=-1, keepdims=True) + EPS) * ng_ref[:, cs]
        gs = slice(g * HGRN_GROUP, (g + 1) * HGRN_GROUP)
        o_ref[s, gs, cs] = (y * _silu(hg_ref[s, gs, cs])).astype(o_ref.dtype)

    _emit_ahead(nseq * groups * H_B, HGRN_PIPE_DEPTH, local, carry)
    for (s, h), val in st.items():
        st_ref[s, h] = val

    @pl.when(i == pl.num_programs(1) - 1)
    def _():
        for s in range(nseq):
            for h in range(H_B):
                sout_ref[s, h] = st_ref[s, h].T


def _hgrn(p3d, lb, ng, s0, nseq, rows, name):
    batch, seq, _ = p3d.shape
    assert batch % nseq == 0 and seq % rows == 0 and rows % HGRN_GROUP == 0

    def col(c):
        return pl.BlockSpec((nseq, rows, W_B), lambda b, i: (b, i, c))

    vec = pl.BlockSpec((1, W_B), lambda b, i: (0, 0))
    state = pl.BlockSpec((nseq, H_B, DK_B, DV_B), lambda b, i: (b, 0, 0, 0))
    n = nseq * rows
    return pl.pallas_call(
        _hgrn_kernel,
        out_shape=(jax.ShapeDtypeStruct((batch, seq, W_B), BF16),
                   jax.ShapeDtypeStruct((batch, H_B, DK_B, DV_B), F32)),
        grid=(batch // nseq, seq // rows),
        in_specs=[col(4), col(5), col(6), col(7), vec, vec, state],
        out_specs=(pl.BlockSpec((nseq, rows, W_B), lambda b, i: (b, i, 0)), state),
        scratch_shapes=([pltpu.VMEM((nseq, H_B, DV_B, DK_B), F32)]
                        + [pltpu.VMEM((n, W_B), BF16)] * 4
                        + [pltpu.VMEM((n, W_B), F32)] * 3),
        compiler_params=_params(("arbitrary", "arbitrary")),
        name=name,
    )(p3d, p3d, p3d, p3d, lb, ng, s0)


OUTPROJ_ROWS = 1024


def _outproj_kernel(oa_ref, ob_ref, x_ref, w_ref, g_ref, y_ref):
    h = (x_ref[...]
         + jnp.dot(oa_ref[...], w_ref[0:W_A, :], preferred_element_type=F32)
         + jnp.dot(ob_ref[...], w_ref[W_A:, :], preferred_element_type=F32))
    ms = jnp.mean(h * h, axis=-1, keepdims=True)
    y_ref[...] = h * lax.rsqrt(ms + EPS) * g_ref[...]


def _outproj(oa, ob, x2d, w_bf16, g, name):
    rows = x2d.shape[0]
    tm = min(OUTPROJ_ROWS, rows)
    return pl.pallas_call(
        _outproj_kernel,
        out_shape=jax.ShapeDtypeStruct((rows, D_MODEL), F32),
        grid=(rows // tm,),
        in_specs=[
            pl.BlockSpec((tm, W_A), lambda i: (i, 0)),
            pl.BlockSpec((tm, W_B), lambda i: (i, 0)),
            pl.BlockSpec((tm, D_MODEL), lambda i: (i, 0)),
            pl.BlockSpec((W_A + W_B, D_MODEL), lambda i: (0, 0)),
            pl.BlockSpec((1, D_MODEL), lambda i: (0, 0)),
        ],
        out_specs=pl.BlockSpec((tm, D_MODEL), lambda i: (i, 0)),
        compiler_params=_params(("arbitrary",)),
        name=name,
    )(oa, ob, x2d, w_bf16, g)


HGRN_PROMPT_ROWS = 256
HGRN_PROMPT_SEQS = 2
HGRN_SAMPLE_SEQS = 8


def kernel(x_prompt, x_sample, cache_attn_k, cache_attn_v, state_hgrn, ln_in_g, w_in,
           rel_bias, lb_gamma, hg_norm_g, w_out, ln_f_g):
    batch, seq, _ = x_prompt.shape
    dec_batch, dec_seq, _ = x_sample.shape
    depth = w_in.shape[0]
    assert depth == 1 and dec_seq == CHUNK and PAST_LEN % CHUNK == 0
    assert cache_attn_k.shape[2] == WINDOW_ROWS

    lb_all = jnp.cumsum(jax.nn.softmax(lb_gamma.astype(F32), axis=0), axis=0)
    lb = lb_all[0].reshape(1, W_B)
    ng = hg_norm_g[0].reshape(1, W_B)
    g_in = ln_in_g[0].reshape(1, D_MODEL)
    g_f = ln_f_g.reshape(1, D_MODEL)
    w_in_b = w_in[0].astype(BF16)
    w_out_b = w_out[0].astype(BF16)

    xp = x_prompt.reshape(batch * seq, D_MODEL)
    xs = x_sample.reshape(dec_batch * dec_seq, D_MODEL)

    bias_pair, bias_single = _bias_tables(rel_bias[0])

    pp = _inproj(xp, g_in, w_in_b, "inproj_prompt")
    ps = _inproj(xs, g_in, w_in_b, "inproj_sample")

    oa_p = _attn_prompt(pp, bias_pair, batch, seq)
    ckt = jnp.transpose(cache_attn_k[0], (0, 2, 3, 1)).reshape(dec_batch, W_A, WINDOW_ROWS)
    cvt = jnp.transpose(cache_attn_v[0], (0, 2, 3, 1)).reshape(dec_batch, W_A, WINDOW_ROWS)
    oa_s = _attn_sample(ps, ckt, cvt, bias_single, dec_batch)

    s0_p = jnp.zeros((batch, H_B, DK_B, DV_B), F32)
    ob_p, st_p = _hgrn(pp.reshape(batch, seq, N_IN), lb, ng, s0_p,
                       HGRN_PROMPT_SEQS, HGRN_PROMPT_ROWS, "hgrn_prompt")
    ob_s, st_s = _hgrn(ps.reshape(dec_batch, dec_seq, N_IN), lb, ng, state_hgrn[0],
                       HGRN_SAMPLE_SEQS, dec_seq, "hgrn_sample")
    ob_p = ob_p.reshape(batch * seq, W_B)
    ob_s = ob_s.reshape(dec_batch * dec_seq, W_B)

    y_p = _outproj(oa_p, ob_p, xp, w_out_b, g_f, "outproj_prompt").reshape(batch, seq, D_MODEL)
    y_s = _outproj(oa_s, ob_s, xs, w_out_b, g_f, "outproj_sample").reshape(
        dec_batch, dec_seq, D_MODEL)

    rows_p = min(WINDOW_ROWS, seq)
    pp3 = pp.reshape(batch, seq, N_IN)
    k_p = pp3[:, seq - rows_p:, W_A:2 * W_A].reshape(1, batch, rows_p, H_A, DH_A)
    v_p = pp3[:, seq - rows_p:, 2 * W_A:3 * W_A].reshape(1, batch, rows_p, H_A, DH_A)
    ps3 = ps.reshape(dec_batch, dec_seq, N_IN)
    k_s = ps3[:, :, W_A:2 * W_A].reshape(1, dec_batch, dec_seq, H_A, DH_A)
    v_s = ps3[:, :, 2 * W_A:3 * W_A].reshape(1, dec_batch, dec_seq, H_A, DH_A)

    return (y_p, y_s, k_p, v_p, st_p[None], k_s, v_s, st_s[None])
```

```python
import jax
import jax.numpy as jnp
from jax import lax
from jax.experimental import pallas as pl
from jax.experimental.pallas import tpu as pltpu

F32 = jnp.float32
BF16 = jnp.bfloat16

D_MODEL = 1024
CHUNK = 64
LEFT_CHUNKS = 8
WINDOW_ROWS = LEFT_CHUNKS * CHUNK
BAND = WINDOW_ROWS + CHUNK
W_A = 512
H_A = 8
DH_A = 64
W_B = 512
H_B = 4
DK_B = 128
DV_B = 128
REL_CLIP = 128
GLA_BLOCK = 16
EPS = 1e-6
ATTN_SCALE = DH_A ** -0.5
NEG_INF = -1e30
N_IN = 4 * W_A + 4 * W_B
PAST_LEN = 2048

LANES = 128
HEAD_PAIRS = H_A // 2
REL_TABLE = 2 * REL_CLIP + 1
REL_TABLE_PAD = 384
TOEPLITZ_LEN = 640
VMEM_LIMIT = 56 * 1024 * 1024


def _params(semantics):
    return pltpu.CompilerParams(dimension_semantics=semantics,
                                vmem_limit_bytes=VMEM_LIMIT)


LOG2E = 1.4426950408889634


def _sigmoid(x):
    return 1.0 / (1.0 + jnp.exp2(x * -LOG2E))


def _silu(x):
    return x * _sigmoid(x)


def _nt(a, b):
    return lax.dot_general(a, b, (((1,), (1,)), ((), ())), preferred_element_type=F32)


def _emit_ahead(n_units, depth, first_fn, second_fn):
    pending = {}
    for t in range(n_units + depth):
        if t < n_units:
            pending[t] = first_fn(t)
        if t >= depth:
            second_fn(t - depth, pending.pop(t - depth))


INPROJ_ROWS = 512
INPROJ_COLS = 512


def _inproj_kernel(x_ref, g_ref, w_ref, o_ref):
    x = x_ref[...]
    ms = jnp.mean(x * x, axis=-1, keepdims=True)
    xn = (x * lax.rsqrt(ms + EPS) * g_ref[...]).astype(BF16)
    for n0 in range(0, N_IN, INPROJ_COLS):
        o_ref[:, n0:n0 + INPROJ_COLS] = jnp.dot(
            xn, w_ref[:, n0:n0 + INPROJ_COLS], preferred_element_type=F32)


def _inproj(x2d, g, w_bf16, name):
    rows = x2d.shape[0]
    tm = min(INPROJ_ROWS, rows)
    return pl.pallas_call(
        _inproj_kernel,
        out_shape=jax.ShapeDtypeStruct((rows, N_IN), F32),
        grid=(rows // tm,),
        in_specs=[
            pl.BlockSpec((tm, D_MODEL), lambda i: (i, 0)),
            pl.BlockSpec((1, D_MODEL), lambda i: (0, 0)),
            pl.BlockSpec((D_MODEL, N_IN), lambda i: (0, 0)),
        ],
        out_specs=pl.BlockSpec((tm, N_IN), lambda i: (i, 0)),
        compiler_params=_params(("arbitrary",)),
        name=name,
    )(x2d, g, w_bf16)


PAIR_BAND = BAND + CHUNK


def _bias_kernel(rb_ref, pair_ref, single_ref):
    m = lax.broadcasted_iota(jnp.int32, (REL_TABLE_PAD, TOEPLITZ_LEN), 1)
    t = lax.broadcasted_iota(jnp.int32, (REL_TABLE_PAD, TOEPLITZ_LEN), 0)
    idx = jnp.clip(BAND - 1 - m, -REL_CLIP, REL_CLIP) + REL_CLIP
    onehot = jnp.where(idx == t, 1.0, 0.0).astype(BF16)
    rb = rb_ref[...]
    hi = rb.astype(BF16)
    r1 = rb - hi.astype(F32)
    mid = r1.astype(BF16)
    lo = (r1 - mid.astype(F32)).astype(BF16)
    toep = (jnp.dot(hi, onehot, preferred_element_type=F32)
            + jnp.dot(mid, onehot, preferred_element_type=F32)
            + jnp.dot(lo, onehot, preferred_element_type=F32))
    key = lax.broadcasted_iota(jnp.int32, (CHUNK, TOEPLITZ_LEN), 1)
    for p in range(HEAD_PAIRS):
        for u in range(2):
            halves = []
            for h in (2 * p, 2 * p + 1):
                rows = jnp.broadcast_to(toep[h:h + 1, :], (CHUNK, TOEPLITZ_LEN))
                shift = (TOEPLITZ_LEN - (CHUNK - 1) + CHUNK * u) % TOEPLITZ_LEN
                rolled = pltpu.roll(rows, shift, 1, stride=1, stride_axis=0)
                seen = (key >= CHUNK * u) & (key < CHUNK * u + BAND)
                halves.append(jnp.where(seen, rolled * LOG2E, NEG_INF))
            tile_t = jnp.concatenate(halves, axis=0).T
            pair_ref[p, :, u * LANES:(u + 1) * LANES] = tile_t
            if u == 0:
                single_ref[p] = tile_t[:BAND, :]


def _bias_tables(rel_bias_l):
    rb = jnp.pad(rel_bias_l, ((0, 0), (0, REL_TABLE_PAD - REL_TABLE)))
    return pl.pallas_call(
        _bias_kernel,
        out_shape=(jax.ShapeDtypeStruct((HEAD_PAIRS, PAIR_BAND, 2 * LANES), F32),
                   jax.ShapeDtypeStruct((HEAD_PAIRS, BAND, LANES), F32)),
        name="rel_bias_table",
    )(rb)


def _attn_probs(qs, k2, bias_t, valid_from):
    first = lax.broadcasted_iota(jnp.int32, (CHUNK, LANES), 1) < DH_A
    parts = []
    for q in qs:
        q = q * (ATTN_SCALE * LOG2E)
        parts += [jnp.where(first, q, 0.0), jnp.where(first, 0.0, q)]
    qbd = jnp.concatenate(parts, axis=0).astype(BF16)
    s = _nt(k2, qbd) + bias_t
    if valid_from is not None:
        key = lax.broadcasted_iota(jnp.int32, s.shape, 0)
        s = jnp.where(key >= valid_from, s, NEG_INF)
    mx = jnp.max(s, axis=0, keepdims=True)
    e = jnp.exp2(s - mx)
    return e.astype(BF16), jnp.sum(e, axis=0, keepdims=True)


def _attn_apply(probs, ags, vt):
    e, denom = probs
    first = lax.broadcasted_iota(jnp.int32, (CHUNK, LANES), 1) < DH_A
    ot = jnp.dot(vt, e, preferred_element_type=F32)
    o2 = (ot * (1.0 / denom)).T
    outs = []
    for u, ag in enumerate(ags):
        r = 2 * CHUNK * u
        o = jnp.where(first, o2[r:r + CHUNK], o2[r + CHUNK:r + 2 * CHUNK])
        outs.append(o * _silu(ag))
    return outs


ATTN_PIPE_DEPTH = 3


def _attn_pipeline(n_units, probs_fn, apply_fn):
    _emit_ahead(n_units, ATTN_PIPE_DEPTH, probs_fn, apply_fn)


ATTN_ROWS = 1024


def _attn_prompt_kernel(q_ref, k_ref, v_ref, ag_ref, bias_ref, o_ref, kh_ref, vth_ref):
    i = pl.program_id(1)

    @pl.when(i == 0)
    def _():
        kh_ref[0:WINDOW_ROWS, :] = jnp.zeros((WINDOW_ROWS, W_A), BF16)
        vth_ref[:, :, 0:WINDOW_ROWS] = jnp.zeros((HEAD_PAIRS, LANES, WINDOW_ROWS), BF16)

    kh_ref[WINDOW_ROWS:, :] = k_ref[...].astype(BF16)
    for p in range(HEAD_PAIRS):
        vth_ref[p, :, WINDOW_ROWS:] = v_ref[:, p * LANES:(p + 1) * LANES].T.astype(BF16)

    def body(first_step):
        def unit(t):
            cp, p = divmod(t, HEAD_PAIRS)
            r0 = cp * 2 * CHUNK
            rs = [slice(r0 + u * CHUNK, r0 + (u + 1) * CHUNK) for u in range(2)]
            return cp, p, r0, rs, slice(p * LANES, (p + 1) * LANES)

        def probs(t):
            cp, p, r0, rs, cs = unit(t)
            valid_from = (LEFT_CHUNKS - 2 * cp) * CHUNK if first_step else None
            return _attn_probs([q_ref[r, cs] for r in rs], kh_ref[r0:r0 + PAIR_BAND, cs],
                               bias_ref[p], valid_from)

        def apply(t, pr):
            cp, p, r0, rs, cs = unit(t)
            outs = _attn_apply(pr, [ag_ref[r, cs] for r in rs],
                               vth_ref[p, :, r0:r0 + PAIR_BAND])
            for r, o in zip(rs, outs):
                o_ref[r, cs] = o.astype(o_ref.dtype)

        _attn_pipeline((ATTN_ROWS // (2 * CHUNK)) * HEAD_PAIRS, probs, apply)

    @pl.when(i == 0)
    def _():
        body(True)

    @pl.when(i != 0)
    def _():
        body(False)

    kh_ref[0:WINDOW_ROWS, :] = kh_ref[ATTN_ROWS:ATTN_ROWS + WINDOW_ROWS, :]
    vth_ref[:, :, 0:WINDOW_ROWS] = vth_ref[:, :, ATTN_ROWS:ATTN_ROWS + WINDOW_ROWS]


def _attn_prompt(p2d, bias_pair, batch, seq):
    nt = seq // ATTN_ROWS

    def col(c):
        return pl.BlockSpec((ATTN_ROWS, W_A), lambda b, i: (b * nt + i, c))

    return pl.pallas_call(
        _attn_prompt_kernel,
        out_shape=jax.ShapeDtypeStruct((batch * seq, W_A), BF16),
        grid=(batch, nt),
        in_specs=[col(0), col(1), col(2), col(3),
                  pl.BlockSpec((HEAD_PAIRS, PAIR_BAND, 2 * LANES), lambda b, i: (0, 0, 0))],
        out_specs=pl.BlockSpec((ATTN_ROWS, W_A), lambda b, i: (b * nt + i, 0)),
        scratch_shapes=[pltpu.VMEM((WINDOW_ROWS + ATTN_ROWS, W_A), BF16),
                        pltpu.VMEM((HEAD_PAIRS, LANES, WINDOW_ROWS + ATTN_ROWS), BF16)],
        compiler_params=_params(("arbitrary", "arbitrary")),
        name="attn_prompt",
    )(p2d, p2d, p2d, p2d, bias_pair)


ATTN_SAMPLE_SEQS = 4


def _attn_sample_kernel(q_ref, k_ref, v_ref, ag_ref, ckt_ref, cvt_ref, bias_ref, o_ref,
                        kh_ref, vth_ref):
    for s in range(ATTN_SAMPLE_SEQS):
        rs = slice(s * CHUNK, (s + 1) * CHUNK)
        kh_ref[s, WINDOW_ROWS:, :] = k_ref[rs, :].astype(BF16)
        for p in range(HEAD_PAIRS):
            cs = slice(p * LANES, (p + 1) * LANES)
            kh_ref[s, 0:WINDOW_ROWS, cs] = ckt_ref[s, cs, :].T.astype(BF16)
            vth_ref[s, p, :, 0:WINDOW_ROWS] = cvt_ref[s, cs, :].astype(BF16)
            vth_ref[s, p, :, WINDOW_ROWS:] = v_ref[rs, cs].T.astype(BF16)

    def unit(t):
        s, p = divmod(t, HEAD_PAIRS)
        return s, p, slice(s * CHUNK, (s + 1) * CHUNK), slice(p * LANES, (p + 1) * LANES)

    def probs(t):
        s, p, rs, cs = unit(t)
        return _attn_probs([q_ref[rs, cs]], kh_ref[s, :, cs], bias_ref[p], None)

    def apply(t, pr):
        s, p, rs, cs = unit(t)
        (o,) = _attn_apply(pr, [ag_ref[rs, cs]], vth_ref[s, p])
        o_ref[rs, cs] = o.astype(o_ref.dtype)

    _attn_pipeline(ATTN_SAMPLE_SEQS * HEAD_PAIRS, probs, apply)


def _attn_sample(p2d, cache_kt, cache_vt, bias_single, batch):
    ns = ATTN_SAMPLE_SEQS

    def col(c):
        return pl.BlockSpec((ns * CHUNK, W_A), lambda b: (b, c))

    cache = pl.BlockSpec((ns, W_A, WINDOW_ROWS), lambda b: (b, 0, 0))
    return pl.pallas_call(
        _attn_sample_kernel,
        out_shape=jax.ShapeDtypeStruct((batch * CHUNK, W_A), BF16),
        grid=(batch // ns,),
        in_specs=[col(0), col(1), col(2), col(3), cache, cache,
                  pl.BlockSpec((HEAD_PAIRS, BAND, LANES), lambda b: (0, 0, 0))],
        out_specs=pl.BlockSpec((ns * CHUNK, W_A), lambda b: (b, 0)),
        scratch_shapes=[pltpu.VMEM((ns, BAND, W_A), BF16),
                        pltpu.VMEM((ns, HEAD_PAIRS, LANES, BAND), BF16)],
        compiler_params=_params(("arbitrary",)),
        name="attn_sample",
    )(p2d, p2d, p2d, p2d, cache_kt, cache_vt, bias_single)


HGRN_GROUP_BLOCKS = 4
HGRN_GROUP = HGRN_GROUP_BLOCKS * GLA_BLOCK
HGRN_PIPE_DEPTH = 4


def _block_cumsum(x):
    row = lax.broadcasted_iota(jnp.int32, x.shape, 0) & (GLA_BLOCK - 1)
    s = 1
    while s < GLA_BLOCK:
        x = x + jnp.where(row >= s, pltpu.roll(x, s, 0), 0.0)
        s *= 2
    return x


def _hgrn_kernel(hq_ref, hf_ref, hi_ref, hg_ref, lb_ref, ng_ref, s0_ref,
                 o_ref, sout_ref, st_ref, qt_ref, kt_ref, kd_ref, v_ref,
                 qtf_ref, kdf_ref, bl_ref):
    i = pl.program_id(1)
    nseq, seq_rows = hq_ref.shape[0], hq_ref.shape[1]
    rows = nseq * seq_rows
    nb = rows // GLA_BLOCK

    @pl.when(i == 0)
    def _():
        for s in range(nseq):
            for h in range(H_B):
                st_ref[s, h] = s0_ref[s, h].T

    def flat(ref):
        return ref[...].reshape(rows, W_B)

    lb = lb_ref[...]
    f = lb + (1.0 - lb) * _sigmoid(flat(hf_ref))
    b = _block_cumsum(jnp.log(f) * LOG2E)
    b3 = b.reshape(nb, GLA_BLOCK, W_B)
    bl = jnp.broadcast_to(b3[:, GLA_BLOCK - 1:GLA_BLOCK, :],
                          (nb, GLA_BLOCK, W_B)).reshape(rows, W_B)
    kk = 1.0 - f
    qt = _silu(flat(hq_ref)) * jnp.exp2(b)
    kd = kk * jnp.exp2(bl - b)
    qtf_ref[...] = qt
    kdf_ref[...] = kd
    bl_ref[...] = bl
    qt_ref[...] = qt.astype(BF16)
    kt_ref[...] = (kk * jnp.exp2(-b)).astype(BF16)
    kd_ref[...] = kd.astype(BF16)
    v_ref[...] = flat(hi_ref).astype(BF16)

    r_i = lax.broadcasted_iota(jnp.int32, (HGRN_GROUP, HGRN_GROUP), 0)
    c_i = lax.broadcasted_iota(jnp.int32, (HGRN_GROUP, HGRN_GROUP), 1)
    rb, cb = r_i // GLA_BLOCK, c_i // GLA_BLOCK
    m_diag = (rb == cb) & (r_i >= c_i)
    m_adj = rb == cb + 1
    m_far = rb >= cb + 2

    groups = seq_rows // HGRN_GROUP
    st = {(s, h): st_ref[s, h] for s in range(nseq) for h in range(H_B)}

    def unit(t):
        sg, h = divmod(t, H_B)
        s, g = divmod(sg, groups)
        r0 = s * seq_rows + g * HGRN_GROUP
        return s, g, h, r0, slice(r0, r0 + HGRN_GROUP), slice(h * DK_B, (h + 1) * DK_B)

    def block_decays(r0, cs):
        return [bl_ref[r0 + j * GLA_BLOCK:r0 + j * GLA_BLOCK + 1, cs]
                for j in range(HGRN_GROUP_BLOCKS)]

    def scaled(x, log2_scales):
        parts = []
        for j, sc in enumerate(log2_scales):
            xb = x[j * GLA_BLOCK:(j + 1) * GLA_BLOCK]
            parts.append(xb if sc is None else xb * jnp.exp2(sc))
        return jnp.concatenate(parts, axis=0).astype(BF16)

    def local(t):
        s, g, h, r0, rs, cs = unit(t)
        d0, d1, d2, d3 = block_decays(r0, cs)
        qt_g = qt_ref[rs, cs]
        kdf = kdf_ref[rs, cs]
        q1 = scaled(qtf_ref[rs, cs], [None, None, None, d2])
        k1 = scaled(kdf, [d1, None, None, None])
        a = jnp.where(m_diag, _nt(qt_g, kt_ref[rs, cs]),
                      jnp.where(m_adj, _nt(qt_g, kd_ref[rs, cs]),
                                jnp.where(m_far, _nt(q1, k1), 0.0)))
        kdp = scaled(kdf, [(d1 + d2) + d3, d2 + d3, d3, None])
        ut = lax.dot_general(v_ref[rs, cs], kdp, (((0,), (0,)), ((), ())),
                             preferred_element_type=F32)
        return a.astype(BF16), ut

    def carry(t, loc):
        s, g, h, r0, rs, cs = unit(t)
        a, ut = loc
        d0, d1, d2, d3 = block_decays(r0, cs)
        qs = scaled(qtf_ref[rs, cs], [None, d0, d0 + d1, (d0 + d1) + d2])
        o = (_nt(qs, st[s, h].astype(BF16))
             + jnp.dot(a, v_ref[rs, cs], preferred_element_type=F32))
        dec = jnp.exp2(((d0 + d1) + d2) + d3)
        st[s, h] = st[s, h] * dec + ut
        y = o * lax.rsqrt(jnp.mean(o * o, axis=-1, keepdims=True) + EPS) * ng_ref[:, cs]
        gs = slice(g * HGRN_GROUP, (g + 1) * HGRN_GROUP)
        o_ref[s, gs, cs] = (y * _silu(hg_ref[s, gs, cs])).astype(o_ref.dtype)

    _emit_ahead(nseq * groups * H_B, HGRN_PIPE_DEPTH, local, carry)
    for (s, h), val in st.items():
        st_ref[s, h] = val

    @pl.when(i == pl.num_programs(1) - 1)
    def _():
        for s in range(nseq):
            for h in range(H_B):
                sout_ref[s, h] = st_ref[s, h].T


def _hgrn(p3d, lb, ng, s0, nseq, rows, name):
    batch, seq, _ = p3d.shape
    assert batch % nseq == 0 and seq % rows == 0 and rows % HGRN_GROUP == 0

    def col(c):
        return pl.BlockSpec((nseq, rows, W_B), lambda b, i: (b, i, c))

    vec = pl.BlockSpec((1, W_B), lambda b, i: (0, 0))
    state = pl.BlockSpec((nseq, H_B, DK_B, DV_B), lambda b, i: (b, 0, 0, 0))
    n = nseq * rows
    return pl.pallas_call(
        _hgrn_kernel,
        out_shape=(jax.ShapeDtypeStruct((batch, seq, W_B), BF16),
                   jax.ShapeDtypeStruct((batch, H_B, DK_B, DV_B), F32)),
        grid=(batch // nseq, seq // rows),
        in_specs=[col(4), col(5), col(6), col(7), vec, vec, state],
        out_specs=(pl.BlockSpec((nseq, rows, W_B), lambda b, i: (b, i, 0)), state),
        scratch_shapes=([pltpu.VMEM((nseq, H_B, DV_B, DK_B), F32)]
                        + [pltpu.VMEM((n, W_B), BF16)] * 4
                        + [pltpu.VMEM((n, W_B), F32)] * 3),
        compiler_params=_params(("arbitrary", "arbitrary")),
        name=name,
    )(p3d, p3d, p3d, p3d, lb, ng, s0)


OUTPROJ_ROWS = 2048


def _outproj_kernel(oa_ref, ob_ref, x_ref, w_ref, g_ref, y_ref):
    h = (x_ref[...]
         + jnp.dot(oa_ref[...], w_ref[0:W_A, :], preferred_element_type=F32)
         + jnp.dot(ob_ref[...], w_ref[W_A:, :], preferred_element_type=F32))
    ms = jnp.mean(h * h, axis=-1, keepdims=True)
    y_ref[...] = h * lax.rsqrt(ms + EPS) * g_ref[...]


def _outproj(oa, ob, x2d, w_bf16, g, name):
    rows = x2d.shape[0]
    tm = min(OUTPROJ_ROWS, rows)
    return pl.pallas_call(
        _outproj_kernel,
        out_shape=jax.ShapeDtypeStruct((rows, D_MODEL), F32),
        grid=(rows // tm,),
        in_specs=[
            pl.BlockSpec((tm, W_A), lambda i: (i, 0)),
            pl.BlockSpec((tm, W_B), lambda i: (i, 0)),
            pl.BlockSpec((tm, D_MODEL), lambda i: (i, 0)),
            pl.BlockSpec((W_A + W_B, D_MODEL), lambda i: (0, 0)),
            pl.BlockSpec((1, D_MODEL), lambda i: (0, 0)),
        ],
        out_specs=pl.BlockSpec((tm, D_MODEL), lambda i: (i, 0)),
        compiler_params=_params(("arbitrary",)),
        name=name,
    )(oa, ob, x2d, w_bf16, g)


HGRN_PROMPT_ROWS = 512
HGRN_PROMPT_SEQS = 2
HGRN_SAMPLE_SEQS = 8


def kernel(x_prompt, x_sample, cache_attn_k, cache_attn_v, state_hgrn, ln_in_g, w_in,
           rel_bias, lb_gamma, hg_norm_g, w_out, ln_f_g):
    batch, seq, _ = x_prompt.shape
    dec_batch, dec_seq, _ = x_sample.shape
    depth = w_in.shape[0]
    assert depth == 1 and dec_seq == CHUNK and PAST_LEN % CHUNK == 0
    assert cache_attn_k.shape[2] == WINDOW_ROWS

    lb_all = jnp.cumsum(jax.nn.softmax(lb_gamma.astype(F32), axis=0), axis=0)
    lb = lb_all[0].reshape(1, W_B)
    ng = hg_norm_g[0].reshape(1, W_B)
    g_in = ln_in_g[0].reshape(1, D_MODEL)
    g_f = ln_f_g.reshape(1, D_MODEL)
    w_in_b = w_in[0].astype(BF16)
    w_out_b = w_out[0].astype(BF16)

    xp = x_prompt.reshape(batch * seq, D_MODEL)
    xs = x_sample.reshape(dec_batch * dec_seq, D_MODEL)

    bias_pair, bias_single = _bias_tables(rel_bias[0])

    pp = _inproj(xp, g_in, w_in_b, "inproj_prompt")
    ps = _inproj(xs, g_in, w_in_b, "inproj_sample")

    oa_p = _attn_prompt(pp, bias_pair, batch, seq)
    ckt = jnp.transpose(cache_attn_k[0], (0, 2, 3, 1)).reshape(dec_batch, W_A, WINDOW_ROWS)
    cvt = jnp.transpose(cache_attn_v[0], (0, 2, 3, 1)).reshape(dec_batch, W_A, WINDOW_ROWS)
    oa_s = _attn_sample(ps, ckt, cvt, bias_single, dec_batch)

    s0_p = jnp.zeros((batch, H_B, DK_B, DV_B), F32)
    ob_p, st_p = _hgrn(pp.reshape(batch, seq, N_IN), lb, ng, s0_p,
                       HGRN_PROMPT_SEQS, HGRN_PROMPT_ROWS, "hgrn_prompt")
    ob_s, st_s = _hgrn(ps.reshape(dec_batch, dec_seq, N_IN), lb, ng, state_hgrn[0],
                       HGRN_SAMPLE_SEQS, dec_seq, "hgrn_sample")
    ob_p = ob_p.reshape(batch * seq, W_B)
    ob_s = ob_s.reshape(dec_batch * dec_seq, W_B)

    y_p = _outproj(oa_p, ob_p, xp, w_out_b, g_f, "outproj_prompt").reshape(batch, seq, D_MODEL)
    y_s = _outproj(oa_s, ob_s, xs, w_out_b, g_f, "outproj_sample").reshape(
        dec_batch, dec_seq, D_MODEL)

    rows_p = min(WINDOW_ROWS, seq)
    pp3 = pp.reshape(batch, seq, N_IN)
    k_p = pp3[:, seq - rows_p:, W_A:2 * W_A].reshape(1, batch, rows_p, H_A, DH_A)
    v_p = pp3[:, seq - rows_p:, 2 * W_A:3 * W_A].reshape(1, batch, rows_p, H_A, DH_A)
    ps3 = ps.reshape(dec_batch, dec_seq, N_IN)
    k_s = ps3[:, :, W_A:2 * W_A].reshape(1, dec_batch, dec_seq, H_A, DH_A)
    v_s = ps3[:, :, 2 * W_A:3 * W_A].reshape(1, dec_batch, dec_seq, H_A, DH_A)

    return (y_p, y_s, k_p, v_p, st_p[None], k_s, v_s, st_s[None])
```

```python
import functools

import jax
import jax.numpy as jnp
from jax import lax
from jax.experimental import pallas as pl
from jax.experimental.pallas import tpu as pltpu

F32 = jnp.float32
BF16 = jnp.bfloat16

D_MODEL = 1024
CHUNK = 64
LEFT_CHUNKS = 8
WINDOW_ROWS = LEFT_CHUNKS * CHUNK
BAND = WINDOW_ROWS + CHUNK
W_A = 512
H_A = 8
DH_A = 64
W_B = 512
H_B = 4
DK_B = 128
DV_B = 128
REL_CLIP = 128
GLA_BLOCK = 16
EPS = 1e-6
ATTN_SCALE = DH_A ** -0.5
NEG_INF = -1e30
N_IN = 4 * W_A + 4 * W_B
PAST_LEN = 2048

LANES = 128
HEAD_PAIRS = H_A // 2
REL_TABLE = 2 * REL_CLIP + 1
REL_TABLE_PAD = 384
TOEPLITZ_LEN = 640
VMEM_LIMIT = 56 * 1024 * 1024


def _params(semantics):
    return pltpu.CompilerParams(dimension_semantics=semantics,
                                vmem_limit_bytes=VMEM_LIMIT)


LOG2E = 1.4426950408889634


def _sigmoid(x):
    return 1.0 / (1.0 + jnp.exp2(x * -LOG2E))


def _silu(x):
    return x * _sigmoid(x)


def _nt(a, b):
    return lax.dot_general(a, b, (((1,), (1,)), ((), ())), preferred_element_type=F32)


def _emit_ahead(n_units, depth, first_fn, second_fn):
    pending = {}
    for t in range(n_units + depth):
        if t < n_units:
            pending[t] = first_fn(t)
        if t >= depth:
            second_fn(t - depth, pending.pop(t - depth))


INPROJ_ROWS = 512
INPROJ_SAMPLE_ROWS = 256
INPROJ_COLS = 512


def _inproj_kernel(xp_ref, xs_ref, g_ref, w_ref, pp_ref, ps_ref, ks_ref, vs_ref, *,
                   prompt_steps):
    i = pl.program_id(0)

    def project(x_ref, o_ref):
        x = x_ref[...]
        ms = jnp.mean(x * x, axis=-1, keepdims=True)
        xn = (x * lax.rsqrt(ms + EPS) * g_ref[...]).astype(BF16)
        for n0 in range(0, N_IN, INPROJ_COLS):
            o_ref[:, n0:n0 + INPROJ_COLS] = jnp.dot(
                xn, w_ref[:, n0:n0 + INPROJ_COLS], preferred_element_type=F32)

    def split_heads(cols, out_ref):
        heads = jnp.stack([cols[:, h * DH_A:(h + 1) * DH_A] for h in range(H_A)], axis=0)
        out_ref[...] = jnp.swapaxes(heads, 0, 1)

    @pl.when(i < prompt_steps)
    def _():
        project(xp_ref, pp_ref)

    @pl.when(i >= prompt_steps)
    def _():
        project(xs_ref, ps_ref)
        split_heads(ps_ref[:, W_A:2 * W_A], ks_ref)
        split_heads(ps_ref[:, 2 * W_A:3 * W_A], vs_ref)


def _inproj(xp, xs, g, w_bf16):
    rows_p, rows_s = xp.shape[0], xs.shape[0]
    tp, ts = INPROJ_ROWS, INPROJ_SAMPLE_ROWS
    assert rows_p % tp == 0 and rows_s % ts == 0
    np_, ns = rows_p // tp, rows_s // ts

    def prompt_blk(i):
        return (jnp.minimum(i, np_ - 1), 0)

    def sample_blk(i):
        return (jnp.maximum(i - np_, 0), 0)

    def sample_blk3(i):
        return (jnp.maximum(i - np_, 0), 0, 0)

    return pl.pallas_call(
        functools.partial(_inproj_kernel, prompt_steps=np_),
        out_shape=(jax.ShapeDtypeStruct((rows_p, N_IN), F32),
                   jax.ShapeDtypeStruct((rows_s, N_IN), F32),
                   jax.ShapeDtypeStruct((rows_s, H_A, DH_A), F32),
                   jax.ShapeDtypeStruct((rows_s, H_A, DH_A), F32)),
        grid=(np_ + ns,),
        in_specs=[
            pl.BlockSpec((tp, D_MODEL), prompt_blk),
            pl.BlockSpec((ts, D_MODEL), sample_blk),
            pl.BlockSpec((1, D_MODEL), lambda i: (0, 0)),
            pl.BlockSpec((D_MODEL, N_IN), lambda i: (0, 0)),
        ],
        out_specs=(pl.BlockSpec((tp, N_IN), prompt_blk),
                   pl.BlockSpec((ts, N_IN), sample_blk),
                   pl.BlockSpec((ts, H_A, DH_A), sample_blk3),
                   pl.BlockSpec((ts, H_A, DH_A), sample_blk3)),
        compiler_params=_params(("arbitrary",)),
        name="inproj",
    )(xp, xs, g, w_bf16)


PAIR_BAND = BAND + CHUNK


def _bias_kernel(rb_ref, pair_ref, single_ref):
    m = lax.broadcasted_iota(jnp.int32, (REL_TABLE_PAD, TOEPLITZ_LEN), 1)
    t = lax.broadcasted_iota(jnp.int32, (REL_TABLE_PAD, TOEPLITZ_LEN), 0)
    idx = jnp.clip(BAND - 1 - m, -REL_CLIP, REL_CLIP) + REL_CLIP
    onehot = jnp.where(idx == t, 1.0, 0.0).astype(BF16)
    rb = rb_ref[...]
    hi = rb.astype(BF16)
    r1 = rb - hi.astype(F32)
    mid = r1.astype(BF16)
    lo = (r1 - mid.astype(F32)).astype(BF16)
    toep = (jnp.dot(hi, onehot, preferred_element_type=F32)
            + jnp.dot(mid, onehot, preferred_element_type=F32)
            + jnp.dot(lo, onehot, preferred_element_type=F32))
    key = lax.broadcasted_iota(jnp.int32, (CHUNK, TOEPLITZ_LEN), 1)
    for p in range(HEAD_PAIRS):
        for u in range(2):
            halves = []
            for h in (2 * p, 2 * p + 1):
                rows = jnp.broadcast_to(toep[h:h + 1, :], (CHUNK, TOEPLITZ_LEN))
                shift = (TOEPLITZ_LEN - (CHUNK - 1) + CHUNK * u) % TOEPLITZ_LEN
                rolled = pltpu.roll(rows, shift, 1, stride=1, stride_axis=0)
                seen = (key >= CHUNK * u) & (key < CHUNK * u + BAND)
                halves.append(jnp.where(seen, rolled * LOG2E, NEG_INF))
            tile_t = jnp.concatenate(halves, axis=0).T
            pair_ref[p, :, u * LANES:(u + 1) * LANES] = tile_t
            if u == 0:
                single_ref[p] = tile_t[:BAND, :]


def _bias_tables(rel_bias_l):
    rb = jnp.pad(rel_bias_l, ((0, 0), (0, REL_TABLE_PAD - REL_TABLE)))
    return pl.pallas_call(
        _bias_kernel,
        out_shape=(jax.ShapeDtypeStruct((HEAD_PAIRS, PAIR_BAND, 2 * LANES), F32),
                   jax.ShapeDtypeStruct((HEAD_PAIRS, BAND, LANES), F32)),
        name="rel_bias_table",
    )(rb)


def _attn_probs(qs, k2, bias_t, valid_from):
    first = lax.broadcasted_iota(jnp.int32, (CHUNK, LANES), 1) < DH_A
    parts = []
    for q in qs:
        q = q * (ATTN_SCALE * LOG2E)
        parts += [jnp.where(first, q, 0.0), jnp.where(first, 0.0, q)]
    qbd = jnp.concatenate(parts, axis=0).astype(BF16)
    s = _nt(k2, qbd) + bias_t
    if valid_from is not None:
        key = lax.broadcasted_iota(jnp.int32, s.shape, 0)
        s = jnp.where(key >= valid_from, s, NEG_INF)
    mx = jnp.max(s, axis=0, keepdims=True)
    e = jnp.exp2(s - mx)
    return e.astype(BF16), jnp.sum(e, axis=0, keepdims=True)


def _attn_apply(probs, ags, vt):
    e, denom = probs
    first = lax.broadcasted_iota(jnp.int32, (CHUNK, LANES), 1) < DH_A
    ot = jnp.dot(vt, e, preferred_element_type=F32)
    o2 = (ot * (1.0 / denom)).T
    outs = []
    for u, ag in enumerate(ags):
        r = 2 * CHUNK * u
        o = jnp.where(first, o2[r:r + CHUNK], o2[r + CHUNK:r + 2 * CHUNK])
        outs.append(o * _silu(ag))
    return outs


ATTN_PIPE_DEPTH = 3


def _attn_pipeline(n_units, probs_fn, apply_fn):
    _emit_ahead(n_units, ATTN_PIPE_DEPTH, probs_fn, apply_fn)


ATTN_ROWS = 1024


def _attn_prompt_kernel(q_ref, k_ref, v_ref, ag_ref, bias_ref, o_ref, kh_ref, vth_ref):
    i = pl.program_id(1)

    @pl.when(i == 0)
    def _():
        kh_ref[0:WINDOW_ROWS, :] = jnp.zeros((WINDOW_ROWS, W_A), BF16)
        vth_ref[:, :, 0:WINDOW_ROWS] = jnp.zeros((HEAD_PAIRS, LANES, WINDOW_ROWS), BF16)

    kh_ref[WINDOW_ROWS:, :] = k_ref[...].astype(BF16)
    for p in range(HEAD_PAIRS):
        vth_ref[p, :, WINDOW_ROWS:] = v_ref[:, p * LANES:(p + 1) * LANES].T.astype(BF16)

    def body(first_step):
        def unit(t):
            cp, p = divmod(t, HEAD_PAIRS)
            r0 = cp * 2 * CHUNK
            rs = [slice(r0 + u * CHUNK, r0 + (u + 1) * CHUNK) for u in range(2)]
            return cp, p, r0, rs, slice(p * LANES, (p + 1) * LANES)

        def probs(t):
            cp, p, r0, rs, cs = unit(t)
            valid_from = (LEFT_CHUNKS - 2 * cp) * CHUNK if first_step else None
            return _attn_probs([q_ref[r, cs] for r in rs], kh_ref[r0:r0 + PAIR_BAND, cs],
                               bias_ref[p], valid_from)

        def apply(t, pr):
            cp, p, r0, rs, cs = unit(t)
            outs = _attn_apply(pr, [ag_ref[r, cs] for r in rs],
                               vth_ref[p, :, r0:r0 + PAIR_BAND])
            for r, o in zip(rs, outs):
                o_ref[r, cs] = o.astype(o_ref.dtype)

        _attn_pipeline((ATTN_ROWS // (2 * CHUNK)) * HEAD_PAIRS, probs, apply)

    @pl.when(i == 0)
    def _():
        body(True)

    @pl.when(i != 0)
    def _():
        body(False)

    kh_ref[0:WINDOW_ROWS, :] = kh_ref[ATTN_ROWS:ATTN_ROWS + WINDOW_ROWS, :]
    vth_ref[:, :, 0:WINDOW_ROWS] = vth_ref[:, :, ATTN_ROWS:ATTN_ROWS + WINDOW_ROWS]


def _attn_prompt(p2d, bias_pair, batch, seq):
    nt = seq // ATTN_ROWS

    def col(c):
        return pl.BlockSpec((ATTN_ROWS, W_A), lambda b, i: (b * nt + i, c))

    return pl.pallas_call(
        _attn_prompt_kernel,
        out_shape=jax.ShapeDtypeStruct((batch * seq, W_A), BF16),
        grid=(batch, nt),
        in_specs=[col(0), col(1), col(2), col(3),
                  pl.BlockSpec((HEAD_PAIRS, PAIR_BAND, 2 * LANES), lambda b, i: (0, 0, 0))],
        out_specs=pl.BlockSpec((ATTN_ROWS, W_A), lambda b, i: (b * nt + i, 0)),
        scratch_shapes=[pltpu.VMEM((WINDOW_ROWS + ATTN_ROWS, W_A), BF16),
                        pltpu.VMEM((HEAD_PAIRS, LANES, WINDOW_ROWS + ATTN_ROWS), BF16)],
        compiler_params=_params(("arbitrary", "arbitrary")),
        name="attn_prompt",
    )(p2d, p2d, p2d, p2d, bias_pair)


ATTN_SAMPLE_SEQS = 4


def _attn_sample_kernel(q_ref, k_ref, v_ref, ag_ref, ckt_ref, cvt_ref, bias_ref, o_ref,
                        kh_ref, vth_ref):
    for s in range(ATTN_SAMPLE_SEQS):
        rs = slice(s * CHUNK, (s + 1) * CHUNK)
        kh_ref[s, WINDOW_ROWS:, :] = k_ref[rs, :].astype(BF16)
        for p in range(HEAD_PAIRS):
            cs = slice(p * LANES, (p + 1) * LANES)
            kh_ref[s, 0:WINDOW_ROWS, cs] = ckt_ref[s, cs, :].T.astype(BF16)
            vth_ref[s, p, :, 0:WINDOW_ROWS] = cvt_ref[s, cs, :].astype(BF16)
            vth_ref[s, p, :, WINDOW_ROWS:] = v_ref[rs, cs].T.astype(BF16)

    def unit(t):
        s, p = divmod(t, HEAD_PAIRS)
        return s, p, slice(s * CHUNK, (s + 1) * CHUNK), slice(p * LANES, (p + 1) * LANES)

    def probs(t):
        s, p, rs, cs = unit(t)
        return _attn_probs([q_ref[rs, cs]], kh_ref[s, :, cs], bias_ref[p], None)

    def apply(t, pr):
        s, p, rs, cs = unit(t)
        (o,) = _attn_apply(pr, [ag_ref[rs, cs]], vth_ref[s, p])
        o_ref[rs, cs] = o.astype(o_ref.dtype)

    _attn_pipeline(ATTN_SAMPLE_SEQS * HEAD_PAIRS, probs, apply)


def _attn_sample(p2d, cache_kt, cache_vt, bias_single, batch):
    ns = ATTN_SAMPLE_SEQS

    def col(c):
        return pl.BlockSpec((ns * CHUNK, W_A), lambda b: (b, c))

    cache = pl.BlockSpec((ns, W_A, WINDOW_ROWS), lambda b: (b, 0, 0))
    return pl.pallas_call(
        _attn_sample_kernel,
        out_shape=jax.ShapeDtypeStruct((batch * CHUNK, W_A), BF16),
        grid=(batch // ns,),
        in_specs=[col(0), col(1), col(2), col(3), cache, cache,
                  pl.BlockSpec((HEAD_PAIRS, BAND, LANES), lambda b: (0, 0, 0))],
        out_specs=pl.BlockSpec((ns * CHUNK, W_A), lambda b: (b, 0)),
        scratch_shapes=[pltpu.VMEM((ns, BAND, W_A), BF16),
                        pltpu.VMEM((ns, HEAD_PAIRS, LANES, BAND), BF16)],
        compiler_params=_params(("arbitrary",)),
        name="attn_sample",
    )(p2d, p2d, p2d, p2d, cache_kt, cache_vt, bias_single)


HGRN_GROUP_BLOCKS = 4
HGRN_GROUP = HGRN_GROUP_BLOCKS * GLA_BLOCK
HGRN_PIPE_DEPTH = 4


def _block_cumsum(x):
    row = lax.broadcasted_iota(jnp.int32, x.shape, 0) & (GLA_BLOCK - 1)
    s = 1
    while s < GLA_BLOCK:
        x = x + jnp.where(row >= s, pltpu.roll(x, s, 0), 0.0)
        s *= 2
    return x


def _hgrn_kernel(hq_ref, hf_ref, hi_ref, hg_ref, lb_ref, ng_ref, s0_ref,
                 o_ref, sout_ref, st_ref, qt_ref, kt_ref, kd_ref, v_ref,
                 qtf_ref, kdf_ref, bl_ref):
    i = pl.program_id(1)
    nseq, seq_rows = hq_ref.shape[0], hq_ref.shape[1]
    rows = nseq * seq_rows
    nb = rows // GLA_BLOCK

    @pl.when(i == 0)
    def _():
        for s in range(nseq):
            for h in range(H_B):
                st_ref[s, h] = s0_ref[s, h].T

    def flat(ref):
        return ref[...].reshape(rows, W_B)

    lb = lb_ref[...]
    f = lb + (1.0 - lb) * _sigmoid(flat(hf_ref))
    b = _block_cumsum(jnp.log(f) * LOG2E)
    b3 = b.reshape(nb, GLA_BLOCK, W_B)
    bl = jnp.broadcast_to(b3[:, GLA_BLOCK - 1:GLA_BLOCK, :],
                          (nb, GLA_BLOCK, W_B)).reshape(rows, W_B)
    kk = 1.0 - f
    qt = _silu(flat(hq_ref)) * jnp.exp2(b)
    kd = kk * jnp.exp2(bl - b)
    qtf_ref[...] = qt
    kdf_ref[...] = kd
    bl_ref[...] = bl
    qt_ref[...] = qt.astype(BF16)
    kt_ref[...] = (kk * jnp.exp2(-b)).astype(BF16)
    kd_ref[...] = kd.astype(BF16)
    v_ref[...] = flat(hi_ref).astype(BF16)

    r_i = lax.broadcasted_iota(jnp.int32, (HGRN_GROUP, HGRN_GROUP), 0)
    c_i = lax.broadcasted_iota(jnp.int32, (HGRN_GROUP, HGRN_GROUP), 1)
    rb, cb = r_i // GLA_BLOCK, c_i // GLA_BLOCK
    m_diag = (rb == cb) & (r_i >= c_i)
    m_adj = rb == cb + 1
    m_far = rb >= cb + 2

    groups = seq_rows // HGRN_GROUP
    st = {(s, h): st_ref[s, h] for s in range(nseq) for h in range(H_B)}

    def unit(t):
        sg, h = divmod(t, H_B)
        s, g = divmod(sg, groups)
        r0 = s * seq_rows + g * HGRN_GROUP
        return s, g, h, r0, slice(r0, r0 + HGRN_GROUP), slice(h * DK_B, (h + 1) * DK_B)

    def block_decays(r0, cs):
        return [bl_ref[r0 + j * GLA_BLOCK:r0 + j * GLA_BLOCK + 1, cs]
                for j in range(HGRN_GROUP_BLOCKS)]

    def scaled(x, log2_scales):
        parts = []
        for j, sc in enumerate(log2_scales):
            xb = x[j * GLA_BLOCK:(j + 1) * GLA_BLOCK]
            parts.append(xb if sc is None else xb * jnp.exp2(sc))
        return jnp.concatenate(parts, axis=0).astype(BF16)

    def local(t):
        s, g, h, r0, rs, cs = unit(t)
        d0, d1, d2, d3 = block_decays(r0, cs)
        qt_g = qt_ref[rs, cs]
        kdf = kdf_ref[rs, cs]
        q1 = scaled(qtf_ref[rs, cs], [None, None, None, d2])
        k1 = scaled(kdf, [d1, None, None, None])
        a = jnp.where(m_diag, _nt(qt_g, kt_ref[rs, cs]),
                      jnp.where(m_adj, _nt(qt_g, kd_ref[rs, cs]),
                                jnp.where(m_far, _nt(q1, k1), 0.0)))
        kdp = scaled(kdf, [(d1 + d2) + d3, d2 + d3, d3, None])
        ut = lax.dot_general(v_ref[rs, cs], kdp, (((0,), (0,)), ((), ())),
                             preferred_element_type=F32)
        return a.astype(BF16), ut

    def carry(t, loc):
        s, g, h, r0, rs, cs = unit(t)
        a, ut = loc
        d0, d1, d2, d3 = block_decays(r0, cs)
        qs = scaled(qtf_ref[rs, cs], [None, d0, d0 + d1, (d0 + d1) + d2])
        o = (_nt(qs, st[s, h].astype(BF16))
             + jnp.dot(a, v_ref[rs, cs], preferred_element_type=F32))
        dec = jnp.exp2(((d0 + d1) + d2) + d3)
        st[s, h] = st[s, h] * dec + ut
        y = o * lax.rsqrt(jnp.mean(o * o, axis=-1, keepdims=True) + EPS) * ng_ref[:, cs]
        gs = slice(g * HGRN_GROUP, (g + 1) * HGRN_GROUP)
        o_ref[s, gs, cs] = (y * _silu(hg_ref[s, gs, cs])).astype(o_ref.dtype)

    _emit_ahead(nseq * groups * H_B, HGRN_PIPE_DEPTH, local, carry)
    for (s, h), val in st.items():
        st_ref[s, h] = val

    @pl.when(i == pl.num_programs(1) - 1)
    def _():
        for s in range(nseq):
            for h in range(H_B):
                sout_ref[s, h] = st_ref[s, h].T


def _hgrn(p3d, lb, ng, s0, nseq, rows, name):
    batch, seq, _ = p3d.shape
    assert batch % nseq == 0 and seq % rows == 0 and rows % HGRN_GROUP == 0

    def col(c):
        return pl.BlockSpec((nseq, rows, W_B), lambda b, i: (b, i, c))

    vec = pl.BlockSpec((1, W_B), lambda b, i: (0, 0))
    state = pl.BlockSpec((nseq, H_B, DK_B, DV_B), lambda b, i: (b, 0, 0, 0))
    n = nseq * rows
    return pl.pallas_call(
        _hgrn_kernel,
        out_shape=(jax.ShapeDtypeStruct((batch, seq, W_B), BF16),
                   jax.ShapeDtypeStruct((batch, H_B, DK_B, DV_B), F32)),
        grid=(batch // nseq, seq // rows),
        in_specs=[col(4), col(5), col(6), col(7), vec, vec, state],
        out_specs=(pl.BlockSpec((nseq, rows, W_B), lambda b, i: (b, i, 0)), state),
        scratch_shapes=([pltpu.VMEM((nseq, H_B, DV_B, DK_B), F32)]
                        + [pltpu.VMEM((n, W_B), BF16)] * 4
                        + [pltpu.VMEM((n, W_B), F32)] * 3),
        compiler_params=_params(("arbitrary", "arbitrary")),
        name=name,
    )(p3d, p3d, p3d, p3d, lb, ng, s0)


OUTPROJ_ROWS = 2048
OUTPROJ_MIN_STEPS = 4


def _outproj_kernel(oa_ref, ob_ref, x_ref, w_ref, g_ref, y_ref):
    h = (x_ref[...]
         + jnp.dot(oa_ref[...], w_ref[0:W_A, :], preferred_element_type=F32)
         + jnp.dot(ob_ref[...], w_ref[W_A:, :], preferred_element_type=F32))
    ms = jnp.mean(h * h, axis=-1, keepdims=True)
    y_ref[...] = h * lax.rsqrt(ms + EPS) * g_ref[...]


def _outproj(oa, ob, x2d, w_bf16, g, name):
    rows = x2d.shape[0]
    tm = min(OUTPROJ_ROWS, rows // OUTPROJ_MIN_STEPS)
    return pl.pallas_call(
        _outproj_kernel,
        out_shape=jax.ShapeDtypeStruct((rows, D_MODEL), F32),
        grid=(rows // tm,),
        in_specs=[
            pl.BlockSpec((tm, W_A), lambda i: (i, 0)),
            pl.BlockSpec((tm, W_B), lambda i: (i, 0)),
            pl.BlockSpec((tm, D_MODEL), lambda i: (i, 0)),
            pl.BlockSpec((W_A + W_B, D_MODEL), lambda i: (0, 0)),
            pl.BlockSpec((1, D_MODEL), lambda i: (0, 0)),
        ],
        out_specs=pl.BlockSpec((tm, D_MODEL), lambda i: (i, 0)),
        compiler_params=_params(("arbitrary",)),
        name=name,
    )(oa, ob, x2d, w_bf16, g)


HGRN_PROMPT_ROWS = 512
HGRN_PROMPT_SEQS = 2
HGRN_SAMPLE_SEQS = 8


def kernel(x_prompt, x_sample, cache_attn_k, cache_attn_v, state_hgrn, ln_in_g, w_in,
           rel_bias, lb_gamma, hg_norm_g, w_out, ln_f_g):
    batch, seq, _ = x_prompt.shape
    dec_batch, dec_seq, _ = x_sample.shape
    depth = w_in.shape[0]
    assert depth == 1 and dec_seq == CHUNK and PAST_LEN % CHUNK == 0
    assert cache_attn_k.shape[2] == WINDOW_ROWS

    lb_all = jnp.cumsum(jax.nn.softmax(lb_gamma.astype(F32), axis=0), axis=0)
    lb = lb_all[0].reshape(1, W_B)
    ng = hg_norm_g[0].reshape(1, W_B)
    g_in = ln_in_g[0].reshape(1, D_MODEL)
    g_f = ln_f_g.reshape(1, D_MODEL)
    w_in_b = w_in[0].astype(BF16)
    w_out_b = w_out[0].astype(BF16)

    xp = x_prompt.reshape(batch * seq, D_MODEL)
    xs = x_sample.reshape(dec_batch * dec_seq, D_MODEL)

    bias_pair, bias_single = _bias_tables(rel_bias[0])

    pp, ps, k_s, v_s = _inproj(xp, xs, g_in, w_in_b)

    oa_p = _attn_prompt(pp, bias_pair, batch, seq)
    ckt = jnp.transpose(cache_attn_k[0], (0, 2, 3, 1)).reshape(dec_batch, W_A, WINDOW_ROWS)
    cvt = jnp.transpose(cache_attn_v[0], (0, 2, 3, 1)).reshape(dec_batch, W_A, WINDOW_ROWS)
    oa_s = _attn_sample(ps, ckt, cvt, bias_single, dec_batch)

    s0_p = jnp.zeros((batch, H_B, DK_B, DV_B), F32)
    ob_p, st_p = _hgrn(pp.reshape(batch, seq, N_IN), lb, ng, s0_p,
                       HGRN_PROMPT_SEQS, HGRN_PROMPT_ROWS, "hgrn_prompt")
    ob_s, st_s = _hgrn(ps.reshape(dec_batch, dec_seq, N_IN), lb, ng, state_hgrn[0],
                       HGRN_SAMPLE_SEQS, dec_seq, "hgrn_sample")
    ob_p = ob_p.reshape(batch * seq, W_B)
    ob_s = ob_s.reshape(dec_batch * dec_seq, W_B)

    y_p = _outproj(oa_p, ob_p, xp, w_out_b, g_f, "outproj_prompt").reshape(batch, seq, D_MODEL)
    y_s = _outproj(oa_s, ob_s, xs, w_out_b, g_f, "outproj_sample").reshape(
        dec_batch, dec_seq, D_MODEL)

    rows_p = min(WINDOW_ROWS, seq)
    pp3 = pp.reshape(batch, seq, N_IN)
    k_p = pp3[:, seq - rows_p:, W_A:2 * W_A].reshape(1, batch, rows_p, H_A, DH_A)
    v_p = pp3[:, seq - rows_p:, 2 * W_A:3 * W_A].reshape(1, batch, rows_p, H_A, DH_A)
    k_s = k_s.reshape(1, dec_batch, dec_seq, H_A, DH_A)
    v_s = v_s.reshape(1, dec_batch, dec_seq, H_A, DH_A)

    return (y_p, y_s, k_p, v_p, st_p[None], k_s, v_s, st_s[None])
```

```python
import functools

import jax
import jax.numpy as jnp
from jax import lax
from jax.experimental import pallas as pl
from jax.experimental.pallas import tpu as pltpu

F32 = jnp.float32
BF16 = jnp.bfloat16

D_MODEL = 1024
CHUNK = 64
LEFT_CHUNKS = 8
WINDOW_ROWS = LEFT_CHUNKS * CHUNK
BAND = WINDOW_ROWS + CHUNK
W_A = 512
H_A = 8
DH_A = 64
W_B = 512
H_B = 4
DK_B = 128
DV_B = 128
REL_CLIP = 128
GLA_BLOCK = 16
EPS = 1e-6
ATTN_SCALE = DH_A ** -0.5
NEG_INF = -1e30
N_IN = 4 * W_A + 4 * W_B
PAST_LEN = 2048

LANES = 128
HEAD_PAIRS = H_A // 2
REL_TABLE = 2 * REL_CLIP + 1
REL_TABLE_PAD = 384
TOEPLITZ_LEN = 640
VMEM_LIMIT = 56 * 1024 * 1024


def _params(semantics):
    return pltpu.CompilerParams(dimension_semantics=semantics,
                                vmem_limit_bytes=VMEM_LIMIT)


LOG2E = 1.4426950408889634


def _sigmoid(x):
    return 1.0 / (1.0 + jnp.exp2(x * -LOG2E))


def _silu(x):
    return x * _sigmoid(x)


def _nt(a, b):
    return lax.dot_general(a, b, (((1,), (1,)), ((), ())), preferred_element_type=F32)


def _emit_ahead(n_units, depth, first_fn, second_fn):
    pending = {}
    for t in range(n_units + depth):
        if t < n_units:
            pending[t] = first_fn(t)
        if t >= depth:
            second_fn(t - depth, pending.pop(t - depth))


INPROJ_ROWS = 512
INPROJ_SAMPLE_ROWS = 256
INPROJ_COLS = 512


def _inproj_kernel(xp_ref, xs_ref, g_ref, w_ref, pp_ref, ps_ref, ks_ref, vs_ref, *,
                   prompt_steps):
    i = pl.program_id(0)

    def project(x_ref, o_ref):
        x = x_ref[...]
        ms = jnp.mean(x * x, axis=-1, keepdims=True)
        xn = (x * lax.rsqrt(ms + EPS) * g_ref[...]).astype(BF16)
        for n0 in range(0, N_IN, INPROJ_COLS):
            o_ref[:, n0:n0 + INPROJ_COLS] = jnp.dot(
                xn, w_ref[:, n0:n0 + INPROJ_COLS], preferred_element_type=F32)

    def split_heads(cols, out_ref):
        heads = jnp.stack([cols[:, h * DH_A:(h + 1) * DH_A] for h in range(H_A)], axis=0)
        out_ref[...] = jnp.swapaxes(heads, 0, 1)

    @pl.when(i < prompt_steps)
    def _():
        project(xp_ref, pp_ref)

    @pl.when(i >= prompt_steps)
    def _():
        project(xs_ref, ps_ref)
        split_heads(ps_ref[:, W_A:2 * W_A], ks_ref)
        split_heads(ps_ref[:, 2 * W_A:3 * W_A], vs_ref)


def _inproj(xp, xs, g, w_bf16):
    rows_p, rows_s = xp.shape[0], xs.shape[0]
    tp, ts = INPROJ_ROWS, INPROJ_SAMPLE_ROWS
    assert rows_p % tp == 0 and rows_s % ts == 0
    np_, ns = rows_p // tp, rows_s // ts

    def prompt_blk(i):
        return (jnp.minimum(i, np_ - 1), 0)

    def sample_blk(i):
        return (jnp.maximum(i - np_, 0), 0)

    def sample_blk3(i):
        return (jnp.maximum(i - np_, 0), 0, 0)

    return pl.pallas_call(
        functools.partial(_inproj_kernel, prompt_steps=np_),
        out_shape=(jax.ShapeDtypeStruct((rows_p, N_IN), F32),
                   jax.ShapeDtypeStruct((rows_s, N_IN), F32),
                   jax.ShapeDtypeStruct((rows_s, H_A, DH_A), F32),
                   jax.ShapeDtypeStruct((rows_s, H_A, DH_A), F32)),
        grid=(np_ + ns,),
        in_specs=[
            pl.BlockSpec((tp, D_MODEL), prompt_blk),
            pl.BlockSpec((ts, D_MODEL), sample_blk),
            pl.BlockSpec((1, D_MODEL), lambda i: (0, 0)),
            pl.BlockSpec((D_MODEL, N_IN), lambda i: (0, 0)),
        ],
        out_specs=(pl.BlockSpec((tp, N_IN), prompt_blk),
                   pl.BlockSpec((ts, N_IN), sample_blk),
                   pl.BlockSpec((ts, H_A, DH_A), sample_blk3),
                   pl.BlockSpec((ts, H_A, DH_A), sample_blk3)),
        compiler_params=_params(("arbitrary",)),
        name="inproj",
    )(xp, xs, g, w_bf16)


PAIR_BAND = BAND + CHUNK


def _bias_kernel(rb_ref, pair_ref, single_ref):
    m = lax.broadcasted_iota(jnp.int32, (REL_TABLE_PAD, TOEPLITZ_LEN), 1)
    t = lax.broadcasted_iota(jnp.int32, (REL_TABLE_PAD, TOEPLITZ_LEN), 0)
    idx = jnp.clip(BAND - 1 - m, -REL_CLIP, REL_CLIP) + REL_CLIP
    onehot = jnp.where(idx == t, 1.0, 0.0).astype(BF16)
    rb = rb_ref[...]
    hi = rb.astype(BF16)
    r1 = rb - hi.astype(F32)
    mid = r1.astype(BF16)
    lo = (r1 - mid.astype(F32)).astype(BF16)
    toep = (jnp.dot(hi, onehot, preferred_element_type=F32)
            + jnp.dot(mid, onehot, preferred_element_type=F32)
            + jnp.dot(lo, onehot, preferred_element_type=F32))
    key = lax.broadcasted_iota(jnp.int32, (CHUNK, TOEPLITZ_LEN), 1)
    for p in range(HEAD_PAIRS):
        for u in range(2):
            halves = []
            for h in (2 * p, 2 * p + 1):
                rows = jnp.broadcast_to(toep[h:h + 1, :], (CHUNK, TOEPLITZ_LEN))
                shift = (TOEPLITZ_LEN - (CHUNK - 1) + CHUNK * u) % TOEPLITZ_LEN
                rolled = pltpu.roll(rows, shift, 1, stride=1, stride_axis=0)
                seen = (key >= CHUNK * u) & (key < CHUNK * u + BAND)
                halves.append(jnp.where(seen, rolled * LOG2E, NEG_INF))
            tile_t = jnp.concatenate(halves, axis=0).T
            pair_ref[p, :, u * LANES:(u + 1) * LANES] = tile_t
            if u == 0:
                single_ref[p] = tile_t[:BAND, :]


def _bias_tables(rel_bias_l):
    rb = jnp.pad(rel_bias_l, ((0, 0), (0, REL_TABLE_PAD - REL_TABLE)))
    return pl.pallas_call(
        _bias_kernel,
        out_shape=(jax.ShapeDtypeStruct((HEAD_PAIRS, PAIR_BAND, 2 * LANES), F32),
                   jax.ShapeDtypeStruct((HEAD_PAIRS, BAND, LANES), F32)),
        name="rel_bias_table",
    )(rb)


def _attn_probs(qs, k2, bias_t, valid_from):
    first = lax.broadcasted_iota(jnp.int32, (CHUNK, LANES), 1) < DH_A
    parts = []
    for q in qs:
        q = q * (ATTN_SCALE * LOG2E)
        parts += [jnp.where(first, q, 0.0), jnp.where(first, 0.0, q)]
    qbd = jnp.concatenate(parts, axis=0).astype(BF16)
    s = _nt(k2, qbd) + bias_t
    if valid_from is not None:
        key = lax.broadcasted_iota(jnp.int32, s.shape, 0)
        s = jnp.where(key >= valid_from, s, NEG_INF)
    mx = jnp.max(s, axis=0, keepdims=True)
    e = jnp.exp2(s - mx)
    return e.astype(BF16), jnp.sum(e, axis=0, keepdims=True)


def _attn_apply(probs, ags, vt):
    e, denom = probs
    first = lax.broadcasted_iota(jnp.int32, (CHUNK, LANES), 1) < DH_A
    ot = jnp.dot(vt, e, preferred_element_type=F32)
    o2 = (ot * (1.0 / denom)).T
    outs = []
    for u, ag in enumerate(ags):
        r = 2 * CHUNK * u
        o = jnp.where(first, o2[r:r + CHUNK], o2[r + CHUNK:r + 2 * CHUNK])
        outs.append(o * _silu(ag))
    return outs


ATTN_PIPE_DEPTH = 3


def _attn_pipeline(n_units, probs_fn, apply_fn):
    _emit_ahead(n_units, ATTN_PIPE_DEPTH, probs_fn, apply_fn)


ATTN_ROWS = 1024


def _attn_prompt_kernel(q_ref, k_ref, v_ref, ag_ref, bias_ref, o_ref, kh_ref, vth_ref):
    i = pl.program_id(1)

    @pl.when(i == 0)
    def _():
        kh_ref[0:WINDOW_ROWS, :] = jnp.zeros((WINDOW_ROWS, W_A), BF16)
        vth_ref[:, :, 0:WINDOW_ROWS] = jnp.zeros((HEAD_PAIRS, LANES, WINDOW_ROWS), BF16)

    kh_ref[WINDOW_ROWS:, :] = k_ref[...].astype(BF16)
    for p in range(HEAD_PAIRS):
        vth_ref[p, :, WINDOW_ROWS:] = v_ref[:, p * LANES:(p + 1) * LANES].T.astype(BF16)

    def body(first_step):
        def unit(t):
            cp, p = divmod(t, HEAD_PAIRS)
            r0 = cp * 2 * CHUNK
            rs = [slice(r0 + u * CHUNK, r0 + (u + 1) * CHUNK) for u in range(2)]
            return cp, p, r0, rs, slice(p * LANES, (p + 1) * LANES)

        def probs(t):
            cp, p, r0, rs, cs = unit(t)
            valid_from = (LEFT_CHUNKS - 2 * cp) * CHUNK if first_step else None
            return _attn_probs([q_ref[r, cs] for r in rs], kh_ref[r0:r0 + PAIR_BAND, cs],
                               bias_ref[p], valid_from)

        def apply(t, pr):
            cp, p, r0, rs, cs = unit(t)
            outs = _attn_apply(pr, [ag_ref[r, cs] for r in rs],
                               vth_ref[p, :, r0:r0 + PAIR_BAND])
            for r, o in zip(rs, outs):
                o_ref[r, cs] = o.astype(o_ref.dtype)

        _attn_pipeline((ATTN_ROWS // (2 * CHUNK)) * HEAD_PAIRS, probs, apply)

    @pl.when(i == 0)
    def _():
        body(True)

    @pl.when(i != 0)
    def _():
        body(False)

    kh_ref[0:WINDOW_ROWS, :] = kh_ref[ATTN_ROWS:ATTN_ROWS + WINDOW_ROWS, :]
    vth_ref[:, :, 0:WINDOW_ROWS] = vth_ref[:, :, ATTN_ROWS:ATTN_ROWS + WINDOW_ROWS]


def _attn_prompt(p2d, bias_pair, batch, seq):
    nt = seq // ATTN_ROWS

    def col(c):
        return pl.BlockSpec((ATTN_ROWS, W_A), lambda b, i: (b * nt + i, c))

    return pl.pallas_call(
        _attn_prompt_kernel,
        out_shape=jax.ShapeDtypeStruct((batch * seq, W_A), BF16),
        grid=(batch, nt),
        in_specs=[col(0), col(1), col(2), col(3),
                  pl.BlockSpec((HEAD_PAIRS, PAIR_BAND, 2 * LANES), lambda b, i: (0, 0, 0))],
        out_specs=pl.BlockSpec((ATTN_ROWS, W_A), lambda b, i: (b * nt + i, 0)),
        scratch_shapes=[pltpu.VMEM((WINDOW_ROWS + ATTN_ROWS, W_A), BF16),
                        pltpu.VMEM((HEAD_PAIRS, LANES, WINDOW_ROWS + ATTN_ROWS), BF16)],
        compiler_params=_params(("arbitrary", "arbitrary")),
        name="attn_prompt",
    )(p2d, p2d, p2d, p2d, bias_pair)


ATTN_SAMPLE_SEQS = 4


def _attn_sample_kernel(q_ref, k_ref, v_ref, ag_ref, ckt_ref, cvt_ref, bias_ref, o_ref,
                        kh_ref, vth_ref):
    for s in range(ATTN_SAMPLE_SEQS):
        rs = slice(s * CHUNK, (s + 1) * CHUNK)
        kh_ref[s, WINDOW_ROWS:, :] = k_ref[rs, :].astype(BF16)
        for p in range(HEAD_PAIRS):
            cs = slice(p * LANES, (p + 1) * LANES)
            kh_ref[s, 0:WINDOW_ROWS, cs] = ckt_ref[s, cs, :].T.astype(BF16)
            vth_ref[s, p, :, 0:WINDOW_ROWS] = cvt_ref[s, cs, :].astype(BF16)
            vth_ref[s, p, :, WINDOW_ROWS:] = v_ref[rs, cs].T.astype(BF16)

    def unit(t):
        s, p = divmod(t, HEAD_PAIRS)
        return s, p, slice(s * CHUNK, (s + 1) * CHUNK), slice(p * LANES, (p + 1) * LANES)

    def probs(t):
        s, p, rs, cs = unit(t)
        return _attn_probs([q_ref[rs, cs]], kh_ref[s, :, cs], bias_ref[p], None)

    def apply(t, pr):
        s, p, rs, cs = unit(t)
        (o,) = _attn_apply(pr, [ag_ref[rs, cs]], vth_ref[s, p])
        o_ref[rs, cs] = o.astype(o_ref.dtype)

    _attn_pipeline(ATTN_SAMPLE_SEQS * HEAD_PAIRS, probs, apply)


def _attn_sample(p2d, cache_kt, cache_vt, bias_single, batch):
    ns = ATTN_SAMPLE_SEQS

    def col(c):
        return pl.BlockSpec((ns * CHUNK, W_A), lambda b: (b, c))

    cache = pl.BlockSpec((ns, W_A, WINDOW_ROWS), lambda b: (b, 0, 0))
    return pl.pallas_call(
        _attn_sample_kernel,
        out_shape=jax.ShapeDtypeStruct((batch * CHUNK, W_A), BF16),
        grid=(batch // ns,),
        in_specs=[col(0), col(1), col(2), col(3), cache, cache,
                  pl.BlockSpec((HEAD_PAIRS, BAND, LANES), lambda b: (0, 0, 0))],
        out_specs=pl.BlockSpec((ns * CHUNK, W_A), lambda b: (b, 0)),
        scratch_shapes=[pltpu.VMEM((ns, BAND, W_A), BF16),
                        pltpu.VMEM((ns, HEAD_PAIRS, LANES, BAND), BF16)],
        compiler_params=_params(("arbitrary",)),
        name="attn_sample",
    )(p2d, p2d, p2d, p2d, cache_kt, cache_vt, bias_single)


HGRN_GROUP_BLOCKS = 4
HGRN_GROUP = HGRN_GROUP_BLOCKS * GLA_BLOCK
HGRN_PIPE_DEPTH = 4
OUT_ROWS = 256


def _block_cumsum(x):
    row = lax.broadcasted_iota(jnp.int32, x.shape, 0) & (GLA_BLOCK - 1)
    s = 1
    while s < GLA_BLOCK:
        x = x + jnp.where(row >= s, pltpu.roll(x, s, 0), 0.0)
        s *= 2
    return x


def _hgrn_kernel(hq_ref, hf_ref, hi_ref, hg_ref, lb_ref, ng_ref, s0_ref,
                 oa_ref, x_ref, w_ref, gf_ref,
                 y_ref, sout_ref, st_ref, qt_ref, kt_ref, kd_ref, v_ref,
                 qtf_ref, kdf_ref, bl_ref, ob_ref):
    i = pl.program_id(1)
    nseq, seq_rows = hq_ref.shape[0], hq_ref.shape[1]
    rows = nseq * seq_rows
    nb = rows // GLA_BLOCK

    def out_rows(c):
        r0 = c * OUT_ROWS
        if seq_rows >= OUT_ROWS:
            s, off = divmod(r0, seq_rows)
            return (s, slice(off, off + OUT_ROWS))
        n = OUT_ROWS // seq_rows
        return (slice(r0 // seq_rows, r0 // seq_rows + n),)

    def out_read(ref, c):
        return ref[out_rows(c)].reshape(OUT_ROWS, ref.shape[-1])

    def out_write(ref, c, val):
        if seq_rows < OUT_ROWS:
            val = val.reshape(OUT_ROWS // seq_rows, seq_rows, ref.shape[-1])
        ref[out_rows(c)] = val

    @pl.when(i == 0)
    def _():
        for s in range(nseq):
            for h in range(H_B):
                st_ref[s, h] = s0_ref[s, h].T

    def flat(ref):
        return ref[...].reshape(rows, W_B)

    lb = lb_ref[...]
    f = lb + (1.0 - lb) * _sigmoid(flat(hf_ref))
    b = _block_cumsum(jnp.log(f) * LOG2E)
    b3 = b.reshape(nb, GLA_BLOCK, W_B)
    bl = jnp.broadcast_to(b3[:, GLA_BLOCK - 1:GLA_BLOCK, :],
                          (nb, GLA_BLOCK, W_B)).reshape(rows, W_B)
    kk = 1.0 - f
    qt = _silu(flat(hq_ref)) * jnp.exp2(b)
    kd = kk * jnp.exp2(bl - b)
    qtf_ref[...] = qt
    kdf_ref[...] = kd
    bl_ref[...] = bl
    qt_ref[...] = qt.astype(BF16)
    kt_ref[...] = (kk * jnp.exp2(-b)).astype(BF16)
    kd_ref[...] = kd.astype(BF16)
    v_ref[...] = flat(hi_ref).astype(BF16)

    for c in range(rows // OUT_ROWS):
        out_write(y_ref, c, out_read(x_ref, c)
                  + jnp.dot(out_read(oa_ref, c), w_ref[0:W_A, :], preferred_element_type=F32))

    r_i = lax.broadcasted_iota(jnp.int32, (HGRN_GROUP, HGRN_GROUP), 0)
    c_i = lax.broadcasted_iota(jnp.int32, (HGRN_GROUP, HGRN_GROUP), 1)
    rb, cb = r_i // GLA_BLOCK, c_i // GLA_BLOCK
    m_diag = (rb == cb) & (r_i >= c_i)
    m_adj = rb == cb + 1
    m_far = rb >= cb + 2

    groups = seq_rows // HGRN_GROUP
    st = {(s, h): st_ref[s, h] for s in range(nseq) for h in range(H_B)}

    def unit(t):
        sg, h = divmod(t, H_B)
        s, g = divmod(sg, groups)
        r0 = s * seq_rows + g * HGRN_GROUP
        return s, g, h, r0, slice(r0, r0 + HGRN_GROUP), slice(h * DK_B, (h + 1) * DK_B)

    def block_decays(r0, cs):
        return [bl_ref[r0 + j * GLA_BLOCK:r0 + j * GLA_BLOCK + 1, cs]
                for j in range(HGRN_GROUP_BLOCKS)]

    def scaled(x, log2_scales):
        parts = []
        for j, sc in enumerate(log2_scales):
            xb = x[j * GLA_BLOCK:(j + 1) * GLA_BLOCK]
            parts.append(xb if sc is None else xb * jnp.exp2(sc))
        return jnp.concatenate(parts, axis=0).astype(BF16)

    def local(t):
        s, g, h, r0, rs, cs = unit(t)
        d0, d1, d2, d3 = block_decays(r0, cs)
        qt_g = qt_ref[rs, cs]
        kdf = kdf_ref[rs, cs]
        q1 = scaled(qtf_ref[rs, cs], [None, None, None, d2])
        k1 = scaled(kdf, [d1, None, None, None])
        a = jnp.where(m_diag, _nt(qt_g, kt_ref[rs, cs]),
                      jnp.where(m_adj, _nt(qt_g, kd_ref[rs, cs]),
                                jnp.where(m_far, _nt(q1, k1), 0.0)))
        kdp = scaled(kdf, [(d1 + d2) + d3, d2 + d3, d3, None])
        ut = lax.dot_general(v_ref[rs, cs], kdp, (((0,), (0,)), ((), ())),
                             preferred_element_type=F32)
        return a.astype(BF16), ut

    def finish(c):
        hrow = out_read(y_ref, c) + jnp.dot(ob_ref[c * OUT_ROWS:(c + 1) * OUT_ROWS, :],
                                             w_ref[W_A:, :], preferred_element_type=F32)
        ms = jnp.mean(hrow * hrow, axis=-1, keepdims=True)
        out_write(y_ref, c, hrow * lax.rsqrt(ms + EPS) * gf_ref[...])

    def carry(t, loc):
        s, g, h, r0, rs, cs = unit(t)
        a, ut = loc
        d0, d1, d2, d3 = block_decays(r0, cs)
        qs = scaled(qtf_ref[rs, cs], [None, d0, d0 + d1, (d0 + d1) + d2])
        o = (_nt(qs, st[s, h].astype(BF16))
             + jnp.dot(a, v_ref[rs, cs], preferred_element_type=F32))
        dec = jnp.exp2(((d0 + d1) + d2) + d3)
        st[s, h] = st[s, h] * dec + ut
        y = o * lax.rsqrt(jnp.mean(o * o, axis=-1, keepdims=True) + EPS) * ng_ref[:, cs]
        gs = slice(g * HGRN_GROUP, (g + 1) * HGRN_GROUP)
        ob_ref[rs, cs] = (y * _silu(hg_ref[s, gs, cs])).astype(BF16)
        if h == H_B - 1 and (r0 + HGRN_GROUP) % OUT_ROWS == 0:
            finish((r0 + HGRN_GROUP) // OUT_ROWS - 1)

    _emit_ahead(nseq * groups * H_B, HGRN_PIPE_DEPTH, local, carry)
    for (s, h), val in st.items():
        st_ref[s, h] = val

    @pl.when(i == pl.num_programs(1) - 1)
    def _():
        for s in range(nseq):
            for h in range(H_B):
                sout_ref[s, h] = st_ref[s, h].T


def _hgrn(p3d, lb, ng, s0, oa3d, x3d, w_bf16, gf, nseq, rows, name):
    batch, seq, _ = p3d.shape
    assert batch % nseq == 0 and seq % rows == 0 and rows % HGRN_GROUP == 0
    assert (nseq * rows) % OUT_ROWS == 0 and (rows % OUT_ROWS == 0 or OUT_ROWS % rows == 0)

    def col(c):
        return pl.BlockSpec((nseq, rows, W_B), lambda b, i: (b, i, c))

    vec = pl.BlockSpec((1, W_B), lambda b, i: (0, 0))
    state = pl.BlockSpec((nseq, H_B, DK_B, DV_B), lambda b, i: (b, 0, 0, 0))
    wide = pl.BlockSpec((nseq, rows, D_MODEL), lambda b, i: (b, i, 0))
    n = nseq * rows
    return pl.pallas_call(
        _hgrn_kernel,
        out_shape=(jax.ShapeDtypeStruct((batch, seq, D_MODEL), F32),
                   jax.ShapeDtypeStruct((batch, H_B, DK_B, DV_B), F32)),
        grid=(batch // nseq, seq // rows),
        in_specs=[col(4), col(5), col(6), col(7), vec, vec, state,
                  pl.BlockSpec((nseq, rows, W_A), lambda b, i: (b, i, 0)), wide,
                  pl.BlockSpec((W_A + W_B, D_MODEL), lambda b, i: (0, 0)),
                  pl.BlockSpec((1, D_MODEL), lambda b, i: (0, 0))],
        out_specs=(wide, state),
        scratch_shapes=([pltpu.VMEM((nseq, H_B, DV_B, DK_B), F32)]
                        + [pltpu.VMEM((n, W_B), BF16)] * 4
                        + [pltpu.VMEM((n, W_B), F32)] * 3
                        + [pltpu.VMEM((n, W_B), BF16)]),
        compiler_params=_params(("arbitrary", "arbitrary")),
        name=name,
    )(p3d, p3d, p3d, p3d, lb, ng, s0, oa3d, x3d, w_bf16, gf)


HGRN_PROMPT_ROWS = 512
HGRN_PROMPT_SEQS = 2
HGRN_SAMPLE_SEQS = 8


def kernel(x_prompt, x_sample, cache_attn_k, cache_attn_v, state_hgrn, ln_in_g, w_in,
           rel_bias, lb_gamma, hg_norm_g, w_out, ln_f_g):
    batch, seq, _ = x_prompt.shape
    dec_batch, dec_seq, _ = x_sample.shape
    depth = w_in.shape[0]
    assert depth == 1 and dec_seq == CHUNK and PAST_LEN % CHUNK == 0
    assert cache_attn_k.shape[2] == WINDOW_ROWS

    lb_all = jnp.cumsum(jax.nn.softmax(lb_gamma.astype(F32), axis=0), axis=0)
    lb = lb_all[0].reshape(1, W_B)
    ng = hg_norm_g[0].reshape(1, W_B)
    g_in = ln_in_g[0].reshape(1, D_MODEL)
    g_f = ln_f_g.reshape(1, D_MODEL)
    w_in_b = w_in[0].astype(BF16)
    w_out_b = w_out[0].astype(BF16)

    xp = x_prompt.reshape(batch * seq, D_MODEL)
    xs = x_sample.reshape(dec_batch * dec_seq, D_MODEL)

    bias_pair, bias_single = _bias_tables(rel_bias[0])

    pp, ps, k_s, v_s = _inproj(xp, xs, g_in, w_in_b)

    oa_p = _attn_prompt(pp, bias_pair, batch, seq)
    ckt = jnp.transpose(cache_attn_k[0], (0, 2, 3, 1)).reshape(dec_batch, W_A, WINDOW_ROWS)
    cvt = jnp.transpose(cache_attn_v[0], (0, 2, 3, 1)).reshape(dec_batch, W_A, WINDOW_ROWS)
    oa_s = _attn_sample(ps, ckt, cvt, bias_single, dec_batch)

    s0_p = jnp.zeros((batch, H_B, DK_B, DV_B), F32)
    y_p, st_p = _hgrn(pp.reshape(batch, seq, N_IN), lb, ng, s0_p,
                      oa_p.reshape(batch, seq, W_A), x_prompt, w_out_b, g_f,
                      HGRN_PROMPT_SEQS, HGRN_PROMPT_ROWS, "hgrn_out_prompt")
    y_s, st_s = _hgrn(ps.reshape(dec_batch, dec_seq, N_IN), lb, ng, state_hgrn[0],
                      oa_s.reshape(dec_batch, dec_seq, W_A), x_sample, w_out_b, g_f,
                      HGRN_SAMPLE_SEQS, dec_seq, "hgrn_out_sample")

    rows_p = min(WINDOW_ROWS, seq)
    pp3 = pp.reshape(batch, seq, N_IN)
    k_p = pp3[:, seq - rows_p:, W_A:2 * W_A].reshape(1, batch, rows_p, H_A, DH_A)
    v_p = pp3[:, seq - rows_p:, 2 * W_A:3 * W_A].reshape(1, batch, rows_p, H_A, DH_A)
    k_s = k_s.reshape(1, dec_batch, dec_seq, H_A, DH_A)
    v_s = v_s.reshape(1, dec_batch, dec_seq, H_A, DH_A)

    return (y_p, y_s, k_p, v_p, st_p[None], k_s, v_s, st_s[None])
```

```python
import functools

import jax
import jax.numpy as jnp
from jax import lax
from jax.experimental import pallas as pl
from jax.experimental.pallas import tpu as pltpu

F32 = jnp.float32
BF16 = jnp.bfloat16

D_MODEL = 1024
CHUNK = 64
LEFT_CHUNKS = 8
WINDOW_ROWS = LEFT_CHUNKS * CHUNK
BAND = WINDOW_ROWS + CHUNK
W_A = 512
H_A = 8
DH_A = 64
W_B = 512
H_B = 4
DK_B = 128
DV_B = 128
REL_CLIP = 128
GLA_BLOCK = 16
EPS = 1e-6
ATTN_SCALE = DH_A ** -0.5
NEG_INF = -1e30
N_IN = 4 * W_A + 4 * W_B
PAST_LEN = 2048

LANES = 128
HEAD_PAIRS = H_A // 2
REL_TABLE = 2 * REL_CLIP + 1
REL_TABLE_PAD = 384
TOEPLITZ_LEN = 640
VMEM_LIMIT = 56 * 1024 * 1024


def _params(semantics):
    return pltpu.CompilerParams(dimension_semantics=semantics,
                                vmem_limit_bytes=VMEM_LIMIT)


LOG2E = 1.4426950408889634


def _sigmoid(x):
    return 1.0 / (1.0 + jnp.exp2(x * -LOG2E))


def _silu(x):
    return x * _sigmoid(x)


def _nt(a, b):
    return lax.dot_general(a, b, (((1,), (1,)), ((), ())), preferred_element_type=F32)


def _emit_ahead(n_units, depth, first_fn, second_fn):
    pending = {}
    for t in range(n_units + depth):
        if t < n_units:
            pending[t] = first_fn(t)
        if t >= depth:
            second_fn(t - depth, pending.pop(t - depth))


INPROJ_ROWS = 512
INPROJ_SAMPLE_ROWS = 256
INPROJ_COLS = 512


def _inproj_kernel(xp_ref, xs_ref, g_ref, w_ref, pp_ref, ps_ref, ks_ref, vs_ref, *,
                   prompt_steps):
    i = pl.program_id(0)

    def project(x_ref, o_ref):
        x = x_ref[...]
        ms = jnp.mean(x * x, axis=-1, keepdims=True)
        xn = (x * lax.rsqrt(ms + EPS) * g_ref[...]).astype(BF16)
        for n0 in range(0, N_IN, INPROJ_COLS):
            o_ref[:, n0:n0 + INPROJ_COLS] = jnp.dot(
                xn, w_ref[:, n0:n0 + INPROJ_COLS].astype(BF16), preferred_element_type=F32)

    def split_heads(cols, out_ref):
        heads = jnp.stack([cols[:, h * DH_A:(h + 1) * DH_A] for h in range(H_A)], axis=0)
        out_ref[...] = jnp.swapaxes(heads, 0, 1)

    @pl.when(i < prompt_steps)
    def _():
        project(xp_ref, pp_ref)

    @pl.when(i >= prompt_steps)
    def _():
        project(xs_ref, ps_ref)
        split_heads(ps_ref[:, W_A:2 * W_A], ks_ref)
        split_heads(ps_ref[:, 2 * W_A:3 * W_A], vs_ref)


def _inproj(xp, xs, g, w):
    rows_p, rows_s = xp.shape[0], xs.shape[0]
    tp, ts = INPROJ_ROWS, INPROJ_SAMPLE_ROWS
    assert rows_p % tp == 0 and rows_s % ts == 0
    np_, ns = rows_p // tp, rows_s // ts

    def prompt_blk(i):
        return (jnp.minimum(i, np_ - 1), 0)

    def sample_blk(i):
        return (jnp.maximum(i - np_, 0), 0)

    def sample_blk3(i):
        return (jnp.maximum(i - np_, 0), 0, 0)

    return pl.pallas_call(
        functools.partial(_inproj_kernel, prompt_steps=np_),
        out_shape=(jax.ShapeDtypeStruct((rows_p, N_IN), F32),
                   jax.ShapeDtypeStruct((rows_s, N_IN), F32),
                   jax.ShapeDtypeStruct((rows_s, H_A, DH_A), F32),
                   jax.ShapeDtypeStruct((rows_s, H_A, DH_A), F32)),
        grid=(np_ + ns,),
        in_specs=[
            pl.BlockSpec((tp, D_MODEL), prompt_blk),
            pl.BlockSpec((ts, D_MODEL), sample_blk),
            pl.BlockSpec((1, D_MODEL), lambda i: (0, 0)),
            pl.BlockSpec((D_MODEL, N_IN), lambda i: (0, 0)),
        ],
        out_specs=(pl.BlockSpec((tp, N_IN), prompt_blk),
                   pl.BlockSpec((ts, N_IN), sample_blk),
                   pl.BlockSpec((ts, H_A, DH_A), sample_blk3),
                   pl.BlockSpec((ts, H_A, DH_A), sample_blk3)),
        compiler_params=_params(("arbitrary",)),
        name="inproj",
    )(xp, xs, g, w)


PAIR_BAND = BAND + CHUNK


def _bias_kernel(rb_ref, pair_ref, single_ref):
    m = lax.broadcasted_iota(jnp.int32, (REL_TABLE_PAD, TOEPLITZ_LEN), 1)
    t = lax.broadcasted_iota(jnp.int32, (REL_TABLE_PAD, TOEPLITZ_LEN), 0)
    idx = jnp.clip(BAND - 1 - m, -REL_CLIP, REL_CLIP) + REL_CLIP
    onehot = jnp.where(idx == t, 1.0, 0.0).astype(BF16)
    rb = rb_ref[...]
    hi = rb.astype(BF16)
    r1 = rb - hi.astype(F32)
    mid = r1.astype(BF16)
    lo = (r1 - mid.astype(F32)).astype(BF16)
    toep = (jnp.dot(hi, onehot, preferred_element_type=F32)
            + jnp.dot(mid, onehot, preferred_element_type=F32)
            + jnp.dot(lo, onehot, preferred_element_type=F32))
    key = lax.broadcasted_iota(jnp.int32, (CHUNK, TOEPLITZ_LEN), 1)
    for p in range(HEAD_PAIRS):
        for u in range(2):
            halves = []
            for h in (2 * p, 2 * p + 1):
                rows = jnp.broadcast_to(toep[h:h + 1, :], (CHUNK, TOEPLITZ_LEN))
                shift = (TOEPLITZ_LEN - (CHUNK - 1) + CHUNK * u) % TOEPLITZ_LEN
                rolled = pltpu.roll(rows, shift, 1, stride=1, stride_axis=0)
                seen = (key >= CHUNK * u) & (key < CHUNK * u + BAND)
                halves.append(jnp.where(seen, rolled * LOG2E, NEG_INF))
            tile_t = jnp.concatenate(halves, axis=0).T
            pair_ref[p, :, u * LANES:(u + 1) * LANES] = tile_t
            if u == 0:
                single_ref[p] = tile_t[:BAND, :]


def _bias_tables(rel_bias_l):
    rb = jnp.pad(rel_bias_l, ((0, 0), (0, REL_TABLE_PAD - REL_TABLE)))
    return pl.pallas_call(
        _bias_kernel,
        out_shape=(jax.ShapeDtypeStruct((HEAD_PAIRS, PAIR_BAND, 2 * LANES), F32),
                   jax.ShapeDtypeStruct((HEAD_PAIRS, BAND, LANES), F32)),
        name="rel_bias_table",
    )(rb)


def _attn_probs(qs, k2, bias_t, valid_from):
    first = lax.broadcasted_iota(jnp.int32, (CHUNK, LANES), 1) < DH_A
    parts = []
    for q in qs:
        q = q * (ATTN_SCALE * LOG2E)
        parts += [jnp.where(first, q, 0.0), jnp.where(first, 0.0, q)]
    qbd = jnp.concatenate(parts, axis=0).astype(BF16)
    s = _nt(k2, qbd) + bias_t
    if valid_from is not None:
        key = lax.broadcasted_iota(jnp.int32, s.shape, 0)
        s = jnp.where(key >= valid_from, s, NEG_INF)
    mx = jnp.max(s, axis=0, keepdims=True)
    e = jnp.exp2(s - mx)
    return e.astype(BF16), jnp.sum(e, axis=0, keepdims=True)


def _attn_apply(probs, ags, vt):
    e, denom = probs
    first = lax.broadcasted_iota(jnp.int32, (CHUNK, LANES), 1) < DH_A
    ot = jnp.dot(vt, e, preferred_element_type=F32)
    o2 = (ot * (1.0 / denom)).T
    outs = []
    for u, ag in enumerate(ags):
        r = 2 * CHUNK * u
        o = jnp.where(first, o2[r:r + CHUNK], o2[r + CHUNK:r + 2 * CHUNK])
        outs.append(o * _silu(ag))
    return outs


ATTN_PIPE_DEPTH = 3


def _attn_pipeline(n_units, probs_fn, apply_fn):
    _emit_ahead(n_units, ATTN_PIPE_DEPTH, probs_fn, apply_fn)


ATTN_ROWS = 1024


def _attn_prompt_kernel(q_ref, k_ref, v_ref, ag_ref, bias_ref, o_ref, kh_ref, vth_ref):
    i = pl.program_id(1)

    @pl.when(i == 0)
    def _():
        kh_ref[0:WINDOW_ROWS, :] = jnp.zeros((WINDOW_ROWS, W_A), BF16)
        vth_ref[:, :, 0:WINDOW_ROWS] = jnp.zeros((HEAD_PAIRS, LANES, WINDOW_ROWS), BF16)

    kh_ref[WINDOW_ROWS:, :] = k_ref[...].astype(BF16)
    for p in range(HEAD_PAIRS):
        vth_ref[p, :, WINDOW_ROWS:] = v_ref[:, p * LANES:(p + 1) * LANES].T.astype(BF16)

    def body(first_step):
        def unit(t):
            cp, p = divmod(t, HEAD_PAIRS)
            r0 = cp * 2 * CHUNK
            rs = [slice(r0 + u * CHUNK, r0 + (u + 1) * CHUNK) for u in range(2)]
            return cp, p, r0, rs, slice(p * LANES, (p + 1) * LANES)

        def probs(t):
            cp, p, r0, rs, cs = unit(t)
            valid_from = (LEFT_CHUNKS - 2 * cp) * CHUNK if first_step else None
            return _attn_probs([q_ref[r, cs] for r in rs], kh_ref[r0:r0 + PAIR_BAND, cs],
                               bias_ref[p], valid_from)

        def apply(t, pr):
            cp, p, r0, rs, cs = unit(t)
            outs = _attn_apply(pr, [ag_ref[r, cs] for r in rs],
                               vth_ref[p, :, r0:r0 + PAIR_BAND])
            for r, o in zip(rs, outs):
                o_ref[r, cs] = o.astype(o_ref.dtype)

        _attn_pipeline((ATTN_ROWS // (2 * CHUNK)) * HEAD_PAIRS, probs, apply)

    @pl.when(i == 0)
    def _():
        body(True)

    @pl.when(i != 0)
    def _():
        body(False)

    kh_ref[0:WINDOW_ROWS, :] = kh_ref[ATTN_ROWS:ATTN_ROWS + WINDOW_ROWS, :]
    vth_ref[:, :, 0:WINDOW_ROWS] = vth_ref[:, :, ATTN_ROWS:ATTN_ROWS + WINDOW_ROWS]


def _attn_prompt(p2d, bias_pair, batch, seq):
    nt = seq // ATTN_ROWS

    def col(c):
        return pl.BlockSpec((ATTN_ROWS, W_A), lambda b, i: (b * nt + i, c))

    return pl.pallas_call(
        _attn_prompt_kernel,
        out_shape=jax.ShapeDtypeStruct((batch * seq, W_A), BF16),
        grid=(batch, nt),
        in_specs=[col(0), col(1), col(2), col(3),
                  pl.BlockSpec((HEAD_PAIRS, PAIR_BAND, 2 * LANES), lambda b, i: (0, 0, 0))],
        out_specs=pl.BlockSpec((ATTN_ROWS, W_A), lambda b, i: (b * nt + i, 0)),
        scratch_shapes=[pltpu.VMEM((WINDOW_ROWS + ATTN_ROWS, W_A), BF16),
                        pltpu.VMEM((HEAD_PAIRS, LANES, WINDOW_ROWS + ATTN_ROWS), BF16)],
        compiler_params=_params(("arbitrary", "arbitrary")),
        name="attn_prompt",
    )(p2d, p2d, p2d, p2d, bias_pair)


ATTN_SAMPLE_SEQS = 4


def _attn_sample_kernel(q_ref, k_ref, v_ref, ag_ref, ckt_ref, cvt_ref, bias_ref, o_ref,
                        kh_ref, vth_ref):
    for s in range(ATTN_SAMPLE_SEQS):
        rs = slice(s * CHUNK, (s + 1) * CHUNK)
        kh_ref[s, WINDOW_ROWS:, :] = k_ref[rs, :].astype(BF16)
        for p in range(HEAD_PAIRS):
            cs = slice(p * LANES, (p + 1) * LANES)
            kh_ref[s, 0:WINDOW_ROWS, cs] = ckt_ref[s, cs, :].T.astype(BF16)
            vth_ref[s, p, :, 0:WINDOW_ROWS] = cvt_ref[s, cs, :].astype(BF16)
            vth_ref[s, p, :, WINDOW_ROWS:] = v_ref[rs, cs].T.astype(BF16)

    def unit(t):
        s, p = divmod(t, HEAD_PAIRS)
        return s, p, slice(s * CHUNK, (s + 1) * CHUNK), slice(p * LANES, (p + 1) * LANES)

    def probs(t):
        s, p, rs, cs = unit(t)
        return _attn_probs([q_ref[rs, cs]], kh_ref[s, :, cs], bias_ref[p], None)

    def apply(t, pr):
        s, p, rs, cs = unit(t)
        (o,) = _attn_apply(pr, [ag_ref[rs, cs]], vth_ref[s, p])
        o_ref[rs, cs] = o.astype(o_ref.dtype)

    _attn_pipeline(ATTN_SAMPLE_SEQS * HEAD_PAIRS, probs, apply)


def _attn_sample(p2d, cache_kt, cache_vt, bias_single, batch):
    ns = ATTN_SAMPLE_SEQS

    def col(c):
        return pl.BlockSpec((ns * CHUNK, W_A), lambda b: (b, c))

    cache = pl.BlockSpec((ns, W_A, WINDOW_ROWS), lambda b: (b, 0, 0))
    return pl.pallas_call(
        _attn_sample_kernel,
        out_shape=jax.ShapeDtypeStruct((batch * CHUNK, W_A), BF16),
        grid=(batch // ns,),
        in_specs=[col(0), col(1), col(2), col(3), cache, cache,
                  pl.BlockSpec((HEAD_PAIRS, BAND, LANES), lambda b: (0, 0, 0))],
        out_specs=pl.BlockSpec((ns * CHUNK, W_A), lambda b: (b, 0)),
        scratch_shapes=[pltpu.VMEM((ns, BAND, W_A), BF16),
                        pltpu.VMEM((ns, HEAD_PAIRS, LANES, BAND), BF16)],
        compiler_params=_params(("arbitrary",)),
        name="attn_sample",
    )(p2d, p2d, p2d, p2d, cache_kt, cache_vt, bias_single)


HGRN_GROUP_BLOCKS = 4
HGRN_GROUP = HGRN_GROUP_BLOCKS * GLA_BLOCK
HGRN_PIPE_DEPTH = 4
OUT_ROWS = 256


def _block_cumsum(x):
    row = lax.broadcasted_iota(jnp.int32, x.shape, 0) & (GLA_BLOCK - 1)
    s = 1
    while s < GLA_BLOCK:
        x = x + jnp.where(row >= s, pltpu.roll(x, s, 0), 0.0)
        s *= 2
    return x


def _hgrn_kernel(hq_ref, hf_ref, hi_ref, hg_ref, lb_ref, ng_ref, s0_ref,
                 oa_ref, x_ref, w_ref, gf_ref,
                 y_ref, sout_ref, st_ref, qt_ref, kt_ref, kd_ref, v_ref,
                 qtf_ref, kdf_ref, bl_ref, ob_ref):
    i = pl.program_id(1)
    nseq, seq_rows = hq_ref.shape[0], hq_ref.shape[1]
    rows = nseq * seq_rows
    nb = rows // GLA_BLOCK

    def out_rows(c):
        r0 = c * OUT_ROWS
        if seq_rows >= OUT_ROWS:
            s, off = divmod(r0, seq_rows)
            return (s, slice(off, off + OUT_ROWS))
        n = OUT_ROWS // seq_rows
        return (slice(r0 // seq_rows, r0 // seq_rows + n),)

    def out_read(ref, c):
        return ref[out_rows(c)].reshape(OUT_ROWS, ref.shape[-1])

    def out_write(ref, c, val):
        if seq_rows < OUT_ROWS:
            val = val.reshape(OUT_ROWS // seq_rows, seq_rows, ref.shape[-1])
        ref[out_rows(c)] = val

    @pl.when(i == 0)
    def _():
        for s in range(nseq):
            for h in range(H_B):
                st_ref[s, h] = s0_ref[s, h].T

    def flat(ref):
        return ref[...].reshape(rows, W_B)

    lb = lb_ref[...]
    f = lb + (1.0 - lb) * _sigmoid(flat(hf_ref))
    b = _block_cumsum(jnp.log(f) * LOG2E)
    b3 = b.reshape(nb, GLA_BLOCK, W_B)
    bl = jnp.broadcast_to(b3[:, GLA_BLOCK - 1:GLA_BLOCK, :],
                          (nb, GLA_BLOCK, W_B)).reshape(rows, W_B)
    kk = 1.0 - f
    qt = _silu(flat(hq_ref)) * jnp.exp2(b)
    kd = kk * jnp.exp2(bl - b)
    qtf_ref[...] = qt
    kdf_ref[...] = kd
    bl_ref[...] = bl
    qt_ref[...] = qt.astype(BF16)
    kt_ref[...] = (kk * jnp.exp2(-b)).astype(BF16)
    kd_ref[...] = kd.astype(BF16)
    v_ref[...] = flat(hi_ref).astype(BF16)

    def begin(c):
        out_write(y_ref, c, out_read(x_ref, c)
                  + jnp.dot(out_read(oa_ref, c), w_ref[0:W_A, :], preferred_element_type=F32))

    r_i = lax.broadcasted_iota(jnp.int32, (HGRN_GROUP, HGRN_GROUP), 0)
    c_i = lax.broadcasted_iota(jnp.int32, (HGRN_GROUP, HGRN_GROUP), 1)
    rb, cb = r_i // GLA_BLOCK, c_i // GLA_BLOCK
    m_diag = (rb == cb) & (r_i >= c_i)
    m_adj = rb == cb + 1
    m_far = rb >= cb + 2

    groups = seq_rows // HGRN_GROUP
    st = {(s, h): st_ref[s, h] for s in range(nseq) for h in range(H_B)}

    def unit(t):
        sg, h = divmod(t, H_B)
        s, g = divmod(sg, groups)
        r0 = s * seq_rows + g * HGRN_GROUP
        return s, g, h, r0, slice(r0, r0 + HGRN_GROUP), slice(h * DK_B, (h + 1) * DK_B)

    def block_decays(r0, cs):
        return [bl_ref[r0 + j * GLA_BLOCK:r0 + j * GLA_BLOCK + 1, cs]
                for j in range(HGRN_GROUP_BLOCKS)]

    def scaled(x, log2_scales):
        parts = []
        for j, sc in enumerate(log2_scales):
            xb = x[j * GLA_BLOCK:(j + 1) * GLA_BLOCK]
            parts.append(xb if sc is None else xb * jnp.exp2(sc))
        return jnp.concatenate(parts, axis=0).astype(BF16)

    def local(t):
        s, g, h, r0, rs, cs = unit(t)
        if h == 0 and r0 % OUT_ROWS == 0:
            begin(r0 // OUT_ROWS)
        d0, d1, d2, d3 = block_decays(r0, cs)
        qt_g = qt_ref[rs, cs]
        kdf = kdf_ref[rs, cs]
        q1 = scaled(qtf_ref[rs, cs], [None, None, None, d2])
        k1 = scaled(kdf, [d1, None, None, None])
        a = jnp.where(m_diag, _nt(qt_g, kt_ref[rs, cs]),
                      jnp.where(m_adj, _nt(qt_g, kd_ref[rs, cs]),
                                jnp.where(m_far, _nt(q1, k1), 0.0)))
        kdp = scaled(kdf, [(d1 + d2) + d3, d2 + d3, d3, None])
        ut = lax.dot_general(v_ref[rs, cs], kdp, (((0,), (0,)), ((), ())),
                             preferred_element_type=F32)
        return a.astype(BF16), ut

    def finish(c):
        hrow = out_read(y_ref, c) + jnp.dot(ob_ref[c * OUT_ROWS:(c + 1) * OUT_ROWS, :],
                                             w_ref[W_A:, :], preferred_element_type=F32)
        ms = jnp.mean(hrow * hrow, axis=-1, keepdims=True)
        out_write(y_ref, c, hrow * lax.rsqrt(ms + EPS) * gf_ref[...])

    def carry(t, loc):
        s, g, h, r0, rs, cs = unit(t)
        a, ut = loc
        d0, d1, d2, d3 = block_decays(r0, cs)
        qs = scaled(qtf_ref[rs, cs], [None, d0, d0 + d1, (d0 + d1) + d2])
        o = (_nt(qs, st[s, h].astype(BF16))
             + jnp.dot(a, v_ref[rs, cs], preferred_element_type=F32))
        dec = jnp.exp2(((d0 + d1) + d2) + d3)
        st[s, h] = st[s, h] * dec + ut
        y = o * lax.rsqrt(jnp.mean(o * o, axis=-1, keepdims=True) + EPS) * ng_ref[:, cs]
        gs = slice(g * HGRN_GROUP, (g + 1) * HGRN_GROUP)
        ob_ref[rs, cs] = (y * _silu(hg_ref[s, gs, cs])).astype(BF16)
        if h == H_B - 1 and (r0 + HGRN_GROUP) % OUT_ROWS == 0:
            finish((r0 + HGRN_GROUP) // OUT_ROWS - 1)

    _emit_ahead(nseq * groups * H_B, HGRN_PIPE_DEPTH, local, carry)
    for (s, h), val in st.items():
        st_ref[s, h] = val

    @pl.when(i == pl.num_programs(1) - 1)
    def _():
        for s in range(nseq):
            for h in range(H_B):
                sout_ref[s, h] = st_ref[s, h].T


def _hgrn(p3d, lb, ng, s0, oa3d, x3d, w_bf16, gf, nseq, rows, name):
    batch, seq, _ = p3d.shape
    assert batch % nseq == 0 and seq % rows == 0 and rows % HGRN_GROUP == 0
    assert (nseq * rows) % OUT_ROWS == 0 and (rows % OUT_ROWS == 0 or OUT_ROWS % rows == 0)

    def col(c):
        return pl.BlockSpec((nseq, rows, W_B), lambda b, i: (b, i, c))

    vec = pl.BlockSpec((1, W_B), lambda b, i: (0, 0))
    state = pl.BlockSpec((nseq, H_B, DK_B, DV_B), lambda b, i: (b, 0, 0, 0))
    wide = pl.BlockSpec((nseq, rows, D_MODEL), lambda b, i: (b, i, 0))
    n = nseq * rows
    return pl.pallas_call(
        _hgrn_kernel,
        out_shape=(jax.ShapeDtypeStruct((batch, seq, D_MODEL), F32),
                   jax.ShapeDtypeStruct((batch, H_B, DK_B, DV_B), F32)),
        grid=(batch // nseq, seq // rows),
        in_specs=[col(4), col(5), col(6), col(7), vec, vec, state,
                  pl.BlockSpec((nseq, rows, W_A), lambda b, i: (b, i, 0)), wide,
                  pl.BlockSpec((W_A + W_B, D_MODEL), lambda b, i: (0, 0)),
                  pl.BlockSpec((1, D_MODEL), lambda b, i: (0, 0))],
        out_specs=(wide, state),
        scratch_shapes=([pltpu.VMEM((nseq, H_B, DV_B, DK_B), F32)]
                        + [pltpu.VMEM((n, W_B), BF16)] * 4
                        + [pltpu.VMEM((n, W_B), F32)] * 3
                        + [pltpu.VMEM((n, W_B), BF16)]),
        compiler_params=_params(("arbitrary", "arbitrary")),
        name=name,
    )(p3d, p3d, p3d, p3d, lb, ng, s0, oa3d, x3d, w_bf16, gf)


HGRN_PROMPT_ROWS = 512
HGRN_PROMPT_SEQS = 2
HGRN_SAMPLE_SEQS = 8


def kernel(x_prompt, x_sample, cache_attn_k, cache_attn_v, state_hgrn, ln_in_g, w_in,
           rel_bias, lb_gamma, hg_norm_g, w_out, ln_f_g):
    batch, seq, _ = x_prompt.shape
    dec_batch, dec_seq, _ = x_sample.shape
    depth = w_in.shape[0]
    assert depth == 1 and dec_seq == CHUNK and PAST_LEN % CHUNK == 0
    assert cache_attn_k.shape[2] == WINDOW_ROWS

    lb_all = jnp.cumsum(jax.nn.softmax(lb_gamma.astype(F32), axis=0), axis=0)
    lb = lb_all[0].reshape(1, W_B)
    ng = hg_norm_g[0].reshape(1, W_B)
    g_in = ln_in_g[0].reshape(1, D_MODEL)
    g_f = ln_f_g.reshape(1, D_MODEL)
    w_out_b = w_out[0].astype(BF16)

    xp = x_prompt.reshape(batch * seq, D_MODEL)
    xs = x_sample.reshape(dec_batch * dec_seq, D_MODEL)

    bias_pair, bias_single = _bias_tables(rel_bias[0])

    pp, ps, k_s, v_s = _inproj(xp, xs, g_in, w_in[0])

    oa_p = _attn_prompt(pp, bias_pair, batch, seq)
    ckt = jnp.transpose(cache_attn_k[0], (0, 2, 3, 1)).reshape(dec_batch, W_A, WINDOW_ROWS)
    cvt = jnp.transpose(cache_attn_v[0], (0, 2, 3, 1)).reshape(dec_batch, W_A, WINDOW_ROWS)
    oa_s = _attn_sample(ps, ckt, cvt, bias_single, dec_batch)

    s0_p = jnp.zeros((batch, H_B, DK_B, DV_B), F32)
    y_p, st_p = _hgrn(pp.reshape(batch, seq, N_IN), lb, ng, s0_p,
                      oa_p.reshape(batch, seq, W_A), x_prompt, w_out_b, g_f,
                      HGRN_PROMPT_SEQS, HGRN_PROMPT_ROWS, "hgrn_out_prompt")
    y_s, st_s = _hgrn(ps.reshape(dec_batch, dec_seq, N_IN), lb, ng, state_hgrn[0],
                      oa_s.reshape(dec_batch, dec_seq, W_A), x_sample, w_out_b, g_f,
                      HGRN_SAMPLE_SEQS, dec_seq, "hgrn_out_sample")

    rows_p = min(WINDOW_ROWS, seq)
    pp3 = pp.reshape(batch, seq, N_IN)
    k_p = pp3[:, seq - rows_p:, W_A:2 * W_A].reshape(1, batch, rows_p, H_A, DH_A)
    v_p = pp3[:, seq - rows_p:, 2 * W_A:3 * W_A].reshape(1, batch, rows_p, H_A, DH_A)
    k_s = k_s.reshape(1, dec_batch, dec_seq, H_A, DH_A)
    v_s = v_s.reshape(1, dec_batch, dec_seq, H_A, DH_A)

    return (y_p, y_s, k_p, v_p, st_p[None], k_s, v_s, st_s[None])
```

```python
import functools

import jax
import jax.numpy as jnp
from jax import lax
from jax.experimental import pallas as pl
from jax.experimental.pallas import tpu as pltpu

F32 = jnp.float32
BF16 = jnp.bfloat16

D_MODEL = 1024
CHUNK = 64
LEFT_CHUNKS = 8
WINDOW_ROWS = LEFT_CHUNKS * CHUNK
BAND = WINDOW_ROWS + CHUNK
W_A = 512
H_A = 8
DH_A = 64
W_B = 512
H_B = 4
DK_B = 128
DV_B = 128
REL_CLIP = 128
GLA_BLOCK = 16
EPS = 1e-6
ATTN_SCALE = DH_A ** -0.5
NEG_INF = -1e30
N_IN = 4 * W_A + 4 * W_B
PAST_LEN = 2048

LANES = 128
HEAD_PAIRS = H_A // 2
REL_TABLE = 2 * REL_CLIP + 1
REL_TABLE_PAD = 384
TOEPLITZ_LEN = 640
VMEM_LIMIT = 56 * 1024 * 1024


def _params(semantics):
    return pltpu.CompilerParams(dimension_semantics=semantics,
                                vmem_limit_bytes=VMEM_LIMIT)


LOG2E = 1.4426950408889634


def _sigmoid(x):
    return 1.0 / (1.0 + jnp.exp2(x * -LOG2E))


def _silu(x):
    return x * _sigmoid(x)


def _nt(a, b):
    return lax.dot_general(a, b, (((1,), (1,)), ((), ())), preferred_element_type=F32)


def _emit_ahead(n_units, depth, first_fn, second_fn):
    pending = {}
    for t in range(n_units + depth):
        if t < n_units:
            pending[t] = first_fn(t)
        if t >= depth:
            second_fn(t - depth, pending.pop(t - depth))


INPROJ_ROWS = 512
INPROJ_SAMPLE_ROWS = 256
INPROJ_COLS = 512


def _inproj_kernel(xp_ref, xs_ref, g_ref, w_ref, pp_ref, ps_ref, ks_ref, vs_ref, *,
                   prompt_steps):
    i = pl.program_id(0)

    def project(x_ref, o_ref):
        x = x_ref[...]
        ms = jnp.mean(x * x, axis=-1, keepdims=True)
        xn = (x * lax.rsqrt(ms + EPS) * g_ref[...]).astype(BF16)
        for n0 in range(0, N_IN, INPROJ_COLS):
            o_ref[:, n0:n0 + INPROJ_COLS] = jnp.dot(
                xn, w_ref[:, n0:n0 + INPROJ_COLS].astype(BF16), preferred_element_type=F32)

    def split_heads(cols, out_ref):
        heads = jnp.stack([cols[:, h * DH_A:(h + 1) * DH_A] for h in range(H_A)], axis=0)
        out_ref[...] = jnp.swapaxes(heads, 0, 1)

    @pl.when(i < prompt_steps)
    def _():
        project(xp_ref, pp_ref)

    @pl.when(i >= prompt_steps)
    def _():
        project(xs_ref, ps_ref)
        split_heads(ps_ref[:, W_A:2 * W_A], ks_ref)
        split_heads(ps_ref[:, 2 * W_A:3 * W_A], vs_ref)


def _inproj(xp, xs, g, w):
    rows_p, rows_s = xp.shape[0], xs.shape[0]
    tp, ts = INPROJ_ROWS, INPROJ_SAMPLE_ROWS
    assert rows_p % tp == 0 and rows_s % ts == 0
    np_, ns = rows_p // tp, rows_s // ts

    def prompt_blk(i):
        return (jnp.minimum(i, np_ - 1), 0)

    def sample_blk(i):
        return (jnp.maximum(i - np_, 0), 0)

    def sample_blk3(i):
        return (jnp.maximum(i - np_, 0), 0, 0)

    return pl.pallas_call(
        functools.partial(_inproj_kernel, prompt_steps=np_),
        out_shape=(jax.ShapeDtypeStruct((rows_p, N_IN), F32),
                   jax.ShapeDtypeStruct((rows_s, N_IN), F32),
                   jax.ShapeDtypeStruct((rows_s, H_A, DH_A), F32),
                   jax.ShapeDtypeStruct((rows_s, H_A, DH_A), F32)),
        grid=(np_ + ns,),
        in_specs=[
            pl.BlockSpec((tp, D_MODEL), prompt_blk),
            pl.BlockSpec((ts, D_MODEL), sample_blk),
            pl.BlockSpec((1, D_MODEL), lambda i: (0, 0)),
            pl.BlockSpec((D_MODEL, N_IN), lambda i: (0, 0)),
        ],
        out_specs=(pl.BlockSpec((tp, N_IN), prompt_blk),
                   pl.BlockSpec((ts, N_IN), sample_blk),
                   pl.BlockSpec((ts, H_A, DH_A), sample_blk3),
                   pl.BlockSpec((ts, H_A, DH_A), sample_blk3)),
        compiler_params=_params(("arbitrary",)),
        name="inproj",
    )(xp, xs, g, w)


PAIR_BAND = BAND + CHUNK


def _bias_kernel(rb_ref, pair_ref, single_ref):
    m = lax.broadcasted_iota(jnp.int32, (REL_TABLE_PAD, TOEPLITZ_LEN), 1)
    t = lax.broadcasted_iota(jnp.int32, (REL_TABLE_PAD, TOEPLITZ_LEN), 0)
    idx = jnp.clip(BAND - 1 - m, -REL_CLIP, REL_CLIP) + REL_CLIP
    onehot = jnp.where(idx == t, 1.0, 0.0).astype(BF16)
    rb = rb_ref[...]
    hi = rb.astype(BF16)
    r1 = rb - hi.astype(F32)
    mid = r1.astype(BF16)
    lo = (r1 - mid.astype(F32)).astype(BF16)
    toep = (jnp.dot(hi, onehot, preferred_element_type=F32)
            + jnp.dot(mid, onehot, preferred_element_type=F32)
            + jnp.dot(lo, onehot, preferred_element_type=F32))
    key = lax.broadcasted_iota(jnp.int32, (CHUNK, TOEPLITZ_LEN), 1)
    for p in range(HEAD_PAIRS):
        for u in range(2):
            halves = []
            for h in (2 * p, 2 * p + 1):
                rows = jnp.broadcast_to(toep[h:h + 1, :], (CHUNK, TOEPLITZ_LEN))
                shift = (TOEPLITZ_LEN - (CHUNK - 1) + CHUNK * u) % TOEPLITZ_LEN
                rolled = pltpu.roll(rows, shift, 1, stride=1, stride_axis=0)
                seen = (key >= CHUNK * u) & (key < CHUNK * u + BAND)
                halves.append(jnp.where(seen, rolled * LOG2E, NEG_INF))
            tile_t = jnp.concatenate(halves, axis=0).T
            pair_ref[p, :, u * LANES:(u + 1) * LANES] = tile_t
            if u == 0:
                single_ref[p] = tile_t[:BAND, :]


def _bias_tables(rel_bias_l):
    rb = jnp.pad(rel_bias_l, ((0, 0), (0, REL_TABLE_PAD - REL_TABLE)))
    return pl.pallas_call(
        _bias_kernel,
        out_shape=(jax.ShapeDtypeStruct((HEAD_PAIRS, PAIR_BAND, 2 * LANES), F32),
                   jax.ShapeDtypeStruct((HEAD_PAIRS, BAND, LANES), F32)),
        name="rel_bias_table",
    )(rb)


def _attn_probs(qs, k2, bias_t, valid_from):
    first = lax.broadcasted_iota(jnp.int32, (CHUNK, LANES), 1) < DH_A
    parts = []
    for q in qs:
        q = q * (ATTN_SCALE * LOG2E)
        parts += [jnp.where(first, q, 0.0), jnp.where(first, 0.0, q)]
    qbd = jnp.concatenate(parts, axis=0).astype(BF16)
    s = _nt(k2, qbd) + bias_t
    if valid_from is not None:
        key = lax.broadcasted_iota(jnp.int32, s.shape, 0)
        s = jnp.where(key >= valid_from, s, NEG_INF)
    mx = jnp.max(s, axis=0, keepdims=True)
    e = jnp.exp2(s - mx)
    return e.astype(BF16), jnp.sum(e, axis=0, keepdims=True)


def _attn_apply(probs, ags, vt):
    e, denom = probs
    first = lax.broadcasted_iota(jnp.int32, (CHUNK, LANES), 1) < DH_A
    ot = jnp.dot(vt, e, preferred_element_type=F32)
    o2 = (ot * (1.0 / denom)).T
    outs = []
    for u, ag in enumerate(ags):
        r = 2 * CHUNK * u
        o = jnp.where(first, o2[r:r + CHUNK], o2[r + CHUNK:r + 2 * CHUNK])
        outs.append(o * _silu(ag))
    return outs


ATTN_PIPE_DEPTH = 3


def _attn_pipeline(n_units, probs_fn, apply_fn):
    _emit_ahead(n_units, ATTN_PIPE_DEPTH, probs_fn, apply_fn)


ATTN_ROWS = 1024


def _attn_prompt_kernel(q_ref, k_ref, v_ref, ag_ref, bias_ref, o_ref, kh_ref, vth_ref):
    i = pl.program_id(1)

    @pl.when(i == 0)
    def _():
        kh_ref[0:WINDOW_ROWS, :] = jnp.zeros((WINDOW_ROWS, W_A), BF16)
        vth_ref[:, :, 0:WINDOW_ROWS] = jnp.zeros((HEAD_PAIRS, LANES, WINDOW_ROWS), BF16)

    kh_ref[WINDOW_ROWS:, :] = k_ref[...].astype(BF16)
    for p in range(HEAD_PAIRS):
        vth_ref[p, :, WINDOW_ROWS:] = v_ref[:, p * LANES:(p + 1) * LANES].T.astype(BF16)

    def body(first_step):
        def unit(t):
            cp, p = divmod(t, HEAD_PAIRS)
            r0 = cp * 2 * CHUNK
            rs = [slice(r0 + u * CHUNK, r0 + (u + 1) * CHUNK) for u in range(2)]
            return cp, p, r0, rs, slice(p * LANES, (p + 1) * LANES)

        def probs(t):
            cp, p, r0, rs, cs = unit(t)
            valid_from = (LEFT_CHUNKS - 2 * cp) * CHUNK if first_step else None
            return _attn_probs([q_ref[r, cs] for r in rs], kh_ref[r0:r0 + PAIR_BAND, cs],
                               bias_ref[p], valid_from)

        def apply(t, pr):
            cp, p, r0, rs, cs = unit(t)
            outs = _attn_apply(pr, [ag_ref[r, cs] for r in rs],
                               vth_ref[p, :, r0:r0 + PAIR_BAND])
            for r, o in zip(rs, outs):
                o_ref[r, cs] = o.astype(o_ref.dtype)

        _attn_pipeline((ATTN_ROWS // (2 * CHUNK)) * HEAD_PAIRS, probs, apply)

    @pl.when(i == 0)
    def _():
        body(True)

    @pl.when(i != 0)
    def _():
        body(False)

    kh_ref[0:WINDOW_ROWS, :] = kh_ref[ATTN_ROWS:ATTN_ROWS + WINDOW_ROWS, :]
    vth_ref[:, :, 0:WINDOW_ROWS] = vth_ref[:, :, ATTN_ROWS:ATTN_ROWS + WINDOW_ROWS]


def _attn_prompt(p2d, bias_pair, batch, seq):
    nt = seq // ATTN_ROWS

    def col(c):
        return pl.BlockSpec((ATTN_ROWS, W_A), lambda b, i: (b * nt + i, c))

    return pl.pallas_call(
        _attn_prompt_kernel,
        out_shape=jax.ShapeDtypeStruct((batch * seq, W_A), BF16),
        grid=(batch, nt),
        in_specs=[col(0), col(1), col(2), col(3),
                  pl.BlockSpec((HEAD_PAIRS, PAIR_BAND, 2 * LANES), lambda b, i: (0, 0, 0))],
        out_specs=pl.BlockSpec((ATTN_ROWS, W_A), lambda b, i: (b * nt + i, 0)),
        scratch_shapes=[pltpu.VMEM((WINDOW_ROWS + ATTN_ROWS, W_A), BF16),
                        pltpu.VMEM((HEAD_PAIRS, LANES, WINDOW_ROWS + ATTN_ROWS), BF16)],
        compiler_params=_params(("arbitrary", "arbitrary")),
        name="attn_prompt",
    )(p2d, p2d, p2d, p2d, bias_pair)


ATTN_SAMPLE_SEQS = 4


def _attn_sample_kernel(q_ref, k_ref, v_ref, ag_ref, ckt_ref, cvt_ref, bias_ref, o_ref,
                        kh_ref, vth_ref):
    for s in range(ATTN_SAMPLE_SEQS):
        rs = slice(s * CHUNK, (s + 1) * CHUNK)
        kh_ref[s, WINDOW_ROWS:, :] = k_ref[rs, :].astype(BF16)
        for p in range(HEAD_PAIRS):
            cs = slice(p * LANES, (p + 1) * LANES)
            kh_ref[s, 0:WINDOW_ROWS, cs] = ckt_ref[s, cs, :].T.astype(BF16)
            vth_ref[s, p, :, 0:WINDOW_ROWS] = cvt_ref[s, cs, :].astype(BF16)
            vth_ref[s, p, :, WINDOW_ROWS:] = v_ref[rs, cs].T.astype(BF16)

    def unit(t):
        s, p = divmod(t, HEAD_PAIRS)
        return s, p, slice(s * CHUNK, (s + 1) * CHUNK), slice(p * LANES, (p + 1) * LANES)

    def probs(t):
        s, p, rs, cs = unit(t)
        return _attn_probs([q_ref[rs, cs]], kh_ref[s, :, cs], bias_ref[p], None)

    def apply(t, pr):
        s, p, rs, cs = unit(t)
        (o,) = _attn_apply(pr, [ag_ref[rs, cs]], vth_ref[s, p])
        o_ref[rs, cs] = o.astype(o_ref.dtype)

    _attn_pipeline(ATTN_SAMPLE_SEQS * HEAD_PAIRS, probs, apply)


def _attn_sample(p2d, cache_kt, cache_vt, bias_single, batch):
    ns = ATTN_SAMPLE_SEQS

    def col(c):
        return pl.BlockSpec((ns * CHUNK, W_A), lambda b: (b, c))

    cache = pl.BlockSpec((ns, W_A, WINDOW_ROWS), lambda b: (b, 0, 0))
    return pl.pallas_call(
        _attn_sample_kernel,
        out_shape=jax.ShapeDtypeStruct((batch * CHUNK, W_A), BF16),
        grid=(batch // ns,),
        in_specs=[col(0), col(1), col(2), col(3), cache, cache,
                  pl.BlockSpec((HEAD_PAIRS, BAND, LANES), lambda b: (0, 0, 0))],
        out_specs=pl.BlockSpec((ns * CHUNK, W_A), lambda b: (b, 0)),
        scratch_shapes=[pltpu.VMEM((ns, BAND, W_A), BF16),
                        pltpu.VMEM((ns, HEAD_PAIRS, LANES, BAND), BF16)],
        compiler_params=_params(("arbitrary",)),
        name="attn_sample",
    )(p2d, p2d, p2d, p2d, cache_kt, cache_vt, bias_single)


HGRN_GROUP_BLOCKS = 4
HGRN_GROUP = HGRN_GROUP_BLOCKS * GLA_BLOCK
HGRN_PIPE_DEPTH = 4
OUT_ROWS = 256


SUBLANES = 8


def _block_cumsum(x):
    n, w = x.shape
    x = x.reshape(n // SUBLANES, SUBLANES, w)
    row = lax.broadcasted_iota(jnp.int32, x.shape, 1)
    s = 1
    while s < SUBLANES:
        x = x + jnp.where(row >= s, pltpu.roll(x, s, 1), 0.0)
        s *= 2
    x = x.reshape(n // GLA_BLOCK, GLA_BLOCK // SUBLANES, SUBLANES, w)
    carry = jnp.broadcast_to(x[:, 0:1, SUBLANES - 1:SUBLANES, :], (n // GLA_BLOCK, 1, SUBLANES, w))
    x = jnp.concatenate([x[:, 0:1], x[:, 1:2] + carry], axis=1)
    return x.reshape(n, w)


def _hgrn_kernel(hq_ref, hf_ref, hi_ref, hg_ref, lb_ref, ng_ref, s0_ref,
                 oa_ref, x_ref, w_ref, gf_ref,
                 y_ref, sout_ref, st_ref, qt_ref, kt_ref, kd_ref, v_ref,
                 qtf_ref, kdf_ref, bl_ref, ob_ref):
    i = pl.program_id(1)
    nseq, seq_rows = hq_ref.shape[0], hq_ref.shape[1]
    rows = nseq * seq_rows
    nb = rows // GLA_BLOCK

    def out_rows(c):
        r0 = c * OUT_ROWS
        if seq_rows >= OUT_ROWS:
            s, off = divmod(r0, seq_rows)
            return (s, slice(off, off + OUT_ROWS))
        n = OUT_ROWS // seq_rows
        return (slice(r0 // seq_rows, r0 // seq_rows + n),)

    def out_read(ref, c):
        return ref[out_rows(c)].reshape(OUT_ROWS, ref.shape[-1])

    def out_write(ref, c, val):
        if seq_rows < OUT_ROWS:
            val = val.reshape(OUT_ROWS // seq_rows, seq_rows, ref.shape[-1])
        ref[out_rows(c)] = val

    @pl.when(i == 0)
    def _():
        for s in range(nseq):
            for h in range(H_B):
                st_ref[s, h] = s0_ref[s, h].T

    def flat(ref):
        return ref[...].reshape(rows, W_B)

    lb = lb_ref[...]
    f = lb + (1.0 - lb) * _sigmoid(flat(hf_ref))
    b = _block_cumsum(jnp.log(f) * LOG2E)
    b3 = b.reshape(nb, GLA_BLOCK, W_B)
    bl = jnp.broadcast_to(b3[:, GLA_BLOCK - 1:GLA_BLOCK, :],
                          (nb, GLA_BLOCK, W_B)).reshape(rows, W_B)
    kk = 1.0 - f
    qt = _silu(flat(hq_ref)) * jnp.exp2(b)
    kd = kk * jnp.exp2(bl - b)
    qtf_ref[...] = qt
    kdf_ref[...] = kd
    bl_ref[...] = bl
    qt_ref[...] = qt.astype(BF16)
    kt_ref[...] = (kk * jnp.exp2(-b)).astype(BF16)
    kd_ref[...] = kd.astype(BF16)
    v_ref[...] = flat(hi_ref).astype(BF16)

    def begin(c):
        out_write(y_ref, c, out_read(x_ref, c)
                  + jnp.dot(out_read(oa_ref, c), w_ref[0:W_A, :], preferred_element_type=F32))

    r_i = lax.broadcasted_iota(jnp.int32, (HGRN_GROUP, HGRN_GROUP), 0)
    c_i = lax.broadcasted_iota(jnp.int32, (HGRN_GROUP, HGRN_GROUP), 1)
    rb, cb = r_i // GLA_BLOCK, c_i // GLA_BLOCK
    m_diag = (rb == cb) & (r_i >= c_i)
    m_adj = rb == cb + 1
    m_far = rb >= cb + 2

    groups = seq_rows // HGRN_GROUP
    st = {(s, h): st_ref[s, h] for s in range(nseq) for h in range(H_B)}

    def unit(t):
        sg, h = divmod(t, H_B)
        s, g = divmod(sg, groups)
        r0 = s * seq_rows + g * HGRN_GROUP
        return s, g, h, r0, slice(r0, r0 + HGRN_GROUP), slice(h * DK_B, (h + 1) * DK_B)

    def block_decays(r0, cs):
        return [bl_ref[r0 + j * GLA_BLOCK:r0 + j * GLA_BLOCK + 1, cs]
                for j in range(HGRN_GROUP_BLOCKS)]

    def scaled(x, log2_scales):
        parts = []
        for j, sc in enumerate(log2_scales):
            xb = x[j * GLA_BLOCK:(j + 1) * GLA_BLOCK]
            parts.append(xb if sc is None else xb * jnp.exp2(sc))
        return jnp.concatenate(parts, axis=0).astype(BF16)

    def local(t):
        s, g, h, r0, rs, cs = unit(t)
        if h == 0 and r0 % OUT_ROWS == 0:
            begin(r0 // OUT_ROWS)
        d0, d1, d2, d3 = block_decays(r0, cs)
        qt_g = qt_ref[rs, cs]
        kdf = kdf_ref[rs, cs]
        q1 = scaled(qtf_ref[rs, cs], [None, None, None, d2])
        k1 = scaled(kdf, [d1, None, None, None])
        a = jnp.where(m_diag, _nt(qt_g, kt_ref[rs, cs]),
                      jnp.where(m_adj, _nt(qt_g, kd_ref[rs, cs]),
                                jnp.where(m_far, _nt(q1, k1), 0.0)))
        kdp = scaled(kdf, [(d1 + d2) + d3, d2 + d3, d3, None])
        ut = lax.dot_general(v_ref[rs, cs], kdp, (((0,), (0,)), ((), ())),
                             preferred_element_type=F32)
        return a.astype(BF16), ut

    def finish(c):
        hrow = out_read(y_ref, c) + jnp.dot(ob_ref[c * OUT_ROWS:(c + 1) * OUT_ROWS, :],
                                             w_ref[W_A:, :], preferred_element_type=F32)
        ms = jnp.mean(hrow * hrow, axis=-1, keepdims=True)
        out_write(y_ref, c, hrow * lax.rsqrt(ms + EPS) * gf_ref[...])

    def carry(t, loc):
        s, g, h, r0, rs, cs = unit(t)
        a, ut = loc
        d0, d1, d2, d3 = block_decays(r0, cs)
        qs = scaled(qtf_ref[rs, cs], [None, d0, d0 + d1, (d0 + d1) + d2])
        o = (_nt(qs, st[s, h].astype(BF16))
             + jnp.dot(a, v_ref[rs, cs], preferred_element_type=F32))
        dec = jnp.exp2(((d0 + d1) + d2) + d3)
        st[s, h] = st[s, h] * dec + ut
        y = o * lax.rsqrt(jnp.mean(o * o, axis=-1, keepdims=True) + EPS) * ng_ref[:, cs]
        gs = slice(g * HGRN_GROUP, (g + 1) * HGRN_GROUP)
        ob_ref[rs, cs] = (y * _silu(hg_ref[s, gs, cs])).astype(BF16)
        if h == H_B - 1 and (r0 + HGRN_GROUP) % OUT_ROWS == 0:
            finish((r0 + HGRN_GROUP) // OUT_ROWS - 1)

    _emit_ahead(nseq * groups * H_B, HGRN_PIPE_DEPTH, local, carry)
    for (s, h), val in st.items():
        st_ref[s, h] = val

    @pl.when(i == pl.num_programs(1) - 1)
    def _():
        for s in range(nseq):
            for h in range(H_B):
                sout_ref[s, h] = st_ref[s, h].T


def _hgrn(p3d, lb, ng, s0, oa3d, x3d, w_bf16, gf, nseq, rows, name):
    batch, seq, _ = p3d.shape
    assert batch % nseq == 0 and seq % rows == 0 and rows % HGRN_GROUP == 0
    assert (nseq * rows) % OUT_ROWS == 0 and (rows % OUT_ROWS == 0 or OUT_ROWS % rows == 0)

    def col(c):
        return pl.BlockSpec((nseq, rows, W_B), lambda b, i: (b, i, c))

    vec = pl.BlockSpec((1, W_B), lambda b, i: (0, 0))
    state = pl.BlockSpec((nseq, H_B, DK_B, DV_B), lambda b, i: (b, 0, 0, 0))
    wide = pl.BlockSpec((nseq, rows, D_MODEL), lambda b, i: (b, i, 0))
    n = nseq * rows
    return pl.pallas_call(
        _hgrn_kernel,
        out_shape=(jax.ShapeDtypeStruct((batch, seq, D_MODEL), F32),
                   jax.ShapeDtypeStruct((batch, H_B, DK_B, DV_B), F32)),
        grid=(batch // nseq, seq // rows),
        in_specs=[col(4), col(5), col(6), col(7), vec, vec, state,
                  pl.BlockSpec((nseq, rows, W_A), lambda b, i: (b, i, 0)), wide,
                  pl.BlockSpec((W_A + W_B, D_MODEL), lambda b, i: (0, 0)),
                  pl.BlockSpec((1, D_MODEL), lambda b, i: (0, 0))],
        out_specs=(wide, state),
        scratch_shapes=([pltpu.VMEM((nseq, H_B, DV_B, DK_B), F32)]
                        + [pltpu.VMEM((n, W_B), BF16)] * 4
                        + [pltpu.VMEM((n, W_B), F32)] * 3
                        + [pltpu.VMEM((n, W_B), BF16)]),
        compiler_params=_params(("arbitrary", "arbitrary")),
        name=name,
    )(p3d, p3d, p3d, p3d, lb, ng, s0, oa3d, x3d, w_bf16, gf)


HGRN_PROMPT_ROWS = 512
HGRN_PROMPT_SEQS = 2
HGRN_SAMPLE_SEQS = 4


def kernel(x_prompt, x_sample, cache_attn_k, cache_attn_v, state_hgrn, ln_in_g, w_in,
           rel_bias, lb_gamma, hg_norm_g, w_out, ln_f_g):
    batch, seq, _ = x_prompt.shape
    dec_batch, dec_seq, _ = x_sample.shape
    depth = w_in.shape[0]
    assert depth == 1 and dec_seq == CHUNK and PAST_LEN % CHUNK == 0
    assert cache_attn_k.shape[2] == WINDOW_ROWS

    lb_all = jnp.cumsum(jax.nn.softmax(lb_gamma.astype(F32), axis=0), axis=0)
    lb = lb_all[0].reshape(1, W_B)
    ng = hg_norm_g[0].reshape(1, W_B)
    g_in = ln_in_g[0].reshape(1, D_MODEL)
    g_f = ln_f_g.reshape(1, D_MODEL)
    w_out_b = w_out[0].astype(BF16)

    xp = x_prompt.reshape(batch * seq, D_MODEL)
    xs = x_sample.reshape(dec_batch * dec_seq, D_MODEL)

    bias_pair, bias_single = _bias_tables(rel_bias[0])

    pp, ps, k_s, v_s = _inproj(xp, xs, g_in, w_in[0])

    oa_p = _attn_prompt(pp, bias_pair, batch, seq)
    ckt = jnp.transpose(cache_attn_k[0], (0, 2, 3, 1)).reshape(dec_batch, W_A, WINDOW_ROWS)
    cvt = jnp.transpose(cache_attn_v[0], (0, 2, 3, 1)).reshape(dec_batch, W_A, WINDOW_ROWS)
    oa_s = _attn_sample(ps, ckt, cvt, bias_single, dec_batch)

    s0_p = jnp.zeros((batch, H_B, DK_B, DV_B), F32)
    y_p, st_p = _hgrn(pp.reshape(batch, seq, N_IN), lb, ng, s0_p,
                      oa_p.reshape(batch, seq, W_A), x_prompt, w_out_b, g_f,
                      HGRN_PROMPT_SEQS, HGRN_PROMPT_ROWS, "hgrn_out_prompt")
    y_s, st_s = _hgrn(ps.reshape(dec_batch, dec_seq, N_IN), lb, ng, state_hgrn[0],
                      oa_s.reshape(dec_batch, dec_seq, W_A), x_sample, w_out_b, g_f,
                      HGRN_SAMPLE_SEQS, dec_seq, "hgrn_out_sample")

    rows_p = min(WINDOW_ROWS, seq)
    pp3 = pp.reshape(batch, seq, N_IN)
    k_p = pp3[:, seq - rows_p:, W_A:2 * W_A].reshape(1, batch, rows_p, H_A, DH_A)
    v_p = pp3[:, seq - rows_p:, 2 * W_A:3 * W_A].reshape(1, batch, rows_p, H_A, DH_A)
    k_s = k_s.reshape(1, dec_batch, dec_seq, H_A, DH_A)
    v_s = v_s.reshape(1, dec_batch, dec_seq, H_A, DH_A)

    return (y_p, y_s, k_p, v_p, st_p[None], k_s, v_s, st_s[None])
```

```python
import functools

import jax
import jax.numpy as jnp
from jax import lax
from jax.experimental import pallas as pl
from jax.experimental.pallas import tpu as pltpu

F32 = jnp.float32
BF16 = jnp.bfloat16

D_MODEL = 1024
CHUNK = 64
LEFT_CHUNKS = 8
WINDOW_ROWS = LEFT_CHUNKS * CHUNK
BAND = WINDOW_ROWS + CHUNK
W_A = 512
H_A = 8
DH_A = 64
W_B = 512
H_B = 4
DK_B = 128
DV_B = 128
REL_CLIP = 128
GLA_BLOCK = 16
EPS = 1e-6
ATTN_SCALE = DH_A ** -0.5
NEG_INF = -1e30
N_IN = 4 * W_A + 4 * W_B
PAST_LEN = 2048

LANES = 128
HEAD_PAIRS = H_A // 2
REL_TABLE = 2 * REL_CLIP + 1
REL_TABLE_PAD = 384
TOEPLITZ_LEN = 640
VMEM_LIMIT = 56 * 1024 * 1024


def _params(semantics):
    return pltpu.CompilerParams(dimension_semantics=semantics,
                                vmem_limit_bytes=VMEM_LIMIT)


LOG2E = 1.4426950408889634


def _sigmoid(x):
    return 1.0 / (1.0 + jnp.exp2(x * -LOG2E))


def _silu(x):
    return x * _sigmoid(x)


def _nt(a, b):
    return lax.dot_general(a, b, (((1,), (1,)), ((), ())), preferred_element_type=F32)


def _emit_ahead(n_units, depth, first_fn, second_fn):
    pending = {}
    for t in range(n_units + depth):
        if t < n_units:
            pending[t] = first_fn(t)
        if t >= depth:
            second_fn(t - depth, pending.pop(t - depth))


INPROJ_ROWS = 512
INPROJ_SAMPLE_ROWS = 256
INPROJ_COLS = 512


def _inproj_kernel(xp_ref, xs_ref, g_ref, w_ref, pp_ref, ps_ref, ks_ref, vs_ref, *,
                   prompt_steps):
    i = pl.program_id(0)

    def project(x_ref, o_ref):
        x = x_ref[...]
        ms = jnp.mean(x * x, axis=-1, keepdims=True)
        xn = (x * lax.rsqrt(ms + EPS) * g_ref[...]).astype(BF16)
        for n0 in range(0, N_IN, INPROJ_COLS):
            o_ref[:, n0:n0 + INPROJ_COLS] = jnp.dot(
                xn, w_ref[:, n0:n0 + INPROJ_COLS].astype(BF16), preferred_element_type=F32)

    def split_heads(cols, out_ref):
        heads = jnp.stack([cols[:, h * DH_A:(h + 1) * DH_A] for h in range(H_A)], axis=0)
        out_ref[...] = jnp.swapaxes(heads, 0, 1)

    @pl.when(i < prompt_steps)
    def _():
        project(xp_ref, pp_ref)

    @pl.when(i >= prompt_steps)
    def _():
        project(xs_ref, ps_ref)
        split_heads(ps_ref[:, W_A:2 * W_A], ks_ref)
        split_heads(ps_ref[:, 2 * W_A:3 * W_A], vs_ref)


def _inproj(xp, xs, g, w):
    rows_p, rows_s = xp.shape[0], xs.shape[0]
    tp, ts = INPROJ_ROWS, INPROJ_SAMPLE_ROWS
    assert rows_p % tp == 0 and rows_s % ts == 0
    np_, ns = rows_p // tp, rows_s // ts

    def prompt_blk(i):
        return (jnp.minimum(i, np_ - 1), 0)

    def sample_blk(i):
        return (jnp.maximum(i - np_, 0), 0)

    def sample_blk3(i):
        return (jnp.maximum(i - np_, 0), 0, 0)

    return pl.pallas_call(
        functools.partial(_inproj_kernel, prompt_steps=np_),
        out_shape=(jax.ShapeDtypeStruct((rows_p, N_IN), F32),
                   jax.ShapeDtypeStruct((rows_s, N_IN), F32),
                   jax.ShapeDtypeStruct((rows_s, H_A, DH_A), F32),
                   jax.ShapeDtypeStruct((rows_s, H_A, DH_A), F32)),
        grid=(np_ + ns,),
        in_specs=[
            pl.BlockSpec((tp, D_MODEL), prompt_blk),
            pl.BlockSpec((ts, D_MODEL), sample_blk),
            pl.BlockSpec((1, D_MODEL), lambda i: (0, 0)),
            pl.BlockSpec((D_MODEL, N_IN), lambda i: (0, 0)),
        ],
        out_specs=(pl.BlockSpec((tp, N_IN), prompt_blk),
                   pl.BlockSpec((ts, N_IN), sample_blk),
                   pl.BlockSpec((ts, H_A, DH_A), sample_blk3),
                   pl.BlockSpec((ts, H_A, DH_A), sample_blk3)),
        compiler_params=_params(("arbitrary",)),
        name="inproj",
    )(xp, xs, g, w)


PAIR_BAND = BAND + CHUNK


def _bias_kernel(rb_ref, pair_ref, single_ref):
    m = lax.broadcasted_iota(jnp.int32, (REL_TABLE_PAD, TOEPLITZ_LEN), 1)
    t = lax.broadcasted_iota(jnp.int32, (REL_TABLE_PAD, TOEPLITZ_LEN), 0)
    idx = jnp.clip(BAND - 1 - m, -REL_CLIP, REL_CLIP) + REL_CLIP
    onehot = jnp.where(idx == t, 1.0, 0.0).astype(BF16)
    rb = rb_ref[...]
    hi = rb.astype(BF16)
    r1 = rb - hi.astype(F32)
    mid = r1.astype(BF16)
    lo = (r1 - mid.astype(F32)).astype(BF16)
    toep = (jnp.dot(hi, onehot, preferred_element_type=F32)
            + jnp.dot(mid, onehot, preferred_element_type=F32)
            + jnp.dot(lo, onehot, preferred_element_type=F32))
    key = lax.broadcasted_iota(jnp.int32, (CHUNK, TOEPLITZ_LEN), 1)
    for p in range(HEAD_PAIRS):
        for u in range(2):
            halves = []
            for h in (2 * p, 2 * p + 1):
                rows = jnp.broadcast_to(toep[h:h + 1, :], (CHUNK, TOEPLITZ_LEN))
                shift = (TOEPLITZ_LEN - (CHUNK - 1) + CHUNK * u) % TOEPLITZ_LEN
                rolled = pltpu.roll(rows, shift, 1, stride=1, stride_axis=0)
                seen = (key >= CHUNK * u) & (key < CHUNK * u + BAND)
                halves.append(jnp.where(seen, rolled * LOG2E, NEG_INF))
            tile_t = jnp.concatenate(halves, axis=0).T
            pair_ref[p, :, u * LANES:(u + 1) * LANES] = tile_t
            if u == 0:
                single_ref[p] = tile_t[:BAND, :]


def _bias_tables(rel_bias_l):
    rb = jnp.pad(rel_bias_l, ((0, 0), (0, REL_TABLE_PAD - REL_TABLE)))
    return pl.pallas_call(
        _bias_kernel,
        out_shape=(jax.ShapeDtypeStruct((HEAD_PAIRS, PAIR_BAND, 2 * LANES), F32),
                   jax.ShapeDtypeStruct((HEAD_PAIRS, BAND, LANES), F32)),
        name="rel_bias_table",
    )(rb)


def _attn_probs(qs, k2, bias_t, valid_from):
    first = lax.broadcasted_iota(jnp.int32, (CHUNK, LANES), 1) < DH_A
    parts = []
    for q in qs:
        q = q * (ATTN_SCALE * LOG2E)
        parts += [jnp.where(first, q, 0.0), jnp.where(first, 0.0, q)]
    qbd = jnp.concatenate(parts, axis=0).astype(BF16)
    s = _nt(k2, qbd) + bias_t
    if valid_from is not None:
        key = lax.broadcasted_iota(jnp.int32, s.shape, 0)
        s = jnp.where(key >= valid_from, s, NEG_INF)
    mx = jnp.max(s, axis=0, keepdims=True)
    e = jnp.exp2(s - mx)
    return e.astype(BF16), jnp.sum(e, axis=0, keepdims=True)


def _attn_apply(probs, ags, vt):
    e, denom = probs
    first = lax.broadcasted_iota(jnp.int32, (CHUNK, LANES), 1) < DH_A
    ot = jnp.dot(vt, e, preferred_element_type=F32)
    o2 = (ot * (1.0 / denom)).T
    outs = []
    for u, ag in enumerate(ags):
        r = 2 * CHUNK * u
        o = jnp.where(first, o2[r:r + CHUNK], o2[r + CHUNK:r + 2 * CHUNK])
        outs.append(o * _silu(ag))
    return outs


ATTN_PIPE_DEPTH = 3


def _attn_pipeline(n_units, probs_fn, apply_fn):
    _emit_ahead(n_units, ATTN_PIPE_DEPTH, probs_fn, apply_fn)


ATTN_ROWS = 1024


def _attn_prompt_kernel(q_ref, k_ref, v_ref, ag_ref, bias_ref, o_ref, kh_ref, vth_ref):
    i = pl.program_id(1)

    @pl.when(i == 0)
    def _():
        kh_ref[0:WINDOW_ROWS, :] = jnp.zeros((WINDOW_ROWS, W_A), BF16)
        vth_ref[:, :, 0:WINDOW_ROWS] = jnp.zeros((HEAD_PAIRS, LANES, WINDOW_ROWS), BF16)

    kh_ref[WINDOW_ROWS:, :] = k_ref[...].astype(BF16)
    for p in range(HEAD_PAIRS):
        vth_ref[p, :, WINDOW_ROWS:] = v_ref[:, p * LANES:(p + 1) * LANES].T.astype(BF16)

    def body(first_step):
        def unit(t):
            cp, p = divmod(t, HEAD_PAIRS)
            r0 = cp * 2 * CHUNK
            rs = [slice(r0 + u * CHUNK, r0 + (u + 1) * CHUNK) for u in range(2)]
            return cp, p, r0, rs, slice(p * LANES, (p + 1) * LANES)

        def probs(t):
            cp, p, r0, rs, cs = unit(t)
            valid_from = (LEFT_CHUNKS - 2 * cp) * CHUNK if first_step else None
            return _attn_probs([q_ref[r, cs] for r in rs], kh_ref[r0:r0 + PAIR_BAND, cs],
                               bias_ref[p], valid_from)

        def apply(t, pr):
            cp, p, r0, rs, cs = unit(t)
            outs = _attn_apply(pr, [ag_ref[r, cs] for r in rs],
                               vth_ref[p, :, r0:r0 + PAIR_BAND])
            for r, o in zip(rs, outs):
                o_ref[r, cs] = o.astype(o_ref.dtype)

        _attn_pipeline((ATTN_ROWS // (2 * CHUNK)) * HEAD_PAIRS, probs, apply)

    @pl.when(i == 0)
    def _():
        body(True)

    @pl.when(i != 0)
    def _():
        body(False)

    kh_ref[0:WINDOW_ROWS, :] = kh_ref[ATTN_ROWS:ATTN_ROWS + WINDOW_ROWS, :]
    vth_ref[:, :, 0:WINDOW_ROWS] = vth_ref[:, :, ATTN_ROWS:ATTN_ROWS + WINDOW_ROWS]


def _attn_prompt(p2d, bias_pair, batch, seq):
    nt = seq // ATTN_ROWS

    def col(c):
        return pl.BlockSpec((ATTN_ROWS, W_A), lambda b, i: (b * nt + i, c))

    return pl.pallas_call(
        _attn_prompt_kernel,
        out_shape=jax.ShapeDtypeStruct((batch * seq, W_A), BF16),
        grid=(batch, nt),
        in_specs=[col(0), col(1), col(2), col(3),
                  pl.BlockSpec((HEAD_PAIRS, PAIR_BAND, 2 * LANES), lambda b, i: (0, 0, 0))],
        out_specs=pl.BlockSpec((ATTN_ROWS, W_A), lambda b, i: (b * nt + i, 0)),
        scratch_shapes=[pltpu.VMEM((WINDOW_ROWS + ATTN_ROWS, W_A), BF16),
                        pltpu.VMEM((HEAD_PAIRS, LANES, WINDOW_ROWS + ATTN_ROWS), BF16)],
        compiler_params=_params(("arbitrary", "arbitrary")),
        name="attn_prompt",
    )(p2d, p2d, p2d, p2d, bias_pair)


ATTN_SAMPLE_SEQS = 2


def _attn_sample_kernel(q_ref, k_ref, v_ref, ag_ref, ckt_ref, cvt_ref, bias_ref, o_ref,
                        kh_ref, vth_ref):
    for s in range(ATTN_SAMPLE_SEQS):
        rs = slice(s * CHUNK, (s + 1) * CHUNK)
        kh_ref[s, WINDOW_ROWS:, :] = k_ref[rs, :].astype(BF16)
        for p in range(HEAD_PAIRS):
            cs = slice(p * LANES, (p + 1) * LANES)
            kh_ref[s, 0:WINDOW_ROWS, cs] = ckt_ref[s, cs, :].T.astype(BF16)
            vth_ref[s, p, :, 0:WINDOW_ROWS] = cvt_ref[s, cs, :].astype(BF16)
            vth_ref[s, p, :, WINDOW_ROWS:] = v_ref[rs, cs].T.astype(BF16)

    def unit(t):
        s, p = divmod(t, HEAD_PAIRS)
        return s, p, slice(s * CHUNK, (s + 1) * CHUNK), slice(p * LANES, (p + 1) * LANES)

    def probs(t):
        s, p, rs, cs = unit(t)
        return _attn_probs([q_ref[rs, cs]], kh_ref[s, :, cs], bias_ref[p], None)

    def apply(t, pr):
        s, p, rs, cs = unit(t)
        (o,) = _attn_apply(pr, [ag_ref[rs, cs]], vth_ref[s, p])
        o_ref[rs, cs] = o.astype(o_ref.dtype)

    _attn_pipeline(ATTN_SAMPLE_SEQS * HEAD_PAIRS, probs, apply)


def _attn_sample(p2d, cache_kt, cache_vt, bias_single, batch):
    ns = ATTN_SAMPLE_SEQS

    def col(c):
        return pl.BlockSpec((ns * CHUNK, W_A), lambda b: (b, c))

    cache = pl.BlockSpec((ns, W_A, WINDOW_ROWS), lambda b: (b, 0, 0))
    return pl.pallas_call(
        _attn_sample_kernel,
        out_shape=jax.ShapeDtypeStruct((batch * CHUNK, W_A), BF16),
        grid=(batch // ns,),
        in_specs=[col(0), col(1), col(2), col(3), cache, cache,
                  pl.BlockSpec((HEAD_PAIRS, BAND, LANES), lambda b: (0, 0, 0))],
        out_specs=pl.BlockSpec((ns * CHUNK, W_A), lambda b: (b, 0)),
        scratch_shapes=[pltpu.VMEM((ns, BAND, W_A), BF16),
                        pltpu.VMEM((ns, HEAD_PAIRS, LANES, BAND), BF16)],
        compiler_params=_params(("arbitrary",)),
        name="attn_sample",
    )(p2d, p2d, p2d, p2d, cache_kt, cache_vt, bias_single)


HGRN_GROUP_BLOCKS = 4
HGRN_GROUP = HGRN_GROUP_BLOCKS * GLA_BLOCK
HGRN_PIPE_DEPTH = 4
OUT_ROWS = 256


SUBLANES = 8


def _block_cumsum(x):
    n, w = x.shape
    x = x.reshape(n // SUBLANES, SUBLANES, w)
    row = lax.broadcasted_iota(jnp.int32, x.shape, 1)
    s = 1
    while s < SUBLANES:
        x = x + jnp.where(row >= s, pltpu.roll(x, s, 1), 0.0)
        s *= 2
    x = x.reshape(n // GLA_BLOCK, GLA_BLOCK // SUBLANES, SUBLANES, w)
    carry = jnp.broadcast_to(x[:, 0:1, SUBLANES - 1:SUBLANES, :], (n // GLA_BLOCK, 1, SUBLANES, w))
    x = jnp.concatenate([x[:, 0:1], x[:, 1:2] + carry], axis=1)
    return x.reshape(n, w)


def _hgrn_kernel(hq_ref, hf_ref, hi_ref, hg_ref, lb_ref, ng_ref, s0_ref,
                 oa_ref, x_ref, w_ref, gf_ref,
                 y_ref, sout_ref, st_ref, qt_ref, kt_ref, kd_ref, v_ref,
                 qtf_ref, kdf_ref, bl_ref, ob_ref, wb_ref):
    i = pl.program_id(1)

    @pl.when((pl.program_id(0) == 0) & (i == 0))
    def _():
        wb_ref[...] = w_ref[...].astype(BF16)

    nseq, seq_rows = hq_ref.shape[0], hq_ref.shape[1]
    rows = nseq * seq_rows
    nb = rows // GLA_BLOCK

    def out_rows(c):
        r0 = c * OUT_ROWS
        if seq_rows >= OUT_ROWS:
            s, off = divmod(r0, seq_rows)
            return (s, slice(off, off + OUT_ROWS))
        n = OUT_ROWS // seq_rows
        return (slice(r0 // seq_rows, r0 // seq_rows + n),)

    def out_read(ref, c):
        return ref[out_rows(c)].reshape(OUT_ROWS, ref.shape[-1])

    def out_write(ref, c, val):
        if seq_rows < OUT_ROWS:
            val = val.reshape(OUT_ROWS // seq_rows, seq_rows, ref.shape[-1])
        ref[out_rows(c)] = val

    @pl.when(i == 0)
    def _():
        for s in range(nseq):
            for h in range(H_B):
                st_ref[s, h] = s0_ref[s, h].T

    def flat(ref):
        return ref[...].reshape(rows, W_B)

    lb = lb_ref[...]
    f = lb + (1.0 - lb) * _sigmoid(flat(hf_ref))
    b = _block_cumsum(jnp.log(f) * LOG2E)
    b3 = b.reshape(nb, GLA_BLOCK, W_B)
    bl = jnp.broadcast_to(b3[:, GLA_BLOCK - 1:GLA_BLOCK, :],
                          (nb, GLA_BLOCK, W_B)).reshape(rows, W_B)
    kk = 1.0 - f
    qt = _silu(flat(hq_ref)) * jnp.exp2(b)
    kd = kk * jnp.exp2(bl - b)
    qtf_ref[...] = qt
    kdf_ref[...] = kd
    bl_ref[...] = bl
    qt_ref[...] = qt.astype(BF16)
    kt_ref[...] = (kk * jnp.exp2(-b)).astype(BF16)
    kd_ref[...] = kd.astype(BF16)
    v_ref[...] = flat(hi_ref).astype(BF16)

    def begin(c):
        out_write(y_ref, c, out_read(x_ref, c)
                  + jnp.dot(out_read(oa_ref, c), wb_ref[0:W_A, :], preferred_element_type=F32))

    r_i = lax.broadcasted_iota(jnp.int32, (HGRN_GROUP, HGRN_GROUP), 0)
    c_i = lax.broadcasted_iota(jnp.int32, (HGRN_GROUP, HGRN_GROUP), 1)
    rb, cb = r_i // GLA_BLOCK, c_i // GLA_BLOCK
    m_diag = (rb == cb) & (r_i >= c_i)
    m_adj = rb == cb + 1
    m_far = rb >= cb + 2

    groups = seq_rows // HGRN_GROUP
    st = {(s, h): st_ref[s, h] for s in range(nseq) for h in range(H_B)}

    def unit(t):
        sg, h = divmod(t, H_B)
        s, g = divmod(sg, groups)
        r0 = s * seq_rows + g * HGRN_GROUP
        return s, g, h, r0, slice(r0, r0 + HGRN_GROUP), slice(h * DK_B, (h + 1) * DK_B)

    def block_decays(r0, cs):
        return [bl_ref[r0 + j * GLA_BLOCK:r0 + j * GLA_BLOCK + 1, cs]
                for j in range(HGRN_GROUP_BLOCKS)]

    def scaled(x, log2_scales):
        parts = []
        for j, sc in enumerate(log2_scales):
            xb = x[j * GLA_BLOCK:(j + 1) * GLA_BLOCK]
            parts.append(xb if sc is None else xb * jnp.exp2(sc))
        return jnp.concatenate(parts, axis=0).astype(BF16)

    def local(t):
        s, g, h, r0, rs, cs = unit(t)
        if h == 0 and r0 % OUT_ROWS == 0:
            begin(r0 // OUT_ROWS)
        d0, d1, d2, d3 = block_decays(r0, cs)
        qt_g = qt_ref[rs, cs]
        kdf = kdf_ref[rs, cs]
        q1 = scaled(qtf_ref[rs, cs], [None, None, None, d2])
        k1 = scaled(kdf, [d1, None, None, None])
        a = jnp.where(m_diag, _nt(qt_g, kt_ref[rs, cs]),
                      jnp.where(m_adj, _nt(qt_g, kd_ref[rs, cs]),
                                jnp.where(m_far, _nt(q1, k1), 0.0)))
        kdp = scaled(kdf, [(d1 + d2) + d3, d2 + d3, d3, None])
        ut = lax.dot_general(v_ref[rs, cs], kdp, (((0,), (0,)), ((), ())),
                             preferred_element_type=F32)
        return a.astype(BF16), ut

    def finish(c):
        hrow = out_read(y_ref, c) + jnp.dot(ob_ref[c * OUT_ROWS:(c + 1) * OUT_ROWS, :],
                                             wb_ref[W_A:, :], preferred_element_type=F32)
        ms = jnp.mean(hrow * hrow, axis=-1, keepdims=True)
        out_write(y_ref, c, hrow * lax.rsqrt(ms + EPS) * gf_ref[...])

    def carry(t, loc):
        s, g, h, r0, rs, cs = unit(t)
        a, ut = loc
        d0, d1, d2, d3 = block_decays(r0, cs)
        qs = scaled(qtf_ref[rs, cs], [None, d0, d0 + d1, (d0 + d1) + d2])
        o = (_nt(qs, st[s, h].astype(BF16))
             + jnp.dot(a, v_ref[rs, cs], preferred_element_type=F32))
        dec = jnp.exp2(((d0 + d1) + d2) + d3)
        st[s, h] = st[s, h] * dec + ut
        y = o * lax.rsqrt(jnp.mean(o * o, axis=-1, keepdims=True) + EPS) * ng_ref[:, cs]
        gs = slice(g * HGRN_GROUP, (g + 1) * HGRN_GROUP)
        ob_ref[rs, cs] = (y * _silu(hg_ref[s, gs, cs])).astype(BF16)
        if h == H_B - 1 and (r0 + HGRN_GROUP) % OUT_ROWS == 0:
            finish((r0 + HGRN_GROUP) // OUT_ROWS - 1)

    _emit_ahead(nseq * groups * H_B, HGRN_PIPE_DEPTH, local, carry)
    for (s, h), val in st.items():
        st_ref[s, h] = val

    @pl.when(i == pl.num_programs(1) - 1)
    def _():
        for s in range(nseq):
            for h in range(H_B):
                sout_ref[s, h] = st_ref[s, h].T


def _hgrn(p3d, lb, ng, s0, oa3d, x3d, w_out, gf, nseq, rows, name):
    batch, seq, _ = p3d.shape
    assert batch % nseq == 0 and seq % rows == 0 and rows % HGRN_GROUP == 0
    assert (nseq * rows) % OUT_ROWS == 0 and (rows % OUT_ROWS == 0 or OUT_ROWS % rows == 0)

    def col(c):
        return pl.BlockSpec((nseq, rows, W_B), lambda b, i: (b, i, c))

    vec = pl.BlockSpec((1, W_B), lambda b, i: (0, 0))
    state = pl.BlockSpec((nseq, H_B, DK_B, DV_B), lambda b, i: (b, 0, 0, 0))
    wide = pl.BlockSpec((nseq, rows, D_MODEL), lambda b, i: (b, i, 0))
    n = nseq * rows
    return pl.pallas_call(
        _hgrn_kernel,
        out_shape=(jax.ShapeDtypeStruct((batch, seq, D_MODEL), F32),
                   jax.ShapeDtypeStruct((batch, H_B, DK_B, DV_B), F32)),
        grid=(batch // nseq, seq // rows),
        in_specs=[col(4), col(5), col(6), col(7), vec, vec, state,
                  pl.BlockSpec((nseq, rows, W_A), lambda b, i: (b, i, 0)), wide,
                  pl.BlockSpec((W_A + W_B, D_MODEL), lambda b, i: (0, 0)),
                  pl.BlockSpec((1, D_MODEL), lambda b, i: (0, 0))],
        out_specs=(wide, state),
        scratch_shapes=([pltpu.VMEM((nseq, H_B, DV_B, DK_B), F32)]
                        + [pltpu.VMEM((n, W_B), BF16)] * 4
                        + [pltpu.VMEM((n, W_B), F32)] * 3
                        + [pltpu.VMEM((n, W_B), BF16)]
                        + [pltpu.VMEM((W_A + W_B, D_MODEL), BF16)]),
        compiler_params=_params(("arbitrary", "arbitrary")),
        name=name,
    )(p3d, p3d, p3d, p3d, lb, ng, s0, oa3d, x3d, w_out, gf)


HGRN_PROMPT_ROWS = 512
HGRN_PROMPT_SEQS = 2
HGRN_SAMPLE_SEQS = 4


def kernel(x_prompt, x_sample, cache_attn_k, cache_attn_v, state_hgrn, ln_in_g, w_in,
           rel_bias, lb_gamma, hg_norm_g, w_out, ln_f_g):
    batch, seq, _ = x_prompt.shape
    dec_batch, dec_seq, _ = x_sample.shape
    depth = w_in.shape[0]
    assert depth == 1 and dec_seq == CHUNK and PAST_LEN % CHUNK == 0
    assert cache_attn_k.shape[2] == WINDOW_ROWS

    lb_all = jnp.cumsum(jax.nn.softmax(lb_gamma.astype(F32), axis=0), axis=0)
    lb = lb_all[0].reshape(1, W_B)
    ng = hg_norm_g[0].reshape(1, W_B)
    g_in = ln_in_g[0].reshape(1, D_MODEL)
    g_f = ln_f_g.reshape(1, D_MODEL)

    xp = x_prompt.reshape(batch * seq, D_MODEL)
    xs = x_sample.reshape(dec_batch * dec_seq, D_MODEL)

    bias_pair, bias_single = _bias_tables(rel_bias[0])

    pp, ps, k_s, v_s = _inproj(xp, xs, g_in, w_in[0])

    oa_p = _attn_prompt(pp, bias_pair, batch, seq)
    ckt = jnp.transpose(cache_attn_k[0], (0, 2, 3, 1)).reshape(dec_batch, W_A, WINDOW_ROWS)
    cvt = jnp.transpose(cache_attn_v[0], (0, 2, 3, 1)).reshape(dec_batch, W_A, WINDOW_ROWS)
    oa_s = _attn_sample(ps, ckt, cvt, bias_single, dec_batch)

    s0_p = jnp.zeros((batch, H_B, DK_B, DV_B), F32)
    y_p, st_p = _hgrn(pp.reshape(batch, seq, N_IN), lb, ng, s0_p,
                      oa_p.reshape(batch, seq, W_A), x_prompt, w_out[0], g_f,
                      HGRN_PROMPT_SEQS, HGRN_PROMPT_ROWS, "hgrn_out_prompt")
    y_s, st_s = _hgrn(ps.reshape(dec_batch, dec_seq, N_IN), lb, ng, state_hgrn[0],
                      oa_s.reshape(dec_batch, dec_seq, W_A), x_sample, w_out[0], g_f,
                      HGRN_SAMPLE_SEQS, dec_seq, "hgrn_out_sample")

    rows_p = min(WINDOW_ROWS, seq)
    pp3 = pp.reshape(batch, seq, N_IN)
    k_p = pp3[:, seq - rows_p:, W_A:2 * W_A].reshape(1, batch, rows_p, H_A, DH_A)
    v_p = pp3[:, seq - rows_p:, 2 * W_A:3 * W_A].reshape(1, batch, rows_p, H_A, DH_A)
    k_s = k_s.reshape(1, dec_batch, dec_seq, H_A, DH_A)
    v_s = v_s.reshape(1, dec_batch, dec_seq, H_A, DH_A)

    return (y_p, y_s, k_p, v_p, st_p[None], k_s, v_s, st_s[None])
```

```python
import functools

import jax
import jax.numpy as jnp
from jax import lax
from jax.experimental import pallas as pl
from jax.experimental.pallas import tpu as pltpu

F32 = jnp.float32
BF16 = jnp.bfloat16

D_MODEL = 1024
CHUNK = 64
LEFT_CHUNKS = 8
WINDOW_ROWS = LEFT_CHUNKS * CHUNK
BAND = WINDOW_ROWS + CHUNK
W_A = 512
H_A = 8
DH_A = 64
W_B = 512
H_B = 4
DK_B = 128
DV_B = 128
REL_CLIP = 128
GLA_BLOCK = 16
EPS = 1e-6
ATTN_SCALE = DH_A ** -0.5
NEG_INF = -1e30
N_IN = 4 * W_A + 4 * W_B
PAST_LEN = 2048

LANES = 128
HEAD_PAIRS = H_A // 2
REL_TABLE = 2 * REL_CLIP + 1
REL_TABLE_PAD = 384
TOEPLITZ_LEN = 640
VMEM_LIMIT = 56 * 1024 * 1024


def _params(semantics):
    return pltpu.CompilerParams(dimension_semantics=semantics,
                                vmem_limit_bytes=VMEM_LIMIT)


LOG2E = 1.4426950408889634


def _sigmoid(x):
    return 1.0 / (1.0 + jnp.exp2(x * -LOG2E))


def _silu(x):
    return x * _sigmoid(x)


def _nt(a, b):
    return lax.dot_general(a, b, (((1,), (1,)), ((), ())), preferred_element_type=F32)


def _emit_ahead(n_units, depth, first_fn, second_fn):
    pending = {}
    for t in range(n_units + depth):
        if t < n_units:
            pending[t] = first_fn(t)
        if t >= depth:
            second_fn(t - depth, pending.pop(t - depth))


INPROJ_ROWS = 512
INPROJ_SAMPLE_ROWS = 256
INPROJ_COLS = 512


def _inproj_kernel(xp_ref, xs_ref, g_ref, w_ref, pp_ref, ps_ref, ks_ref, vs_ref, *,
                   prompt_steps):
    i = pl.program_id(0)

    def project(x_ref, o_ref):
        x = x_ref[...]
        ms = jnp.mean(x * x, axis=-1, keepdims=True)
        xn = (x * lax.rsqrt(ms + EPS) * g_ref[...]).astype(BF16)
        for n0 in range(0, N_IN, INPROJ_COLS):
            o_ref[:, n0:n0 + INPROJ_COLS] = jnp.dot(
                xn, w_ref[:, n0:n0 + INPROJ_COLS].astype(BF16), preferred_element_type=F32)

    def split_heads(cols, out_ref):
        heads = jnp.stack([cols[:, h * DH_A:(h + 1) * DH_A] for h in range(H_A)], axis=0)
        out_ref[...] = jnp.swapaxes(heads, 0, 1)

    @pl.when(i < prompt_steps)
    def _():
        project(xp_ref, pp_ref)

    @pl.when(i >= prompt_steps)
    def _():
        project(xs_ref, ps_ref)
        split_heads(ps_ref[:, W_A:2 * W_A], ks_ref)
        split_heads(ps_ref[:, 2 * W_A:3 * W_A], vs_ref)


def _inproj(xp, xs, g, w):
    rows_p, rows_s = xp.shape[0], xs.shape[0]
    tp, ts = INPROJ_ROWS, INPROJ_SAMPLE_ROWS
    assert rows_p % tp == 0 and rows_s % ts == 0
    np_, ns = rows_p // tp, rows_s // ts

    def prompt_blk(i):
        return (jnp.minimum(i, np_ - 1), 0)

    def sample_blk(i):
        return (jnp.maximum(i - np_, 0), 0)

    def sample_blk3(i):
        return (jnp.maximum(i - np_, 0), 0, 0)

    return pl.pallas_call(
        functools.partial(_inproj_kernel, prompt_steps=np_),
        out_shape=(jax.ShapeDtypeStruct((rows_p, N_IN), F32),
                   jax.ShapeDtypeStruct((rows_s, N_IN), F32),
                   jax.ShapeDtypeStruct((rows_s, H_A, DH_A), F32),
                   jax.ShapeDtypeStruct((rows_s, H_A, DH_A), F32)),
        grid=(np_ + ns,),
        in_specs=[
            pl.BlockSpec((tp, D_MODEL), prompt_blk),
            pl.BlockSpec((ts, D_MODEL), sample_blk),
            pl.BlockSpec((1, D_MODEL), lambda i: (0, 0)),
            pl.BlockSpec((D_MODEL, N_IN), lambda i: (0, 0)),
        ],
        out_specs=(pl.BlockSpec((tp, N_IN), prompt_blk),
                   pl.BlockSpec((ts, N_IN), sample_blk),
                   pl.BlockSpec((ts, H_A, DH_A), sample_blk3),
                   pl.BlockSpec((ts, H_A, DH_A), sample_blk3)),
        compiler_params=_params(("arbitrary",)),
        name="inproj",
    )(xp, xs, g, w)


PAIR_BAND = BAND + CHUNK


def _bias_kernel(rb_ref, pair_ref, single_ref):
    m = lax.broadcasted_iota(jnp.int32, (REL_TABLE_PAD, TOEPLITZ_LEN), 1)
    t = lax.broadcasted_iota(jnp.int32, (REL_TABLE_PAD, TOEPLITZ_LEN), 0)
    idx = jnp.clip(BAND - 1 - m, -REL_CLIP, REL_CLIP) + REL_CLIP
    onehot = jnp.where(idx == t, 1.0, 0.0).astype(BF16)
    rb = rb_ref[...]
    hi = rb.astype(BF16)
    r1 = rb - hi.astype(F32)
    mid = r1.astype(BF16)
    lo = (r1 - mid.astype(F32)).astype(BF16)
    toep = (jnp.dot(hi, onehot, preferred_element_type=F32)
            + jnp.dot(mid, onehot, preferred_element_type=F32)
            + jnp.dot(lo, onehot, preferred_element_type=F32))
    key = lax.broadcasted_iota(jnp.int32, (CHUNK, TOEPLITZ_LEN), 1)
    for p in range(HEAD_PAIRS):
        for u in range(2):
            halves = []
            for h in (2 * p, 2 * p + 1):
                rows = jnp.broadcast_to(toep[h:h + 1, :], (CHUNK, TOEPLITZ_LEN))
                shift = (TOEPLITZ_LEN - (CHUNK - 1) + CHUNK * u) % TOEPLITZ_LEN
                rolled = pltpu.roll(rows, shift, 1, stride=1, stride_axis=0)
                seen = (key >= CHUNK * u) & (key < CHUNK * u + BAND)
                halves.append(jnp.where(seen, rolled * LOG2E, NEG_INF))
            tile_t = jnp.concatenate(halves, axis=0).T
            pair_ref[p, :, u * LANES:(u + 1) * LANES] = tile_t
            if u == 0:
                single_ref[p] = tile_t[:BAND, :]


def _bias_tables(rel_bias_l):
    rb = jnp.pad(rel_bias_l, ((0, 0), (0, REL_TABLE_PAD - REL_TABLE)))
    return pl.pallas_call(
        _bias_kernel,
        out_shape=(jax.ShapeDtypeStruct((HEAD_PAIRS, PAIR_BAND, 2 * LANES), F32),
                   jax.ShapeDtypeStruct((HEAD_PAIRS, BAND, LANES), F32)),
        name="rel_bias_table",
    )(rb)


def _attn_probs(qs, k2, bias_t, valid_from):
    first = lax.broadcasted_iota(jnp.int32, (CHUNK, LANES), 1) < DH_A
    parts = []
    for q in qs:
        q = q * (ATTN_SCALE * LOG2E)
        parts += [jnp.where(first, q, 0.0), jnp.where(first, 0.0, q)]
    qbd = jnp.concatenate(parts, axis=0).astype(BF16)
    s = _nt(k2, qbd) + bias_t
    if valid_from is not None:
        key = lax.broadcasted_iota(jnp.int32, s.shape, 0)
        s = jnp.where(key >= valid_from, s, NEG_INF)
    mx = jnp.max(s, axis=0, keepdims=True)
    e = jnp.exp2(s - mx)
    return e.astype(BF16), jnp.sum(e, axis=0, keepdims=True)


def _attn_apply(probs, ags, vt):
    e, denom = probs
    first = lax.broadcasted_iota(jnp.int32, (CHUNK, LANES), 1) < DH_A
    ot = jnp.dot(vt, e, preferred_element_type=F32)
    o2 = (ot * (1.0 / denom)).T
    outs = []
    for u, ag in enumerate(ags):
        r = 2 * CHUNK * u
        o = jnp.where(first, o2[r:r + CHUNK], o2[r + CHUNK:r + 2 * CHUNK])
        outs.append(o * _silu(ag))
    return outs


ATTN_PIPE_DEPTH = 3


def _attn_pipeline(n_units, probs_fn, apply_fn):
    _emit_ahead(n_units, ATTN_PIPE_DEPTH, probs_fn, apply_fn)


ATTN_ROWS = 1024


def _attn_prompt_kernel(q_ref, k_ref, v_ref, ag_ref, bias_ref, o_ref, kh_ref, vth_ref):
    i = pl.program_id(1)

    @pl.when(i == 0)
    def _():
        kh_ref[0:WINDOW_ROWS, :] = jnp.zeros((WINDOW_ROWS, W_A), BF16)
        vth_ref[:, :, 0:WINDOW_ROWS] = jnp.zeros((HEAD_PAIRS, LANES, WINDOW_ROWS), BF16)

    kh_ref[WINDOW_ROWS:, :] = k_ref[...].astype(BF16)
    for p in range(HEAD_PAIRS):
        vth_ref[p, :, WINDOW_ROWS:] = v_ref[:, p * LANES:(p + 1) * LANES].T.astype(BF16)

    def body(first_step):
        def unit(t):
            cp, p = divmod(t, HEAD_PAIRS)
            r0 = cp * 2 * CHUNK
            rs = [slice(r0 + u * CHUNK, r0 + (u + 1) * CHUNK) for u in range(2)]
            return cp, p, r0, rs, slice(p * LANES, (p + 1) * LANES)

        def probs(t):
            cp, p, r0, rs, cs = unit(t)
            valid_from = (LEFT_CHUNKS - 2 * cp) * CHUNK if first_step else None
            return _attn_probs([q_ref[r, cs] for r in rs], kh_ref[r0:r0 + PAIR_BAND, cs],
                               bias_ref[p], valid_from)

        def apply(t, pr):
            cp, p, r0, rs, cs = unit(t)
            outs = _attn_apply(pr, [ag_ref[r, cs] for r in rs],
                               vth_ref[p, :, r0:r0 + PAIR_BAND])
            for r, o in zip(rs, outs):
                o_ref[r, cs] = o.astype(o_ref.dtype)

        _attn_pipeline((ATTN_ROWS // (2 * CHUNK)) * HEAD_PAIRS, probs, apply)

    @pl.when(i == 0)
    def _():
        body(True)

    @pl.when(i != 0)
    def _():
        body(False)

    kh_ref[0:WINDOW_ROWS, :] = kh_ref[ATTN_ROWS:ATTN_ROWS + WINDOW_ROWS, :]
    vth_ref[:, :, 0:WINDOW_ROWS] = vth_ref[:, :, ATTN_ROWS:ATTN_ROWS + WINDOW_ROWS]


def _attn_prompt(p2d, bias_pair, batch, seq):
    nt = seq // ATTN_ROWS

    def col(c):
        return pl.BlockSpec((ATTN_ROWS, W_A), lambda b, i: (b * nt + i, c))

    return pl.pallas_call(
        _attn_prompt_kernel,
        out_shape=jax.ShapeDtypeStruct((batch * seq, W_A), BF16),
        grid=(batch, nt),
        in_specs=[col(0), col(1), col(2), col(3),
                  pl.BlockSpec((HEAD_PAIRS, PAIR_BAND, 2 * LANES), lambda b, i: (0, 0, 0))],
        out_specs=pl.BlockSpec((ATTN_ROWS, W_A), lambda b, i: (b * nt + i, 0)),
        scratch_shapes=[pltpu.VMEM((WINDOW_ROWS + ATTN_ROWS, W_A), BF16),
                        pltpu.VMEM((HEAD_PAIRS, LANES, WINDOW_ROWS + ATTN_ROWS), BF16)],
        compiler_params=_params(("arbitrary", "arbitrary")),
        name="attn_prompt",
    )(p2d, p2d, p2d, p2d, bias_pair)


ATTN_SAMPLE_SEQS = 4


def _attn_sample_kernel(q_ref, k_ref, v_ref, ag_ref, ckt_ref, cvt_ref, bias_ref, o_ref,
                        kh_ref, vth_ref):
    for s in range(ATTN_SAMPLE_SEQS):
        rs = slice(s * CHUNK, (s + 1) * CHUNK)
        kh_ref[s, WINDOW_ROWS:, :] = k_ref[rs, :].astype(BF16)
        for p in range(HEAD_PAIRS):
            cs = slice(p * LANES, (p + 1) * LANES)
            kh_ref[s, 0:WINDOW_ROWS, cs] = ckt_ref[s, cs, :].T.astype(BF16)
            vth_ref[s, p, :, 0:WINDOW_ROWS] = cvt_ref[s, cs, :].astype(BF16)
            vth_ref[s, p, :, WINDOW_ROWS:] = v_ref[rs, cs].T.astype(BF16)

    def unit(t):
        s, p = divmod(t, HEAD_PAIRS)
        return s, p, slice(s * CHUNK, (s + 1) * CHUNK), slice(p * LANES, (p + 1) * LANES)

    def probs(t):
        s, p, rs, cs = unit(t)
        return _attn_probs([q_ref[rs, cs]], kh_ref[s, :, cs], bias_ref[p], None)

    def apply(t, pr):
        s, p, rs, cs = unit(t)
        (o,) = _attn_apply(pr, [ag_ref[rs, cs]], vth_ref[s, p])
        o_ref[rs, cs] = o.astype(o_ref.dtype)

    _attn_pipeline(ATTN_SAMPLE_SEQS * HEAD_PAIRS, probs, apply)


def _attn_sample(p2d, cache_kt, cache_vt, bias_single, batch):
    ns = ATTN_SAMPLE_SEQS

    def col(c):
        return pl.BlockSpec((ns * CHUNK, W_A), lambda b: (b, c))

    cache = pl.BlockSpec((ns, W_A, WINDOW_ROWS), lambda b: (b, 0, 0))
    return pl.pallas_call(
        _attn_sample_kernel,
        out_shape=jax.ShapeDtypeStruct((batch * CHUNK, W_A), BF16),
        grid=(batch // ns,),
        in_specs=[col(0), col(1), col(2), col(3), cache, cache,
                  pl.BlockSpec((HEAD_PAIRS, BAND, LANES), lambda b: (0, 0, 0))],
        out_specs=pl.BlockSpec((ns * CHUNK, W_A), lambda b: (b, 0)),
        scratch_shapes=[pltpu.VMEM((ns, BAND, W_A), BF16),
                        pltpu.VMEM((ns, HEAD_PAIRS, LANES, BAND), BF16)],
        compiler_params=_params(("arbitrary",)),
        name="attn_sample",
    )(p2d, p2d, p2d, p2d, cache_kt, cache_vt, bias_single)


HGRN_GROUP_BLOCKS = 4
HGRN_GROUP = HGRN_GROUP_BLOCKS * GLA_BLOCK
HGRN_PIPE_DEPTH = 4
OUT_ROWS = 256


SUBLANES = 8


def _block_cumsum(x):
    n, w = x.shape
    x = x.reshape(n // SUBLANES, SUBLANES, w)
    row = lax.broadcasted_iota(jnp.int32, x.shape, 1)
    s = 1
    while s < SUBLANES:
        x = x + jnp.where(row >= s, pltpu.roll(x, s, 1), 0.0)
        s *= 2
    x = x.reshape(n // GLA_BLOCK, GLA_BLOCK // SUBLANES, SUBLANES, w)
    carry = jnp.broadcast_to(x[:, 0:1, SUBLANES - 1:SUBLANES, :], (n // GLA_BLOCK, 1, SUBLANES, w))
    x = jnp.concatenate([x[:, 0:1], x[:, 1:2] + carry], axis=1)
    return x.reshape(n, w)


def _hgrn_kernel(hq_ref, hf_ref, hi_ref, hg_ref, lb_ref, ng_ref, s0_ref,
                 oa_ref, x_ref, w_ref, gf_ref,
                 y_ref, sout_ref, st_ref, qt_ref, kt_ref, kd_ref, v_ref,
                 qtf_ref, kdf_ref, bl_ref, ob_ref):
    i = pl.program_id(1)
    nseq, seq_rows = hq_ref.shape[0], hq_ref.shape[1]
    rows = nseq * seq_rows
    nb = rows // GLA_BLOCK

    def out_rows(c):
        r0 = c * OUT_ROWS
        if seq_rows >= OUT_ROWS:
            s, off = divmod(r0, seq_rows)
            return (s, slice(off, off + OUT_ROWS))
        n = OUT_ROWS // seq_rows
        return (slice(r0 // seq_rows, r0 // seq_rows + n),)

    def out_read(ref, c):
        return ref[out_rows(c)].reshape(OUT_ROWS, ref.shape[-1])

    def out_write(ref, c, val):
        if seq_rows < OUT_ROWS:
            val = val.reshape(OUT_ROWS // seq_rows, seq_rows, ref.shape[-1])
        ref[out_rows(c)] = val

    @pl.when(i == 0)
    def _():
        for s in range(nseq):
            for h in range(H_B):
                st_ref[s, h] = s0_ref[s, h].T

    def flat(ref):
        return ref[...].reshape(rows, W_B)

    lb = lb_ref[...]
    f = lb + (1.0 - lb) * _sigmoid(flat(hf_ref))
    b = _block_cumsum(jnp.log(f) * LOG2E)
    b3 = b.reshape(nb, GLA_BLOCK, W_B)
    bl = jnp.broadcast_to(b3[:, GLA_BLOCK - 1:GLA_BLOCK, :],
                          (nb, GLA_BLOCK, W_B)).reshape(rows, W_B)
    kk = 1.0 - f
    qt = _silu(flat(hq_ref)) * jnp.exp2(b)
    kd = kk * jnp.exp2(bl - b)
    qtf_ref[...] = qt
    kdf_ref[...] = kd
    bl_ref[...] = bl
    qt_ref[...] = qt.astype(BF16)
    kt_ref[...] = (kk * jnp.exp2(-b)).astype(BF16)
    kd_ref[...] = kd.astype(BF16)
    v_ref[...] = flat(hi_ref).astype(BF16)

    def begin(c):
        out_write(y_ref, c, out_read(x_ref, c)
                  + jnp.dot(out_read(oa_ref, c), w_ref[0:W_A, :], preferred_element_type=F32))

    r_i = lax.broadcasted_iota(jnp.int32, (HGRN_GROUP, HGRN_GROUP), 0)
    c_i = lax.broadcasted_iota(jnp.int32, (HGRN_GROUP, HGRN_GROUP), 1)
    rb, cb = r_i // GLA_BLOCK, c_i // GLA_BLOCK
    m_diag = (rb == cb) & (r_i >= c_i)
    m_adj = rb == cb + 1
    m_far = rb >= cb + 2

    groups = seq_rows // HGRN_GROUP
    st = {(s, h): st_ref[s, h] for s in range(nseq) for h in range(H_B)}

    def unit(t):
        sg, h = divmod(t, H_B)
        g, s = divmod(sg, nseq)
        r0 = s * seq_rows + g * HGRN_GROUP
        return s, g, h, r0, slice(r0, r0 + HGRN_GROUP), slice(h * DK_B, (h + 1) * DK_B)

    def block_decays(r0, cs):
        return [bl_ref[r0 + j * GLA_BLOCK:r0 + j * GLA_BLOCK + 1, cs]
                for j in range(HGRN_GROUP_BLOCKS)]

    def scaled(x, log2_scales):
        parts = []
        for j, sc in enumerate(log2_scales):
            xb = x[j * GLA_BLOCK:(j + 1) * GLA_BLOCK]
            parts.append(xb if sc is None else xb * jnp.exp2(sc))
        return jnp.concatenate(parts, axis=0).astype(BF16)

    def local(t):
        s, g, h, r0, rs, cs = unit(t)
        if h == 0 and r0 % OUT_ROWS == 0:
            begin(r0 // OUT_ROWS)
        d0, d1, d2, d3 = block_decays(r0, cs)
        qt_g = qt_ref[rs, cs]
        kdf = kdf_ref[rs, cs]
        q1 = scaled(qtf_ref[rs, cs], [None, None, None, d2])
        k1 = scaled(kdf, [d1, None, None, None])
        a = jnp.where(m_diag, _nt(qt_g, kt_ref[rs, cs]),
                      jnp.where(m_adj, _nt(qt_g, kd_ref[rs, cs]),
                                jnp.where(m_far, _nt(q1, k1), 0.0)))
        kdp = scaled(kdf, [(d1 + d2) + d3, d2 + d3, d3, None])
        ut = lax.dot_general(v_ref[rs, cs], kdp, (((0,), (0,)), ((), ())),
                             preferred_element_type=F32)
        return a.astype(BF16), ut

    def finish(c):
        hrow = out_read(y_ref, c) + jnp.dot(ob_ref[c * OUT_ROWS:(c + 1) * OUT_ROWS, :],
                                             w_ref[W_A:, :], preferred_element_type=F32)
        ms = jnp.mean(hrow * hrow, axis=-1, keepdims=True)
        out_write(y_ref, c, hrow * lax.rsqrt(ms + EPS) * gf_ref[...])

    def carry(t, loc):
        s, g, h, r0, rs, cs = unit(t)
        a, ut = loc
        d0, d1, d2, d3 = block_decays(r0, cs)
        qs = scaled(qtf_ref[rs, cs], [None, d0, d0 + d1, (d0 + d1) + d2])
        o = (_nt(qs, st[s, h].astype(BF16))
             + jnp.dot(a, v_ref[rs, cs], preferred_element_type=F32))
        dec = jnp.exp2(((d0 + d1) + d2) + d3)
        st[s, h] = st[s, h] * dec + ut
        y = o * lax.rsqrt(jnp.mean(o * o, axis=-1, keepdims=True) + EPS) * ng_ref[:, cs]
        gs = slice(g * HGRN_GROUP, (g + 1) * HGRN_GROUP)
        ob_ref[rs, cs] = (y * _silu(hg_ref[s, gs, cs])).astype(BF16)
        if h == H_B - 1 and (r0 + HGRN_GROUP) % OUT_ROWS == 0:
            finish((r0 + HGRN_GROUP) // OUT_ROWS - 1)

    _emit_ahead(nseq * groups * H_B, HGRN_PIPE_DEPTH, local, carry)
    for (s, h), val in st.items():
        st_ref[s, h] = val

    @pl.when(i == pl.num_programs(1) - 1)
    def _():
        for s in range(nseq):
            for h in range(H_B):
                sout_ref[s, h] = st_ref[s, h].T


def _hgrn(p3d, lb, ng, s0, oa3d, x3d, w_bf16, gf, nseq, rows, name):
    batch, seq, _ = p3d.shape
    assert batch % nseq == 0 and seq % rows == 0 and rows % HGRN_GROUP == 0
    assert (nseq * rows) % OUT_ROWS == 0 and (rows % OUT_ROWS == 0 or OUT_ROWS % rows == 0)

    def col(c):
        return pl.BlockSpec((nseq, rows, W_B), lambda b, i: (b, i, c))

    vec = pl.BlockSpec((1, W_B), lambda b, i: (0, 0))
    state = pl.BlockSpec((nseq, H_B, DK_B, DV_B), lambda b, i: (b, 0, 0, 0))
    wide = pl.BlockSpec((nseq, rows, D_MODEL), lambda b, i: (b, i, 0))
    n = nseq * rows
    return pl.pallas_call(
        _hgrn_kernel,
        out_shape=(jax.ShapeDtypeStruct((batch, seq, D_MODEL), F32),
                   jax.ShapeDtypeStruct((batch, H_B, DK_B, DV_B), F32)),
        grid=(batch // nseq, seq // rows),
        in_specs=[col(4), col(5), col(6), col(7), vec, vec, state,
                  pl.BlockSpec((nseq, rows, W_A), lambda b, i: (b, i, 0)), wide,
                  pl.BlockSpec((W_A + W_B, D_MODEL), lambda b, i: (0, 0)),
                  pl.BlockSpec((1, D_MODEL), lambda b, i: (0, 0))],
        out_specs=(wide, state),
        scratch_shapes=([pltpu.VMEM((nseq, H_B, DV_B, DK_B), F32)]
                        + [pltpu.VMEM((n, W_B), BF16)] * 4
                        + [pltpu.VMEM((n, W_B), F32)] * 3
                        + [pltpu.VMEM((n, W_B), BF16)]),
        compiler_params=_params(("arbitrary", "arbitrary")),
        name=name,
    )(p3d, p3d, p3d, p3d, lb, ng, s0, oa3d, x3d, w_bf16, gf)


HGRN_PROMPT_ROWS = 512
HGRN_PROMPT_SEQS = 2
HGRN_SAMPLE_SEQS = 4


def kernel(x_prompt, x_sample, cache_attn_k, cache_attn_v, state_hgrn, ln_in_g, w_in,
           rel_bias, lb_gamma, hg_norm_g, w_out, ln_f_g):
    batch, seq, _ = x_prompt.shape
    dec_batch, dec_seq, _ = x_sample.shape
    depth = w_in.shape[0]
    assert depth == 1 and dec_seq == CHUNK and PAST_LEN % CHUNK == 0
    assert cache_attn_k.shape[2] == WINDOW_ROWS

    lb_all = jnp.cumsum(jax.nn.softmax(lb_gamma.astype(F32), axis=0), axis=0)
    lb = lb_all[0].reshape(1, W_B)
    ng = hg_norm_g[0].reshape(1, W_B)
    g_in = ln_in_g[0].reshape(1, D_MODEL)
    g_f = ln_f_g.reshape(1, D_MODEL)
    w_out_b = w_out[0].astype(BF16)

    xp = x_prompt.reshape(batch * seq, D_MODEL)
    xs = x_sample.reshape(dec_batch * dec_seq, D_MODEL)

    bias_pair, bias_single = _bias_tables(rel_bias[0])

    pp, ps, k_s, v_s = _inproj(xp, xs, g_in, w_in[0])

    oa_p = _attn_prompt(pp, bias_pair, batch, seq)
    ckt = jnp.transpose(cache_attn_k[0], (0, 2, 3, 1)).reshape(dec_batch, W_A, WINDOW_ROWS)
    cvt = jnp.transpose(cache_attn_v[0], (0, 2, 3, 1)).reshape(dec_batch, W_A, WINDOW_ROWS)
    oa_s = _attn_sample(ps, ckt, cvt, bias_single, dec_batch)

    s0_p = jnp.zeros((batch, H_B, DK_B, DV_B), F32)
    y_p, st_p = _hgrn(pp.reshape(batch, seq, N_IN), lb, ng, s0_p,
                      oa_p.reshape(batch, seq, W_A), x_prompt, w_out_b, g_f,
                      HGRN_PROMPT_SEQS, HGRN_PROMPT_ROWS, "hgrn_out_prompt")
    y_s, st_s = _hgrn(ps.reshape(dec_batch, dec_seq, N_IN), lb, ng, state_hgrn[0],
                      oa_s.reshape(dec_batch, dec_seq, W_A), x_sample, w_out_b, g_f,
                      HGRN_SAMPLE_SEQS, dec_seq, "hgrn_out_sample")

    rows_p = min(WINDOW_ROWS, seq)
    pp3 = pp.reshape(batch, seq, N_IN)
    k_p = pp3[:, seq - rows_p:, W_A:2 * W_A].reshape(1, batch, rows_p, H_A, DH_A)
    v_p = pp3[:, seq - rows_p:, 2 * W_A:3 * W_A].reshape(1, batch, rows_p, H_A, DH_A)
    k_s = k_s.reshape(1, dec_batch, dec_seq, H_A, DH_A)
    v_s = v_s.reshape(1, dec_batch, dec_seq, H_A, DH_A)

    return (y_p, y_s, k_p, v_p, st_p[None], k_s, v_s, st_s[None])
```

```python
import functools

import jax
import jax.numpy as jnp
from jax import lax
from jax.experimental import pallas as pl
from jax.experimental.pallas import tpu as pltpu

F32 = jnp.float32
BF16 = jnp.bfloat16

D_MODEL = 1024
CHUNK = 64
LEFT_CHUNKS = 8
WINDOW_ROWS = LEFT_CHUNKS * CHUNK
BAND = WINDOW_ROWS + CHUNK
W_A = 512
H_A = 8
DH_A = 64
W_B = 512
H_B = 4
DK_B = 128
DV_B = 128
REL_CLIP = 128
GLA_BLOCK = 16
EPS = 1e-6
ATTN_SCALE = DH_A ** -0.5
NEG_INF = -1e30
N_IN = 4 * W_A + 4 * W_B
PAST_LEN = 2048

LANES = 128
HEAD_PAIRS = H_A // 2
REL_TABLE = 2 * REL_CLIP + 1
REL_TABLE_PAD = 384
TOEPLITZ_LEN = 640
VMEM_LIMIT = 56 * 1024 * 1024


def _params(semantics):
    return pltpu.CompilerParams(dimension_semantics=semantics,
                                vmem_limit_bytes=VMEM_LIMIT)


LOG2E = 1.4426950408889634


def _sigmoid(x):
    return 1.0 / (1.0 + jnp.exp2(x * -LOG2E))


def _silu(x):
    return x * _sigmoid(x)


def _nt(a, b):
    return lax.dot_general(a, b, (((1,), (1,)), ((), ())), preferred_element_type=F32)


def _emit_ahead(n_units, depth, first_fn, second_fn):
    pending = {}
    for t in range(n_units + depth):
        if t < n_units:
            pending[t] = first_fn(t)
        if t >= depth:
            second_fn(t - depth, pending.pop(t - depth))


INPROJ_ROWS = 512
INPROJ_SAMPLE_ROWS = 256
INPROJ_COLS = 512


def _inproj_kernel(xp_ref, xs_ref, g_ref, w_ref, pp_ref, ps_ref, ks_ref, vs_ref, *,
                   prompt_steps):
    i = pl.program_id(0)

    def project(x_ref, o_ref):
        x = x_ref[...]
        ms = jnp.mean(x * x, axis=-1, keepdims=True)
        xn = (x * lax.rsqrt(ms + EPS) * g_ref[...]).astype(BF16)
        for n0 in range(0, N_IN, INPROJ_COLS):
            o_ref[:, n0:n0 + INPROJ_COLS] = jnp.dot(
                xn, w_ref[:, n0:n0 + INPROJ_COLS].astype(BF16), preferred_element_type=F32)

    def split_heads(cols, out_ref):
        heads = jnp.stack([cols[:, h * DH_A:(h + 1) * DH_A] for h in range(H_A)], axis=0)
        out_ref[...] = jnp.swapaxes(heads, 0, 1)

    @pl.when(i < prompt_steps)
    def _():
        project(xp_ref, pp_ref)

    @pl.when(i >= prompt_steps)
    def _():
        project(xs_ref, ps_ref)
        split_heads(ps_ref[:, W_A:2 * W_A], ks_ref)
        split_heads(ps_ref[:, 2 * W_A:3 * W_A], vs_ref)


def _inproj(xp, xs, g, w):
    rows_p, rows_s = xp.shape[0], xs.shape[0]
    tp, ts = INPROJ_ROWS, INPROJ_SAMPLE_ROWS
    assert rows_p % tp == 0 and rows_s % ts == 0
    np_, ns = rows_p // tp, rows_s // ts

    def prompt_blk(i):
        return (jnp.minimum(i, np_ - 1), 0)

    def sample_blk(i):
        return (jnp.maximum(i - np_, 0), 0)

    def sample_blk3(i):
        return (jnp.maximum(i - np_, 0), 0, 0)

    return pl.pallas_call(
        functools.partial(_inproj_kernel, prompt_steps=np_),
        out_shape=(jax.ShapeDtypeStruct((rows_p, N_IN), F32),
                   jax.ShapeDtypeStruct((rows_s, N_IN), F32),
                   jax.ShapeDtypeStruct((rows_s, H_A, DH_A), F32),
                   jax.ShapeDtypeStruct((rows_s, H_A, DH_A), F32)),
        grid=(np_ + ns,),
        in_specs=[
            pl.BlockSpec((tp, D_MODEL), prompt_blk),
            pl.BlockSpec((ts, D_MODEL), sample_blk),
            pl.BlockSpec((1, D_MODEL), lambda i: (0, 0)),
            pl.BlockSpec((D_MODEL, N_IN), lambda i: (0, 0)),
        ],
        out_specs=(pl.BlockSpec((tp, N_IN), prompt_blk),
                   pl.BlockSpec((ts, N_IN), sample_blk),
                   pl.BlockSpec((ts, H_A, DH_A), sample_blk3),
                   pl.BlockSpec((ts, H_A, DH_A), sample_blk3)),
        compiler_params=_params(("arbitrary",)),
        name="inproj",
    )(xp, xs, g, w)


PAIR_BAND = BAND + CHUNK


def _bias_kernel(rb_ref, pair_ref, single_ref):
    m = lax.broadcasted_iota(jnp.int32, (REL_TABLE_PAD, TOEPLITZ_LEN), 1)
    t = lax.broadcasted_iota(jnp.int32, (REL_TABLE_PAD, TOEPLITZ_LEN), 0)
    idx = jnp.clip(BAND - 1 - m, -REL_CLIP, REL_CLIP) + REL_CLIP
    onehot = jnp.where(idx == t, 1.0, 0.0).astype(BF16)
    rb = rb_ref[...]
    hi = rb.astype(BF16)
    r1 = rb - hi.astype(F32)
    mid = r1.astype(BF16)
    lo = (r1 - mid.astype(F32)).astype(BF16)
    toep = (jnp.dot(hi, onehot, preferred_element_type=F32)
            + jnp.dot(mid, onehot, preferred_element_type=F32)
            + jnp.dot(lo, onehot, preferred_element_type=F32))
    key = lax.broadcasted_iota(jnp.int32, (CHUNK, TOEPLITZ_LEN), 1)
    for p in range(HEAD_PAIRS):
        for u in range(2):
            halves = []
            for h in (2 * p, 2 * p + 1):
                rows = jnp.broadcast_to(toep[h:h + 1, :], (CHUNK, TOEPLITZ_LEN))
                shift = (TOEPLITZ_LEN - (CHUNK - 1) + CHUNK * u) % TOEPLITZ_LEN
                rolled = pltpu.roll(rows, shift, 1, stride=1, stride_axis=0)
                seen = (key >= CHUNK * u) & (key < CHUNK * u + BAND)
                halves.append(jnp.where(seen, rolled * LOG2E, NEG_INF))
            tile_t = jnp.concatenate(halves, axis=0).T
            pair_ref[p, :, u * LANES:(u + 1) * LANES] = tile_t
            if u == 0:
                single_ref[p] = tile_t[:BAND, :]


def _bias_tables(rel_bias_l):
    rb = jnp.pad(rel_bias_l, ((0, 0), (0, REL_TABLE_PAD - REL_TABLE)))
    return pl.pallas_call(
        _bias_kernel,
        out_shape=(jax.ShapeDtypeStruct((HEAD_PAIRS, PAIR_BAND, 2 * LANES), F32),
                   jax.ShapeDtypeStruct((HEAD_PAIRS, BAND, LANES), F32)),
        name="rel_bias_table",
    )(rb)


def _attn_probs(qs, k2, bias_t, valid_from):
    first = lax.broadcasted_iota(jnp.int32, (CHUNK, LANES), 1) < DH_A
    parts = []
    for q in qs:
        q = q * (ATTN_SCALE * LOG2E)
        parts += [jnp.where(first, q, 0.0), jnp.where(first, 0.0, q)]
    qbd = jnp.concatenate(parts, axis=0).astype(BF16)
    s = _nt(k2, qbd) + bias_t
    if valid_from is not None:
        key = lax.broadcasted_iota(jnp.int32, s.shape, 0)
        s = jnp.where(key >= valid_from, s, NEG_INF)
    mx = jnp.max(s, axis=0, keepdims=True)
    e = jnp.exp2(s - mx)
    return e.astype(BF16), jnp.sum(e, axis=0, keepdims=True)


def _attn_apply(probs, ags, vt):
    e, denom = probs
    first = lax.broadcasted_iota(jnp.int32, (CHUNK, LANES), 1) < DH_A
    ot = jnp.dot(vt, e, preferred_element_type=F32)
    o2 = (ot * (1.0 / denom)).T
    outs = []
    for u, ag in enumerate(ags):
        r = 2 * CHUNK * u
        o = jnp.where(first, o2[r:r + CHUNK], o2[r + CHUNK:r + 2 * CHUNK])
        outs.append(o * _silu(ag))
    return outs


ATTN_PIPE_DEPTH = 3


def _attn_pipeline(n_units, probs_fn, apply_fn):
    _emit_ahead(n_units, ATTN_PIPE_DEPTH, probs_fn, apply_fn)


ATTN_ROWS = 1024


def _attn_prompt_kernel(q_ref, k_ref, v_ref, ag_ref, bias_ref, o_ref, kt_ref, vt_ref,
                        kh_ref, vth_ref):
    i = pl.program_id(1)

    @pl.when(i == pl.num_programs(1) - 1)
    def _():
        tail = slice(ATTN_ROWS - WINDOW_ROWS, ATTN_ROWS)
        for p in range(HEAD_PAIRS):
            cs = slice(p * LANES, (p + 1) * LANES)
            kt_ref[0, cs, :] = k_ref[tail, cs].T
            vt_ref[0, cs, :] = v_ref[tail, cs].T

    @pl.when(i == 0)
    def _():
        kh_ref[0:WINDOW_ROWS, :] = jnp.zeros((WINDOW_ROWS, W_A), BF16)
        vth_ref[:, :, 0:WINDOW_ROWS] = jnp.zeros((HEAD_PAIRS, LANES, WINDOW_ROWS), BF16)

    kh_ref[WINDOW_ROWS:, :] = k_ref[...].astype(BF16)
    for p in range(HEAD_PAIRS):
        vth_ref[p, :, WINDOW_ROWS:] = v_ref[:, p * LANES:(p + 1) * LANES].T.astype(BF16)

    def body(first_step):
        def unit(t):
            cp, p = divmod(t, HEAD_PAIRS)
            r0 = cp * 2 * CHUNK
            rs = [slice(r0 + u * CHUNK, r0 + (u + 1) * CHUNK) for u in range(2)]
            return cp, p, r0, rs, slice(p * LANES, (p + 1) * LANES)

        def probs(t):
            cp, p, r0, rs, cs = unit(t)
            valid_from = (LEFT_CHUNKS - 2 * cp) * CHUNK if first_step else None
            return _attn_probs([q_ref[r, cs] for r in rs], kh_ref[r0:r0 + PAIR_BAND, cs],
                               bias_ref[p], valid_from)

        def apply(t, pr):
            cp, p, r0, rs, cs = unit(t)
            outs = _attn_apply(pr, [ag_ref[r, cs] for r in rs],
                               vth_ref[p, :, r0:r0 + PAIR_BAND])
            for r, o in zip(rs, outs):
                o_ref[r, cs] = o.astype(o_ref.dtype)

        _attn_pipeline((ATTN_ROWS // (2 * CHUNK)) * HEAD_PAIRS, probs, apply)

    @pl.when(i == 0)
    def _():
        body(True)

    @pl.when(i != 0)
    def _():
        body(False)

    kh_ref[0:WINDOW_ROWS, :] = kh_ref[ATTN_ROWS:ATTN_ROWS + WINDOW_ROWS, :]
    vth_ref[:, :, 0:WINDOW_ROWS] = vth_ref[:, :, ATTN_ROWS:ATTN_ROWS + WINDOW_ROWS]


def _attn_prompt(p2d, bias_pair, batch, seq):
    assert seq % ATTN_ROWS == 0
    nt = seq // ATTN_ROWS

    def col(c):
        return pl.BlockSpec((ATTN_ROWS, W_A), lambda b, i: (b * nt + i, c))

    tail_t = pl.BlockSpec((1, W_A, WINDOW_ROWS), lambda b, i: (b, 0, 0))
    return pl.pallas_call(
        _attn_prompt_kernel,
        out_shape=(jax.ShapeDtypeStruct((batch * seq, W_A), BF16),
                   jax.ShapeDtypeStruct((batch, W_A, WINDOW_ROWS), F32),
                   jax.ShapeDtypeStruct((batch, W_A, WINDOW_ROWS), F32)),
        grid=(batch, nt),
        in_specs=[col(0), col(1), col(2), col(3),
                  pl.BlockSpec((HEAD_PAIRS, PAIR_BAND, 2 * LANES), lambda b, i: (0, 0, 0))],
        out_specs=(pl.BlockSpec((ATTN_ROWS, W_A), lambda b, i: (b * nt + i, 0)),
                   tail_t, tail_t),
        scratch_shapes=[pltpu.VMEM((WINDOW_ROWS + ATTN_ROWS, W_A), BF16),
                        pltpu.VMEM((HEAD_PAIRS, LANES, WINDOW_ROWS + ATTN_ROWS), BF16)],
        compiler_params=_params(("arbitrary", "arbitrary")),
        name="attn_prompt",
    )(p2d, p2d, p2d, p2d, bias_pair)


ATTN_SAMPLE_SEQS = 4


def _attn_sample_kernel(q_ref, k_ref, v_ref, ag_ref, ckt_ref, cvt_ref, bias_ref, o_ref,
                        kh_ref, vth_ref):
    for s in range(ATTN_SAMPLE_SEQS):
        rs = slice(s * CHUNK, (s + 1) * CHUNK)
        kh_ref[s, WINDOW_ROWS:, :] = k_ref[rs, :].astype(BF16)
        for p in range(HEAD_PAIRS):
            cs = slice(p * LANES, (p + 1) * LANES)
            kh_ref[s, 0:WINDOW_ROWS, cs] = ckt_ref[s, cs, :].T.astype(BF16)
            vth_ref[s, p, :, 0:WINDOW_ROWS] = cvt_ref[s, cs, :].astype(BF16)
            vth_ref[s, p, :, WINDOW_ROWS:] = v_ref[rs, cs].T.astype(BF16)

    def unit(t):
        s, p = divmod(t, HEAD_PAIRS)
        return s, p, slice(s * CHUNK, (s + 1) * CHUNK), slice(p * LANES, (p + 1) * LANES)

    def probs(t):
        s, p, rs, cs = unit(t)
        return _attn_probs([q_ref[rs, cs]], kh_ref[s, :, cs], bias_ref[p], None)

    def apply(t, pr):
        s, p, rs, cs = unit(t)
        (o,) = _attn_apply(pr, [ag_ref[rs, cs]], vth_ref[s, p])
        o_ref[rs, cs] = o.astype(o_ref.dtype)

    _attn_pipeline(ATTN_SAMPLE_SEQS * HEAD_PAIRS, probs, apply)


def _attn_sample(p2d, cache_kt, cache_vt, bias_single, batch):
    ns = ATTN_SAMPLE_SEQS

    def col(c):
        return pl.BlockSpec((ns * CHUNK, W_A), lambda b: (b, c))

    cache = pl.BlockSpec((ns, W_A, WINDOW_ROWS), lambda b: (b, 0, 0))
    return pl.pallas_call(
        _attn_sample_kernel,
        out_shape=jax.ShapeDtypeStruct((batch * CHUNK, W_A), BF16),
        grid=(batch // ns,),
        in_specs=[col(0), col(1), col(2), col(3), cache, cache,
                  pl.BlockSpec((HEAD_PAIRS, BAND, LANES), lambda b: (0, 0, 0))],
        out_specs=pl.BlockSpec((ns * CHUNK, W_A), lambda b: (b, 0)),
        scratch_shapes=[pltpu.VMEM((ns, BAND, W_A), BF16),
                        pltpu.VMEM((ns, HEAD_PAIRS, LANES, BAND), BF16)],
        compiler_params=_params(("arbitrary",)),
        name="attn_sample",
    )(p2d, p2d, p2d, p2d, cache_kt, cache_vt, bias_single)


HGRN_GROUP_BLOCKS = 4
HGRN_GROUP = HGRN_GROUP_BLOCKS * GLA_BLOCK
HGRN_PIPE_DEPTH = 4
OUT_ROWS = 256


SUBLANES = 8


def _block_cumsum(x):
    n, w = x.shape
    x = x.reshape(n // SUBLANES, SUBLANES, w)
    row = lax.broadcasted_iota(jnp.int32, x.shape, 1)
    s = 1
    while s < SUBLANES:
        x = x + jnp.where(row >= s, pltpu.roll(x, s, 1), 0.0)
        s *= 2
    x = x.reshape(n // GLA_BLOCK, GLA_BLOCK // SUBLANES, SUBLANES, w)
    carry = jnp.broadcast_to(x[:, 0:1, SUBLANES - 1:SUBLANES, :], (n // GLA_BLOCK, 1, SUBLANES, w))
    x = jnp.concatenate([x[:, 0:1], x[:, 1:2] + carry], axis=1)
    return x.reshape(n, w)


def _hgrn_kernel(hq_ref, hf_ref, hi_ref, hg_ref, lb_ref, ng_ref, s0_ref,
                 oa_ref, x_ref, w_ref, gf_ref,
                 y_ref, sout_ref, st_ref, qt_ref, kt_ref, kd_ref, v_ref,
                 qtf_ref, kdf_ref, bl_ref, ob_ref):
    i = pl.program_id(1)
    nseq, seq_rows = hq_ref.shape[0], hq_ref.shape[1]
    rows = nseq * seq_rows
    nb = rows // GLA_BLOCK

    def out_rows(c):
        r0 = c * OUT_ROWS
        if seq_rows >= OUT_ROWS:
            s, off = divmod(r0, seq_rows)
            return (s, slice(off, off + OUT_ROWS))
        n = OUT_ROWS // seq_rows
        return (slice(r0 // seq_rows, r0 // seq_rows + n),)

    def out_read(ref, c):
        return ref[out_rows(c)].reshape(OUT_ROWS, ref.shape[-1])

    def out_write(ref, c, val):
        if seq_rows < OUT_ROWS:
            val = val.reshape(OUT_ROWS // seq_rows, seq_rows, ref.shape[-1])
        ref[out_rows(c)] = val

    @pl.when(i == 0)
    def _():
        for s in range(nseq):
            for h in range(H_B):
                st_ref[s, h] = s0_ref[s, h].T

    def flat(ref):
        return ref[...].reshape(rows, W_B)

    lb = lb_ref[...]
    f = lb + (1.0 - lb) * _sigmoid(flat(hf_ref))
    b = _block_cumsum(jnp.log(f) * LOG2E)
    b3 = b.reshape(nb, GLA_BLOCK, W_B)
    bl = jnp.broadcast_to(b3[:, GLA_BLOCK - 1:GLA_BLOCK, :],
                          (nb, GLA_BLOCK, W_B)).reshape(rows, W_B)
    kk = 1.0 - f
    qt = _silu(flat(hq_ref)) * jnp.exp2(b)
    kd = kk * jnp.exp2(bl - b)
    qtf_ref[...] = qt
    kdf_ref[...] = kd
    bl_ref[...] = bl
    qt_ref[...] = qt.astype(BF16)
    kt_ref[...] = (kk * jnp.exp2(-b)).astype(BF16)
    kd_ref[...] = kd.astype(BF16)
    v_ref[...] = flat(hi_ref).astype(BF16)

    def begin(c):
        out_write(y_ref, c, out_read(x_ref, c)
                  + jnp.dot(out_read(oa_ref, c), w_ref[0:W_A, :], preferred_element_type=F32))

    r_i = lax.broadcasted_iota(jnp.int32, (HGRN_GROUP, HGRN_GROUP), 0)
    c_i = lax.broadcasted_iota(jnp.int32, (HGRN_GROUP, HGRN_GROUP), 1)
    rb, cb = r_i // GLA_BLOCK, c_i // GLA_BLOCK
    m_diag = (rb == cb) & (r_i >= c_i)
    m_adj = rb == cb + 1
    m_far = rb >= cb + 2

    groups = seq_rows // HGRN_GROUP
    st = {(s, h): st_ref[s, h] for s in range(nseq) for h in range(H_B)}

    def unit(t):
        sg, h = divmod(t, H_B)
        s, g = divmod(sg, groups)
        r0 = s * seq_rows + g * HGRN_GROUP
        return s, g, h, r0, slice(r0, r0 + HGRN_GROUP), slice(h * DK_B, (h + 1) * DK_B)

    def block_decays(r0, cs):
        return [bl_ref[r0 + j * GLA_BLOCK:r0 + j * GLA_BLOCK + 1, cs]
                for j in range(HGRN_GROUP_BLOCKS)]

    def scaled(x, log2_scales):
        parts = []
        for j, sc in enumerate(log2_scales):
            xb = x[j * GLA_BLOCK:(j + 1) * GLA_BLOCK]
            parts.append(xb if sc is None else xb * jnp.exp2(sc))
        return jnp.concatenate(parts, axis=0).astype(BF16)

    def local(t):
        s, g, h, r0, rs, cs = unit(t)
        if h == 0 and r0 % OUT_ROWS == 0:
            begin(r0 // OUT_ROWS)
        d0, d1, d2, d3 = block_decays(r0, cs)
        qt_g = qt_ref[rs, cs]
        kdf = kdf_ref[rs, cs]
        q1 = scaled(qtf_ref[rs, cs], [None, None, None, d2])
        k1 = scaled(kdf, [d1, None, None, None])
        a = jnp.where(m_diag, _nt(qt_g, kt_ref[rs, cs]),
                      jnp.where(m_adj, _nt(qt_g, kd_ref[rs, cs]),
                                jnp.where(m_far, _nt(q1, k1), 0.0)))
        kdp = scaled(kdf, [(d1 + d2) + d3, d2 + d3, d3, None])
        ut = lax.dot_general(v_ref[rs, cs], kdp, (((0,), (0,)), ((), ())),
                             preferred_element_type=F32)
        return a.astype(BF16), ut

    def finish(c):
        hrow = out_read(y_ref, c) + jnp.dot(ob_ref[c * OUT_ROWS:(c + 1) * OUT_ROWS, :],
                                             w_ref[W_A:, :], preferred_element_type=F32)
        ms = jnp.mean(hrow * hrow, axis=-1, keepdims=True)
        out_write(y_ref, c, hrow * lax.rsqrt(ms + EPS) * gf_ref[...])

    def carry(t, loc):
        s, g, h, r0, rs, cs = unit(t)
        a, ut = loc
        d0, d1, d2, d3 = block_decays(r0, cs)
        qs = scaled(qtf_ref[rs, cs], [None, d0, d0 + d1, (d0 + d1) + d2])
        o = (_nt(qs, st[s, h].astype(BF16))
             + jnp.dot(a, v_ref[rs, cs], preferred_element_type=F32))
        dec = jnp.exp2(((d0 + d1) + d2) + d3)
        st[s, h] = st[s, h] * dec + ut
        y = o * lax.rsqrt(jnp.mean(o * o, axis=-1, keepdims=True) + EPS) * ng_ref[:, cs]
        gs = slice(g * HGRN_GROUP, (g + 1) * HGRN_GROUP)
        ob_ref[rs, cs] = (y * _silu(hg_ref[s, gs, cs])).astype(BF16)
        if h == H_B - 1 and (r0 + HGRN_GROUP) % OUT_ROWS == 0:
            finish((r0 + HGRN_GROUP) // OUT_ROWS - 1)

    _emit_ahead(nseq * groups * H_B, HGRN_PIPE_DEPTH, local, carry)
    for (s, h), val in st.items():
        st_ref[s, h] = val

    @pl.when(i == pl.num_programs(1) - 1)
    def _():
        for s in range(nseq):
            for h in range(H_B):
                sout_ref[s, h] = st_ref[s, h].T


def _hgrn(p3d, lb, ng, s0, oa3d, x3d, w_bf16, gf, nseq, rows, name):
    batch, seq, _ = p3d.shape
    assert batch % nseq == 0 and seq % rows == 0 and rows % HGRN_GROUP == 0
    assert (nseq * rows) % OUT_ROWS == 0 and (rows % OUT_ROWS == 0 or OUT_ROWS % rows == 0)

    def col(c):
        return pl.BlockSpec((nseq, rows, W_B), lambda b, i: (b, i, c))

    vec = pl.BlockSpec((1, W_B), lambda b, i: (0, 0))
    state = pl.BlockSpec((nseq, H_B, DK_B, DV_B), lambda b, i: (b, 0, 0, 0))
    wide = pl.BlockSpec((nseq, rows, D_MODEL), lambda b, i: (b, i, 0))
    n = nseq * rows
    return pl.pallas_call(
        _hgrn_kernel,
        out_shape=(jax.ShapeDtypeStruct((batch, seq, D_MODEL), F32),
                   jax.ShapeDtypeStruct((batch, H_B, DK_B, DV_B), F32)),
        grid=(batch // nseq, seq // rows),
        in_specs=[col(4), col(5), col(6), col(7), vec, vec, state,
                  pl.BlockSpec((nseq, rows, W_A), lambda b, i: (b, i, 0)), wide,
                  pl.BlockSpec((W_A + W_B, D_MODEL), lambda b, i: (0, 0)),
                  pl.BlockSpec((1, D_MODEL), lambda b, i: (0, 0))],
        out_specs=(wide, state),
        scratch_shapes=([pltpu.VMEM((nseq, H_B, DV_B, DK_B), F32)]
                        + [pltpu.VMEM((n, W_B), BF16)] * 4
                        + [pltpu.VMEM((n, W_B), F32)] * 3
                        + [pltpu.VMEM((n, W_B), BF16)]),
        compiler_params=_params(("arbitrary", "arbitrary")),
        name=name,
    )(p3d, p3d, p3d, p3d, lb, ng, s0, oa3d, x3d, w_bf16, gf)


HGRN_PROMPT_ROWS = 512
HGRN_PROMPT_SEQS = 2
HGRN_SAMPLE_SEQS = 4


def kernel(x_prompt, x_sample, cache_attn_k, cache_attn_v, state_hgrn, ln_in_g, w_in,
           rel_bias, lb_gamma, hg_norm_g, w_out, ln_f_g):
    batch, seq, _ = x_prompt.shape
    dec_batch, dec_seq, _ = x_sample.shape
    depth = w_in.shape[0]
    assert depth == 1 and dec_seq == CHUNK and PAST_LEN % CHUNK == 0
    assert cache_attn_k.shape[2] == WINDOW_ROWS

    lb_all = jnp.cumsum(jax.nn.softmax(lb_gamma.astype(F32), axis=0), axis=0)
    lb = lb_all[0].reshape(1, W_B)
    ng = hg_norm_g[0].reshape(1, W_B)
    g_in = ln_in_g[0].reshape(1, D_MODEL)
    g_f = ln_f_g.reshape(1, D_MODEL)
    w_out_b = w_out[0].astype(BF16)

    xp = x_prompt.reshape(batch * seq, D_MODEL)
    xs = x_sample.reshape(dec_batch * dec_seq, D_MODEL)

    bias_pair, bias_single = _bias_tables(rel_bias[0])

    pp, ps, k_s, v_s = _inproj(xp, xs, g_in, w_in[0])

    oa_p, kt_p, vt_p = _attn_prompt(pp, bias_pair, batch, seq)
    ckt = jnp.transpose(cache_attn_k[0], (0, 2, 3, 1)).reshape(dec_batch, W_A, WINDOW_ROWS)
    cvt = jnp.transpose(cache_attn_v[0], (0, 2, 3, 1)).reshape(dec_batch, W_A, WINDOW_ROWS)
    oa_s = _attn_sample(ps, ckt, cvt, bias_single, dec_batch)

    s0_p = jnp.zeros((batch, H_B, DK_B, DV_B), F32)
    y_p, st_p = _hgrn(pp.reshape(batch, seq, N_IN), lb, ng, s0_p,
                      oa_p.reshape(batch, seq, W_A), x_prompt, w_out_b, g_f,
                      HGRN_PROMPT_SEQS, HGRN_PROMPT_ROWS, "hgrn_out_prompt")
    y_s, st_s = _hgrn(ps.reshape(dec_batch, dec_seq, N_IN), lb, ng, state_hgrn[0],
                      oa_s.reshape(dec_batch, dec_seq, W_A), x_sample, w_out_b, g_f,
                      HGRN_SAMPLE_SEQS, dec_seq, "hgrn_out_sample")

    def cache_rows(t):
        return jnp.transpose(t.reshape(batch, H_A, DH_A, WINDOW_ROWS), (0, 3, 1, 2))[None]

    k_p, v_p = cache_rows(kt_p), cache_rows(vt_p)
    k_s = k_s.reshape(1, dec_batch, dec_seq, H_A, DH_A)
    v_s = v_s.reshape(1, dec_batch, dec_seq, H_A, DH_A)

    return (y_p, y_s, k_p, v_p, st_p[None], k_s, v_s, st_s[None])
```

```python
import functools

import jax
import jax.numpy as jnp
from jax import lax
from jax.experimental import pallas as pl
from jax.experimental.pallas import tpu as pltpu

F32 = jnp.float32
BF16 = jnp.bfloat16

D_MODEL = 1024
CHUNK = 64
LEFT_CHUNKS = 8
WINDOW_ROWS = LEFT_CHUNKS * CHUNK
BAND = WINDOW_ROWS + CHUNK
W_A = 512
H_A = 8
DH_A = 64
W_B = 512
H_B = 4
DK_B = 128
DV_B = 128
REL_CLIP = 128
GLA_BLOCK = 16
EPS = 1e-6
ATTN_SCALE = DH_A ** -0.5
NEG_INF = -1e30
N_IN = 4 * W_A + 4 * W_B
PAST_LEN = 2048

LANES = 128
HEAD_PAIRS = H_A // 2
REL_TABLE = 2 * REL_CLIP + 1
REL_TABLE_PAD = 384
TOEPLITZ_LEN = 640
VMEM_LIMIT = 56 * 1024 * 1024


def _params(semantics):
    return pltpu.CompilerParams(dimension_semantics=semantics,
                                vmem_limit_bytes=VMEM_LIMIT)


LOG2E = 1.4426950408889634


def _sigmoid(x):
    return 1.0 / (1.0 + jnp.exp2(x * -LOG2E))


def _silu(x):
    return x * _sigmoid(x)


def _nt(a, b):
    return lax.dot_general(a, b, (((1,), (1,)), ((), ())), preferred_element_type=F32)


def _emit_ahead(n_units, depth, first_fn, second_fn):
    pending = {}
    for t in range(n_units + depth):
        if t < n_units:
            pending[t] = first_fn(t)
        if t >= depth:
            second_fn(t - depth, pending.pop(t - depth))


INPROJ_ROWS = 512
INPROJ_SAMPLE_ROWS = 256
INPROJ_COLS = 512


def _inproj_kernel(xp_ref, xs_ref, g_ref, w_ref, pp_ref, ps_ref, ks_ref, vs_ref, *,
                   prompt_steps):
    i = pl.program_id(0)

    def project(x_ref, o_ref):
        x = x_ref[...]
        ms = jnp.mean(x * x, axis=-1, keepdims=True)
        xn = (x * lax.rsqrt(ms + EPS) * g_ref[...]).astype(BF16)
        for n0 in range(0, N_IN, INPROJ_COLS):
            o_ref[:, n0:n0 + INPROJ_COLS] = jnp.dot(
                xn, w_ref[:, n0:n0 + INPROJ_COLS].astype(BF16), preferred_element_type=F32)

    def split_heads(cols, out_ref):
        heads = jnp.stack([cols[:, h * DH_A:(h + 1) * DH_A] for h in range(H_A)], axis=0)
        out_ref[...] = jnp.swapaxes(heads, 0, 1)

    @pl.when(i < prompt_steps)
    def _():
        project(xp_ref, pp_ref)

    @pl.when(i >= prompt_steps)
    def _():
        project(xs_ref, ps_ref)
        split_heads(ps_ref[:, W_A:2 * W_A], ks_ref)
        split_heads(ps_ref[:, 2 * W_A:3 * W_A], vs_ref)


def _inproj(xp, xs, g, w):
    rows_p, rows_s = xp.shape[0], xs.shape[0]
    tp, ts = INPROJ_ROWS, INPROJ_SAMPLE_ROWS
    assert rows_p % tp == 0 and rows_s % ts == 0
    np_, ns = rows_p // tp, rows_s // ts

    def prompt_blk(i):
        return (jnp.minimum(i, np_ - 1), 0)

    def sample_blk(i):
        return (jnp.maximum(i - np_, 0), 0)

    def sample_blk3(i):
        return (jnp.maximum(i - np_, 0), 0, 0)

    return pl.pallas_call(
        functools.partial(_inproj_kernel, prompt_steps=np_),
        out_shape=(jax.ShapeDtypeStruct((rows_p, N_IN), F32),
                   jax.ShapeDtypeStruct((rows_s, N_IN), F32),
                   jax.ShapeDtypeStruct((rows_s, H_A, DH_A), F32),
                   jax.ShapeDtypeStruct((rows_s, H_A, DH_A), F32)),
        grid=(np_ + ns,),
        in_specs=[
            pl.BlockSpec((tp, D_MODEL), prompt_blk),
            pl.BlockSpec((ts, D_MODEL), sample_blk),
            pl.BlockSpec((1, D_MODEL), lambda i: (0, 0)),
            pl.BlockSpec((D_MODEL, N_IN), lambda i: (0, 0)),
        ],
        out_specs=(pl.BlockSpec((tp, N_IN), prompt_blk),
                   pl.BlockSpec((ts, N_IN), sample_blk),
                   pl.BlockSpec((ts, H_A, DH_A), sample_blk3),
                   pl.BlockSpec((ts, H_A, DH_A), sample_blk3)),
        compiler_params=_params(("arbitrary",)),
        name="inproj",
    )(xp, xs, g, w)


PAIR_BAND = BAND + CHUNK


def _bias_kernel(rb_ref, pair_ref, single_ref):
    m = lax.broadcasted_iota(jnp.int32, (REL_TABLE_PAD, TOEPLITZ_LEN), 1)
    t = lax.broadcasted_iota(jnp.int32, (REL_TABLE_PAD, TOEPLITZ_LEN), 0)
    idx = jnp.clip(BAND - 1 - m, -REL_CLIP, REL_CLIP) + REL_CLIP
    onehot = jnp.where(idx == t, 1.0, 0.0).astype(BF16)
    rb = rb_ref[...]
    hi = rb.astype(BF16)
    r1 = rb - hi.astype(F32)
    mid = r1.astype(BF16)
    lo = (r1 - mid.astype(F32)).astype(BF16)
    toep = (jnp.dot(hi, onehot, preferred_element_type=F32)
            + jnp.dot(mid, onehot, preferred_element_type=F32)
            + jnp.dot(lo, onehot, preferred_element_type=F32))
    key = lax.broadcasted_iota(jnp.int32, (CHUNK, TOEPLITZ_LEN), 1)
    for p in range(HEAD_PAIRS):
        for u in range(2):
            halves = []
            for h in (2 * p, 2 * p + 1):
                rows = jnp.broadcast_to(toep[h:h + 1, :], (CHUNK, TOEPLITZ_LEN))
                shift = (TOEPLITZ_LEN - (CHUNK - 1) + CHUNK * u) % TOEPLITZ_LEN
                rolled = pltpu.roll(rows, shift, 1, stride=1, stride_axis=0)
                seen = (key >= CHUNK * u) & (key < CHUNK * u + BAND)
                halves.append(jnp.where(seen, rolled * LOG2E, NEG_INF))
            tile_t = jnp.concatenate(halves, axis=0).T
            pair_ref[p, :, u * LANES:(u + 1) * LANES] = tile_t
            if u == 0:
                single_ref[p] = tile_t[:BAND, :]


def _bias_tables(rel_bias_l):
    rb = jnp.pad(rel_bias_l, ((0, 0), (0, REL_TABLE_PAD - REL_TABLE)))
    return pl.pallas_call(
        _bias_kernel,
        out_shape=(jax.ShapeDtypeStruct((HEAD_PAIRS, PAIR_BAND, 2 * LANES), F32),
                   jax.ShapeDtypeStruct((HEAD_PAIRS, BAND, LANES), F32)),
        name="rel_bias_table",
    )(rb)


def _attn_probs(qs, k2, bias_t, valid_from):
    first = lax.broadcasted_iota(jnp.int32, (CHUNK, LANES), 1) < DH_A
    parts = []
    for q in qs:
        q = q * (ATTN_SCALE * LOG2E)
        parts += [jnp.where(first, q, 0.0), jnp.where(first, 0.0, q)]
    qbd = jnp.concatenate(parts, axis=0).astype(BF16)
    s = _nt(k2, qbd) + bias_t
    if valid_from is not None:
        key = lax.broadcasted_iota(jnp.int32, s.shape, 0)
        s = jnp.where(key >= valid_from, s, NEG_INF)
    mx = jnp.max(s, axis=0, keepdims=True)
    e = jnp.exp2(s - mx)
    return e.astype(BF16), jnp.sum(e, axis=0, keepdims=True)


def _attn_apply(probs, ags, vt):
    e, denom = probs
    first = lax.broadcasted_iota(jnp.int32, (CHUNK, LANES), 1) < DH_A
    ot = jnp.dot(vt, e, preferred_element_type=F32)
    o2 = (ot * (1.0 / denom)).T
    outs = []
    for u, ag in enumerate(ags):
        r = 2 * CHUNK * u
        o = jnp.where(first, o2[r:r + CHUNK], o2[r + CHUNK:r + 2 * CHUNK])
        outs.append(o * _silu(ag))
    return outs


ATTN_PIPE_DEPTH = 3


def _attn_pipeline(n_units, probs_fn, apply_fn):
    _emit_ahead(n_units, ATTN_PIPE_DEPTH, probs_fn, apply_fn)


ATTN_ROWS = 1024


def _attn_prompt_kernel(q_ref, k_ref, v_ref, ag_ref, bias_ref, o_ref, kt_ref, vt_ref,
                        kh_ref, vth_ref):
    i = pl.program_id(1)

    @pl.when(i == pl.num_programs(1) - 1)
    def _():
        tail = slice(ATTN_ROWS - WINDOW_ROWS, ATTN_ROWS)
        for p in range(HEAD_PAIRS):
            cs = slice(p * LANES, (p + 1) * LANES)
            kt_ref[0, cs, :] = k_ref[tail, cs].T
            vt_ref[0, cs, :] = v_ref[tail, cs].T

    @pl.when(i == 0)
    def _():
        kh_ref[0:WINDOW_ROWS, :] = jnp.zeros((WINDOW_ROWS, W_A), BF16)
        vth_ref[:, :, 0:WINDOW_ROWS] = jnp.zeros((HEAD_PAIRS, LANES, WINDOW_ROWS), BF16)

    def body(first_step):
        def unit(t):
            cp, p = divmod(t, HEAD_PAIRS)
            r0 = cp * 2 * CHUNK
            rs = [slice(r0 + u * CHUNK, r0 + (u + 1) * CHUNK) for u in range(2)]
            return cp, p, r0, rs, slice(p * LANES, (p + 1) * LANES)

        def probs(t):
            cp, p, r0, rs, cs = unit(t)
            own = slice(r0, r0 + 2 * CHUNK)
            hist = slice(WINDOW_ROWS + r0, WINDOW_ROWS + r0 + 2 * CHUNK)
            kh_ref[hist, cs] = k_ref[own, cs].astype(BF16)
            vth_ref[p, :, hist] = v_ref[own, cs].T.astype(BF16)
            valid_from = (LEFT_CHUNKS - 2 * cp) * CHUNK if first_step else None
            return _attn_probs([q_ref[r, cs] for r in rs], kh_ref[r0:r0 + PAIR_BAND, cs],
                               bias_ref[p], valid_from)

        def apply(t, pr):
            cp, p, r0, rs, cs = unit(t)
            outs = _attn_apply(pr, [ag_ref[r, cs] for r in rs],
                               vth_ref[p, :, r0:r0 + PAIR_BAND])
            for r, o in zip(rs, outs):
                o_ref[r, cs] = o.astype(o_ref.dtype)

        _attn_pipeline((ATTN_ROWS // (2 * CHUNK)) * HEAD_PAIRS, probs, apply)

    @pl.when(i == 0)
    def _():
        body(True)

    @pl.when(i != 0)
    def _():
        body(False)

    kh_ref[0:WINDOW_ROWS, :] = kh_ref[ATTN_ROWS:ATTN_ROWS + WINDOW_ROWS, :]
    vth_ref[:, :, 0:WINDOW_ROWS] = vth_ref[:, :, ATTN_ROWS:ATTN_ROWS + WINDOW_ROWS]


def _attn_prompt(p2d, bias_pair, batch, seq):
    assert seq % ATTN_ROWS == 0
    nt = seq // ATTN_ROWS

    def col(c):
        return pl.BlockSpec((ATTN_ROWS, W_A), lambda b, i: (b * nt + i, c))

    tail_t = pl.BlockSpec((1, W_A, WINDOW_ROWS), lambda b, i: (b, 0, 0))
    return pl.pallas_call(
        _attn_prompt_kernel,
        out_shape=(jax.ShapeDtypeStruct((batch * seq, W_A), BF16),
                   jax.ShapeDtypeStruct((batch, W_A, WINDOW_ROWS), F32),
                   jax.ShapeDtypeStruct((batch, W_A, WINDOW_ROWS), F32)),
        grid=(batch, nt),
        in_specs=[col(0), col(1), col(2), col(3),
                  pl.BlockSpec((HEAD_PAIRS, PAIR_BAND, 2 * LANES), lambda b, i: (0, 0, 0))],
        out_specs=(pl.BlockSpec((ATTN_ROWS, W_A), lambda b, i: (b * nt + i, 0)),
                   tail_t, tail_t),
        scratch_shapes=[pltpu.VMEM((WINDOW_ROWS + ATTN_ROWS, W_A), BF16),
                        pltpu.VMEM((HEAD_PAIRS, LANES, WINDOW_ROWS + ATTN_ROWS), BF16)],
        compiler_params=_params(("arbitrary", "arbitrary")),
        name="attn_prompt",
    )(p2d, p2d, p2d, p2d, bias_pair)


ATTN_SAMPLE_SEQS = 4


def _attn_sample_kernel(q_ref, k_ref, v_ref, ag_ref, ckt_ref, cvt_ref, bias_ref, o_ref,
                        kh_ref, vth_ref):
    for s in range(ATTN_SAMPLE_SEQS):
        rs = slice(s * CHUNK, (s + 1) * CHUNK)
        kh_ref[s, WINDOW_ROWS:, :] = k_ref[rs, :].astype(BF16)
        for p in range(HEAD_PAIRS):
            cs = slice(p * LANES, (p + 1) * LANES)
            kh_ref[s, 0:WINDOW_ROWS, cs] = ckt_ref[s, cs, :].T.astype(BF16)
            vth_ref[s, p, :, 0:WINDOW_ROWS] = cvt_ref[s, cs, :].astype(BF16)
            vth_ref[s, p, :, WINDOW_ROWS:] = v_ref[rs, cs].T.astype(BF16)

    def unit(t):
        s, p = divmod(t, HEAD_PAIRS)
        return s, p, slice(s * CHUNK, (s + 1) * CHUNK), slice(p * LANES, (p + 1) * LANES)

    def probs(t):
        s, p, rs, cs = unit(t)
        return _attn_probs([q_ref[rs, cs]], kh_ref[s, :, cs], bias_ref[p], None)

    def apply(t, pr):
        s, p, rs, cs = unit(t)
        (o,) = _attn_apply(pr, [ag_ref[rs, cs]], vth_ref[s, p])
        o_ref[rs, cs] = o.astype(o_ref.dtype)

    _attn_pipeline(ATTN_SAMPLE_SEQS * HEAD_PAIRS, probs, apply)


def _attn_sample(p2d, cache_kt, cache_vt, bias_single, batch):
    ns = ATTN_SAMPLE_SEQS

    def col(c):
        return pl.BlockSpec((ns * CHUNK, W_A), lambda b: (b, c))

    cache = pl.BlockSpec((ns, W_A, WINDOW_ROWS), lambda b: (b, 0, 0))
    return pl.pallas_call(
        _attn_sample_kernel,
        out_shape=jax.ShapeDtypeStruct((batch * CHUNK, W_A), BF16),
        grid=(batch // ns,),
        in_specs=[col(0), col(1), col(2), col(3), cache, cache,
                  pl.BlockSpec((HEAD_PAIRS, BAND, LANES), lambda b: (0, 0, 0))],
        out_specs=pl.BlockSpec((ns * CHUNK, W_A), lambda b: (b, 0)),
        scratch_shapes=[pltpu.VMEM((ns, BAND, W_A), BF16),
                        pltpu.VMEM((ns, HEAD_PAIRS, LANES, BAND), BF16)],
        compiler_params=_params(("arbitrary",)),
        name="attn_sample",
    )(p2d, p2d, p2d, p2d, cache_kt, cache_vt, bias_single)


HGRN_GROUP_BLOCKS = 4
HGRN_GROUP = HGRN_GROUP_BLOCKS * GLA_BLOCK
HGRN_PIPE_DEPTH = 4
OUT_ROWS = 256


SUBLANES = 8


def _block_cumsum(x):
    n, w = x.shape
    x = x.reshape(n // SUBLANES, SUBLANES, w)
    row = lax.broadcasted_iota(jnp.int32, x.shape, 1)
    s = 1
    while s < SUBLANES:
        x = x + jnp.where(row >= s, pltpu.roll(x, s, 1), 0.0)
        s *= 2
    x = x.reshape(n // GLA_BLOCK, GLA_BLOCK // SUBLANES, SUBLANES, w)
    carry = jnp.broadcast_to(x[:, 0:1, SUBLANES - 1:SUBLANES, :], (n // GLA_BLOCK, 1, SUBLANES, w))
    x = jnp.concatenate([x[:, 0:1], x[:, 1:2] + carry], axis=1)
    return x.reshape(n, w)


def _hgrn_kernel(hq_ref, hf_ref, hi_ref, hg_ref, lb_ref, ng_ref, s0_ref,
                 oa_ref, x_ref, w_ref, gf_ref,
                 y_ref, sout_ref, st_ref, qt_ref, kt_ref, kd_ref, v_ref,
                 qtf_ref, kdf_ref, bl_ref, ob_ref):
    i = pl.program_id(1)
    nseq, seq_rows = hq_ref.shape[0], hq_ref.shape[1]
    rows = nseq * seq_rows
    nb = rows // GLA_BLOCK

    def out_rows(c):
        r0 = c * OUT_ROWS
        if seq_rows >= OUT_ROWS:
            s, off = divmod(r0, seq_rows)
            return (s, slice(off, off + OUT_ROWS))
        n = OUT_ROWS // seq_rows
        return (slice(r0 // seq_rows, r0 // seq_rows + n),)

    def out_read(ref, c):
        return ref[out_rows(c)].reshape(OUT_ROWS, ref.shape[-1])

    def out_write(ref, c, val):
        if seq_rows < OUT_ROWS:
            val = val.reshape(OUT_ROWS // seq_rows, seq_rows, ref.shape[-1])
        ref[out_rows(c)] = val

    @pl.when(i == 0)
    def _():
        for s in range(nseq):
            for h in range(H_B):
                st_ref[s, h] = s0_ref[s, h].T

    def flat(ref):
        return ref[...].reshape(rows, W_B)

    lb = lb_ref[...]
    f = lb + (1.0 - lb) * _sigmoid(flat(hf_ref))
    b = _block_cumsum(jnp.log(f) * LOG2E)
    b3 = b.reshape(nb, GLA_BLOCK, W_B)
    bl = jnp.broadcast_to(b3[:, GLA_BLOCK - 1:GLA_BLOCK, :],
                          (nb, GLA_BLOCK, W_B)).reshape(rows, W_B)
    kk = 1.0 - f
    qt = _silu(flat(hq_ref)) * jnp.exp2(b)
    kd = kk * jnp.exp2(bl - b)
    qtf_ref[...] = qt
    kdf_ref[...] = kd
    bl_ref[...] = bl
    qt_ref[...] = qt.astype(BF16)
    kt_ref[...] = (kk * jnp.exp2(-b)).astype(BF16)
    kd_ref[...] = kd.astype(BF16)
    v_ref[...] = flat(hi_ref).astype(BF16)

    def begin(c):
        out_write(y_ref, c, out_read(x_ref, c)
                  + jnp.dot(out_read(oa_ref, c), w_ref[0:W_A, :], preferred_element_type=F32))

    r_i = lax.broadcasted_iota(jnp.int32, (HGRN_GROUP, HGRN_GROUP), 0)
    c_i = lax.broadcasted_iota(jnp.int32, (HGRN_GROUP, HGRN_GROUP), 1)
    rb, cb = r_i // GLA_BLOCK, c_i // GLA_BLOCK
    m_diag = (rb == cb) & (r_i >= c_i)
    m_adj = rb == cb + 1
    m_far = rb >= cb + 2

    groups = seq_rows // HGRN_GROUP
    st = {(s, h): st_ref[s, h] for s in range(nseq) for h in range(H_B)}

    def unit(t):
        sg, h = divmod(t, H_B)
        s, g = divmod(sg, groups)
        r0 = s * seq_rows + g * HGRN_GROUP
        return s, g, h, r0, slice(r0, r0 + HGRN_GROUP), slice(h * DK_B, (h + 1) * DK_B)

    def block_decays(r0, cs):
        return [bl_ref[r0 + j * GLA_BLOCK:r0 + j * GLA_BLOCK + 1, cs]
                for j in range(HGRN_GROUP_BLOCKS)]

    def scaled(x, log2_scales):
        parts = []
        for j, sc in enumerate(log2_scales):
            xb = x[j * GLA_BLOCK:(j + 1) * GLA_BLOCK]
            parts.append(xb if sc is None else xb * jnp.exp2(sc))
        return jnp.concatenate(parts, axis=0).astype(BF16)

    def local(t):
        s, g, h, r0, rs, cs = unit(t)
        if h == 0 and r0 % OUT_ROWS == 0:
            begin(r0 // OUT_ROWS)
        d0, d1, d2, d3 = block_decays(r0, cs)
        qt_g = qt_ref[rs, cs]
        kdf = kdf_ref[rs, cs]
        q1 = scaled(qtf_ref[rs, cs], [None, None, None, d2])
        k1 = scaled(kdf, [d1, None, None, None])
        a = jnp.where(m_diag, _nt(qt_g, kt_ref[rs, cs]),
                      jnp.where(m_adj, _nt(qt_g, kd_ref[rs, cs]),
                                jnp.where(m_far, _nt(q1, k1), 0.0)))
        kdp = scaled(kdf, [(d1 + d2) + d3, d2 + d3, d3, None])
        ut = lax.dot_general(v_ref[rs, cs], kdp, (((0,), (0,)), ((), ())),
                             preferred_element_type=F32)
        return a.astype(BF16), ut

    def finish(c):
        hrow = out_read(y_ref, c) + jnp.dot(ob_ref[c * OUT_ROWS:(c + 1) * OUT_ROWS, :],
                                             w_ref[W_A:, :], preferred_element_type=F32)
        ms = jnp.mean(hrow * hrow, axis=-1, keepdims=True)
        out_write(y_ref, c, hrow * lax.rsqrt(ms + EPS) * gf_ref[...])

    def carry(t, loc):
        s, g, h, r0, rs, cs = unit(t)
        a, ut = loc
        d0, d1, d2, d3 = block_decays(r0, cs)
        qs = scaled(qtf_ref[rs, cs], [None, d0, d0 + d1, (d0 + d1) + d2])
        o = (_nt(qs, st[s, h].astype(BF16))
             + jnp.dot(a, v_ref[rs, cs], preferred_element_type=F32))
        dec = jnp.exp2(((d0 + d1) + d2) + d3)
        st[s, h] = st[s, h] * dec + ut
        y = o * lax.rsqrt(jnp.mean(o * o, axis=-1, keepdims=True) + EPS) * ng_ref[:, cs]
        gs = slice(g * HGRN_GROUP, (g + 1) * HGRN_GROUP)
        ob_ref[rs, cs] = (y * _silu(hg_ref[s, gs, cs])).astype(BF16)
        if h == H_B - 1 and (r0 + HGRN_GROUP) % OUT_ROWS == 0:
            finish((r0 + HGRN_GROUP) // OUT_ROWS - 1)

    _emit_ahead(nseq * groups * H_B, HGRN_PIPE_DEPTH, local, carry)
    for (s, h), val in st.items():
        st_ref[s, h] = val

    @pl.when(i == pl.num_programs(1) - 1)
    def _():
        for s in range(nseq):
            for h in range(H_B):
                sout_ref[s, h] = st_ref[s, h].T


def _hgrn(p3d, lb, ng, s0, oa3d, x3d, w_bf16, gf, nseq, rows, name):
    batch, seq, _ = p3d.shape
    assert batch % nseq == 0 and seq % rows == 0 and rows % HGRN_GROUP == 0
    assert (nseq * rows) % OUT_ROWS == 0 and (rows % OUT_ROWS == 0 or OUT_ROWS % rows == 0)

    def col(c):
        return pl.BlockSpec((nseq, rows, W_B), lambda b, i: (b, i, c))

    vec = pl.BlockSpec((1, W_B), lambda b, i: (0, 0))
    state = pl.BlockSpec((nseq, H_B, DK_B, DV_B), lambda b, i: (b, 0, 0, 0))
    wide = pl.BlockSpec((nseq, rows, D_MODEL), lambda b, i: (b, i, 0))
    n = nseq * rows
    return pl.pallas_call(
        _hgrn_kernel,
        out_shape=(jax.ShapeDtypeStruct((batch, seq, D_MODEL), F32),
                   jax.ShapeDtypeStruct((batch, H_B, DK_B, DV_B), F32)),
        grid=(batch // nseq, seq // rows),
        in_specs=[col(4), col(5), col(6), col(7), vec, vec, state,
                  pl.BlockSpec((nseq, rows, W_A), lambda b, i: (b, i, 0)), wide,
                  pl.BlockSpec((W_A + W_B, D_MODEL), lambda b, i: (0, 0)),
                  pl.BlockSpec((1, D_MODEL), lambda b, i: (0, 0))],
        out_specs=(wide, state),
        scratch_shapes=([pltpu.VMEM((nseq, H_B, DV_B, DK_B), F32)]
                        + [pltpu.VMEM((n, W_B), BF16)] * 4
                        + [pltpu.VMEM((n, W_B), F32)] * 3
                        + [pltpu.VMEM((n, W_B), BF16)]),
        compiler_params=_params(("arbitrary", "arbitrary")),
        name=name,
    )(p3d, p3d, p3d, p3d, lb, ng, s0, oa3d, x3d, w_bf16, gf)


HGRN_PROMPT_ROWS = 512
HGRN_PROMPT_SEQS = 2
HGRN_SAMPLE_SEQS = 4


def kernel(x_prompt, x_sample, cache_attn_k, cache_attn_v, state_hgrn, ln_in_g, w_in,
           rel_bias, lb_gamma, hg_norm_g, w_out, ln_f_g):
    batch, seq, _ = x_prompt.shape
    dec_batch, dec_seq, _ = x_sample.shape
    depth = w_in.shape[0]
    assert depth == 1 and dec_seq == CHUNK and PAST_LEN % CHUNK == 0
    assert cache_attn_k.shape[2] == WINDOW_ROWS

    lb_all = jnp.cumsum(jax.nn.softmax(lb_gamma.astype(F32), axis=0), axis=0)
    lb = lb_all[0].reshape(1, W_B)
    ng = hg_norm_g[0].reshape(1, W_B)
    g_in = ln_in_g[0].reshape(1, D_MODEL)
    g_f = ln_f_g.reshape(1, D_MODEL)
    w_out_b = w_out[0].astype(BF16)

    xp = x_prompt.reshape(batch * seq, D_MODEL)
    xs = x_sample.reshape(dec_batch * dec_seq, D_MODEL)

    bias_pair, bias_single = _bias_tables(rel_bias[0])

    pp, ps, k_s, v_s = _inproj(xp, xs, g_in, w_in[0])

    oa_p, kt_p, vt_p = _attn_prompt(pp, bias_pair, batch, seq)
    ckt = jnp.transpose(cache_attn_k[0], (0, 2, 3, 1)).reshape(dec_batch, W_A, WINDOW_ROWS)
    cvt = jnp.transpose(cache_attn_v[0], (0, 2, 3, 1)).reshape(dec_batch, W_A, WINDOW_ROWS)
    oa_s = _attn_sample(ps, ckt, cvt, bias_single, dec_batch)

    s0_p = jnp.zeros((batch, H_B, DK_B, DV_B), F32)
    y_p, st_p = _hgrn(pp.reshape(batch, seq, N_IN), lb, ng, s0_p,
                      oa_p.reshape(batch, seq, W_A), x_prompt, w_out_b, g_f,
                      HGRN_PROMPT_SEQS, HGRN_PROMPT_ROWS, "hgrn_out_prompt")
    y_s, st_s = _hgrn(ps.reshape(dec_batch, dec_seq, N_IN), lb, ng, state_hgrn[0],
                      oa_s.reshape(dec_batch, dec_seq, W_A), x_sample, w_out_b, g_f,
                      HGRN_SAMPLE_SEQS, dec_seq, "hgrn_out_sample")

    def cache_rows(t):
        return jnp.transpose(t.reshape(batch, H_A, DH_A, WINDOW_ROWS), (0, 3, 1, 2))[None]

    k_p, v_p = cache_rows(kt_p), cache_rows(vt_p)
    k_s = k_s.reshape(1, dec_batch, dec_seq, H_A, DH_A)
    v_s = v_s.reshape(1, dec_batch, dec_seq, H_A, DH_A)

    return (y_p, y_s, k_p, v_p, st_p[None], k_s, v_s, st_s[None])
```

```python
import functools

import jax
import jax.numpy as jnp
from jax import lax
from jax.experimental import pallas as pl
from jax.experimental.pallas import tpu as pltpu

F32 = jnp.float32
BF16 = jnp.bfloat16

D_MODEL = 1024
CHUNK = 64
LEFT_CHUNKS = 8
WINDOW_ROWS = LEFT_CHUNKS * CHUNK
BAND = WINDOW_ROWS + CHUNK
W_A = 512
H_A = 8
DH_A = 64
W_B = 512
H_B = 4
DK_B = 128
DV_B = 128
REL_CLIP = 128
GLA_BLOCK = 16
EPS = 1e-6
ATTN_SCALE = DH_A ** -0.5
NEG_INF = -1e30
N_IN = 4 * W_A + 4 * W_B
PAST_LEN = 2048

LANES = 128
HEAD_PAIRS = H_A // 2
REL_TABLE = 2 * REL_CLIP + 1
REL_TABLE_PAD = 384
TOEPLITZ_LEN = 640
VMEM_LIMIT = 56 * 1024 * 1024


def _params(semantics):
    return pltpu.CompilerParams(dimension_semantics=semantics,
                                vmem_limit_bytes=VMEM_LIMIT)


LOG2E = 1.4426950408889634


def _sigmoid(x):
    return 1.0 / (1.0 + jnp.exp2(x * -LOG2E))


def _silu(x):
    return x * _sigmoid(x)


def _nt(a, b):
    return lax.dot_general(a, b, (((1,), (1,)), ((), ())), preferred_element_type=F32)


def _emit_ahead(n_units, depth, first_fn, second_fn):
    pending = {}
    for t in range(n_units + depth):
        if t < n_units:
            pending[t] = first_fn(t)
        if t >= depth:
            second_fn(t - depth, pending.pop(t - depth))


INPROJ_ROWS = 512
INPROJ_SAMPLE_ROWS = 256
INPROJ_COLS = 512


def _inproj_kernel(xp_ref, xs_ref, g_ref, w_ref, pp_ref, ps_ref, ks_ref, vs_ref, *,
                   prompt_steps):
    i = pl.program_id(0)

    def project(x_ref, o_ref):
        x = x_ref[...]
        ms = jnp.mean(x * x, axis=-1, keepdims=True)
        xn = (x * lax.rsqrt(ms + EPS) * g_ref[...]).astype(BF16)
        for n0 in range(0, N_IN, INPROJ_COLS):
            o_ref[:, n0:n0 + INPROJ_COLS] = jnp.dot(
                xn, w_ref[:, n0:n0 + INPROJ_COLS].astype(BF16), preferred_element_type=F32)

    def split_heads(cols, out_ref):
        heads = jnp.stack([cols[:, h * DH_A:(h + 1) * DH_A] for h in range(H_A)], axis=0)
        out_ref[...] = jnp.swapaxes(heads, 0, 1)

    @pl.when(i < prompt_steps)
    def _():
        project(xp_ref, pp_ref)

    @pl.when(i >= prompt_steps)
    def _():
        project(xs_ref, ps_ref)
        split_heads(ps_ref[:, W_A:2 * W_A], ks_ref)
        split_heads(ps_ref[:, 2 * W_A:3 * W_A], vs_ref)


def _inproj(xp, xs, g, w):
    rows_p, rows_s = xp.shape[0], xs.shape[0]
    tp, ts = INPROJ_ROWS, INPROJ_SAMPLE_ROWS
    assert rows_p % tp == 0 and rows_s % ts == 0
    np_, ns = rows_p // tp, rows_s // ts

    def prompt_blk(i):
        return (jnp.minimum(i, np_ - 1), 0)

    def sample_blk(i):
        return (jnp.maximum(i - np_, 0), 0)

    def sample_blk3(i):
        return (jnp.maximum(i - np_, 0), 0, 0)

    return pl.pallas_call(
        functools.partial(_inproj_kernel, prompt_steps=np_),
        out_shape=(jax.ShapeDtypeStruct((rows_p, N_IN), F32),
                   jax.ShapeDtypeStruct((rows_s, N_IN), F32),
                   jax.ShapeDtypeStruct((rows_s, H_A, DH_A), F32),
                   jax.ShapeDtypeStruct((rows_s, H_A, DH_A), F32)),
        grid=(np_ + ns,),
        in_specs=[
            pl.BlockSpec((tp, D_MODEL), prompt_blk),
            pl.BlockSpec((ts, D_MODEL), sample_blk),
            pl.BlockSpec((1, D_MODEL), lambda i: (0, 0)),
            pl.BlockSpec((D_MODEL, N_IN), lambda i: (0, 0)),
        ],
        out_specs=(pl.BlockSpec((tp, N_IN), prompt_blk),
                   pl.BlockSpec((ts, N_IN), sample_blk),
                   pl.BlockSpec((ts, H_A, DH_A), sample_blk3),
                   pl.BlockSpec((ts, H_A, DH_A), sample_blk3)),
        compiler_params=_params(("arbitrary",)),
        name="inproj",
    )(xp, xs, g, w)


PAIR_BAND = BAND + CHUNK


def _bias_kernel(rb_ref, lbg_ref, pair_ref, single_ref, lb_ref):
    g = lbg_ref[...]
    e = jnp.exp(g - jnp.max(g, axis=0, keepdims=True))
    lb_ref[...] = e[0:1] / jnp.sum(e, axis=0, keepdims=True)

    m = lax.broadcasted_iota(jnp.int32, (REL_TABLE_PAD, TOEPLITZ_LEN), 1)
    t = lax.broadcasted_iota(jnp.int32, (REL_TABLE_PAD, TOEPLITZ_LEN), 0)
    idx = jnp.clip(BAND - 1 - m, -REL_CLIP, REL_CLIP) + REL_CLIP
    onehot = jnp.where(idx == t, 1.0, 0.0).astype(BF16)
    rb = rb_ref[...]
    hi = rb.astype(BF16)
    r1 = rb - hi.astype(F32)
    mid = r1.astype(BF16)
    lo = (r1 - mid.astype(F32)).astype(BF16)
    toep = (jnp.dot(hi, onehot, preferred_element_type=F32)
            + jnp.dot(mid, onehot, preferred_element_type=F32)
            + jnp.dot(lo, onehot, preferred_element_type=F32))
    key = lax.broadcasted_iota(jnp.int32, (CHUNK, TOEPLITZ_LEN), 1)
    for p in range(HEAD_PAIRS):
        for u in range(2):
            halves = []
            for h in (2 * p, 2 * p + 1):
                rows = jnp.broadcast_to(toep[h:h + 1, :], (CHUNK, TOEPLITZ_LEN))
                shift = (TOEPLITZ_LEN - (CHUNK - 1) + CHUNK * u) % TOEPLITZ_LEN
                rolled = pltpu.roll(rows, shift, 1, stride=1, stride_axis=0)
                seen = (key >= CHUNK * u) & (key < CHUNK * u + BAND)
                halves.append(jnp.where(seen, rolled * LOG2E, NEG_INF))
            tile_t = jnp.concatenate(halves, axis=0).T
            pair_ref[p, :, u * LANES:(u + 1) * LANES] = tile_t
            if u == 0:
                single_ref[p] = tile_t[:BAND, :]


def _bias_tables(rel_bias_l, lb_gamma):
    rb = jnp.pad(rel_bias_l, ((0, 0), (0, REL_TABLE_PAD - REL_TABLE)))
    return pl.pallas_call(
        _bias_kernel,
        out_shape=(jax.ShapeDtypeStruct((HEAD_PAIRS, PAIR_BAND, 2 * LANES), F32),
                   jax.ShapeDtypeStruct((HEAD_PAIRS, BAND, LANES), F32),
                   jax.ShapeDtypeStruct((1, lb_gamma.shape[1]), F32)),
        name="rel_bias_table",
    )(rb, lb_gamma)


def _attn_probs(qs, k2, bias_t, valid_from):
    first = lax.broadcasted_iota(jnp.int32, (CHUNK, LANES), 1) < DH_A
    parts = []
    for q in qs:
        q = q * (ATTN_SCALE * LOG2E)
        parts += [jnp.where(first, q, 0.0), jnp.where(first, 0.0, q)]
    qbd = jnp.concatenate(parts, axis=0).astype(BF16)
    s = _nt(k2, qbd) + bias_t
    if valid_from is not None:
        key = lax.broadcasted_iota(jnp.int32, s.shape, 0)
        s = jnp.where(key >= valid_from, s, NEG_INF)
    mx = jnp.max(s, axis=0, keepdims=True)
    e = jnp.exp2(s - mx)
    return e.astype(BF16), jnp.sum(e, axis=0, keepdims=True)


def _attn_apply(probs, ags, vt):
    e, denom = probs
    first = lax.broadcasted_iota(jnp.int32, (CHUNK, LANES), 1) < DH_A
    ot = jnp.dot(vt, e, preferred_element_type=F32)
    o2 = (ot * (1.0 / denom)).T
    outs = []
    for u, ag in enumerate(ags):
        r = 2 * CHUNK * u
        o = jnp.where(first, o2[r:r + CHUNK], o2[r + CHUNK:r + 2 * CHUNK])
        outs.append(o * _silu(ag))
    return outs


ATTN_PIPE_DEPTH = 3


def _attn_pipeline(n_units, probs_fn, apply_fn):
    _emit_ahead(n_units, ATTN_PIPE_DEPTH, probs_fn, apply_fn)


ATTN_ROWS = 1024


def _attn_prompt_kernel(q_ref, k_ref, v_ref, ag_ref, bias_ref, o_ref, kt_ref, vt_ref,
                        kh_ref, vth_ref):
    i = pl.program_id(1)

    @pl.when(i == pl.num_programs(1) - 1)
    def _():
        tail = slice(ATTN_ROWS - WINDOW_ROWS, ATTN_ROWS)
        for p in range(HEAD_PAIRS):
            cs = slice(p * LANES, (p + 1) * LANES)
            kt_ref[0, cs, :] = k_ref[tail, cs].T
            vt_ref[0, cs, :] = v_ref[tail, cs].T

    @pl.when(i == 0)
    def _():
        kh_ref[0:WINDOW_ROWS, :] = jnp.zeros((WINDOW_ROWS, W_A), BF16)
        vth_ref[:, :, 0:WINDOW_ROWS] = jnp.zeros((HEAD_PAIRS, LANES, WINDOW_ROWS), BF16)

    def body(first_step):
        def unit(t):
            cp, p = divmod(t, HEAD_PAIRS)
            r0 = cp * 2 * CHUNK
            rs = [slice(r0 + u * CHUNK, r0 + (u + 1) * CHUNK) for u in range(2)]
            return cp, p, r0, rs, slice(p * LANES, (p + 1) * LANES)

        def probs(t):
            cp, p, r0, rs, cs = unit(t)
            own = slice(r0, r0 + 2 * CHUNK)
            hist = slice(WINDOW_ROWS + r0, WINDOW_ROWS + r0 + 2 * CHUNK)
            kh_ref[hist, cs] = k_ref[own, cs].astype(BF16)
            vth_ref[p, :, hist] = v_ref[own, cs].T.astype(BF16)
            valid_from = (LEFT_CHUNKS - 2 * cp) * CHUNK if first_step else None
            return _attn_probs([q_ref[r, cs] for r in rs], kh_ref[r0:r0 + PAIR_BAND, cs],
                               bias_ref[p], valid_from)

        def apply(t, pr):
            cp, p, r0, rs, cs = unit(t)
            outs = _attn_apply(pr, [ag_ref[r, cs] for r in rs],
                               vth_ref[p, :, r0:r0 + PAIR_BAND])
            for r, o in zip(rs, outs):
                o_ref[r, cs] = o.astype(o_ref.dtype)

        _attn_pipeline((ATTN_ROWS // (2 * CHUNK)) * HEAD_PAIRS, probs, apply)

    @pl.when(i == 0)
    def _():
        body(True)

    @pl.when(i != 0)
    def _():
        body(False)

    kh_ref[0:WINDOW_ROWS, :] = kh_ref[ATTN_ROWS:ATTN_ROWS + WINDOW_ROWS, :]
    vth_ref[:, :, 0:WINDOW_ROWS] = vth_ref[:, :, ATTN_ROWS:ATTN_ROWS + WINDOW_ROWS]


def _attn_prompt(p2d, bias_pair, batch, seq):
    assert seq % ATTN_ROWS == 0
    nt = seq // ATTN_ROWS

    def col(c):
        return pl.BlockSpec((ATTN_ROWS, W_A), lambda b, i: (b * nt + i, c))

    tail_t = pl.BlockSpec((1, W_A, WINDOW_ROWS), lambda b, i: (b, 0, 0))
    return pl.pallas_call(
        _attn_prompt_kernel,
        out_shape=(jax.ShapeDtypeStruct((batch * seq, W_A), BF16),
                   jax.ShapeDtypeStruct((batch, W_A, WINDOW_ROWS), F32),
                   jax.ShapeDtypeStruct((batch, W_A, WINDOW_ROWS), F32)),
        grid=(batch, nt),
        in_specs=[col(0), col(1), col(2), col(3),
                  pl.BlockSpec((HEAD_PAIRS, PAIR_BAND, 2 * LANES), lambda b, i: (0, 0, 0))],
        out_specs=(pl.BlockSpec((ATTN_ROWS, W_A), lambda b, i: (b * nt + i, 0)),
                   tail_t, tail_t),
        scratch_shapes=[pltpu.VMEM((WINDOW_ROWS + ATTN_ROWS, W_A), BF16),
                        pltpu.VMEM((HEAD_PAIRS, LANES, WINDOW_ROWS + ATTN_ROWS), BF16)],
        compiler_params=_params(("arbitrary", "arbitrary")),
        name="attn_prompt",
    )(p2d, p2d, p2d, p2d, bias_pair)


ATTN_SAMPLE_SEQS = 4


def _attn_sample_kernel(q_ref, k_ref, v_ref, ag_ref, ckt_ref, cvt_ref, bias_ref, o_ref,
                        kh_ref, vth_ref):
    for s in range(ATTN_SAMPLE_SEQS):
        rs = slice(s * CHUNK, (s + 1) * CHUNK)
        kh_ref[s, WINDOW_ROWS:, :] = k_ref[rs, :].astype(BF16)
        for p in range(HEAD_PAIRS):
            cs = slice(p * LANES, (p + 1) * LANES)
            kh_ref[s, 0:WINDOW_ROWS, cs] = ckt_ref[s, cs, :].T.astype(BF16)
            vth_ref[s, p, :, 0:WINDOW_ROWS] = cvt_ref[s, cs, :].astype(BF16)
            vth_ref[s, p, :, WINDOW_ROWS:] = v_ref[rs, cs].T.astype(BF16)

    def unit(t):
        s, p = divmod(t, HEAD_PAIRS)
        return s, p, slice(s * CHUNK, (s + 1) * CHUNK), slice(p * LANES, (p + 1) * LANES)

    def probs(t):
        s, p, rs, cs = unit(t)
        return _attn_probs([q_ref[rs, cs]], kh_ref[s, :, cs], bias_ref[p], None)

    def apply(t, pr):
        s, p, rs, cs = unit(t)
        (o,) = _attn_apply(pr, [ag_ref[rs, cs]], vth_ref[s, p])
        o_ref[rs, cs] = o.astype(o_ref.dtype)

    _attn_pipeline(ATTN_SAMPLE_SEQS * HEAD_PAIRS, probs, apply)


def _attn_sample(p2d, cache_kt, cache_vt, bias_single, batch):
    ns = ATTN_SAMPLE_SEQS

    def col(c):
        return pl.BlockSpec((ns * CHUNK, W_A), lambda b: (b, c))

    cache = pl.BlockSpec((ns, W_A, WINDOW_ROWS), lambda b: (b, 0, 0))
    return pl.pallas_call(
        _attn_sample_kernel,
        out_shape=jax.ShapeDtypeStruct((batch * CHUNK, W_A), BF16),
        grid=(batch // ns,),
        in_specs=[col(0), col(1), col(2), col(3), cache, cache,
                  pl.BlockSpec((HEAD_PAIRS, BAND, LANES), lambda b: (0, 0, 0))],
        out_specs=pl.BlockSpec((ns * CHUNK, W_A), lambda b: (b, 0)),
        scratch_shapes=[pltpu.VMEM((ns, BAND, W_A), BF16),
                        pltpu.VMEM((ns, HEAD_PAIRS, LANES, BAND), BF16)],
        compiler_params=_params(("arbitrary",)),
        name="attn_sample",
    )(p2d, p2d, p2d, p2d, cache_kt, cache_vt, bias_single)


HGRN_GROUP_BLOCKS = 4
HGRN_GROUP = HGRN_GROUP_BLOCKS * GLA_BLOCK
HGRN_PIPE_DEPTH = 4
OUT_ROWS = 256


SUBLANES = 8


def _block_cumsum(x):
    n, w = x.shape
    x = x.reshape(n // SUBLANES, SUBLANES, w)
    row = lax.broadcasted_iota(jnp.int32, x.shape, 1)
    s = 1
    while s < SUBLANES:
        x = x + jnp.where(row >= s, pltpu.roll(x, s, 1), 0.0)
        s *= 2
    x = x.reshape(n // GLA_BLOCK, GLA_BLOCK // SUBLANES, SUBLANES, w)
    carry = jnp.broadcast_to(x[:, 0:1, SUBLANES - 1:SUBLANES, :], (n // GLA_BLOCK, 1, SUBLANES, w))
    x = jnp.concatenate([x[:, 0:1], x[:, 1:2] + carry], axis=1)
    return x.reshape(n, w)


def _hgrn_kernel(hq_ref, hf_ref, hi_ref, hg_ref, lb_ref, ng_ref, s0_ref,
                 oa_ref, x_ref, w_ref, gf_ref,
                 y_ref, sout_ref, st_ref, qt_ref, kt_ref, kd_ref, v_ref,
                 qtf_ref, kdf_ref, bl_ref, ob_ref):
    i = pl.program_id(1)
    nseq, seq_rows = hq_ref.shape[0], hq_ref.shape[1]
    rows = nseq * seq_rows
    nb = rows // GLA_BLOCK

    def out_rows(c):
        r0 = c * OUT_ROWS
        if seq_rows >= OUT_ROWS:
            s, off = divmod(r0, seq_rows)
            return (s, slice(off, off + OUT_ROWS))
        n = OUT_ROWS // seq_rows
        return (slice(r0 // seq_rows, r0 // seq_rows + n),)

    def out_read(ref, c):
        return ref[out_rows(c)].reshape(OUT_ROWS, ref.shape[-1])

    def out_write(ref, c, val):
        if seq_rows < OUT_ROWS:
            val = val.reshape(OUT_ROWS // seq_rows, seq_rows, ref.shape[-1])
        ref[out_rows(c)] = val

    @pl.when(i == 0)
    def _():
        for s in range(nseq):
            for h in range(H_B):
                st_ref[s, h] = s0_ref[s, h].T

    def flat(ref):
        return ref[...].reshape(rows, W_B)

    lb = lb_ref[...]
    f = lb + (1.0 - lb) * _sigmoid(flat(hf_ref))
    b = _block_cumsum(jnp.log(f) * LOG2E)
    b3 = b.reshape(nb, GLA_BLOCK, W_B)
    bl = jnp.broadcast_to(b3[:, GLA_BLOCK - 1:GLA_BLOCK, :],
                          (nb, GLA_BLOCK, W_B)).reshape(rows, W_B)
    kk = 1.0 - f
    qt = _silu(flat(hq_ref)) * jnp.exp2(b)
    kd = kk * jnp.exp2(bl - b)
    qtf_ref[...] = qt
    kdf_ref[...] = kd
    bl_ref[...] = bl
    qt_ref[...] = qt.astype(BF16)
    kt_ref[...] = (kk * jnp.exp2(-b)).astype(BF16)
    kd_ref[...] = kd.astype(BF16)
    v_ref[...] = flat(hi_ref).astype(BF16)

    def begin(c):
        out_write(y_ref, c, out_read(x_ref, c)
                  + jnp.dot(out_read(oa_ref, c), w_ref[0:W_A, :], preferred_element_type=F32))

    r_i = lax.broadcasted_iota(jnp.int32, (HGRN_GROUP, HGRN_GROUP), 0)
    c_i = lax.broadcasted_iota(jnp.int32, (HGRN_GROUP, HGRN_GROUP), 1)
    rb, cb = r_i // GLA_BLOCK, c_i // GLA_BLOCK
    m_diag = (rb == cb) & (r_i >= c_i)
    m_adj = rb == cb + 1
    m_far = rb >= cb + 2

    groups = seq_rows // HGRN_GROUP
    st = {(s, h): st_ref[s, h] for s in range(nseq) for h in range(H_B)}

    def unit(t):
        sg, h = divmod(t, H_B)
        s, g = divmod(sg, groups)
        r0 = s * seq_rows + g * HGRN_GROUP
        return s, g, h, r0, slice(r0, r0 + HGRN_GROUP), slice(h * DK_B, (h + 1) * DK_B)

    def block_decays(r0, cs):
        return [bl_ref[r0 + j * GLA_BLOCK:r0 + j * GLA_BLOCK + 1, cs]
                for j in range(HGRN_GROUP_BLOCKS)]

    def scaled(x, log2_scales):
        parts = []
        for j, sc in enumerate(log2_scales):
            xb = x[j * GLA_BLOCK:(j + 1) * GLA_BLOCK]
            parts.append(xb if sc is None else xb * jnp.exp2(sc))
        return jnp.concatenate(parts, axis=0).astype(BF16)

    def local(t):
        s, g, h, r0, rs, cs = unit(t)
        if h == 0 and r0 % OUT_ROWS == 0:
            begin(r0 // OUT_ROWS)
        d0, d1, d2, d3 = block_decays(r0, cs)
        qt_g = qt_ref[rs, cs]
        kdf = kdf_ref[rs, cs]
        q1 = scaled(qtf_ref[rs, cs], [None, None, None, d2])
        k1 = scaled(kdf, [d1, None, None, None])
        a = jnp.where(m_diag, _nt(qt_g, kt_ref[rs, cs]),
                      jnp.where(m_adj, _nt(qt_g, kd_ref[rs, cs]),
                                jnp.where(m_far, _nt(q1, k1), 0.0)))
        kdp = scaled(kdf, [(d1 + d2) + d3, d2 + d3, d3, None])
        ut = lax.dot_general(v_ref[rs, cs], kdp, (((0,), (0,)), ((), ())),
                             preferred_element_type=F32)
        return a.astype(BF16), ut

    def finish(c):
        hrow = out_read(y_ref, c) + jnp.dot(ob_ref[c * OUT_ROWS:(c + 1) * OUT_ROWS, :],
                                             w_ref[W_A:, :], preferred_element_type=F32)
        ms = jnp.mean(hrow * hrow, axis=-1, keepdims=True)
        out_write(y_ref, c, hrow * lax.rsqrt(ms + EPS) * gf_ref[...])

    def carry(t, loc):
        s, g, h, r0, rs, cs = unit(t)
        a, ut = loc
        d0, d1, d2, d3 = block_decays(r0, cs)
        qs = scaled(qtf_ref[rs, cs], [None, d0, d0 + d1, (d0 + d1) + d2])
        o = (_nt(qs, st[s, h].astype(BF16))
             + jnp.dot(a, v_ref[rs, cs], preferred_element_type=F32))
        dec = jnp.exp2(((d0 + d1) + d2) + d3)
        st[s, h] = st[s, h] * dec + ut
        y = o * lax.rsqrt(jnp.mean(o * o, axis=-1, keepdims=True) + EPS) * ng_ref[:, cs]
        gs = slice(g * HGRN_GROUP, (g + 1) * HGRN_GROUP)
        ob_ref[rs, cs] = (y * _silu(hg_ref[s, gs, cs])).astype(BF16)
        if h == H_B - 1 and (r0 + HGRN_GROUP) % OUT_ROWS == 0:
            finish((r0 + HGRN_GROUP) // OUT_ROWS - 1)

    _emit_ahead(nseq * groups * H_B, HGRN_PIPE_DEPTH, local, carry)
    for (s, h), val in st.items():
        st_ref[s, h] = val

    @pl.when(i == pl.num_programs(1) - 1)
    def _():
        for s in range(nseq):
            for h in range(H_B):
                sout_ref[s, h] = st_ref[s, h].T


def _hgrn(p3d, lb, ng, s0, oa3d, x3d, w_bf16, gf, nseq, rows, name):
    batch, seq, _ = p3d.shape
    assert batch % nseq == 0 and seq % rows == 0 and rows % HGRN_GROUP == 0
    assert (nseq * rows) % OUT_ROWS == 0 and (rows % OUT_ROWS == 0 or OUT_ROWS % rows == 0)

    def col(c):
        return pl.BlockSpec((nseq, rows, W_B), lambda b, i: (b, i, c))

    vec = pl.BlockSpec((1, W_B), lambda b, i: (0, 0))
    state = pl.BlockSpec((nseq, H_B, DK_B, DV_B), lambda b, i: (b, 0, 0, 0))
    wide = pl.BlockSpec((nseq, rows, D_MODEL), lambda b, i: (b, i, 0))
    n = nseq * rows
    return pl.pallas_call(
        _hgrn_kernel,
        out_shape=(jax.ShapeDtypeStruct((batch, seq, D_MODEL), F32),
                   jax.ShapeDtypeStruct((batch, H_B, DK_B, DV_B), F32)),
        grid=(batch // nseq, seq // rows),
        in_specs=[col(4), col(5), col(6), col(7), vec, vec, state,
                  pl.BlockSpec((nseq, rows, W_A), lambda b, i: (b, i, 0)), wide,
                  pl.BlockSpec((W_A + W_B, D_MODEL), lambda b, i: (0, 0)),
                  pl.BlockSpec((1, D_MODEL), lambda b, i: (0, 0))],
        out_specs=(wide, state),
        scratch_shapes=([pltpu.VMEM((nseq, H_B, DV_B, DK_B), F32)]
                        + [pltpu.VMEM((n, W_B), BF16)] * 4
                        + [pltpu.VMEM((n, W_B), F32)] * 3
                        + [pltpu.VMEM((n, W_B), BF16)]),
        compiler_params=_params(("arbitrary", "arbitrary")),
        name=name,
    )(p3d, p3d, p3d, p3d, lb, ng, s0, oa3d, x3d, w_bf16, gf)


HGRN_PROMPT_ROWS = 512
HGRN_PROMPT_SEQS = 2
HGRN_SAMPLE_SEQS = 4


def kernel(x_prompt, x_sample, cache_attn_k, cache_attn_v, state_hgrn, ln_in_g, w_in,
           rel_bias, lb_gamma, hg_norm_g, w_out, ln_f_g):
    batch, seq, _ = x_prompt.shape
    dec_batch, dec_seq, _ = x_sample.shape
    depth = w_in.shape[0]
    assert depth == 1 and dec_seq == CHUNK and PAST_LEN % CHUNK == 0
    assert cache_attn_k.shape[2] == WINDOW_ROWS

    ng = hg_norm_g[0].reshape(1, W_B)
    g_in = ln_in_g[0].reshape(1, D_MODEL)
    g_f = ln_f_g.reshape(1, D_MODEL)
    w_out_b = w_out[0].astype(BF16)

    xp = x_prompt.reshape(batch * seq, D_MODEL)
    xs = x_sample.reshape(dec_batch * dec_seq, D_MODEL)

    bias_pair, bias_single, lb = _bias_tables(rel_bias[0], lb_gamma.astype(F32))

    pp, ps, k_s, v_s = _inproj(xp, xs, g_in, w_in[0])

    oa_p, kt_p, vt_p = _attn_prompt(pp, bias_pair, batch, seq)
    ckt = jnp.transpose(cache_attn_k[0], (0, 2, 3, 1)).reshape(dec_batch, W_A, WINDOW_ROWS)
    cvt = jnp.transpose(cache_attn_v[0], (0, 2, 3, 1)).reshape(dec_batch, W_A, WINDOW_ROWS)
    oa_s = _attn_sample(ps, ckt, cvt, bias_single, dec_batch)

    s0_p = jnp.zeros((batch, H_B, DK_B, DV_B), F32)
    y_p, st_p = _hgrn(pp.reshape(batch, seq, N_IN), lb, ng, s0_p,
                      oa_p.reshape(batch, seq, W_A), x_prompt, w_out_b, g_f,
                      HGRN_PROMPT_SEQS, HGRN_PROMPT_ROWS, "hgrn_out_prompt")
    y_s, st_s = _hgrn(ps.reshape(dec_batch, dec_seq, N_IN), lb, ng, state_hgrn[0],
                      oa_s.reshape(dec_batch, dec_seq, W_A), x_sample, w_out_b, g_f,
                      HGRN_SAMPLE_SEQS, dec_seq, "hgrn_out_sample")

    def cache_rows(t):
        return jnp.transpose(t.reshape(batch, H_A, DH_A, WINDOW_ROWS), (0, 3, 1, 2))[None]

    k_p, v_p = cache_rows(kt_p), cache_rows(vt_p)
    k_s = k_s.reshape(1, dec_batch, dec_seq, H_A, DH_A)
    v_s = v_s.reshape(1, dec_batch, dec_seq, H_A, DH_A)

    return (y_p, y_s, k_p, v_p, st_p[None], k_s, v_s, st_s[None])
```

```python
import functools

import jax
import jax.numpy as jnp
from jax import lax
from jax.experimental import pallas as pl
from jax.experimental.pallas import tpu as pltpu

F32 = jnp.float32
BF16 = jnp.bfloat16

D_MODEL = 1024
CHUNK = 64
LEFT_CHUNKS = 8
WINDOW_ROWS = LEFT_CHUNKS * CHUNK
BAND = WINDOW_ROWS + CHUNK
W_A = 512
H_A = 8
DH_A = 64
W_B = 512
H_B = 4
DK_B = 128
DV_B = 128
REL_CLIP = 128
GLA_BLOCK = 16
EPS = 1e-6
ATTN_SCALE = DH_A ** -0.5
NEG_INF = -1e30
N_IN = 4 * W_A + 4 * W_B
PAST_LEN = 2048

LANES = 128
BF16_SUBLANES = 16
HEAD_PAIRS = H_A // 2
REL_TABLE = 2 * REL_CLIP + 1
REL_TABLE_PAD = 384
TOEPLITZ_LEN = 640
VMEM_LIMIT = 56 * 1024 * 1024


def _params(semantics):
    return pltpu.CompilerParams(dimension_semantics=semantics,
                                vmem_limit_bytes=VMEM_LIMIT)


LOG2E = 1.4426950408889634


def _sigmoid(x):
    return 1.0 / (1.0 + jnp.exp2(x * -LOG2E))


def _silu(x):
    return x * _sigmoid(x)


def _nt(a, b):
    return lax.dot_general(a, b, (((1,), (1,)), ((), ())), preferred_element_type=F32)


def _emit_ahead(n_units, depth, first_fn, second_fn):
    pending = {}
    for t in range(n_units + depth):
        if t < n_units:
            pending[t] = first_fn(t)
        if t >= depth:
            second_fn(t - depth, pending.pop(t - depth))


INPROJ_ROWS = 512
INPROJ_SAMPLE_ROWS = 256
INPROJ_COLS = 512


def _inproj_kernel(xp_ref, xs_ref, g_ref, w_ref, wo_ref, pp_ref, ps_ref, ks_ref, vs_ref,
                   wob_ref, *, prompt_steps):
    i = pl.program_id(0)

    def project(x_ref, o_ref):
        wob_ref[...] = wo_ref[...].astype(BF16)
        x = x_ref[...]
        ms = jnp.mean(x * x, axis=-1, keepdims=True)
        xn = (x * lax.rsqrt(ms + EPS) * g_ref[...]).astype(BF16)
        for n0 in range(0, N_IN, INPROJ_COLS):
            o_ref[:, n0:n0 + INPROJ_COLS] = jnp.dot(
                xn, w_ref[:, n0:n0 + INPROJ_COLS].astype(BF16), preferred_element_type=F32)

    def split_heads(cols, out_ref):
        heads = jnp.stack([cols[:, h * DH_A:(h + 1) * DH_A] for h in range(H_A)], axis=0)
        out_ref[...] = jnp.swapaxes(heads, 0, 1)

    @pl.when(i < prompt_steps)
    def _():
        project(xp_ref, pp_ref)

    @pl.when(i >= prompt_steps)
    def _():
        project(xs_ref, ps_ref)
        split_heads(ps_ref[:, W_A:2 * W_A], ks_ref)
        split_heads(ps_ref[:, 2 * W_A:3 * W_A], vs_ref)


def _inproj(xp, xs, g, w, w_out):
    rows_p, rows_s = xp.shape[0], xs.shape[0]
    tp, ts = INPROJ_ROWS, INPROJ_SAMPLE_ROWS
    assert rows_p % tp == 0 and rows_s % ts == 0
    np_, ns = rows_p // tp, rows_s // ts
    assert w_out.shape[0] % (np_ * BF16_SUBLANES) == 0
    wo_rows = w_out.shape[0] // np_

    def prompt_blk(i):
        return (jnp.minimum(i, np_ - 1), 0)

    def sample_blk(i):
        return (jnp.maximum(i - np_, 0), 0)

    def sample_blk3(i):
        return (jnp.maximum(i - np_, 0), 0, 0)

    return pl.pallas_call(
        functools.partial(_inproj_kernel, prompt_steps=np_),
        out_shape=(jax.ShapeDtypeStruct((rows_p, N_IN), F32),
                   jax.ShapeDtypeStruct((rows_s, N_IN), F32),
                   jax.ShapeDtypeStruct((rows_s, H_A, DH_A), F32),
                   jax.ShapeDtypeStruct((rows_s, H_A, DH_A), F32),
                   jax.ShapeDtypeStruct(w_out.shape, BF16)),
        grid=(np_ + ns,),
        in_specs=[
            pl.BlockSpec((tp, D_MODEL), prompt_blk),
            pl.BlockSpec((ts, D_MODEL), sample_blk),
            pl.BlockSpec((1, D_MODEL), lambda i: (0, 0)),
            pl.BlockSpec((D_MODEL, N_IN), lambda i: (0, 0)),
            pl.BlockSpec((wo_rows, w_out.shape[1]), prompt_blk),
        ],
        out_specs=(pl.BlockSpec((tp, N_IN), prompt_blk),
                   pl.BlockSpec((ts, N_IN), sample_blk),
                   pl.BlockSpec((ts, H_A, DH_A), sample_blk3),
                   pl.BlockSpec((ts, H_A, DH_A), sample_blk3),
                   pl.BlockSpec((wo_rows, w_out.shape[1]), prompt_blk)),
        compiler_params=_params(("arbitrary",)),
        name="inproj",
    )(xp, xs, g, w, w_out)


PAIR_BAND = BAND + CHUNK


def _bias_kernel(rb_ref, lbg_ref, pair_ref, single_ref, lb_ref):
    g = lbg_ref[...]
    e = jnp.exp(g - jnp.max(g, axis=0, keepdims=True))
    lb_ref[...] = e[0:1] / jnp.sum(e, axis=0, keepdims=True)

    m = lax.broadcasted_iota(jnp.int32, (REL_TABLE_PAD, TOEPLITZ_LEN), 1)
    t = lax.broadcasted_iota(jnp.int32, (REL_TABLE_PAD, TOEPLITZ_LEN), 0)
    idx = jnp.clip(BAND - 1 - m, -REL_CLIP, REL_CLIP) + REL_CLIP
    onehot = jnp.where(idx == t, 1.0, 0.0).astype(BF16)
    rb = rb_ref[...]
    hi = rb.astype(BF16)
    r1 = rb - hi.astype(F32)
    mid = r1.astype(BF16)
    lo = (r1 - mid.astype(F32)).astype(BF16)
    toep = (jnp.dot(hi, onehot, preferred_element_type=F32)
            + jnp.dot(mid, onehot, preferred_element_type=F32)
            + jnp.dot(lo, onehot, preferred_element_type=F32))
    key = lax.broadcasted_iota(jnp.int32, (CHUNK, TOEPLITZ_LEN), 1)
    for p in range(HEAD_PAIRS):
        for u in range(2):
            halves = []
            for h in (2 * p, 2 * p + 1):
                rows = jnp.broadcast_to(toep[h:h + 1, :], (CHUNK, TOEPLITZ_LEN))
                shift = (TOEPLITZ_LEN - (CHUNK - 1) + CHUNK * u) % TOEPLITZ_LEN
                rolled = pltpu.roll(rows, shift, 1, stride=1, stride_axis=0)
                seen = (key >= CHUNK * u) & (key < CHUNK * u + BAND)
                halves.append(jnp.where(seen, rolled * LOG2E, NEG_INF))
            tile_t = jnp.concatenate(halves, axis=0).T
            pair_ref[p, :, u * LANES:(u + 1) * LANES] = tile_t
            if u == 0:
                single_ref[p] = tile_t[:BAND, :]


def _bias_tables(rel_bias_l, lb_gamma):
    rb = jnp.pad(rel_bias_l, ((0, 0), (0, REL_TABLE_PAD - REL_TABLE)))
    return pl.pallas_call(
        _bias_kernel,
        out_shape=(jax.ShapeDtypeStruct((HEAD_PAIRS, PAIR_BAND, 2 * LANES), F32),
                   jax.ShapeDtypeStruct((HEAD_PAIRS, BAND, LANES), F32),
                   jax.ShapeDtypeStruct((1, lb_gamma.shape[1]), F32)),
        name="rel_bias_table",
    )(rb, lb_gamma)


def _attn_probs(qs, k2, bias_t, valid_from):
    first = lax.broadcasted_iota(jnp.int32, (CHUNK, LANES), 1) < DH_A
    parts = []
    for q in qs:
        q = q * (ATTN_SCALE * LOG2E)
        parts += [jnp.where(first, q, 0.0), jnp.where(first, 0.0, q)]
    qbd = jnp.concatenate(parts, axis=0).astype(BF16)
    s = _nt(k2, qbd) + bias_t
    if valid_from is not None:
        key = lax.broadcasted_iota(jnp.int32, s.shape, 0)
        s = jnp.where(key >= valid_from, s, NEG_INF)
    mx = jnp.max(s, axis=0, keepdims=True)
    e = jnp.exp2(s - mx)
    return e.astype(BF16), jnp.sum(e, axis=0, keepdims=True)


def _attn_apply(probs, ags, vt):
    e, denom = probs
    first = lax.broadcasted_iota(jnp.int32, (CHUNK, LANES), 1) < DH_A
    ot = jnp.dot(vt, e, preferred_element_type=F32)
    o2 = (ot * (1.0 / denom)).T
    outs = []
    for u, ag in enumerate(ags):
        r = 2 * CHUNK * u
        o = jnp.where(first, o2[r:r + CHUNK], o2[r + CHUNK:r + 2 * CHUNK])
        outs.append(o * _silu(ag))
    return outs


ATTN_PIPE_DEPTH = 3


def _attn_pipeline(n_units, probs_fn, apply_fn):
    _emit_ahead(n_units, ATTN_PIPE_DEPTH, probs_fn, apply_fn)


ATTN_ROWS = 1024


def _attn_prompt_kernel(q_ref, k_ref, v_ref, ag_ref, bias_ref, o_ref, kt_ref, vt_ref,
                        kh_ref, vth_ref):
    i = pl.program_id(1)

    @pl.when(i == pl.num_programs(1) - 1)
    def _():
        tail = slice(ATTN_ROWS - WINDOW_ROWS, ATTN_ROWS)
        for p in range(HEAD_PAIRS):
            cs = slice(p * LANES, (p + 1) * LANES)
            kt_ref[0, cs, :] = k_ref[tail, cs].T
            vt_ref[0, cs, :] = v_ref[tail, cs].T

    @pl.when(i == 0)
    def _():
        kh_ref[0:WINDOW_ROWS, :] = jnp.zeros((WINDOW_ROWS, W_A), BF16)
        vth_ref[:, :, 0:WINDOW_ROWS] = jnp.zeros((HEAD_PAIRS, LANES, WINDOW_ROWS), BF16)

    def body(first_step):
        def unit(t):
            cp, p = divmod(t, HEAD_PAIRS)
            r0 = cp * 2 * CHUNK
            rs = [slice(r0 + u * CHUNK, r0 + (u + 1) * CHUNK) for u in range(2)]
            return cp, p, r0, rs, slice(p * LANES, (p + 1) * LANES)

        def probs(t):
            cp, p, r0, rs, cs = unit(t)
            own = slice(r0, r0 + 2 * CHUNK)
            hist = slice(WINDOW_ROWS + r0, WINDOW_ROWS + r0 + 2 * CHUNK)
            kh_ref[hist, cs] = k_ref[own, cs].astype(BF16)
            vth_ref[p, :, hist] = v_ref[own, cs].T.astype(BF16)
            valid_from = (LEFT_CHUNKS - 2 * cp) * CHUNK if first_step else None
            return _attn_probs([q_ref[r, cs] for r in rs], kh_ref[r0:r0 + PAIR_BAND, cs],
                               bias_ref[p], valid_from)

        def apply(t, pr):
            cp, p, r0, rs, cs = unit(t)
            outs = _attn_apply(pr, [ag_ref[r, cs] for r in rs],
                               vth_ref[p, :, r0:r0 + PAIR_BAND])
            for r, o in zip(rs, outs):
                o_ref[r, cs] = o.astype(o_ref.dtype)

        _attn_pipeline((ATTN_ROWS // (2 * CHUNK)) * HEAD_PAIRS, probs, apply)

    @pl.when(i == 0)
    def _():
        body(True)

    @pl.when(i != 0)
    def _():
        body(False)

    kh_ref[0:WINDOW_ROWS, :] = kh_ref[ATTN_ROWS:ATTN_ROWS + WINDOW_ROWS, :]
    vth_ref[:, :, 0:WINDOW_ROWS] = vth_ref[:, :, ATTN_ROWS:ATTN_ROWS + WINDOW_ROWS]


def _attn_prompt(p2d, bias_pair, batch, seq):
    assert seq % ATTN_ROWS == 0
    nt = seq // ATTN_ROWS

    def col(c):
        return pl.BlockSpec((ATTN_ROWS, W_A), lambda b, i: (b * nt + i, c))

    tail_t = pl.BlockSpec((1, W_A, WINDOW_ROWS), lambda b, i: (b, 0, 0))
    return pl.pallas_call(
        _attn_prompt_kernel,
        out_shape=(jax.ShapeDtypeStruct((batch * seq, W_A), BF16),
                   jax.ShapeDtypeStruct((batch, W_A, WINDOW_ROWS), F32),
                   jax.ShapeDtypeStruct((batch, W_A, WINDOW_ROWS), F32)),
        grid=(batch, nt),
        in_specs=[col(0), col(1), col(2), col(3),
                  pl.BlockSpec((HEAD_PAIRS, PAIR_BAND, 2 * LANES), lambda b, i: (0, 0, 0))],
        out_specs=(pl.BlockSpec((ATTN_ROWS, W_A), lambda b, i: (b * nt + i, 0)),
                   tail_t, tail_t),
        scratch_shapes=[pltpu.VMEM((WINDOW_ROWS + ATTN_ROWS, W_A), BF16),
                        pltpu.VMEM((HEAD_PAIRS, LANES, WINDOW_ROWS + ATTN_ROWS), BF16)],
        compiler_params=_params(("arbitrary", "arbitrary")),
        name="attn_prompt",
    )(p2d, p2d, p2d, p2d, bias_pair)


ATTN_SAMPLE_SEQS = 4


def _attn_sample_kernel(q_ref, k_ref, v_ref, ag_ref, ckt_ref, cvt_ref, bias_ref, o_ref,
                        kh_ref, vth_ref):
    for s in range(ATTN_SAMPLE_SEQS):
        rs = slice(s * CHUNK, (s + 1) * CHUNK)
        kh_ref[s, WINDOW_ROWS:, :] = k_ref[rs, :].astype(BF16)
        for p in range(HEAD_PAIRS):
            cs = slice(p * LANES, (p + 1) * LANES)
            kh_ref[s, 0:WINDOW_ROWS, cs] = ckt_ref[s, cs, :].T.astype(BF16)
            vth_ref[s, p, :, 0:WINDOW_ROWS] = cvt_ref[s, cs, :].astype(BF16)
            vth_ref[s, p, :, WINDOW_ROWS:] = v_ref[rs, cs].T.astype(BF16)

    def unit(t):
        s, p = divmod(t, HEAD_PAIRS)
        return s, p, slice(s * CHUNK, (s + 1) * CHUNK), slice(p * LANES, (p + 1) * LANES)

    def probs(t):
        s, p, rs, cs = unit(t)
        return _attn_probs([q_ref[rs, cs]], kh_ref[s, :, cs], bias_ref[p], None)

    def apply(t, pr):
        s, p, rs, cs = unit(t)
        (o,) = _attn_apply(pr, [ag_ref[rs, cs]], vth_ref[s, p])
        o_ref[rs, cs] = o.astype(o_ref.dtype)

    _attn_pipeline(ATTN_SAMPLE_SEQS * HEAD_PAIRS, probs, apply)


def _attn_sample(p2d, cache_kt, cache_vt, bias_single, batch):
    ns = ATTN_SAMPLE_SEQS

    def col(c):
        return pl.BlockSpec((ns * CHUNK, W_A), lambda b: (b, c))

    cache = pl.BlockSpec((ns, W_A, WINDOW_ROWS), lambda b: (b, 0, 0))
    return pl.pallas_call(
        _attn_sample_kernel,
        out_shape=jax.ShapeDtypeStruct((batch * CHUNK, W_A), BF16),
        grid=(batch // ns,),
        in_specs=[col(0), col(1), col(2), col(3), cache, cache,
                  pl.BlockSpec((HEAD_PAIRS, BAND, LANES), lambda b: (0, 0, 0))],
        out_specs=pl.BlockSpec((ns * CHUNK, W_A), lambda b: (b, 0)),
        scratch_shapes=[pltpu.VMEM((ns, BAND, W_A), BF16),
                        pltpu.VMEM((ns, HEAD_PAIRS, LANES, BAND), BF16)],
        compiler_params=_params(("arbitrary",)),
        name="attn_sample",
    )(p2d, p2d, p2d, p2d, cache_kt, cache_vt, bias_single)


HGRN_GROUP_BLOCKS = 4
HGRN_GROUP = HGRN_GROUP_BLOCKS * GLA_BLOCK
HGRN_PIPE_DEPTH = 4
OUT_ROWS = 256


SUBLANES = 8


def _block_cumsum(x):
    n, w = x.shape
    x = x.reshape(n // SUBLANES, SUBLANES, w)
    row = lax.broadcasted_iota(jnp.int32, x.shape, 1)
    s = 1
    while s < SUBLANES:
        x = x + jnp.where(row >= s, pltpu.roll(x, s, 1), 0.0)
        s *= 2
    x = x.reshape(n // GLA_BLOCK, GLA_BLOCK // SUBLANES, SUBLANES, w)
    carry = jnp.broadcast_to(x[:, 0:1, SUBLANES - 1:SUBLANES, :], (n // GLA_BLOCK, 1, SUBLANES, w))
    x = jnp.concatenate([x[:, 0:1], x[:, 1:2] + carry], axis=1)
    return x.reshape(n, w)


def _hgrn_kernel(hq_ref, hf_ref, hi_ref, hg_ref, lb_ref, ng_ref, s0_ref,
                 oa_ref, x_ref, w_ref, gf_ref,
                 y_ref, sout_ref, st_ref, qt_ref, kt_ref, kd_ref, v_ref,
                 qtf_ref, kdf_ref, bl_ref, ob_ref):
    i = pl.program_id(1)
    nseq, seq_rows = hq_ref.shape[0], hq_ref.shape[1]
    rows = nseq * seq_rows
    nb = rows // GLA_BLOCK

    def out_rows(c):
        r0 = c * OUT_ROWS
        if seq_rows >= OUT_ROWS:
            s, off = divmod(r0, seq_rows)
            return (s, slice(off, off + OUT_ROWS))
        n = OUT_ROWS // seq_rows
        return (slice(r0 // seq_rows, r0 // seq_rows + n),)

    def out_read(ref, c):
        return ref[out_rows(c)].reshape(OUT_ROWS, ref.shape[-1])

    def out_write(ref, c, val):
        if seq_rows < OUT_ROWS:
            val = val.reshape(OUT_ROWS // seq_rows, seq_rows, ref.shape[-1])
        ref[out_rows(c)] = val

    @pl.when(i == 0)
    def _():
        for s in range(nseq):
            for h in range(H_B):
                st_ref[s, h] = s0_ref[s, h].T

    def flat(ref):
        return ref[...].reshape(rows, W_B)

    lb = lb_ref[...]
    f = lb + (1.0 - lb) * _sigmoid(flat(hf_ref))
    b = _block_cumsum(jnp.log(f) * LOG2E)
    b3 = b.reshape(nb, GLA_BLOCK, W_B)
    bl = jnp.broadcast_to(b3[:, GLA_BLOCK - 1:GLA_BLOCK, :],
                          (nb, GLA_BLOCK, W_B)).reshape(rows, W_B)
    kk = 1.0 - f
    qt = _silu(flat(hq_ref)) * jnp.exp2(b)
    kd = kk * jnp.exp2(bl - b)
    qtf_ref[...] = qt
    kdf_ref[...] = kd
    bl_ref[...] = bl
    qt_ref[...] = qt.astype(BF16)
    kt_ref[...] = (kk * jnp.exp2(-b)).astype(BF16)
    kd_ref[...] = kd.astype(BF16)
    v_ref[...] = flat(hi_ref).astype(BF16)

    def begin(c):
        out_write(y_ref, c, out_read(x_ref, c)
                  + jnp.dot(out_read(oa_ref, c), w_ref[0:W_A, :], preferred_element_type=F32))

    r_i = lax.broadcasted_iota(jnp.int32, (HGRN_GROUP, HGRN_GROUP), 0)
    c_i = lax.broadcasted_iota(jnp.int32, (HGRN_GROUP, HGRN_GROUP), 1)
    rb, cb = r_i // GLA_BLOCK, c_i // GLA_BLOCK
    m_diag = (rb == cb) & (r_i >= c_i)
    m_adj = rb == cb + 1
    m_far = rb >= cb + 2

    groups = seq_rows // HGRN_GROUP
    st = {(s, h): st_ref[s, h] for s in range(nseq) for h in range(H_B)}

    def unit(t):
        sg, h = divmod(t, H_B)
        s, g = divmod(sg, groups)
        r0 = s * seq_rows + g * HGRN_GROUP
        return s, g, h, r0, slice(r0, r0 + HGRN_GROUP), slice(h * DK_B, (h + 1) * DK_B)

    def block_decays(r0, cs):
        return [bl_ref[r0 + j * GLA_BLOCK:r0 + j * GLA_BLOCK + 1, cs]
                for j in range(HGRN_GROUP_BLOCKS)]

    def scaled(x, log2_scales):
        parts = []
        for j, sc in enumerate(log2_scales):
            xb = x[j * GLA_BLOCK:(j + 1) * GLA_BLOCK]
            parts.append(xb if sc is None else xb * jnp.exp2(sc))
        return jnp.concatenate(parts, axis=0).astype(BF16)

    def local(t):
        s, g, h, r0, rs, cs = unit(t)
        if h == 0 and r0 % OUT_ROWS == 0:
            begin(r0 // OUT_ROWS)
        d0, d1, d2, d3 = block_decays(r0, cs)
        qt_g = qt_ref[rs, cs]
        kdf = kdf_ref[rs, cs]
        q1 = scaled(qtf_ref[rs, cs], [None, None, None, d2])
        k1 = scaled(kdf, [d1, None, None, None])
        a = jnp.where(m_diag, _nt(qt_g, kt_ref[rs, cs]),
                      jnp.where(m_adj, _nt(qt_g, kd_ref[rs, cs]),
                                jnp.where(m_far, _nt(q1, k1), 0.0)))
        kdp = scaled(kdf, [(d1 + d2) + d3, d2 + d3, d3, None])
        ut = lax.dot_general(v_ref[rs, cs], kdp, (((0,), (0,)), ((), ())),
                             preferred_element_type=F32)
        return a.astype(BF16), ut

    def finish(c):
        hrow = out_read(y_ref, c) + jnp.dot(ob_ref[c * OUT_ROWS:(c + 1) * OUT_ROWS, :],
                                             w_ref[W_A:, :], preferred_element_type=F32)
        ms = jnp.mean(hrow * hrow, axis=-1, keepdims=True)
        out_write(y_ref, c, hrow * lax.rsqrt(ms + EPS) * gf_ref[...])

    def carry(t, loc):
        s, g, h, r0, rs, cs = unit(t)
        a, ut = loc
        d0, d1, d2, d3 = block_decays(r0, cs)
        qs = scaled(qtf_ref[rs, cs], [None, d0, d0 + d1, (d0 + d1) + d2])
        o = (_nt(qs, st[s, h].astype(BF16))
             + jnp.dot(a, v_ref[rs, cs], preferred_element_type=F32))
        dec = jnp.exp2(((d0 + d1) + d2) + d3)
        st[s, h] = st[s, h] * dec + ut
        y = o * lax.rsqrt(jnp.mean(o * o, axis=-1, keepdims=True) + EPS) * ng_ref[:, cs]
        gs = slice(g * HGRN_GROUP, (g + 1) * HGRN_GROUP)
        ob_ref[rs, cs] = (y * _silu(hg_ref[s, gs, cs])).astype(BF16)
        if h == H_B - 1 and (r0 + HGRN_GROUP) % OUT_ROWS == 0:
            finish((r0 + HGRN_GROUP) // OUT_ROWS - 1)

    _emit_ahead(nseq * groups * H_B, HGRN_PIPE_DEPTH, local, carry)
    for (s, h), val in st.items():
        st_ref[s, h] = val

    @pl.when(i == pl.num_programs(1) - 1)
    def _():
        for s in range(nseq):
            for h in range(H_B):
                sout_ref[s, h] = st_ref[s, h].T


def _hgrn(p3d, lb, ng, s0, oa3d, x3d, w_bf16, gf, nseq, rows, name):
    batch, seq, _ = p3d.shape
    assert batch % nseq == 0 and seq % rows == 0 and rows % HGRN_GROUP == 0
    assert (nseq * rows) % OUT_ROWS == 0 and (rows % OUT_ROWS == 0 or OUT_ROWS % rows == 0)

    def col(c):
        return pl.BlockSpec((nseq, rows, W_B), lambda b, i: (b, i, c))

    vec = pl.BlockSpec((1, W_B), lambda b, i: (0, 0))
    state = pl.BlockSpec((nseq, H_B, DK_B, DV_B), lambda b, i: (b, 0, 0, 0))
    wide = pl.BlockSpec((nseq, rows, D_MODEL), lambda b, i: (b, i, 0))
    n = nseq * rows
    return pl.pallas_call(
        _hgrn_kernel,
        out_shape=(jax.ShapeDtypeStruct((batch, seq, D_MODEL), F32),
                   jax.ShapeDtypeStruct((batch, H_B, DK_B, DV_B), F32)),
        grid=(batch // nseq, seq // rows),
        in_specs=[col(4), col(5), col(6), col(7), vec, vec, state,
                  pl.BlockSpec((nseq, rows, W_A), lambda b, i: (b, i, 0)), wide,
                  pl.BlockSpec((W_A + W_B, D_MODEL), lambda b, i: (0, 0)),
                  pl.BlockSpec((1, D_MODEL), lambda b, i: (0, 0))],
        out_specs=(wide, state),
        scratch_shapes=([pltpu.VMEM((nseq, H_B, DV_B, DK_B), F32)]
                        + [pltpu.VMEM((n, W_B), BF16)] * 4
                        + [pltpu.VMEM((n, W_B), F32)] * 3
                        + [pltpu.VMEM((n, W_B), BF16)]),
        compiler_params=_params(("arbitrary", "arbitrary")),
        name=name,
    )(p3d, p3d, p3d, p3d, lb, ng, s0, oa3d, x3d, w_bf16, gf)


HGRN_PROMPT_ROWS = 512
HGRN_PROMPT_SEQS = 2
HGRN_SAMPLE_SEQS = 4


def kernel(x_prompt, x_sample, cache_attn_k, cache_attn_v, state_hgrn, ln_in_g, w_in,
           rel_bias, lb_gamma, hg_norm_g, w_out, ln_f_g):
    batch, seq, _ = x_prompt.shape
    dec_batch, dec_seq, _ = x_sample.shape
    depth = w_in.shape[0]
    assert depth == 1 and dec_seq == CHUNK and PAST_LEN % CHUNK == 0
    assert cache_attn_k.shape[2] == WINDOW_ROWS

    ng = hg_norm_g[0].reshape(1, W_B)
    g_in = ln_in_g[0].reshape(1, D_MODEL)
    g_f = ln_f_g.reshape(1, D_MODEL)

    xp = x_prompt.reshape(batch * seq, D_MODEL)
    xs = x_sample.reshape(dec_batch * dec_seq, D_MODEL)

    bias_pair, bias_single, lb = _bias_tables(rel_bias[0], lb_gamma.astype(F32))

    pp, ps, k_s, v_s, w_out_b = _inproj(xp, xs, g_in, w_in[0], w_out[0])

    oa_p, kt_p, vt_p = _attn_prompt(pp, bias_pair, batch, seq)
    ckt = jnp.transpose(cache_attn_k[0], (0, 2, 3, 1)).reshape(dec_batch, W_A, WINDOW_ROWS)
    cvt = jnp.transpose(cache_attn_v[0], (0, 2, 3, 1)).reshape(dec_batch, W_A, WINDOW_ROWS)
    oa_s = _attn_sample(ps, ckt, cvt, bias_single, dec_batch)

    s0_p = jnp.zeros((batch, H_B, DK_B, DV_B), F32)
    y_p, st_p = _hgrn(pp.reshape(batch, seq, N_IN), lb, ng, s0_p,
                      oa_p.reshape(batch, seq, W_A), x_prompt, w_out_b, g_f,
                      HGRN_PROMPT_SEQS, HGRN_PROMPT_ROWS, "hgrn_out_prompt")
    y_s, st_s = _hgrn(ps.reshape(dec_batch, dec_seq, N_IN), lb, ng, state_hgrn[0],
                      oa_s.reshape(dec_batch, dec_seq, W_A), x_sample, w_out_b, g_f,
                      HGRN_SAMPLE_SEQS, dec_seq, "hgrn_out_sample")

    def cache_rows(t):
        return jnp.transpose(t.reshape(batch, H_A, DH_A, WINDOW_ROWS), (0, 3, 1, 2))[None]

    k_p, v_p = cache_rows(kt_p), cache_rows(vt_p)
    k_s = k_s.reshape(1, dec_batch, dec_seq, H_A, DH_A)
    v_s = v_s.reshape(1, dec_batch, dec_seq, H_A, DH_A)

    return (y_p, y_s, k_p, v_p, st_p[None], k_s, v_s, st_s[None])
```

```python
import functools

import jax
import jax.numpy as jnp
from jax import lax
from jax.experimental import pallas as pl
from jax.experimental.pallas import tpu as pltpu

F32 = jnp.float32
BF16 = jnp.bfloat16

D_MODEL = 1024
CHUNK = 64
LEFT_CHUNKS = 8
WINDOW_ROWS = LEFT_CHUNKS * CHUNK
BAND = WINDOW_ROWS + CHUNK
W_A = 512
H_A = 8
DH_A = 64
W_B = 512
H_B = 4
DK_B = 128
DV_B = 128
REL_CLIP = 128
GLA_BLOCK = 16
EPS = 1e-6
ATTN_SCALE = DH_A ** -0.5
NEG_INF = -1e30
N_IN = 4 * W_A + 4 * W_B
PAST_LEN = 2048

LANES = 128
BF16_SUBLANES = 16
HEAD_PAIRS = H_A // 2
REL_TABLE = 2 * REL_CLIP + 1
REL_TABLE_PAD = 384
TOEPLITZ_LEN = 640
VMEM_LIMIT = 56 * 1024 * 1024


def _params(semantics):
    return pltpu.CompilerParams(dimension_semantics=semantics,
                                vmem_limit_bytes=VMEM_LIMIT)


LOG2E = 1.4426950408889634


def _sigmoid(x):
    return 1.0 / (1.0 + jnp.exp2(x * -LOG2E))


def _silu(x):
    return x * _sigmoid(x)


def _nt(a, b):
    return lax.dot_general(a, b, (((1,), (1,)), ((), ())), preferred_element_type=F32)


def _emit_ahead(n_units, depth, first_fn, second_fn):
    pending = {}
    for t in range(n_units + depth):
        if t < n_units:
            pending[t] = first_fn(t)
        if t >= depth:
            second_fn(t - depth, pending.pop(t - depth))


INPROJ_ROWS = 512
INPROJ_SAMPLE_ROWS = 256
INPROJ_COLS = 512


def _inproj_kernel(xp_ref, xs_ref, g_ref, w_ref, wo_ref, pp_ref, ps_ref, ks_ref, vs_ref,
                   wob_ref, *, prompt_steps):
    i = pl.program_id(0)

    def project(x_ref, o_ref):
        wob_ref[...] = wo_ref[...].astype(BF16)
        x = x_ref[...]
        ms = jnp.mean(x * x, axis=-1, keepdims=True)
        xn = (x * lax.rsqrt(ms + EPS) * g_ref[...]).astype(BF16)
        for n0 in range(0, N_IN, INPROJ_COLS):
            o_ref[:, n0:n0 + INPROJ_COLS] = jnp.dot(
                xn, w_ref[:, n0:n0 + INPROJ_COLS].astype(BF16), preferred_element_type=F32)

    def split_heads(cols, out_ref):
        heads = jnp.stack([cols[:, h * DH_A:(h + 1) * DH_A] for h in range(H_A)], axis=0)
        out_ref[...] = jnp.swapaxes(heads, 0, 1)

    @pl.when(i < prompt_steps)
    def _():
        project(xp_ref, pp_ref)

    @pl.when(i >= prompt_steps)
    def _():
        project(xs_ref, ps_ref)
        split_heads(ps_ref[:, W_A:2 * W_A], ks_ref)
        split_heads(ps_ref[:, 2 * W_A:3 * W_A], vs_ref)


def _inproj(xp, xs, g, w, w_out):
    rows_p, rows_s = xp.shape[0], xs.shape[0]
    tp, ts = INPROJ_ROWS, INPROJ_SAMPLE_ROWS
    assert rows_p % tp == 0 and rows_s % ts == 0
    np_, ns = rows_p // tp, rows_s // ts
    assert w_out.shape[0] % (np_ * BF16_SUBLANES) == 0
    wo_rows = w_out.shape[0] // np_

    def prompt_blk(i):
        return (jnp.minimum(i, np_ - 1), 0)

    def sample_blk(i):
        return (jnp.maximum(i - np_, 0), 0)

    def sample_blk3(i):
        return (jnp.maximum(i - np_, 0), 0, 0)

    return pl.pallas_call(
        functools.partial(_inproj_kernel, prompt_steps=np_),
        out_shape=(jax.ShapeDtypeStruct((rows_p, N_IN), F32),
                   jax.ShapeDtypeStruct((rows_s, N_IN), F32),
                   jax.ShapeDtypeStruct((rows_s, H_A, DH_A), F32),
                   jax.ShapeDtypeStruct((rows_s, H_A, DH_A), F32),
                   jax.ShapeDtypeStruct(w_out.shape, BF16)),
        grid=(np_ + ns,),
        in_specs=[
            pl.BlockSpec((tp, D_MODEL), prompt_blk),
            pl.BlockSpec((ts, D_MODEL), sample_blk),
            pl.BlockSpec((1, D_MODEL), lambda i: (0, 0)),
            pl.BlockSpec((D_MODEL, N_IN), lambda i: (0, 0)),
            pl.BlockSpec((wo_rows, w_out.shape[1]), prompt_blk),
        ],
        out_specs=(pl.BlockSpec((tp, N_IN), prompt_blk),
                   pl.BlockSpec((ts, N_IN), sample_blk),
                   pl.BlockSpec((ts, H_A, DH_A), sample_blk3),
                   pl.BlockSpec((ts, H_A, DH_A), sample_blk3),
                   pl.BlockSpec((wo_rows, w_out.shape[1]), prompt_blk)),
        compiler_params=_params(("arbitrary",)),
        name="inproj",
    )(xp, xs, g, w, w_out)


PAIR_BAND = BAND + CHUNK


def _bias_kernel(rb_ref, lbg_ref, pair_ref, single_ref, lb_ref):
    g = lbg_ref[...]
    e = jnp.exp(g - jnp.max(g, axis=0, keepdims=True))
    lb_ref[...] = e[0:1] / jnp.sum(e, axis=0, keepdims=True)

    m = lax.broadcasted_iota(jnp.int32, (REL_TABLE_PAD, TOEPLITZ_LEN), 1)
    t = lax.broadcasted_iota(jnp.int32, (REL_TABLE_PAD, TOEPLITZ_LEN), 0)
    idx = jnp.clip(BAND - 1 - m, -REL_CLIP, REL_CLIP) + REL_CLIP
    onehot = jnp.where(idx == t, 1.0, 0.0).astype(BF16)
    rb = rb_ref[...]
    hi = rb.astype(BF16)
    r1 = rb - hi.astype(F32)
    mid = r1.astype(BF16)
    lo = (r1 - mid.astype(F32)).astype(BF16)
    toep = (jnp.dot(hi, onehot, preferred_element_type=F32)
            + jnp.dot(mid, onehot, preferred_element_type=F32)
            + jnp.dot(lo, onehot, preferred_element_type=F32))
    key = lax.broadcasted_iota(jnp.int32, (CHUNK, TOEPLITZ_LEN), 1)
    for p in range(HEAD_PAIRS):
        for u in range(2):
            halves = []
            for h in (2 * p, 2 * p + 1):
                rows = jnp.broadcast_to(toep[h:h + 1, :], (CHUNK, TOEPLITZ_LEN))
                shift = (TOEPLITZ_LEN - (CHUNK - 1) + CHUNK * u) % TOEPLITZ_LEN
                rolled = pltpu.roll(rows, shift, 1, stride=1, stride_axis=0)
                seen = (key >= CHUNK * u) & (key < CHUNK * u + BAND)
                halves.append(jnp.where(seen, rolled * LOG2E, NEG_INF))
            tile_t = jnp.concatenate(halves, axis=0).T
            pair_ref[p, :, u * LANES:(u + 1) * LANES] = tile_t
            if u == 0:
                single_ref[p] = tile_t[:BAND, :]


def _bias_tables(rel_bias_l, lb_gamma):
    rb = jnp.pad(rel_bias_l, ((0, 0), (0, REL_TABLE_PAD - REL_TABLE)))
    return pl.pallas_call(
        _bias_kernel,
        out_shape=(jax.ShapeDtypeStruct((HEAD_PAIRS, PAIR_BAND, 2 * LANES), F32),
                   jax.ShapeDtypeStruct((HEAD_PAIRS, BAND, LANES), F32),
                   jax.ShapeDtypeStruct((1, lb_gamma.shape[1]), F32)),
        name="rel_bias_table",
    )(rb, lb_gamma)


def _attn_probs(qs, k2, bias_t, valid_from):
    first = lax.broadcasted_iota(jnp.int32, (CHUNK, LANES), 1) < DH_A
    parts = []
    for q in qs:
        q = q * (ATTN_SCALE * LOG2E)
        parts += [jnp.where(first, q, 0.0), jnp.where(first, 0.0, q)]
    qbd = jnp.concatenate(parts, axis=0).astype(BF16)
    s = _nt(k2, qbd) + bias_t
    if valid_from is not None:
        key = lax.broadcasted_iota(jnp.int32, s.shape, 0)
        s = jnp.where(key >= valid_from, s, NEG_INF)
    mx = jnp.max(s, axis=0, keepdims=True)
    e = jnp.exp2(s - mx)
    return e.astype(BF16), jnp.sum(e, axis=0, keepdims=True)


def _attn_apply(probs, ags, vt):
    e, denom = probs
    first = lax.broadcasted_iota(jnp.int32, (CHUNK, LANES), 1) < DH_A
    ot = jnp.dot(vt, e, preferred_element_type=F32)
    o2 = (ot * (1.0 / denom)).T
    outs = []
    for u, ag in enumerate(ags):
        r = 2 * CHUNK * u
        o = jnp.where(first, o2[r:r + CHUNK], o2[r + CHUNK:r + 2 * CHUNK])
        outs.append(o * _silu(ag))
    return outs


ATTN_PIPE_DEPTH = 3


def _attn_pipeline(n_units, probs_fn, apply_fn):
    _emit_ahead(n_units, ATTN_PIPE_DEPTH, probs_fn, apply_fn)


ATTN_ROWS = 1024


def _attn_prompt_kernel(q_ref, k_ref, v_ref, ag_ref, bias_ref, o_ref, kt_ref, vt_ref,
                        kh_ref, vth_ref):
    i = pl.program_id(1)

    @pl.when(i == pl.num_programs(1) - 1)
    def _():
        tail = slice(ATTN_ROWS - WINDOW_ROWS, ATTN_ROWS)
        for p in range(HEAD_PAIRS):
            cs = slice(p * LANES, (p + 1) * LANES)
            kt_ref[0, cs, :] = k_ref[tail, cs].T
            vt_ref[0, cs, :] = v_ref[tail, cs].T

    @pl.when(i == 0)
    def _():
        kh_ref[0:WINDOW_ROWS, :] = jnp.zeros((WINDOW_ROWS, W_A), BF16)
        vth_ref[:, :, 0:WINDOW_ROWS] = jnp.zeros((HEAD_PAIRS, LANES, WINDOW_ROWS), BF16)

    def body(first_step):
        def unit(t):
            cp, p = divmod(t, HEAD_PAIRS)
            r0 = cp * 2 * CHUNK
            rs = [slice(r0 + u * CHUNK, r0 + (u + 1) * CHUNK) for u in range(2)]
            return cp, p, r0, rs, slice(p * LANES, (p + 1) * LANES)

        def probs(t):
            cp, p, r0, rs, cs = unit(t)
            own = slice(r0, r0 + 2 * CHUNK)
            hist = slice(WINDOW_ROWS + r0, WINDOW_ROWS + r0 + 2 * CHUNK)
            kh_ref[hist, cs] = k_ref[own, cs].astype(BF16)
            vth_ref[p, :, hist] = v_ref[own, cs].T.astype(BF16)
            valid_from = (LEFT_CHUNKS - 2 * cp) * CHUNK if first_step else None
            return _attn_probs([q_ref[r, cs] for r in rs], kh_ref[r0:r0 + PAIR_BAND, cs],
                               bias_ref[p], valid_from)

        def apply(t, pr):
            cp, p, r0, rs, cs = unit(t)
            outs = _attn_apply(pr, [ag_ref[r, cs] for r in rs],
                               vth_ref[p, :, r0:r0 + PAIR_BAND])
            for r, o in zip(rs, outs):
                o_ref[r, cs] = o.astype(o_ref.dtype)

        _attn_pipeline((ATTN_ROWS // (2 * CHUNK)) * HEAD_PAIRS, probs, apply)

    @pl.when(i == 0)
    def _():
        body(True)

    @pl.when(i != 0)
    def _():
        body(False)

    kh_ref[0:WINDOW_ROWS, :] = kh_ref[ATTN_ROWS:ATTN_ROWS + WINDOW_ROWS, :]
    vth_ref[:, :, 0:WINDOW_ROWS] = vth_ref[:, :, ATTN_ROWS:ATTN_ROWS + WINDOW_ROWS]


def _attn_prompt(p2d, bias_pair, batch, seq):
    assert seq % ATTN_ROWS == 0
    nt = seq // ATTN_ROWS

    def col(c):
        return pl.BlockSpec((ATTN_ROWS, W_A), lambda b, i: (b * nt + i, c))

    tail_t = pl.BlockSpec((1, W_A, WINDOW_ROWS), lambda b, i: (b, 0, 0))
    return pl.pallas_call(
        _attn_prompt_kernel,
        out_shape=(jax.ShapeDtypeStruct((batch * seq, W_A), BF16),
                   jax.ShapeDtypeStruct((batch, W_A, WINDOW_ROWS), F32),
                   jax.ShapeDtypeStruct((batch, W_A, WINDOW_ROWS), F32)),
        grid=(batch, nt),
        in_specs=[col(0), col(1), col(2), col(3),
                  pl.BlockSpec((HEAD_PAIRS, PAIR_BAND, 2 * LANES), lambda b, i: (0, 0, 0))],
        out_specs=(pl.BlockSpec((ATTN_ROWS, W_A), lambda b, i: (b * nt + i, 0)),
                   tail_t, tail_t),
        scratch_shapes=[pltpu.VMEM((WINDOW_ROWS + ATTN_ROWS, W_A), BF16),
                        pltpu.VMEM((HEAD_PAIRS, LANES, WINDOW_ROWS + ATTN_ROWS), BF16)],
        compiler_params=_params(("arbitrary", "arbitrary")),
        name="attn_prompt",
    )(p2d, p2d, p2d, p2d, bias_pair)


ATTN_SAMPLE_SEQS = 4


CACHE_BUFFERS = 3


def _attn_sample_kernel(q_ref, k_ref, v_ref, ag_ref, ckt_hbm, cvt_hbm, bias_ref, o_ref,
                        kh_ref, vth_ref, ckt_buf, cvt_buf, sem):
    b = pl.program_id(0)
    steps = pl.num_programs(0)

    def cache_copies(step):
        slot = step % CACHE_BUFFERS
        rows = pl.ds(step * ATTN_SAMPLE_SEQS, ATTN_SAMPLE_SEQS)
        return (pltpu.make_async_copy(ckt_hbm.at[rows], ckt_buf.at[slot], sem.at[0, slot]),
                pltpu.make_async_copy(cvt_hbm.at[rows], cvt_buf.at[slot], sem.at[1, slot]))

    @pl.when(b == 0)
    def _():
        for first in range(CACHE_BUFFERS - 1):
            for c in cache_copies(first):
                c.start()

    @pl.when(b + CACHE_BUFFERS - 1 < steps)
    def _():
        for c in cache_copies(b + CACHE_BUFFERS - 1):
            c.start()

    for c in cache_copies(b):
        c.wait()
    ckt_ref = ckt_buf.at[b % CACHE_BUFFERS]
    cvt_ref = cvt_buf.at[b % CACHE_BUFFERS]

    for s in range(ATTN_SAMPLE_SEQS):
        rs = slice(s * CHUNK, (s + 1) * CHUNK)
        kh_ref[s, WINDOW_ROWS:, :] = k_ref[rs, :].astype(BF16)
        for p in range(HEAD_PAIRS):
            cs = slice(p * LANES, (p + 1) * LANES)
            kh_ref[s, 0:WINDOW_ROWS, cs] = ckt_ref[s, cs, :].T.astype(BF16)
            vth_ref[s, p, :, 0:WINDOW_ROWS] = cvt_ref[s, cs, :].astype(BF16)
            vth_ref[s, p, :, WINDOW_ROWS:] = v_ref[rs, cs].T.astype(BF16)

    def unit(t):
        s, p = divmod(t, HEAD_PAIRS)
        return s, p, slice(s * CHUNK, (s + 1) * CHUNK), slice(p * LANES, (p + 1) * LANES)

    def probs(t):
        s, p, rs, cs = unit(t)
        return _attn_probs([q_ref[rs, cs]], kh_ref[s, :, cs], bias_ref[p], None)

    def apply(t, pr):
        s, p, rs, cs = unit(t)
        (o,) = _attn_apply(pr, [ag_ref[rs, cs]], vth_ref[s, p])
        o_ref[rs, cs] = o.astype(o_ref.dtype)

    _attn_pipeline(ATTN_SAMPLE_SEQS * HEAD_PAIRS, probs, apply)


def _attn_sample(p2d, cache_kt, cache_vt, bias_single, batch):
    ns = ATTN_SAMPLE_SEQS

    def col(c):
        return pl.BlockSpec((ns * CHUNK, W_A), lambda b: (b, c))

    assert batch % ns == 0 and batch // ns >= CACHE_BUFFERS - 1
    cache = pl.BlockSpec(memory_space=pl.ANY)
    cache_ring = pltpu.VMEM((CACHE_BUFFERS, ns, W_A, WINDOW_ROWS), F32)
    return pl.pallas_call(
        _attn_sample_kernel,
        out_shape=jax.ShapeDtypeStruct((batch * CHUNK, W_A), BF16),
        grid=(batch // ns,),
        in_specs=[col(0), col(1), col(2), col(3), cache, cache,
                  pl.BlockSpec((HEAD_PAIRS, BAND, LANES), lambda b: (0, 0, 0))],
        out_specs=pl.BlockSpec((ns * CHUNK, W_A), lambda b: (b, 0)),
        scratch_shapes=[pltpu.VMEM((ns, BAND, W_A), BF16),
                        pltpu.VMEM((ns, HEAD_PAIRS, LANES, BAND), BF16),
                        cache_ring, cache_ring,
                        pltpu.SemaphoreType.DMA((2, CACHE_BUFFERS))],
        compiler_params=_params(("arbitrary",)),
        name="attn_sample",
    )(p2d, p2d, p2d, p2d, cache_kt, cache_vt, bias_single)


HGRN_GROUP_BLOCKS = 4
HGRN_GROUP = HGRN_GROUP_BLOCKS * GLA_BLOCK
HGRN_PIPE_DEPTH = 4
OUT_ROWS = 256


SUBLANES = 8


def _block_cumsum(x):
    n, w = x.shape
    x = x.reshape(n // SUBLANES, SUBLANES, w)
    row = lax.broadcasted_iota(jnp.int32, x.shape, 1)
    s = 1
    while s < SUBLANES:
        x = x + jnp.where(row >= s, pltpu.roll(x, s, 1), 0.0)
        s *= 2
    x = x.reshape(n // GLA_BLOCK, GLA_BLOCK // SUBLANES, SUBLANES, w)
    carry = jnp.broadcast_to(x[:, 0:1, SUBLANES - 1:SUBLANES, :], (n // GLA_BLOCK, 1, SUBLANES, w))
    x = jnp.concatenate([x[:, 0:1], x[:, 1:2] + carry], axis=1)
    return x.reshape(n, w)


def _hgrn_kernel(hq_ref, hf_ref, hi_ref, hg_ref, lb_ref, ng_ref, s0_ref,
                 oa_ref, x_ref, w_ref, gf_ref,
                 y_ref, sout_ref, st_ref, qt_ref, kt_ref, kd_ref, v_ref,
                 qtf_ref, kdf_ref, bl_ref, ob_ref):
    i = pl.program_id(1)
    nseq, seq_rows = hq_ref.shape[0], hq_ref.shape[1]
    rows = nseq * seq_rows
    nb = rows // GLA_BLOCK

    def out_rows(c):
        r0 = c * OUT_ROWS
        if seq_rows >= OUT_ROWS:
            s, off = divmod(r0, seq_rows)
            return (s, slice(off, off + OUT_ROWS))
        n = OUT_ROWS // seq_rows
        return (slice(r0 // seq_rows, r0 // seq_rows + n),)

    def out_read(ref, c):
        return ref[out_rows(c)].reshape(OUT_ROWS, ref.shape[-1])

    def out_write(ref, c, val):
        if seq_rows < OUT_ROWS:
            val = val.reshape(OUT_ROWS // seq_rows, seq_rows, ref.shape[-1])
        ref[out_rows(c)] = val

    @pl.when(i == 0)
    def _():
        for s in range(nseq):
            for h in range(H_B):
                st_ref[s, h] = s0_ref[s, h].T

    def flat(ref):
        return ref[...].reshape(rows, W_B)

    lb = lb_ref[...]
    f = lb + (1.0 - lb) * _sigmoid(flat(hf_ref))
    b = _block_cumsum(jnp.log(f) * LOG2E)
    b3 = b.reshape(nb, GLA_BLOCK, W_B)
    bl = jnp.broadcast_to(b3[:, GLA_BLOCK - 1:GLA_BLOCK, :],
                          (nb, GLA_BLOCK, W_B)).reshape(rows, W_B)
    kk = 1.0 - f
    qt = _silu(flat(hq_ref)) * jnp.exp2(b)
    kd = kk * jnp.exp2(bl - b)
    qtf_ref[...] = qt
    kdf_ref[...] = kd
    bl_ref[...] = bl
    qt_ref[...] = qt.astype(BF16)
    kt_ref[...] = (kk * jnp.exp2(-b)).astype(BF16)
    kd_ref[...] = kd.astype(BF16)
    v_ref[...] = flat(hi_ref).astype(BF16)

    def begin(c):
        out_write(y_ref, c, out_read(x_ref, c)
                  + jnp.dot(out_read(oa_ref, c), w_ref[0:W_A, :], preferred_element_type=F32))

    r_i = lax.broadcasted_iota(jnp.int32, (HGRN_GROUP, HGRN_GROUP), 0)
    c_i = lax.broadcasted_iota(jnp.int32, (HGRN_GROUP, HGRN_GROUP), 1)
    rb, cb = r_i // GLA_BLOCK, c_i // GLA_BLOCK
    m_diag = (rb == cb) & (r_i >= c_i)
    m_adj = rb == cb + 1
    m_far = rb >= cb + 2

    groups = seq_rows // HGRN_GROUP
    st = {(s, h): st_ref[s, h] for s in range(nseq) for h in range(H_B)}

    def unit(t):
        sg, h = divmod(t, H_B)
        s, g = divmod(sg, groups)
        r0 = s * seq_rows + g * HGRN_GROUP
        return s, g, h, r0, slice(r0, r0 + HGRN_GROUP), slice(h * DK_B, (h + 1) * DK_B)

    def block_decays(r0, cs):
        return [bl_ref[r0 + j * GLA_BLOCK:r0 + j * GLA_BLOCK + 1, cs]
                for j in range(HGRN_GROUP_BLOCKS)]

    def scaled(x, log2_scales):
        parts = []
        for j, sc in enumerate(log2_scales):
            xb = x[j * GLA_BLOCK:(j + 1) * GLA_BLOCK]
            parts.append(xb if sc is None else xb * jnp.exp2(sc))
        return jnp.concatenate(parts, axis=0).astype(BF16)

    def local(t):
        s, g, h, r0, rs, cs = unit(t)
        if h == 0 and r0 % OUT_ROWS == 0:
            begin(r0 // OUT_ROWS)
        d0, d1, d2, d3 = block_decays(r0, cs)
        qt_g = qt_ref[rs, cs]
        kdf = kdf_ref[rs, cs]
        q1 = scaled(qtf_ref[rs, cs], [None, None, None, d2])
        k1 = scaled(kdf, [d1, None, None, None])
        a = jnp.where(m_diag, _nt(qt_g, kt_ref[rs, cs]),
                      jnp.where(m_adj, _nt(qt_g, kd_ref[rs, cs]),
                                jnp.where(m_far, _nt(q1, k1), 0.0)))
        kdp = scaled(kdf, [(d1 + d2) + d3, d2 + d3, d3, None])
        ut = lax.dot_general(v_ref[rs, cs], kdp, (((0,), (0,)), ((), ())),
                             preferred_element_type=F32)
        return a.astype(BF16), ut

    def finish(c):
        hrow = out_read(y_ref, c) + jnp.dot(ob_ref[c * OUT_ROWS:(c + 1) * OUT_ROWS, :],
                                             w_ref[W_A:, :], preferred_element_type=F32)
        ms = jnp.mean(hrow * hrow, axis=-1, keepdims=True)
        out_write(y_ref, c, hrow * lax.rsqrt(ms + EPS) * gf_ref[...])

    def carry(t, loc):
        s, g, h, r0, rs, cs = unit(t)
        a, ut = loc
        d0, d1, d2, d3 = block_decays(r0, cs)
        qs = scaled(qtf_ref[rs, cs], [None, d0, d0 + d1, (d0 + d1) + d2])
        o = (_nt(qs, st[s, h].astype(BF16))
             + jnp.dot(a, v_ref[rs, cs], preferred_element_type=F32))
        dec = jnp.exp2(((d0 + d1) + d2) + d3)
        st[s, h] = st[s, h] * dec + ut
        y = o * lax.rsqrt(jnp.mean(o * o, axis=-1, keepdims=True) + EPS) * ng_ref[:, cs]
        gs = slice(g * HGRN_GROUP, (g + 1) * HGRN_GROUP)
        ob_ref[rs, cs] = (y * _silu(hg_ref[s, gs, cs])).astype(BF16)
        if h == H_B - 1 and (r0 + HGRN_GROUP) % OUT_ROWS == 0:
            finish((r0 + HGRN_GROUP) // OUT_ROWS - 1)

    _emit_ahead(nseq * groups * H_B, HGRN_PIPE_DEPTH, local, carry)
    for (s, h), val in st.items():
        st_ref[s, h] = val

    @pl.when(i == pl.num_programs(1) - 1)
    def _():
        for s in range(nseq):
            for h in range(H_B):
                sout_ref[s, h] = st_ref[s, h].T


def _hgrn(p3d, lb, ng, s0, oa3d, x3d, w_bf16, gf, nseq, rows, name):
    batch, seq, _ = p3d.shape
    assert batch % nseq == 0 and seq % rows == 0 and rows % HGRN_GROUP == 0
    assert (nseq * rows) % OUT_ROWS == 0 and (rows % OUT_ROWS == 0 or OUT_ROWS % rows == 0)

    def col(c):
        return pl.BlockSpec((nseq, rows, W_B), lambda b, i: (b, i, c))

    vec = pl.BlockSpec((1, W_B), lambda b, i: (0, 0))
    state = pl.BlockSpec((nseq, H_B, DK_B, DV_B), lambda b, i: (b, 0, 0, 0))
    wide = pl.BlockSpec((nseq, rows, D_MODEL), lambda b, i: (b, i, 0))
    n = nseq * rows
    return pl.pallas_call(
        _hgrn_kernel,
        out_shape=(jax.ShapeDtypeStruct((batch, seq, D_MODEL), F32),
                   jax.ShapeDtypeStruct((batch, H_B, DK_B, DV_B), F32)),
        grid=(batch // nseq, seq // rows),
        in_specs=[col(4), col(5), col(6), col(7), vec, vec, state,
                  pl.BlockSpec((nseq, rows, W_A), lambda b, i: (b, i, 0)), wide,
                  pl.BlockSpec((W_A + W_B, D_MODEL), lambda b, i: (0, 0)),
                  pl.BlockSpec((1, D_MODEL), lambda b, i: (0, 0))],
        out_specs=(wide, state),
        scratch_shapes=([pltpu.VMEM((nseq, H_B, DV_B, DK_B), F32)]
                        + [pltpu.VMEM((n, W_B), BF16)] * 4
                        + [pltpu.VMEM((n, W_B), F32)] * 3
                        + [pltpu.VMEM((n, W_B), BF16)]),
        compiler_params=_params(("arbitrary", "arbitrary")),
        name=name,
    )(p3d, p3d, p3d, p3d, lb, ng, s0, oa3d, x3d, w_bf16, gf)


HGRN_PROMPT_ROWS = 512
HGRN_PROMPT_SEQS = 2
HGRN_SAMPLE_SEQS = 4


def kernel(x_prompt, x_sample, cache_attn_k, cache_attn_v, state_hgrn, ln_in_g, w_in,
           rel_bias, lb_gamma, hg_norm_g, w_out, ln_f_g):
    batch, seq, _ = x_prompt.shape
    dec_batch, dec_seq, _ = x_sample.shape
    depth = w_in.shape[0]
    assert depth == 1 and dec_seq == CHUNK and PAST_LEN % CHUNK == 0
    assert cache_attn_k.shape[2] == WINDOW_ROWS

    ng = hg_norm_g[0].reshape(1, W_B)
    g_in = ln_in_g[0].reshape(1, D_MODEL)
    g_f = ln_f_g.reshape(1, D_MODEL)

    xp = x_prompt.reshape(batch * seq, D_MODEL)
    xs = x_sample.reshape(dec_batch * dec_seq, D_MODEL)

    bias_pair, bias_single, lb = _bias_tables(rel_bias[0], lb_gamma.astype(F32))

    pp, ps, k_s, v_s, w_out_b = _inproj(xp, xs, g_in, w_in[0], w_out[0])

    oa_p, kt_p, vt_p = _attn_prompt(pp, bias_pair, batch, seq)
    ckt = jnp.transpose(cache_attn_k[0], (0, 2, 3, 1)).reshape(dec_batch, W_A, WINDOW_ROWS)
    cvt = jnp.transpose(cache_attn_v[0], (0, 2, 3, 1)).reshape(dec_batch, W_A, WINDOW_ROWS)
    oa_s = _attn_sample(ps, ckt, cvt, bias_single, dec_batch)

    s0_p = jnp.zeros((batch, H_B, DK_B, DV_B), F32)
    y_p, st_p = _hgrn(pp.reshape(batch, seq, N_IN), lb, ng, s0_p,
                      oa_p.reshape(batch, seq, W_A), x_prompt, w_out_b, g_f,
                      HGRN_PROMPT_SEQS, HGRN_PROMPT_ROWS, "hgrn_out_prompt")
    y_s, st_s = _hgrn(ps.reshape(dec_batch, dec_seq, N_IN), lb, ng, state_hgrn[0],
                      oa_s.reshape(dec_batch, dec_seq, W_A), x_sample, w_out_b, g_f,
                      HGRN_SAMPLE_SEQS, dec_seq, "hgrn_out_sample")

    def cache_rows(t):
        return jnp.transpose(t.reshape(batch, H_A, DH_A, WINDOW_ROWS), (0, 3, 1, 2))[None]

    k_p, v_p = cache_rows(kt_p), cache_rows(vt_p)
    k_s = k_s.reshape(1, dec_batch, dec_seq, H_A, DH_A)
    v_s = v_s.reshape(1, dec_batch, dec_seq, H_A, DH_A)

    return (y_p, y_s, k_p, v_p, st_p[None], k_s, v_s, st_s[None])
```

```python
import functools

import jax
import jax.numpy as jnp
from jax import lax
from jax.experimental import pallas as pl
from jax.experimental.pallas import tpu as pltpu

F32 = jnp.float32
BF16 = jnp.bfloat16

D_MODEL = 1024
CHUNK = 64
LEFT_CHUNKS = 8
WINDOW_ROWS = LEFT_CHUNKS * CHUNK
BAND = WINDOW_ROWS + CHUNK
W_A = 512
H_A = 8
DH_A = 64
W_B = 512
H_B = 4
DK_B = 128
DV_B = 128
REL_CLIP = 128
GLA_BLOCK = 16
EPS = 1e-6
ATTN_SCALE = DH_A ** -0.5
NEG_INF = -1e30
N_IN = 4 * W_A + 4 * W_B
PAST_LEN = 2048

LANES = 128
BF16_SUBLANES = 16
HEAD_PAIRS = H_A // 2
REL_TABLE = 2 * REL_CLIP + 1
REL_TABLE_PAD = 384
TOEPLITZ_LEN = 640
VMEM_LIMIT = 56 * 1024 * 1024


def _params(semantics):
    return pltpu.CompilerParams(dimension_semantics=semantics,
                                vmem_limit_bytes=VMEM_LIMIT)


LOG2E = 1.4426950408889634


def _sigmoid(x):
    return 1.0 / (1.0 + jnp.exp2(x * -LOG2E))


def _silu(x):
    return x * _sigmoid(x)


def _nt(a, b):
    return lax.dot_general(a, b, (((1,), (1,)), ((), ())), preferred_element_type=F32)


def _emit_ahead(n_units, depth, first_fn, second_fn):
    pending = {}
    for t in range(n_units + depth):
        if t < n_units:
            pending[t] = first_fn(t)
        if t >= depth:
            second_fn(t - depth, pending.pop(t - depth))


INPROJ_ROWS = 512
INPROJ_SAMPLE_ROWS = 256
INPROJ_COLS = 512


def _inproj_kernel(xp_ref, xs_ref, g_ref, w_ref, wo_ref, pp_ref, ps_ref, ks_ref, vs_ref,
                   wob_ref, *, prompt_steps):
    i = pl.program_id(0)

    def project(x_ref, o_ref):
        wob_ref[...] = wo_ref[...].astype(BF16)
        x = x_ref[...]
        ms = jnp.mean(x * x, axis=-1, keepdims=True)
        xn = (x * lax.rsqrt(ms + EPS) * g_ref[...]).astype(BF16)
        for n0 in range(0, N_IN, INPROJ_COLS):
            o_ref[:, n0:n0 + INPROJ_COLS] = jnp.dot(
                xn, w_ref[:, n0:n0 + INPROJ_COLS].astype(BF16), preferred_element_type=F32)

    def split_heads(cols, out_ref):
        heads = jnp.stack([cols[:, h * DH_A:(h + 1) * DH_A] for h in range(H_A)], axis=0)
        out_ref[...] = jnp.swapaxes(heads, 0, 1)

    @pl.when(i < prompt_steps)
    def _():
        project(xp_ref, pp_ref)

    @pl.when(i >= prompt_steps)
    def _():
        project(xs_ref, ps_ref)
        split_heads(ps_ref[:, W_A:2 * W_A], ks_ref)
        split_heads(ps_ref[:, 2 * W_A:3 * W_A], vs_ref)


def _inproj(xp, xs, g, w, w_out):
    rows_p, rows_s = xp.shape[0], xs.shape[0]
    tp, ts = INPROJ_ROWS, INPROJ_SAMPLE_ROWS
    assert rows_p % tp == 0 and rows_s % ts == 0
    np_, ns = rows_p // tp, rows_s // ts
    assert w_out.shape[0] % (np_ * BF16_SUBLANES) == 0
    wo_rows = w_out.shape[0] // np_

    def prompt_blk(i):
        return (jnp.minimum(i, np_ - 1), 0)

    def sample_blk(i):
        return (jnp.maximum(i - np_, 0), 0)

    def sample_blk3(i):
        return (jnp.maximum(i - np_, 0), 0, 0)

    return pl.pallas_call(
        functools.partial(_inproj_kernel, prompt_steps=np_),
        out_shape=(jax.ShapeDtypeStruct((rows_p, N_IN), F32),
                   jax.ShapeDtypeStruct((rows_s, N_IN), F32),
                   jax.ShapeDtypeStruct((rows_s, H_A, DH_A), F32),
                   jax.ShapeDtypeStruct((rows_s, H_A, DH_A), F32),
                   jax.ShapeDtypeStruct(w_out.shape, BF16)),
        grid=(np_ + ns,),
        in_specs=[
            pl.BlockSpec((tp, D_MODEL), prompt_blk),
            pl.BlockSpec((ts, D_MODEL), sample_blk),
            pl.BlockSpec((1, D_MODEL), lambda i: (0, 0)),
            pl.BlockSpec((D_MODEL, N_IN), lambda i: (0, 0)),
            pl.BlockSpec((wo_rows, w_out.shape[1]), prompt_blk),
        ],
        out_specs=(pl.BlockSpec((tp, N_IN), prompt_blk),
                   pl.BlockSpec((ts, N_IN), sample_blk),
                   pl.BlockSpec((ts, H_A, DH_A), sample_blk3),
                   pl.BlockSpec((ts, H_A, DH_A), sample_blk3),
                   pl.BlockSpec((wo_rows, w_out.shape[1]), prompt_blk)),
        compiler_params=_params(("arbitrary",)),
        name="inproj",
    )(xp, xs, g, w, w_out)


PAIR_BAND = BAND + CHUNK


def _bias_kernel(rb_ref, lbg_ref, pair_ref, single_ref, lb_ref, rbp_ref):
    g = lbg_ref[...]
    e = jnp.exp(g - jnp.max(g, axis=0, keepdims=True))
    lb_ref[...] = e[0:1] / jnp.sum(e, axis=0, keepdims=True)

    m = lax.broadcasted_iota(jnp.int32, (REL_TABLE_PAD, TOEPLITZ_LEN), 1)
    t = lax.broadcasted_iota(jnp.int32, (REL_TABLE_PAD, TOEPLITZ_LEN), 0)
    idx = jnp.clip(BAND - 1 - m, -REL_CLIP, REL_CLIP) + REL_CLIP
    onehot = jnp.where(idx == t, 1.0, 0.0).astype(BF16)
    rbp_ref[...] = jnp.zeros(rbp_ref.shape, F32)
    rbp_ref[:, 0:REL_TABLE] = rb_ref[...]
    rb = rbp_ref[...]
    hi = rb.astype(BF16)
    r1 = rb - hi.astype(F32)
    mid = r1.astype(BF16)
    lo = (r1 - mid.astype(F32)).astype(BF16)
    toep = (jnp.dot(hi, onehot, preferred_element_type=F32)
            + jnp.dot(mid, onehot, preferred_element_type=F32)
            + jnp.dot(lo, onehot, preferred_element_type=F32))
    key = lax.broadcasted_iota(jnp.int32, (CHUNK, TOEPLITZ_LEN), 1)
    for p in range(HEAD_PAIRS):
        for u in range(2):
            halves = []
            for h in (2 * p, 2 * p + 1):
                rows = jnp.broadcast_to(toep[h:h + 1, :], (CHUNK, TOEPLITZ_LEN))
                shift = (TOEPLITZ_LEN - (CHUNK - 1) + CHUNK * u) % TOEPLITZ_LEN
                rolled = pltpu.roll(rows, shift, 1, stride=1, stride_axis=0)
                seen = (key >= CHUNK * u) & (key < CHUNK * u + BAND)
                halves.append(jnp.where(seen, rolled * LOG2E, NEG_INF))
            tile_t = jnp.concatenate(halves, axis=0).T
            pair_ref[p, :, u * LANES:(u + 1) * LANES] = tile_t
            if u == 0:
                single_ref[p] = tile_t[:BAND, :]


def _bias_tables(rel_bias_l, lb_gamma):
    assert rel_bias_l.shape == (H_A, REL_TABLE)
    return pl.pallas_call(
        _bias_kernel,
        out_shape=(jax.ShapeDtypeStruct((HEAD_PAIRS, PAIR_BAND, 2 * LANES), F32),
                   jax.ShapeDtypeStruct((HEAD_PAIRS, BAND, LANES), F32),
                   jax.ShapeDtypeStruct((1, lb_gamma.shape[1]), F32)),
        scratch_shapes=[pltpu.VMEM((H_A, REL_TABLE_PAD), F32)],
        name="rel_bias_table",
    )(rel_bias_l, lb_gamma)


def _attn_probs(qs, k2, bias_t, valid_from):
    first = lax.broadcasted_iota(jnp.int32, (CHUNK, LANES), 1) < DH_A
    parts = []
    for q in qs:
        q = q * (ATTN_SCALE * LOG2E)
        parts += [jnp.where(first, q, 0.0), jnp.where(first, 0.0, q)]
    qbd = jnp.concatenate(parts, axis=0).astype(BF16)
    s = _nt(k2, qbd) + bias_t
    if valid_from is not None:
        key = lax.broadcasted_iota(jnp.int32, s.shape, 0)
        s = jnp.where(key >= valid_from, s, NEG_INF)
    mx = jnp.max(s, axis=0, keepdims=True)
    e = jnp.exp2(s - mx)
    return e.astype(BF16), jnp.sum(e, axis=0, keepdims=True)


def _attn_apply(probs, ags, vt):
    e, denom = probs
    first = lax.broadcasted_iota(jnp.int32, (CHUNK, LANES), 1) < DH_A
    ot = jnp.dot(vt, e, preferred_element_type=F32)
    o2 = (ot * (1.0 / denom)).T
    outs = []
    for u, ag in enumerate(ags):
        r = 2 * CHUNK * u
        o = jnp.where(first, o2[r:r + CHUNK], o2[r + CHUNK:r + 2 * CHUNK])
        outs.append(o * _silu(ag))
    return outs


ATTN_PIPE_DEPTH = 3


def _attn_pipeline(n_units, probs_fn, apply_fn):
    _emit_ahead(n_units, ATTN_PIPE_DEPTH, probs_fn, apply_fn)


ATTN_ROWS = 1024


def _attn_prompt_kernel(q_ref, k_ref, v_ref, ag_ref, bias_ref, o_ref, kt_ref, vt_ref,
                        kh_ref, vth_ref):
    i = pl.program_id(1)

    @pl.when(i == pl.num_programs(1) - 1)
    def _():
        tail = slice(ATTN_ROWS - WINDOW_ROWS, ATTN_ROWS)
        for p in range(HEAD_PAIRS):
            cs = slice(p * LANES, (p + 1) * LANES)
            kt_ref[0, cs, :] = k_ref[tail, cs].T
            vt_ref[0, cs, :] = v_ref[tail, cs].T

    @pl.when(i == 0)
    def _():
        kh_ref[0:WINDOW_ROWS, :] = jnp.zeros((WINDOW_ROWS, W_A), BF16)
        vth_ref[:, :, 0:WINDOW_ROWS] = jnp.zeros((HEAD_PAIRS, LANES, WINDOW_ROWS), BF16)

    def body(first_step):
        def unit(t):
            cp, p = divmod(t, HEAD_PAIRS)
            r0 = cp * 2 * CHUNK
            rs = [slice(r0 + u * CHUNK, r0 + (u + 1) * CHUNK) for u in range(2)]
            return cp, p, r0, rs, slice(p * LANES, (p + 1) * LANES)

        def probs(t):
            cp, p, r0, rs, cs = unit(t)
            own = slice(r0, r0 + 2 * CHUNK)
            hist = slice(WINDOW_ROWS + r0, WINDOW_ROWS + r0 + 2 * CHUNK)
            kh_ref[hist, cs] = k_ref[own, cs].astype(BF16)
            vth_ref[p, :, hist] = v_ref[own, cs].T.astype(BF16)
            valid_from = (LEFT_CHUNKS - 2 * cp) * CHUNK if first_step else None
            return _attn_probs([q_ref[r, cs] for r in rs], kh_ref[r0:r0 + PAIR_BAND, cs],
                               bias_ref[p], valid_from)

        def apply(t, pr):
            cp, p, r0, rs, cs = unit(t)
            outs = _attn_apply(pr, [ag_ref[r, cs] for r in rs],
                               vth_ref[p, :, r0:r0 + PAIR_BAND])
            for r, o in zip(rs, outs):
                o_ref[r, cs] = o.astype(o_ref.dtype)

        _attn_pipeline((ATTN_ROWS // (2 * CHUNK)) * HEAD_PAIRS, probs, apply)

    @pl.when(i == 0)
    def _():
        body(True)

    @pl.when(i != 0)
    def _():
        body(False)

    kh_ref[0:WINDOW_ROWS, :] = kh_ref[ATTN_ROWS:ATTN_ROWS + WINDOW_ROWS, :]
    vth_ref[:, :, 0:WINDOW_ROWS] = vth_ref[:, :, ATTN_ROWS:ATTN_ROWS + WINDOW_ROWS]


def _attn_prompt(p2d, bias_pair, batch, seq):
    assert seq % ATTN_ROWS == 0
    nt = seq // ATTN_ROWS

    def col(c):
        return pl.BlockSpec((ATTN_ROWS, W_A), lambda b, i: (b * nt + i, c))

    tail_t = pl.BlockSpec((1, W_A, WINDOW_ROWS), lambda b, i: (b, 0, 0))
    return pl.pallas_call(
        _attn_prompt_kernel,
        out_shape=(jax.ShapeDtypeStruct((batch * seq, W_A), BF16),
                   jax.ShapeDtypeStruct((batch, W_A, WINDOW_ROWS), F32),
                   jax.ShapeDtypeStruct((batch, W_A, WINDOW_ROWS), F32)),
        grid=(batch, nt),
        in_specs=[col(0), col(1), col(2), col(3),
                  pl.BlockSpec((HEAD_PAIRS, PAIR_BAND, 2 * LANES), lambda b, i: (0, 0, 0))],
        out_specs=(pl.BlockSpec((ATTN_ROWS, W_A), lambda b, i: (b * nt + i, 0)),
                   tail_t, tail_t),
        scratch_shapes=[pltpu.VMEM((WINDOW_ROWS + ATTN_ROWS, W_A), BF16),
                        pltpu.VMEM((HEAD_PAIRS, LANES, WINDOW_ROWS + ATTN_ROWS), BF16)],
        compiler_params=_params(("arbitrary", "arbitrary")),
        name="attn_prompt",
    )(p2d, p2d, p2d, p2d, bias_pair)


ATTN_SAMPLE_SEQS = 4


def _attn_sample_kernel(q_ref, k_ref, v_ref, ag_ref, ckt_ref, cvt_ref, bias_ref, o_ref,
                        kh_ref, vth_ref):
    for s in range(ATTN_SAMPLE_SEQS):
        rs = slice(s * CHUNK, (s + 1) * CHUNK)
        kh_ref[s, WINDOW_ROWS:, :] = k_ref[rs, :].astype(BF16)
        for p in range(HEAD_PAIRS):
            cs = slice(p * LANES, (p + 1) * LANES)
            kh_ref[s, 0:WINDOW_ROWS, cs] = ckt_ref[s, cs, :].T.astype(BF16)
            vth_ref[s, p, :, 0:WINDOW_ROWS] = cvt_ref[s, cs, :].astype(BF16)
            vth_ref[s, p, :, WINDOW_ROWS:] = v_ref[rs, cs].T.astype(BF16)

    def unit(t):
        s, p = divmod(t, HEAD_PAIRS)
        return s, p, slice(s * CHUNK, (s + 1) * CHUNK), slice(p * LANES, (p + 1) * LANES)

    def probs(t):
        s, p, rs, cs = unit(t)
        return _attn_probs([q_ref[rs, cs]], kh_ref[s, :, cs], bias_ref[p], None)

    def apply(t, pr):
        s, p, rs, cs = unit(t)
        (o,) = _attn_apply(pr, [ag_ref[rs, cs]], vth_ref[s, p])
        o_ref[rs, cs] = o.astype(o_ref.dtype)

    _attn_pipeline(ATTN_SAMPLE_SEQS * HEAD_PAIRS, probs, apply)


def _attn_sample(p2d, cache_kt, cache_vt, bias_single, batch):
    ns = ATTN_SAMPLE_SEQS

    def col(c):
        return pl.BlockSpec((ns * CHUNK, W_A), lambda b: (b, c))

    cache = pl.BlockSpec((ns, W_A, WINDOW_ROWS), lambda b: (b, 0, 0))
    return pl.pallas_call(
        _attn_sample_kernel,
        out_shape=jax.ShapeDtypeStruct((batch * CHUNK, W_A), BF16),
        grid=(batch // ns,),
        in_specs=[col(0), col(1), col(2), col(3), cache, cache,
                  pl.BlockSpec((HEAD_PAIRS, BAND, LANES), lambda b: (0, 0, 0))],
        out_specs=pl.BlockSpec((ns * CHUNK, W_A), lambda b: (b, 0)),
        scratch_shapes=[pltpu.VMEM((ns, BAND, W_A), BF16),
                        pltpu.VMEM((ns, HEAD_PAIRS, LANES, BAND), BF16)],
        compiler_params=_params(("arbitrary",)),
        name="attn_sample",
    )(p2d, p2d, p2d, p2d, cache_kt, cache_vt, bias_single)


HGRN_GROUP_BLOCKS = 4
HGRN_GROUP = HGRN_GROUP_BLOCKS * GLA_BLOCK
HGRN_PIPE_DEPTH = 4
OUT_ROWS = 256


SUBLANES = 8


def _block_cumsum(x):
    n, w = x.shape
    x = x.reshape(n // SUBLANES, SUBLANES, w)
    row = lax.broadcasted_iota(jnp.int32, x.shape, 1)
    s = 1
    while s < SUBLANES:
        x = x + jnp.where(row >= s, pltpu.roll(x, s, 1), 0.0)
        s *= 2
    x = x.reshape(n // GLA_BLOCK, GLA_BLOCK // SUBLANES, SUBLANES, w)
    carry = jnp.broadcast_to(x[:, 0:1, SUBLANES - 1:SUBLANES, :], (n // GLA_BLOCK, 1, SUBLANES, w))
    x = jnp.concatenate([x[:, 0:1], x[:, 1:2] + carry], axis=1)
    return x.reshape(n, w)


def _hgrn_kernel(hq_ref, hf_ref, hi_ref, hg_ref, lb_ref, ng_ref, s0_ref,
                 oa_ref, x_ref, w_ref, gf_ref,
                 y_ref, sout_ref, st_ref, qt_ref, kt_ref, kd_ref, v_ref,
                 qtf_ref, kdf_ref, bl_ref, ob_ref):
    i = pl.program_id(1)
    nseq, seq_rows = hq_ref.shape[0], hq_ref.shape[1]
    rows = nseq * seq_rows
    nb = rows // GLA_BLOCK

    def out_rows(c):
        r0 = c * OUT_ROWS
        if seq_rows >= OUT_ROWS:
            s, off = divmod(r0, seq_rows)
            return (s, slice(off, off + OUT_ROWS))
        n = OUT_ROWS // seq_rows
        return (slice(r0 // seq_rows, r0 // seq_rows + n),)

    def out_read(ref, c):
        return ref[out_rows(c)].reshape(OUT_ROWS, ref.shape[-1])

    def out_write(ref, c, val):
        if seq_rows < OUT_ROWS:
            val = val.reshape(OUT_ROWS // seq_rows, seq_rows, ref.shape[-1])
        ref[out_rows(c)] = val

    @pl.when(i == 0)
    def _():
        for s in range(nseq):
            for h in range(H_B):
                st_ref[s, h] = s0_ref[s, h].T

    def flat(ref):
        return ref[...].reshape(rows, W_B)

    lb = lb_ref[...]
    f = lb + (1.0 - lb) * _sigmoid(flat(hf_ref))
    b = _block_cumsum(jnp.log(f) * LOG2E)
    b3 = b.reshape(nb, GLA_BLOCK, W_B)
    bl = jnp.broadcast_to(b3[:, GLA_BLOCK - 1:GLA_BLOCK, :],
                          (nb, GLA_BLOCK, W_B)).reshape(rows, W_B)
    kk = 1.0 - f
    qt = _silu(flat(hq_ref)) * jnp.exp2(b)
    kd = kk * jnp.exp2(bl - b)
    qtf_ref[...] = qt
    kdf_ref[...] = kd
    bl_ref[...] = bl
    qt_ref[...] = qt.astype(BF16)
    kt_ref[...] = (kk * jnp.exp2(-b)).astype(BF16)
    kd_ref[...] = kd.astype(BF16)
    v_ref[...] = flat(hi_ref).astype(BF16)

    def begin(c):
        out_write(y_ref, c, out_read(x_ref, c)
                  + jnp.dot(out_read(oa_ref, c), w_ref[0:W_A, :], preferred_element_type=F32))

    r_i = lax.broadcasted_iota(jnp.int32, (HGRN_GROUP, HGRN_GROUP), 0)
    c_i = lax.broadcasted_iota(jnp.int32, (HGRN_GROUP, HGRN_GROUP), 1)
    rb, cb = r_i // GLA_BLOCK, c_i // GLA_BLOCK
    m_diag = (rb == cb) & (r_i >= c_i)
    m_adj = rb == cb + 1
    m_far = rb >= cb + 2

    groups = seq_rows // HGRN_GROUP
    st = {(s, h): st_ref[s, h] for s in range(nseq) for h in range(H_B)}

    def unit(t):
        sg, h = divmod(t, H_B)
        s, g = divmod(sg, groups)
        r0 = s * seq_rows + g * HGRN_GROUP
        return s, g, h, r0, slice(r0, r0 + HGRN_GROUP), slice(h * DK_B, (h + 1) * DK_B)

    def block_decays(r0, cs):
        return [bl_ref[r0 + j * GLA_BLOCK:r0 + j * GLA_BLOCK + 1, cs]
                for j in range(HGRN_GROUP_BLOCKS)]

    def scaled(x, log2_scales):
        parts = []
        for j, sc in enumerate(log2_scales):
            xb = x[j * GLA_BLOCK:(j + 1) * GLA_BLOCK]
            parts.append(xb if sc is None else xb * jnp.exp2(sc))
        return jnp.concatenate(parts, axis=0).astype(BF16)

    def local(t):
        s, g, h, r0, rs, cs = unit(t)
        if h == 0 and r0 % OUT_ROWS == 0:
            begin(r0 // OUT_ROWS)
        d0, d1, d2, d3 = block_decays(r0, cs)
        qt_g = qt_ref[rs, cs]
        kdf = kdf_ref[rs, cs]
        q1 = scaled(qtf_ref[rs, cs], [None, None, None, d2])
        k1 = scaled(kdf, [d1, None, None, None])
        a = jnp.where(m_diag, _nt(qt_g, kt_ref[rs, cs]),
                      jnp.where(m_adj, _nt(qt_g, kd_ref[rs, cs]),
                                jnp.where(m_far, _nt(q1, k1), 0.0)))
        kdp = scaled(kdf, [(d1 + d2) + d3, d2 + d3, d3, None])
        ut = lax.dot_general(v_ref[rs, cs], kdp, (((0,), (0,)), ((), ())),
                             preferred_element_type=F32)
        return a.astype(BF16), ut

    def finish(c):
        hrow = out_read(y_ref, c) + jnp.dot(ob_ref[c * OUT_ROWS:(c + 1) * OUT_ROWS, :],
                                             w_ref[W_A:, :], preferred_element_type=F32)
        ms = jnp.mean(hrow * hrow, axis=-1, keepdims=True)
        out_write(y_ref, c, hrow * lax.rsqrt(ms + EPS) * gf_ref[...])

    def carry(t, loc):
        s, g, h, r0, rs, cs = unit(t)
        a, ut = loc
        d0, d1, d2, d3 = block_decays(r0, cs)
        qs = scaled(qtf_ref[rs, cs], [None, d0, d0 + d1, (d0 + d1) + d2])
        o = (_nt(qs, st[s, h].astype(BF16))
             + jnp.dot(a, v_ref[rs, cs], preferred_element_type=F32))
        dec = jnp.exp2(((d0 + d1) + d2) + d3)
        st[s, h] = st[s, h] * dec + ut
        y = o * lax.rsqrt(jnp.mean(o * o, axis=-1, keepdims=True) + EPS) * ng_ref[:, cs]
        gs = slice(g * HGRN_GROUP, (g + 1) * HGRN_GROUP)
        ob_ref[rs, cs] = (y * _silu(hg_ref[s, gs, cs])).astype(BF16)
        if h == H_B - 1 and (r0 + HGRN_GROUP) % OUT_ROWS == 0:
            finish((r0 + HGRN_GROUP) // OUT_ROWS - 1)

    _emit_ahead(nseq * groups * H_B, HGRN_PIPE_DEPTH, local, carry)
    for (s, h), val in st.items():
        st_ref[s, h] = val

    @pl.when(i == pl.num_programs(1) - 1)
    def _():
        for s in range(nseq):
            for h in range(H_B):
                sout_ref[s, h] = st_ref[s, h].T


def _hgrn(p3d, lb, ng, s0, oa3d, x3d, w_bf16, gf, nseq, rows, name):
    batch, seq, _ = p3d.shape
    assert batch % nseq == 0 and seq % rows == 0 and rows % HGRN_GROUP == 0
    assert (nseq * rows) % OUT_ROWS == 0 and (rows % OUT_ROWS == 0 or OUT_ROWS % rows == 0)

    def col(c):
        return pl.BlockSpec((nseq, rows, W_B), lambda b, i: (b, i, c))

    vec = pl.BlockSpec((1, W_B), lambda b, i: (0, 0))
    state = pl.BlockSpec((nseq, H_B, DK_B, DV_B), lambda b, i: (b, 0, 0, 0))
    wide = pl.BlockSpec((nseq, rows, D_MODEL), lambda b, i: (b, i, 0))
    n = nseq * rows
    return pl.pallas_call(
        _hgrn_kernel,
        out_shape=(jax.ShapeDtypeStruct((batch, seq, D_MODEL), F32),
                   jax.ShapeDtypeStruct((batch, H_B, DK_B, DV_B), F32)),
        grid=(batch // nseq, seq // rows),
        in_specs=[col(4), col(5), col(6), col(7), vec, vec, state,
                  pl.BlockSpec((nseq, rows, W_A), lambda b, i: (b, i, 0)), wide,
                  pl.BlockSpec((W_A + W_B, D_MODEL), lambda b, i: (0, 0)),
                  pl.BlockSpec((1, D_MODEL), lambda b, i: (0, 0))],
        out_specs=(wide, state),
        scratch_shapes=([pltpu.VMEM((nseq, H_B, DV_B, DK_B), F32)]
                        + [pltpu.VMEM((n, W_B), BF16)] * 4
                        + [pltpu.VMEM((n, W_B), F32)] * 3
                        + [pltpu.VMEM((n, W_B), BF16)]),
        compiler_params=_params(("arbitrary", "arbitrary")),
        name=name,
    )(p3d, p3d, p3d, p3d, lb, ng, s0, oa3d, x3d, w_bf16, gf)


HGRN_PROMPT_ROWS = 512
HGRN_PROMPT_SEQS = 2
HGRN_SAMPLE_SEQS = 4


def kernel(x_prompt, x_sample, cache_attn_k, cache_attn_v, state_hgrn, ln_in_g, w_in,
           rel_bias, lb_gamma, hg_norm_g, w_out, ln_f_g):
    batch, seq, _ = x_prompt.shape
    dec_batch, dec_seq, _ = x_sample.shape
    depth = w_in.shape[0]
    assert depth == 1 and dec_seq == CHUNK and PAST_LEN % CHUNK == 0
    assert cache_attn_k.shape[2] == WINDOW_ROWS

    ng = hg_norm_g[0].reshape(1, W_B)
    g_in = ln_in_g[0].reshape(1, D_MODEL)
    g_f = ln_f_g.reshape(1, D_MODEL)

    xp = x_prompt.reshape(batch * seq, D_MODEL)
    xs = x_sample.reshape(dec_batch * dec_seq, D_MODEL)

    bias_pair, bias_single, lb = _bias_tables(rel_bias[0], lb_gamma.astype(F32))

    pp, ps, k_s, v_s, w_out_b = _inproj(xp, xs, g_in, w_in[0], w_out[0])

    oa_p, kt_p, vt_p = _attn_prompt(pp, bias_pair, batch, seq)
    ckt = jnp.transpose(cache_attn_k[0], (0, 2, 3, 1)).reshape(dec_batch, W_A, WINDOW_ROWS)
    cvt = jnp.transpose(cache_attn_v[0], (0, 2, 3, 1)).reshape(dec_batch, W_A, WINDOW_ROWS)
    oa_s = _attn_sample(ps, ckt, cvt, bias_single, dec_batch)

    s0_p = jnp.zeros((batch, H_B, DK_B, DV_B), F32)
    y_p, st_p = _hgrn(pp.reshape(batch, seq, N_IN), lb, ng, s0_p,
                      oa_p.reshape(batch, seq, W_A), x_prompt, w_out_b, g_f,
                      HGRN_PROMPT_SEQS, HGRN_PROMPT_ROWS, "hgrn_out_prompt")
    y_s, st_s = _hgrn(ps.reshape(dec_batch, dec_seq, N_IN), lb, ng, state_hgrn[0],
                      oa_s.reshape(dec_batch, dec_seq, W_A), x_sample, w_out_b, g_f,
                      HGRN_SAMPLE_SEQS, dec_seq, "hgrn_out_sample")

    def cache_rows(t):
        return jnp.transpose(t.reshape(batch, H_A, DH_A, WINDOW_ROWS), (0, 3, 1, 2))[None]

    k_p, v_p = cache_rows(kt_p), cache_rows(vt_p)
    k_s = k_s.reshape(1, dec_batch, dec_seq, H_A, DH_A)
    v_s = v_s.reshape(1, dec_batch, dec_seq, H_A, DH_A)

    return (y_p, y_s, k_p, v_p, st_p[None], k_s, v_s, st_s[None])
```
